```python
import math
import jax, jax.numpy as jnp
from jax import lax
import numpy as np

D_MODEL = 2048
BATCH = 8
SEQ = 2048
DEPTH = 2

D_MIX = D_MODEL
D_RG = D_MIX // 2
RG_BLOCKS = 8
RG_BLOCK = D_RG // RG_BLOCKS
RG_C = 8.0
CONV_W = 4
H_B = 8
HD_B = (D_MIX - D_RG) // H_B
DSW_PATTERNS = ((128, 1), (512, 4), (2048, 16))
N_BUCKETS = 32
MAX_DIST = 2048
H_C = 4
HD_C = (D_MIX // 2) // H_C
MLSTM_CHUNK = 128
H_D = 8
HD_D = (D_MIX - H_C * HD_C) // H_D
SB_BLOCK = 128
D_FF = 5632
RMS_EPS = 1e-6
N_AB = (DEPTH + 1) // 2
N_CD = DEPTH // 2
W_IN_AB = 2 * D_RG + 3 * H_B * HD_B
W_IN_CD = 4 * H_C * HD_C + 2 * H_C + 3 * H_D * HD_D

kernel_name = 'hybrid_rglru_dilated_mlstm_stickbreaking'

F32 = jnp.float32


def rms_norm(x, g):
    xf = x.astype(F32)
    y = xf * lax.rsqrt(jnp.mean(xf * xf, axis=-1, keepdims=True) + RMS_EPS)
    return (y * g.astype(F32)).astype(x.dtype)


def head_rms(x, g):
    xf = x.astype(F32)
    return xf * lax.rsqrt(jnp.mean(xf * xf, axis=-1, keepdims=True) + RMS_EPS) * g.astype(F32)


def swiglu_ffn(x, w_in, w_out):
    g, u = jnp.split(x @ w_in, 2, axis=-1)
    return (jax.nn.silu(g) * u) @ w_out


def causal_dwconv(x, w, b):
    c = x.shape[-1]
    y = lax.conv_general_dilated(x, w[:, None, :].astype(x.dtype), window_strides=(1,),
                                 padding=[(CONV_W - 1, 0)],
                                 dimension_numbers=('NWC', 'WIO', 'NWC'),
                                 feature_group_count=c)
    return y + b.astype(x.dtype)


def rg_lru(x, w_r, b_r, w_i, b_i, lam):
    bsz, s, _ = x.shape
    xf = x.astype(F32)
    xb = xf.reshape(bsz, s, RG_BLOCKS, RG_BLOCK)
    r = jax.nn.sigmoid(jnp.einsum('bsgc,gcd->bsgd', xb, w_r).reshape(bsz, s, D_RG) + b_r)
    i = jax.nn.sigmoid(jnp.einsum('bsgc,gcd->bsgd', xb, w_i).reshape(bsz, s, D_RG) + b_i)
    log_a = -RG_C * r * jax.nn.softplus(-lam.astype(F32))
    a = jnp.exp(log_a)
    u = jnp.sqrt(-jnp.expm1(2.0 * log_a)) * (i * xf)

    def combine(lhs, rhs):
        a1, b1 = lhs
        a2, b2 = rhs
        return a1 * a2, a2 * b1 + b2

    _, h = lax.associative_scan(combine, (a, u), axis=1)
    return h


def t5_causal_bucket(dist):
    max_exact = N_BUCKETS // 2
    d_f = jnp.maximum(dist, 1).astype(F32)
    large = max_exact + (jnp.log(d_f / max_exact) / math.log(MAX_DIST / max_exact)
                         * (N_BUCKETS - max_exact)).astype(jnp.int32)
    large = jnp.minimum(large, N_BUCKETS - 1)
    return jnp.where(dist < max_exact, dist, large)


def dilated_window_group(q, k, v, rel_bias, window, dilation):
    bsz, s, h, hd = q.shape
    blk = window // dilation
    sub = s // dilation
    nblk = -(-sub // blk)
    pad = nblk * blk - sub

    def to_blocks(t):
        t = t.reshape(bsz, sub, dilation, h, hd).transpose(0, 2, 3, 1, 4)
        t = jnp.pad(t, ((0, 0), (0, 0), (0, 0), (0, pad), (0, 0)))
        return t.reshape(bsz, dilation, h, nblk, blk, hd)

    def with_prev(t):
        prev = jnp.pad(t[:, :, :, :-1], ((0, 0), (0, 0), (0, 0), (1, 0), (0, 0), (0, 0)))
        return jnp.concatenate([prev, t], axis=-2)

    qb = to_blocks(q)
    kw = with_prev(to_blocks(k))
    vw = with_prev(to_blocks(v))
    qi = jnp.arange(blk)[:, None]
    ki = jnp.arange(2 * blk)[None, :]
    rel = blk + qi - ki
    band = (rel >= 0) & (rel <= blk)
    bias = rel_bias[t5_causal_bucket(jnp.clip(rel, 0, blk) * dilation)]
    bias = bias.transpose(2, 0, 1).astype(F32)
    not_before_start = (jnp.arange(nblk)[:, None, None] > 0) | (ki[None] >= blk)
    valid = band[None] & not_before_start
    logits = jnp.einsum('brhnqe,brhnke->brhnqk', qb, kw) / math.sqrt(hd) + bias[:, None]
    logits = jnp.where(valid, logits, -jnp.inf)
    m = jnp.max(logits, axis=-1, keepdims=True)
    p = jnp.exp(logits - m)
    den = jnp.sum(p, axis=-1, keepdims=True)
    o = jnp.einsum('brhnqk,brhnke->brhnqe', p, vw) / den
    lse = (m + jnp.log(den))[..., 0]
    o = o.reshape(bsz, dilation, h, nblk * blk, hd)[:, :, :, :sub]
    o = o.transpose(0, 3, 1, 2, 4).reshape(bsz, s, h, hd)
    lse = lse.reshape(bsz, dilation, h, nblk * blk)[:, :, :, :sub]
    lse = lse.transpose(0, 3, 1, 2).reshape(bsz, s, h)
    return o, lse


def dilated_window_attention(q, k, v, rel_bias):
    outs, lses = [], []
    for window, dilation in DSW_PATTERNS:
        o, l = dilated_window_group(q, k, v, rel_bias, window, dilation)
        outs.append(o)
        lses.append(l)
    wts = jax.nn.softmax(jnp.stack(lses, axis=0), axis=0)
    return jnp.einsum('gbsh,gbshe->bshe', wts, jnp.stack(outs, axis=0))


def mixer_ab(xn, w_in, conv_w, conv_b, w_r, b_r, w_i, b_i, lam, qk_g, rel_bias, w_out):
    bsz, s, _ = xn.shape
    db = H_B * HD_B
    gate, xr, q, k, v = jnp.split(xn @ w_in, [D_RG, 2 * D_RG, 2 * D_RG + db, 2 * D_RG + 2 * db], axis=-1)
    h = rg_lru(causal_dwconv(xr, conv_w, conv_b), w_r, b_r, w_i, b_i, lam)
    y_a = (jax.nn.gelu(gate.astype(F32)) * h).astype(xn.dtype)
    q = head_rms(q.reshape(bsz, s, H_B, HD_B), qk_g[0])
    k = head_rms(k.reshape(bsz, s, H_B, HD_B), qk_g[1])
    v = v.reshape(bsz, s, H_B, HD_B).astype(F32)
    y_b = dilated_window_attention(q, k, v, rel_bias).reshape(bsz, s, db).astype(xn.dtype)
    return jnp.concatenate([y_a, y_b], axis=-1) @ w_out


def mlstm_chunkwise(q, k, v, i_pre, f_pre):
    bsz, s, h, d = q.shape
    L = MLSTM_CHUNK
    nc = s // L

    def chunks(t):
        t = t.astype(F32).reshape(bsz, nc, L, h, *t.shape[3:])
        return jnp.moveaxis(t, 3, 1)

    qc = chunks(q)
    kc = chunks(k) * (d ** -0.5)
    vc = chunks(v)
    ig = chunks(i_pre)
    b = jnp.cumsum(jax.nn.log_sigmoid(chunks(f_pre)), axis=-1)
    causal = jnp.tril(jnp.ones((L, L), dtype=bool))
    log_d = jnp.where(causal, b[..., :, None] - b[..., None, :] + ig[..., None, :], -jnp.inf)
    b_last = b[..., -1]
    g = b_last[..., None] - b + ig
    g_max = jnp.max(g, axis=-1)
    wk = jnp.exp(g - g_max[..., None])[..., None] * kc
    c_loc = jnp.einsum('bhnlk,bhnlv->bhnkv', wk, vc)
    n_loc = jnp.sum(wk, axis=-2)

    def step(carry, xs):
        c_st, n_st, m_st = carry
        c_l, n_l, gm, bl = xs
        m_new = jnp.maximum(bl + m_st, gm)
        decay = jnp.exp(bl + m_st - m_new)
        fresh = jnp.exp(gm - m_new)
        c_new = decay[..., None, None] * c_st + fresh[..., None, None] * c_l
        n_new = decay[..., None] * n_st + fresh[..., None] * n_l
        return (c_new, n_new, m_new), (c_st, n_st, m_st)

    init = (jnp.zeros((bsz, h, d, d), F32), jnp.zeros((bsz, h, d), F32), jnp.zeros((bsz, h), F32))
    xs = tuple(jnp.moveaxis(t, 2, 0) for t in (c_loc, n_loc, g_max, b_last))
    _, (c_in, n_in, m_in) = lax.scan(step, init, xs)
    c_in = jnp.moveaxis(c_in, 0, 2)
    n_in = jnp.moveaxis(n_in, 0, 2)
    m_in = jnp.moveaxis(m_in, 0, 2)
    inter = b + m_in[..., None]
    m_t = jnp.maximum(inter, jnp.max(log_d, axis=-1))
    s_qk = jnp.einsum('bhnld,bhnsd->bhnls', qc, kc) * jnp.exp(log_d - m_t[..., None])
    inter_w = jnp.exp(inter - m_t)
    num = (jnp.einsum('bhnls,bhnsv->bhnlv', s_qk, vc)
           + inter_w[..., None] * jnp.einsum('bhnlk,bhnkv->bhnlv', qc, c_in))
    den = jnp.sum(s_qk, axis=-1) + inter_w * jnp.einsum('bhnlk,bhnk->bhnl', qc, n_in)
    hout = num / jnp.maximum(jnp.abs(den), jnp.exp(-m_t))[..., None]
    return jnp.moveaxis(hout, 1, 3).reshape(bsz, s, h, d)


def stick_breaking_attention(q, k, v):
    bsz, s, h, d = q.shape
    nq = s // SB_BLOCK
    qf = jnp.moveaxis(q.astype(F32), 2, 1) * (d ** -0.5)
    kf = jnp.moveaxis(k.astype(F32), 2, 1)
    vf = jnp.moveaxis(v.astype(F32), 2, 1)
    qblocks = jnp.moveaxis(qf.reshape(bsz, h, nq, SB_BLOCK, d), 2, 0)
    kpos = jnp.arange(s)

    def block(args):
        qb, n = args
        qpos = n * SB_BLOCK + jnp.arange(SB_BLOCK)
        before = kpos[None, :] < qpos[:, None]
        z = jnp.einsum('bhqd,bhkd->bhqk', qb, kf)
        log_stay = jnp.where(before, jax.nn.log_sigmoid(-z), 0.0)
        later = lax.cumsum(log_stay, axis=3, reverse=True) - log_stay
        att = jnp.where(before, jnp.exp(jax.nn.log_sigmoid(z) + later), 0.0)
        return jnp.einsum('bhqk,bhkd->bhqd', att, vf)

    out = lax.map(block, (qblocks, jnp.arange(nq)))
    out = jnp.moveaxis(out, 0, 2).reshape(bsz, h, s, d)
    return jnp.moveaxis(out, 1, 2)


def mixer_cd(xn, w_in, conv_w, conv_b, gate_b, h_gain, w_out):
    bsz, s, _ = xn.shape
    dc = H_C * HD_C
    dd = H_D * HD_D
    cuts = [2 * dc, 3 * dc, 4 * dc, 4 * dc + 2 * H_C, 4 * dc + 2 * H_C + dd, 4 * dc + 2 * H_C + 2 * dd]
    qk_c, v_c, o_c, gates, q_d, k_d, v_d = jnp.split(xn @ w_in, cuts, axis=-1)
    qk_c = jax.nn.silu(causal_dwconv(qk_c, conv_w, conv_b))
    q_c, k_c = jnp.split(qk_c, 2, axis=-1)
    gates = gates.astype(F32)
    i_pre = gates[..., :H_C] + gate_b[0]
    f_pre = gates[..., H_C:] + gate_b[1]
    h_c = mlstm_chunkwise(q_c.reshape(bsz, s, H_C, HD_C), k_c.reshape(bsz, s, H_C, HD_C),
                          v_c.reshape(bsz, s, H_C, HD_C), i_pre, f_pre)
    h_c = head_rms(h_c, h_gain.reshape(H_C, HD_C)).reshape(bsz, s, dc)
    y_c = (jax.nn.sigmoid(o_c.astype(F32)) * h_c).astype(xn.dtype)
    y_d = stick_breaking_attention(q_d.reshape(bsz, s, H_D, HD_D), k_d.reshape(bsz, s, H_D, HD_D),
                                   v_d.reshape(bsz, s, H_D, HD_D))
    y_d = y_d.reshape(bsz, s, dd).astype(xn.dtype)
    return jnp.concatenate([y_c, y_d], axis=-1) @ w_out


def setup_inputs(seed: int = 0) -> dict:
    key = jax.random.key(seed)
    ks = jax.random.split(key, 24)

    def dense(k, shape, fan_in):
        return jax.random.normal(k, shape, F32) * (fan_in ** -0.5)

    def small(k, shape, scale):
        return jax.random.normal(k, shape, F32) * scale

    x = jax.random.normal(ks[0], (BATCH, SEQ, D_MODEL), F32)
    norm_g = 1.0 + small(ks[1], (DEPTH, 3, D_MODEL), 0.05)
    ffn_w_in = dense(ks[2], (DEPTH, 2, D_MODEL, 2 * D_FF), D_MODEL)
    ffn_w_out = dense(ks[3], (DEPTH, 2, D_FF, D_MODEL), D_FF)
    ab_w_in = dense(ks[4], (N_AB, D_MODEL, W_IN_AB), D_MODEL)
    ab_conv_w = dense(ks[5], (N_AB, CONV_W, D_RG), CONV_W)
    ab_conv_b = small(ks[6], (N_AB, D_RG), 0.01)
    rg_w_r = dense(ks[7], (N_AB, RG_BLOCKS, RG_BLOCK, RG_BLOCK), RG_BLOCK)
    rg_b_r = small(ks[8], (N_AB, D_RG), 0.01)
    rg_w_i = dense(ks[9], (N_AB, RG_BLOCKS, RG_BLOCK, RG_BLOCK), RG_BLOCK)
    rg_b_i = small(ks[10], (N_AB, D_RG), 0.01)
    a_c = jax.random.uniform(ks[11], (N_AB, D_RG), F32, minval=0.9, maxval=0.999)
    a0 = a_c ** (1.0 / RG_C)
    rg_lambda = jnp.log(a0) - jnp.log1p(-a0)
    qk_gain = 1.0 + small(ks[12], (N_AB, 2, HD_B), 0.05)
    rel_bias = small(ks[13], (N_BUCKETS, H_B), 0.2)
    ab_w_out = dense(ks[14], (N_AB, D_MIX, D_MODEL), D_MIX)
    cd_w_in = dense(ks[15], (N_CD, D_MODEL, W_IN_CD), D_MODEL)
    cd_conv_w = dense(ks[16], (N_CD, CONV_W, 2 * H_C * HD_C), CONV_W)
    cd_conv_b = small(ks[17], (N_CD, 2 * H_C * HD_C), 0.01)
    ig_b = small(ks[18], (N_CD, H_C), 0.1)
    fg_b = jnp.linspace(3.0, 6.0, H_C, dtype=F32)[None, :] + small(ks[19], (N_CD, H_C), 0.1)
    mlstm_gate_bias = jnp.stack([ig_b, fg_b], axis=1)
    mlstm_h_gain = 1.0 + small(ks[20], (N_CD, H_C * HD_C), 0.05)
    cd_w_out = dense(ks[21], (N_CD, D_MIX, D_MODEL), D_MIX)
    return {'x': x, 'norm_g': norm_g, 'ffn_w_in': ffn_w_in, 'ffn_w_out': ffn_w_out,
            'ab_w_in': ab_w_in, 'ab_conv_w': ab_conv_w, 'ab_conv_b': ab_conv_b,
            'rg_w_r': rg_w_r, 'rg_b_r': rg_b_r, 'rg_w_i': rg_w_i, 'rg_b_i': rg_b_i,
            'rg_lambda': rg_lambda, 'qk_gain': qk_gain, 'rel_bias': rel_bias, 'ab_w_out': ab_w_out,
            'cd_w_in': cd_w_in, 'cd_conv_w': cd_conv_w, 'cd_conv_b': cd_conv_b,
            'mlstm_gate_bias': mlstm_gate_bias, 'mlstm_h_gain': mlstm_h_gain, 'cd_w_out': cd_w_out}


def reference(x, norm_g, ffn_w_in, ffn_w_out, ab_w_in, ab_conv_w, ab_conv_b, rg_w_r, rg_b_r,
              rg_w_i, rg_b_i, rg_lambda, qk_gain, rel_bias, ab_w_out, cd_w_in, cd_conv_w,
              cd_conv_b, mlstm_gate_bias, mlstm_h_gain, cd_w_out):
    for layer in range(DEPTH):
        j = layer // 2
        x = x + 0.5 * swiglu_ffn(rms_norm(x, norm_g[layer, 0]), ffn_w_in[layer, 0], ffn_w_out[layer, 0])
        xn = rms_norm(x, norm_g[layer, 1])
        if layer % 2 == 0:
            y = mixer_ab(xn, ab_w_in[j], ab_conv_w[j], ab_conv_b[j], rg_w_r[j], rg_b_r[j],
                         rg_w_i[j], rg_b_i[j], rg_lambda[j], qk_gain[j], rel_bias, ab_w_out[j])
        else:
            y = mixer_cd(xn, cd_w_in[j], cd_conv_w[j], cd_conv_b[j], mlstm_gate_bias[j],
                         mlstm_h_gain[j], cd_w_out[j])
        x = x + y.astype(x.dtype)
        x = x + 0.5 * swiglu_ffn(rms_norm(x, norm_g[layer, 2]), ffn_w_in[layer, 1], ffn_w_out[layer, 1])
    return x
```

```python
import functools
import math

import numpy as np
import jax
import jax.numpy as jnp
from jax import lax
from jax.experimental import pallas as pl
from jax.experimental.pallas import tpu as pltpu

F32 = jnp.float32
BF16 = jnp.bfloat16

RMS_EPS = 1e-6
RG_C = 8.0
CONV_W = 4
RG_BLOCK = 128
HEAD_B = 128
HEAD_C = 256
HEAD_D = 128
SEQ_BLOCK = 128
DSW_PATTERNS = ((128, 1), (512, 4), (2048, 16))
N_BUCKETS = 32
MAX_DIST = 2048
NEG = -1e30
VMEM_LIMIT = 56 * 1024 * 1024


def _cparams(sem):
    return pltpu.CompilerParams(dimension_semantics=sem, vmem_limit_bytes=VMEM_LIMIT)


def _rms_rows(x, g):
    return x * lax.rsqrt(jnp.mean(x * x, axis=-1, keepdims=True) + RMS_EPS) * g


def _dot(a, b):
    return jnp.dot(a, b, preferred_element_type=F32)


def _dot_nt(a, b):
    return lax.dot_general(a, b, (((1,), (1,)), ((), ())), preferred_element_type=F32)


def _dot_tn(a, b):
    return lax.dot_general(a, b, (((0,), (0,)), ((), ())), preferred_element_type=F32)


def _pick_tile(n, pref):
    t = min(pref, n)
    while n % t:
        t //= 2
    return t


def _ffn_kernel(x_ref, g_ref, wg_ref, wu_ref, wo_ref, o_ref, xn_ref):
    j = pl.program_id(1)

    @pl.when(j == 0)
    def _():
        xn_ref[...] = _rms_rows(x_ref[...], g_ref[...]).astype(BF16)
        o_ref[...] = jnp.zeros_like(o_ref)

    xn = xn_ref[...]
    hg = _dot(xn, wg_ref[...])
    hu = _dot(xn, wu_ref[...])
    h = (hg * jax.nn.sigmoid(hg) * hu).astype(BF16)
    o_ref[...] += _dot(h, wo_ref[...])

    @pl.when(j == pl.num_programs(1) - 1)
    def _():
        o_ref[...] = x_ref[...] + 0.5 * o_ref[...]


def _ffn(x, g, w_in, w_out, tm=512, tf=512):
    n, d = x.shape
    d_ff = w_out.shape[0]
    tm = _pick_tile(n, tm)
    tf = _pick_tile(d_ff, tf)
    nj = d_ff // tf
    return pl.pallas_call(
        _ffn_kernel,
        out_shape=jax.ShapeDtypeStruct((n, d), F32),
        grid=(n // tm, nj),
        in_specs=[
            pl.BlockSpec((tm, d), lambda i, j: (i, 0)),
            pl.BlockSpec((1, d), lambda i, j: (0, 0)),
            pl.BlockSpec((d, tf), lambda i, j: (0, j)),
            pl.BlockSpec((d, tf), lambda i, j: (0, j + nj)),
            pl.BlockSpec((tf, d), lambda i, j: (j, 0)),
        ],
        out_specs=pl.BlockSpec((tm, d), lambda i, j: (i, 0)),
        scratch_shapes=[pltpu.VMEM((tm, d), BF16)],
        compiler_params=_cparams(("parallel", "arbitrary")),
        name="ffn",
    )(x, g.reshape(1, d), w_in, w_in, w_out)


def _norm_matmul_kernel(x_ref, g_ref, w_ref, o_ref, xn_ref):
    @pl.when(pl.program_id(1) == 0)
    def _():
        xn_ref[...] = _rms_rows(x_ref[...], g_ref[...]).astype(BF16)

    o_ref[...] = _dot(xn_ref[...], w_ref[...])


def _norm_matmul(x, g, w, tm=1024, tn=1024):
    n, d = x.shape
    wn = w.shape[1]
    tm = _pick_tile(n, tm)
    while wn % tn:
        tn -= 128
    return pl.pallas_call(
        _norm_matmul_kernel,
        out_shape=jax.ShapeDtypeStruct((n, wn), F32),
        grid=(n // tm, wn // tn),
        in_specs=[
            pl.BlockSpec((tm, d), lambda i, j: (i, 0)),
            pl.BlockSpec((1, d), lambda i, j: (0, 0)),
            pl.BlockSpec((d, tn), lambda i, j: (0, j)),
        ],
        out_specs=pl.BlockSpec((tm, tn), lambda i, j: (i, j)),
        scratch_shapes=[pltpu.VMEM((tm, d), BF16)],
        compiler_params=_cparams(("parallel", "arbitrary")),
        name="norm_matmul",
    )(x, g.reshape(1, d), w)


def _out_proj_kernel(x_ref, ya_ref, yb_ref, wa_ref, wb_ref, o_ref):
    o_ref[...] = x_ref[...] + _dot(ya_ref[...], wa_ref[...]) + _dot(yb_ref[...], wb_ref[...])


def _out_proj(x, ya, yb, w, tm=512):
    n, d = x.shape
    da = ya.shape[1]
    db = yb.shape[1]
    tm = _pick_tile(n, tm)
    return pl.pallas_call(
        _out_proj_kernel,
        out_shape=jax.ShapeDtypeStruct((n, d), F32),
        grid=(n // tm,),
        in_specs=[
            pl.BlockSpec((tm, d), lambda i: (i, 0)),
            pl.BlockSpec((tm, da), lambda i: (i, 0)),
            pl.BlockSpec((tm, db), lambda i: (i, 0)),
            pl.BlockSpec((da, d), lambda i: (0, 0)),
            pl.BlockSpec((db, d), lambda i: (0, 0)),
        ],
        out_specs=pl.BlockSpec((tm, d), lambda i: (i, 0)),
        compiler_params=_cparams(("parallel",)),
        name="out_proj",
    )(x, ya, yb, w[:da], w[da:])


def _causal_conv(x, w, b):
    t = lax.broadcasted_iota(jnp.int32, x.shape, 0)
    y = x * w[CONV_W - 1:CONV_W] + b
    for k in range(1, CONV_W):
        xs = jnp.where(t >= k, pltpu.roll(x, k, axis=0), 0.0)
        y = y + xs * w[CONV_W - 1 - k:CONV_W - k]
    return y


def _mixer_a_kernel(xr_ref, gate_ref, cw_ref, cb_ref, wr_ref, br_ref, wi_ref, bi_ref,
                    lam_ref, o_ref):
    s = xr_ref.shape[0]
    xc = _causal_conv(xr_ref[...], cw_ref[...], cb_ref[...])
    xb = xc.astype(BF16)
    r = jax.nn.sigmoid(_dot(xb, wr_ref[...]) + br_ref[...])
    i = jax.nn.sigmoid(_dot(xb, wi_ref[...]) + bi_ref[...])
    neg_lam = -lam_ref[...]
    softplus = jnp.maximum(neg_lam, 0.0) + jnp.log1p(jnp.exp(-jnp.abs(neg_lam)))
    log_a = -RG_C * r * softplus
    a = jnp.exp(log_a)
    u = jnp.sqrt(-jnp.tanh(log_a) * (a * a + 1.0)) * (i * xc)
    t = lax.broadcasted_iota(jnp.int32, a.shape, 0)
    k = 1
    while k < s:
        keep = t >= k
        a_sh = jnp.where(keep, pltpu.roll(a, k, axis=0), 1.0)
        u_sh = jnp.where(keep, pltpu.roll(u, k, axis=0), 0.0)
        u = a * u_sh + u
        a = a * a_sh
        k *= 2
    o_ref[...] = (jax.nn.gelu(gate_ref[...]) * u).astype(o_ref.dtype)


def _mixer_a(proj, conv_w, conv_b, w_r, b_r, w_i, b_i, lam, d_rg):
    bsz, s, _ = proj.shape
    nb = d_rg // RG_BLOCK
    vec = lambda v: v.reshape(1, d_rg)
    col = lambda off: pl.BlockSpec((None, s, RG_BLOCK), lambda b, g: (b, 0, off + g))
    row = lambda r: pl.BlockSpec((r, RG_BLOCK), lambda b, g: (0, g))
    blk = pl.BlockSpec((None, RG_BLOCK, RG_BLOCK), lambda b, g: (g, 0, 0))
    return pl.pallas_call(
        _mixer_a_kernel,
        out_shape=jax.ShapeDtypeStruct((bsz, s, d_rg), BF16),
        grid=(bsz, nb),
        in_specs=[col(nb), col(0), row(CONV_W), row(1), blk, row(1), blk, row(1), row(1)],
        out_specs=pl.BlockSpec((None, s, RG_BLOCK), lambda b, g: (b, 0, g)),
        compiler_params=_cparams(("parallel", "parallel")),
        name="mixer_a",
    )(proj, proj, conv_w, vec(conv_b), w_r.astype(BF16), vec(b_r), w_i.astype(BF16), vec(b_i),
      vec(lam))


def _t5_bucket_np(dist):
    max_exact = N_BUCKETS // 2
    d_f = np.maximum(dist, 1).astype(np.float32)
    large = max_exact + (np.log(d_f / np.float32(max_exact)) / np.float32(math.log(MAX_DIST / max_exact))
                         * np.float32(N_BUCKETS - max_exact)).astype(np.int32)
    large = np.minimum(large, N_BUCKETS - 1)
    return np.where(dist < max_exact, dist, large).astype(np.int32)


def _dsw_tables(seq):
    nblk = seq // SEQ_BLOCK
    r = np.arange(SEQ_BLOCK)[:, None]
    c = np.arange(SEQ_BLOCK)[None, :]
    dist = np.arange(nblk)[:, None, None] * SEQ_BLOCK + (r - c)[None]
    mult = np.zeros(dist.shape, np.float32)
    for window, dilation in DSW_PATTERNS:
        mult += ((dist >= 0) & (dist % dilation == 0) & (dist <= window)).astype(np.float32)
    logm = np.where(mult > 0, np.log(np.maximum(mult, 1.0)), NEG).astype(np.float32)
    return _t5_bucket_np(np.clip(dist, 0, None)), logm


def _mixer_b_kernel(q_ref, k_ref, v_ref, g_ref, tbl_ref, o_ref, kn_ref):
    qi = pl.program_id(2)
    hd = q_ref.shape[-1]

    @pl.when(qi == 0)
    def _():
        kn_ref[...] = _rms_rows(k_ref[...], g_ref[1:2]).astype(BF16)

    qn = (_rms_rows(q_ref[...], g_ref[0:1]) * (1.0 / math.sqrt(hd))).astype(BF16)

    def body(t, carry):
        m, l, acc = carry
        start = pl.multiple_of((qi - t) * SEQ_BLOCK, SEQ_BLOCK)
        kj = kn_ref[pl.ds(start, SEQ_BLOCK), :]
        vj = v_ref[pl.ds(start, SEQ_BLOCK), :].astype(BF16)
        sc = _dot_nt(qn, kj) + tbl_ref[t]
        m_new = jnp.maximum(m, jnp.max(sc, axis=-1, keepdims=True))
        p = jnp.exp(sc - m_new)
        alpha = jnp.exp(m - m_new)
        l = alpha * l + jnp.sum(p, axis=-1, keepdims=True)
        acc = alpha * acc + _dot(p.astype(BF16), vj)
        return m_new, l, acc

    init = (jnp.full((SEQ_BLOCK, 1), NEG, F32), jnp.zeros((SEQ_BLOCK, 1), F32),
            jnp.zeros((SEQ_BLOCK, hd), F32))
    _, l, acc = lax.fori_loop(0, qi + 1, body, init)
    o_ref[...] = (acc / l).astype(o_ref.dtype)


def _mixer_b(proj, qk_g, rel_bias, col0, n_heads):
    bsz, s, _ = proj.shape
    nq = s // SEQ_BLOCK
    bucket, logm = _dsw_tables(s)
    tbl = jnp.transpose(rel_bias.astype(F32)[bucket], (3, 0, 1, 2)) + logm[None]
    c0 = col0 // HEAD_B
    return pl.pallas_call(
        _mixer_b_kernel,
        out_shape=jax.ShapeDtypeStruct((bsz, s, n_heads * HEAD_B), BF16),
        grid=(n_heads, bsz, nq),
        in_specs=[
            pl.BlockSpec((None, SEQ_BLOCK, HEAD_B), lambda h, b, i: (b, i, c0 + h)),
            pl.BlockSpec((None, s, HEAD_B), lambda h, b, i: (b, 0, c0 + n_heads + h)),
            pl.BlockSpec((None, s, HEAD_B), lambda h, b, i: (b, 0, c0 + 2 * n_heads + h)),
            pl.BlockSpec((2, HEAD_B), lambda h, b, i: (0, 0)),
            pl.BlockSpec((None, nq, SEQ_BLOCK, SEQ_BLOCK), lambda h, b, i: (h, 0, 0, 0)),
        ],
        out_specs=pl.BlockSpec((None, SEQ_BLOCK, HEAD_B), lambda h, b, i: (b, i, h)),
        scratch_shapes=[pltpu.VMEM((s, HEAD_B), BF16)],
        compiler_params=_cparams(("parallel", "parallel", "arbitrary")),
        name="mixer_b",
    )(proj, proj, proj, qk_g, tbl)


def _mixer_c_kernel(gb_ref, q_ref, k_ref, v_ref, og_ref, cwq_ref, cwk_ref, cbq_ref, cbk_ref,
                    ig_ref, fg_ref, gain_ref, o_ref, qc_ref, kc_ref, c_ref, n_ref, m_ref):
    h = pl.program_id(1)
    n = pl.program_id(2)
    L = SEQ_BLOCK
    d = q_ref.shape[-1]

    @pl.when(n == 0)
    def _():
        qc_ref[...] = jax.nn.silu(_causal_conv(q_ref[...], cwq_ref[...], cbq_ref[...]))
        kc_ref[...] = jax.nn.silu(_causal_conv(k_ref[...], cwk_ref[...], cbk_ref[...])) * (d ** -0.5)
        c_ref[...] = jnp.zeros_like(c_ref)
        n_ref[...] = jnp.zeros_like(n_ref)
        m_ref[...] = jnp.zeros_like(m_ref)

    start = pl.multiple_of(n * L, L)
    qc = qc_ref[pl.ds(start, L), :]
    kc = kc_ref[pl.ds(start, L), :]
    vc = v_ref[...]
    qb = qc.astype(BF16)
    kb = kc.astype(BF16)
    vb = vc.astype(BF16)

    ig_row = ig_ref[pl.ds(n, 1), :] + gb_ref[0, h]
    f_row = fg_ref[pl.ds(n, 1), :] + gb_ref[1, h]
    ls_row = jnp.minimum(f_row, 0.0) - jnp.log1p(jnp.exp(-jnp.abs(f_row)))

    ri = lax.broadcasted_iota(jnp.int32, (L, L), 0)
    ci = lax.broadcasted_iota(jnp.int32, (L, L), 1)
    eye = ri == ci
    causal = ci <= ri

    def to_col(row):
        return jnp.sum(jnp.where(eye, row, 0.0), axis=1, keepdims=True)

    ls_col = to_col(ls_row)
    ig_col = to_col(ig_row)
    b_col = jnp.sum(jnp.where(causal, ls_row, 0.0), axis=1, keepdims=True)
    b_row = jnp.sum(jnp.where(ri <= ci, ls_col, 0.0), axis=0, keepdims=True)
    b_last = jnp.sum(ls_row, axis=1, keepdims=True)

    log_d = b_col - b_row + ig_row
    g_row = b_last - b_row + ig_row
    g_col = b_last - b_col + ig_col
    g_max = jnp.max(g_row, axis=1, keepdims=True)

    c_st = c_ref[...]
    n_st = n_ref[...]
    m_st = m_ref[...]

    inter = b_col + m_st
    m_t = jnp.maximum(inter, jnp.max(jnp.where(causal, log_d, -jnp.inf), axis=1, keepdims=True))
    s_qk = jnp.where(causal, _dot_nt(qb, kb) * jnp.exp(log_d - m_t), 0.0)
    inter_w = jnp.exp(inter - m_t)
    num = _dot(s_qk.astype(BF16), vb) + inter_w * _dot(qb, c_st.astype(BF16))
    qn = jnp.sum(qc * n_st, axis=1, keepdims=True)
    den = jnp.sum(s_qk, axis=1, keepdims=True) + inter_w * qn
    hout = num / jnp.maximum(jnp.abs(den), jnp.exp(-m_t))

    hn = _rms_rows(hout, gain_ref[...])
    o_ref[...] = (jax.nn.sigmoid(og_ref[...]) * hn).astype(o_ref.dtype)

    wk = jnp.exp(g_col - g_max) * kc
    c_loc = _dot_tn(wk.astype(BF16), vb)
    n_loc = jnp.sum(wk, axis=0, keepdims=True)
    m_new = jnp.maximum(b_last + m_st, g_max)
    decay = jnp.exp(b_last + m_st - m_new)
    fresh = jnp.exp(g_max - m_new)
    c_ref[...] = decay * c_st + fresh * c_loc
    n_ref[...] = decay * n_st + fresh * n_loc
    m_ref[...] = m_new


def _mixer_c(proj, gates, conv_w, conv_b, gate_b, h_gain, n_heads):
    bsz, s, _ = proj.shape
    d = HEAD_C
    L = SEQ_BLOCK
    nc = s // L
    g_rows = jnp.transpose(gates, (0, 2, 1)).reshape(bsz, 2 * n_heads, nc, L)
    cb = conv_b.reshape(1, -1)
    full = lambda off: pl.BlockSpec((None, s, d), lambda b, h, n: (b, 0, off + h))
    chunk = lambda off: pl.BlockSpec((None, L, d), lambda b, h, n: (b, n, off + h))
    row = lambda r, off: pl.BlockSpec((r, d), lambda b, h, n: (0, off + h))
    grow = lambda off: pl.BlockSpec((None, None, nc, L), lambda b, h, n: (b, off + h, 0, 0))
    return pl.pallas_call(
        _mixer_c_kernel,
        out_shape=jax.ShapeDtypeStruct((bsz, s, n_heads * d), BF16),
        grid=(bsz, n_heads, nc),
        in_specs=[
            pl.BlockSpec(memory_space=pltpu.SMEM),
            full(0), full(n_heads), chunk(2 * n_heads), chunk(3 * n_heads),
            row(CONV_W, 0), row(CONV_W, n_heads), row(1, 0), row(1, n_heads),
            grow(0), grow(n_heads), row(1, 0),
        ],
        out_specs=pl.BlockSpec((None, L, d), lambda b, h, n: (b, n, h)),
        scratch_shapes=[pltpu.VMEM((s, d), F32), pltpu.VMEM((s, d), F32),
                        pltpu.VMEM((d, d), F32), pltpu.VMEM((1, d), F32), pltpu.VMEM((1, 1), F32)],
        compiler_params=_cparams(("parallel", "parallel", "arbitrary")),
        name="mixer_c",
    )(gate_b, proj, proj, proj, proj, conv_w, conv_w, cb, cb, g_rows, g_rows,
      h_gain.reshape(1, -1))


def _mixer_d_kernel(q_ref, k_ref, v_ref, o_ref):
    qi = pl.program_id(2)
    hd = q_ref.shape[-1]
    B = SEQ_BLOCK
    qb = (q_ref[...] * (hd ** -0.5)).astype(BF16)
    ri = lax.broadcasted_iota(jnp.int32, (B, B), 0)
    ci = lax.broadcasted_iota(jnp.int32, (B, B), 1)
    after = (ri > ci).astype(BF16)

    def body(t, carry):
        suffix, acc = carry
        start = pl.multiple_of((qi - t) * B, B)
        kj = k_ref[pl.ds(start, B), :].astype(BF16)
        vj = v_ref[pl.ds(start, B), :].astype(BF16)
        z = _dot_nt(qb, kj)
        before = (ci < ri) | (t > 0)
        sp = jnp.log1p(jnp.exp(-jnp.abs(z)))
        log_beta = jnp.minimum(z, 0.0) - sp
        log_stay = jnp.where(before, -jnp.maximum(z, 0.0) - sp, 0.0)
        hi = log_stay.astype(BF16)
        lo = (log_stay - hi.astype(F32)).astype(BF16)
        later = _dot(hi, after) + _dot(lo, after) + suffix
        att = jnp.where(before, jnp.exp(log_beta + later), 0.0)
        acc = acc + _dot(att.astype(BF16), vj)
        suffix = suffix + jnp.sum(log_stay, axis=-1, keepdims=True)
        return suffix, acc

    init = (jnp.zeros((B, 1), F32), jnp.zeros((B, hd), F32))
    _, acc = lax.fori_loop(0, qi + 1, body, init)
    o_ref[...] = acc.astype(o_ref.dtype)


def _mixer_d(proj, col0, n_heads):
    bsz, s, _ = proj.shape
    nq = s // SEQ_BLOCK
    c0 = col0 // HEAD_D
    return pl.pallas_call(
        _mixer_d_kernel,
        out_shape=jax.ShapeDtypeStruct((bsz, s, n_heads * HEAD_D), BF16),
        grid=(bsz, n_heads, nq),
        in_specs=[
            pl.BlockSpec((None, SEQ_BLOCK, HEAD_D), lambda b, h, i: (b, i, c0 + h)),
            pl.BlockSpec((None, s, HEAD_D), lambda b, h, i: (b, 0, c0 + n_heads + h)),
            pl.BlockSpec((None, s, HEAD_D), lambda b, h, i: (b, 0, c0 + 2 * n_heads + h)),
        ],
        out_specs=pl.BlockSpec((None, SEQ_BLOCK, HEAD_D), lambda b, h, i: (b, i, h)),
        compiler_params=_cparams(("parallel", "parallel", "arbitrary")),
        name="mixer_d",
    )(proj, proj, proj)


def _layer_ab(x, bsz, s, g, w_in, conv_w, conv_b, w_r, b_r, w_i, b_i, lam, qk_g, rel_bias, w_out):
    d_rg = conv_w.shape[-1]
    n_heads = rel_bias.shape[-1]
    proj = _norm_matmul(x, g, w_in.astype(BF16)).reshape(bsz, s, -1)
    ya = _mixer_a(proj, conv_w, conv_b, w_r, b_r, w_i, b_i, lam, d_rg)
    yb = _mixer_b(proj, qk_g, rel_bias, 2 * d_rg, n_heads)
    n = bsz * s
    return _out_proj(x, ya.reshape(n, -1), yb.reshape(n, -1), w_out.astype(BF16))


def _layer_cd(x, bsz, s, g, w_in, conv_w, conv_b, gate_b, h_gain, w_out):
    n_heads_c = gate_b.shape[-1]
    dc = n_heads_c * HEAD_C
    d_mix = w_out.shape[0]
    dd = d_mix - dc
    n_heads_d = dd // HEAD_D
    n_gate = 2 * n_heads_c
    main = 4 * dc + 3 * dd
    width = -(-(main + n_gate) // 512) * 512
    w = jnp.concatenate([w_in[:, :4 * dc], w_in[:, 4 * dc + n_gate:], w_in[:, 4 * dc:4 * dc + n_gate],
                         jnp.zeros((w_in.shape[0], width - main - n_gate), w_in.dtype)], axis=1)
    proj = _norm_matmul(x, g, w.astype(BF16), tn=768).reshape(bsz, s, width)
    gates = proj[:, :, main:main + n_gate]
    yc = _mixer_c(proj, gates, conv_w, conv_b, gate_b, h_gain, n_heads_c)
    yd = _mixer_d(proj, 4 * dc, n_heads_d)
    n = bsz * s
    return _out_proj(x, yc.reshape(n, -1), yd.reshape(n, -1), w_out.astype(BF16))


def kernel(x, norm_g, ffn_w_in, ffn_w_out, ab_w_in, ab_conv_w, ab_conv_b, rg_w_r, rg_b_r, rg_w_i,
           rg_b_i, rg_lambda, qk_gain, rel_bias, ab_w_out, cd_w_in, cd_conv_w, cd_conv_b,
           mlstm_gate_bias, mlstm_h_gain, cd_w_out):
    bsz, s, d = x.shape
    depth = norm_g.shape[0]
    h = x.reshape(bsz * s, d)
    for layer in range(depth):
        j = layer // 2
        h = _ffn(h, norm_g[layer, 0], ffn_w_in[layer, 0].astype(BF16), ffn_w_out[layer, 0].astype(BF16))
        if layer % 2 == 0:
            h = _layer_ab(h, bsz, s, norm_g[layer, 1], ab_w_in[j], ab_conv_w[j], ab_conv_b[j],
                          rg_w_r[j], rg_b_r[j], rg_w_i[j], rg_b_i[j], rg_lambda[j], qk_gain[j],
                          rel_bias, ab_w_out[j])
        else:
            h = _layer_cd(h, bsz, s, norm_g[layer, 1], cd_w_in[j], cd_conv_w[j], cd_conv_b[j],
                          mlstm_gate_bias[j], mlstm_h_gain[j], cd_w_out[j])
        h = _ffn(h, norm_g[layer, 2], ffn_w_in[layer, 1].astype(BF16), ffn_w_out[layer, 1].astype(BF16))
    return h.reshape(bsz, s, d)
```

```python
import functools
import math

import numpy as np
import jax
import jax.numpy as jnp
from jax import lax
from jax.experimental import pallas as pl
from jax.experimental.pallas import tpu as pltpu

F32 = jnp.float32
BF16 = jnp.bfloat16

RMS_EPS = 1e-6
RG_C = 8.0
CONV_W = 4
RG_BLOCK = 128
HEAD_B = 128
HEAD_C = 256
HEAD_D = 128
SEQ_BLOCK = 128
DSW_PATTERNS = ((128, 1), (512, 4), (2048, 16))
N_BUCKETS = 32
MAX_DIST = 2048
NEG = -1e30
VMEM_LIMIT = 56 * 1024 * 1024


def _cparams(sem):
    return pltpu.CompilerParams(dimension_semantics=sem, vmem_limit_bytes=VMEM_LIMIT)


def _rms_rows(x, g):
    return x * lax.rsqrt(jnp.mean(x * x, axis=-1, keepdims=True) + RMS_EPS) * g


def _dot(a, b):
    return jnp.dot(a, b, preferred_element_type=F32)


def _dot_nt(a, b):
    return lax.dot_general(a, b, (((1,), (1,)), ((), ())), preferred_element_type=F32)


def _dot_tn(a, b):
    return lax.dot_general(a, b, (((0,), (0,)), ((), ())), preferred_element_type=F32)


def _pick_tile(n, pref):
    t = min(pref, n)
    while n % t:
        t //= 2
    return t


def _ffn_kernel(x_ref, g_ref, wg_ref, wu_ref, wo_ref, o_ref, xn_ref):
    j = pl.program_id(1)

    @pl.when(j == 0)
    def _():
        xn_ref[...] = _rms_rows(x_ref[...], g_ref[...]).astype(BF16)
        o_ref[...] = jnp.zeros_like(o_ref)

    xn = xn_ref[...]
    hg = _dot(xn, wg_ref[...])
    hu = _dot(xn, wu_ref[...])
    h = (hg * jax.nn.sigmoid(hg) * hu).astype(BF16)
    o_ref[...] += _dot(h, wo_ref[...])

    @pl.when(j == pl.num_programs(1) - 1)
    def _():
        o_ref[...] = x_ref[...] + 0.5 * o_ref[...]


def _ffn(x, g, w_in, w_out, tm=512, tf=512):
    n, d = x.shape
    d_ff = w_out.shape[0]
    tm = _pick_tile(n, tm)
    tf = _pick_tile(d_ff, tf)
    nj = d_ff // tf
    return pl.pallas_call(
        _ffn_kernel,
        out_shape=jax.ShapeDtypeStruct((n, d), F32),
        grid=(n // tm, nj),
        in_specs=[
            pl.BlockSpec((tm, d), lambda i, j: (i, 0)),
            pl.BlockSpec((1, d), lambda i, j: (0, 0)),
            pl.BlockSpec((d, tf), lambda i, j: (0, j)),
            pl.BlockSpec((d, tf), lambda i, j: (0, j + nj)),
            pl.BlockSpec((tf, d), lambda i, j: (j, 0)),
        ],
        out_specs=pl.BlockSpec((tm, d), lambda i, j: (i, 0)),
        scratch_shapes=[pltpu.VMEM((tm, d), BF16)],
        compiler_params=_cparams(("parallel", "arbitrary")),
        name="ffn",
    )(x, g.reshape(1, d), w_in, w_in, w_out)


def _norm_matmul_kernel(x_ref, g_ref, w_ref, o_ref, xn_ref):
    @pl.when(pl.program_id(1) == 0)
    def _():
        xn_ref[...] = _rms_rows(x_ref[...], g_ref[...]).astype(BF16)

    o_ref[...] = _dot(xn_ref[...], w_ref[...])


def _norm_matmul(x, g, w, tm=1024, tn=1024):
    n, d = x.shape
    wn = w.shape[1]
    tm = _pick_tile(n, tm)
    while wn % tn:
        tn -= 128
    return pl.pallas_call(
        _norm_matmul_kernel,
        out_shape=jax.ShapeDtypeStruct((n, wn), F32),
        grid=(n // tm, wn // tn),
        in_specs=[
            pl.BlockSpec((tm, d), lambda i, j: (i, 0)),
            pl.BlockSpec((1, d), lambda i, j: (0, 0)),
            pl.BlockSpec((d, tn), lambda i, j: (0, j)),
        ],
        out_specs=pl.BlockSpec((tm, tn), lambda i, j: (i, j)),
        scratch_shapes=[pltpu.VMEM((tm, d), BF16)],
        compiler_params=_cparams(("parallel", "arbitrary")),
        name="norm_matmul",
    )(x, g.reshape(1, d), w)


def _out_proj_kernel(x_ref, ya_ref, yb_ref, wa_ref, wb_ref, o_ref):
    o_ref[...] = x_ref[...] + _dot(ya_ref[...], wa_ref[...]) + _dot(yb_ref[...], wb_ref[...])


def _out_proj(x, ya, yb, w, tm=512):
    n, d = x.shape
    da = ya.shape[1]
    db = yb.shape[1]
    tm = _pick_tile(n, tm)
    return pl.pallas_call(
        _out_proj_kernel,
        out_shape=jax.ShapeDtypeStruct((n, d), F32),
        grid=(n // tm,),
        in_specs=[
            pl.BlockSpec((tm, d), lambda i: (i, 0)),
            pl.BlockSpec((tm, da), lambda i: (i, 0)),
            pl.BlockSpec((tm, db), lambda i: (i, 0)),
            pl.BlockSpec((da, d), lambda i: (0, 0)),
            pl.BlockSpec((db, d), lambda i: (0, 0)),
        ],
        out_specs=pl.BlockSpec((tm, d), lambda i: (i, 0)),
        compiler_params=_cparams(("parallel",)),
        name="out_proj",
    )(x, ya, yb, w[:da], w[da:])


def _causal_conv(x, w, b):
    t = lax.broadcasted_iota(jnp.int32, x.shape, 0)
    y = x * w[CONV_W - 1:CONV_W] + b
    for k in range(1, CONV_W):
        xs = jnp.where(t >= k, pltpu.roll(x, k, axis=0), 0.0)
        y = y + xs * w[CONV_W - 1 - k:CONV_W - k]
    return y


def _mixer_a_kernel(xr_ref, gate_ref, cw_ref, cb_ref, wr_ref, br_ref, wi_ref, bi_ref,
                    lam_ref, o_ref):
    s = xr_ref.shape[0]
    xc = _causal_conv(xr_ref[...], cw_ref[...], cb_ref[...])
    xb = xc.astype(BF16)
    r = jax.nn.sigmoid(_dot(xb, wr_ref[...]) + br_ref[...])
    i = jax.nn.sigmoid(_dot(xb, wi_ref[...]) + bi_ref[...])
    neg_lam = -lam_ref[...]
    softplus = jnp.maximum(neg_lam, 0.0) + jnp.log1p(jnp.exp(-jnp.abs(neg_lam)))
    log_a = -RG_C * r * softplus
    a = jnp.exp(log_a)
    u = jnp.sqrt(-jnp.tanh(log_a) * (a * a + 1.0)) * (i * xc)
    t = lax.broadcasted_iota(jnp.int32, a.shape, 0)
    k = 1
    while k < s:
        keep = t >= k
        a_sh = jnp.where(keep, pltpu.roll(a, k, axis=0), 1.0)
        u_sh = jnp.where(keep, pltpu.roll(u, k, axis=0), 0.0)
        u = a * u_sh + u
        a = a * a_sh
        k *= 2
    o_ref[...] = (jax.nn.gelu(gate_ref[...]) * u).astype(o_ref.dtype)


def _mixer_a(proj, conv_w, conv_b, w_r, b_r, w_i, b_i, lam, d_rg):
    bsz, s, _ = proj.shape
    nb = d_rg // RG_BLOCK
    vec = lambda v: v.reshape(1, d_rg)
    col = lambda off: pl.BlockSpec((None, s, RG_BLOCK), lambda b, g: (b, 0, off + g))
    row = lambda r: pl.BlockSpec((r, RG_BLOCK), lambda b, g: (0, g))
    blk = pl.BlockSpec((None, RG_BLOCK, RG_BLOCK), lambda b, g: (g, 0, 0))
    return pl.pallas_call(
        _mixer_a_kernel,
        out_shape=jax.ShapeDtypeStruct((bsz, s, d_rg), BF16),
        grid=(bsz, nb),
        in_specs=[col(nb), col(0), row(CONV_W), row(1), blk, row(1), blk, row(1), row(1)],
        out_specs=pl.BlockSpec((None, s, RG_BLOCK), lambda b, g: (b, 0, g)),
        compiler_params=_cparams(("parallel", "parallel")),
        name="mixer_a",
    )(proj, proj, conv_w, vec(conv_b), w_r.astype(BF16), vec(b_r), w_i.astype(BF16), vec(b_i),
      vec(lam))


def _t5_bucket_np(dist):
    max_exact = N_BUCKETS // 2
    d_f = np.maximum(dist, 1).astype(np.float32)
    large = max_exact + (np.log(d_f / np.float32(max_exact)) / np.float32(math.log(MAX_DIST / max_exact))
                         * np.float32(N_BUCKETS - max_exact)).astype(np.int32)
    large = np.minimum(large, N_BUCKETS - 1)
    return np.where(dist < max_exact, dist, large).astype(np.int32)


def _dsw_bias_rows(rel_bias, seq):
    dist = np.arange(seq)
    mult = np.zeros(seq, np.float32)
    for window, dilation in DSW_PATTERNS:
        mult += ((dist % dilation == 0) & (dist <= window)).astype(np.float32)
    logm = np.where(mult > 0, np.log(np.maximum(mult, 1.0)), NEG).astype(np.float32)
    f = rel_bias.astype(F32)[_t5_bucket_np(dist)] + logm[:, None]
    n_heads = f.shape[1]
    p = seq + SEQ_BLOCK
    hv = jnp.concatenate([f[::-1].T, jnp.full((n_heads, SEQ_BLOCK), NEG, F32)], axis=1)
    t = jnp.broadcast_to(hv[:, None, :], (n_heads, SEQ_BLOCK, p)).reshape(n_heads, SEQ_BLOCK * p)
    t = t[:, :SEQ_BLOCK * (p - 1)].reshape(n_heads, SEQ_BLOCK, p - 1)
    return t[:, :, SEQ_BLOCK - 1:SEQ_BLOCK - 1 + seq]


def _mixer_b_kernel(q_ref, k_ref, v_ref, g_ref, tbl_ref, o_ref, kn_ref, vb_ref):
    s, hd = q_ref.shape
    B = SEQ_BLOCK
    nq = s // B
    kn_ref[...] = _rms_rows(k_ref[...], g_ref[1:2]).astype(BF16)
    vb_ref[...] = v_ref[...].astype(BF16)
    for qi in range(nq):
        n_keys = (qi + 1) * B
        rows = slice(qi * B, n_keys)
        qn = (_rms_rows(q_ref[rows, :], g_ref[0:1]) * (1.0 / math.sqrt(hd))).astype(BF16)
        sc = _dot_nt(qn, kn_ref[0:n_keys, :]) + tbl_ref[:, s - n_keys:]
        m = jnp.max(sc, axis=-1, keepdims=True)
        p = jnp.exp(sc - m)
        l = jnp.sum(p, axis=-1, keepdims=True)
        o = _dot(p.astype(BF16), vb_ref[0:n_keys, :])
        o_ref[rows, :] = (o / l).astype(o_ref.dtype)


def _mixer_b(proj, qk_g, rel_bias, col0, n_heads):
    bsz, s, _ = proj.shape
    tbl = _dsw_bias_rows(rel_bias, s)
    c0 = col0 // HEAD_B
    seq_spec = lambda off: pl.BlockSpec((None, s, HEAD_B), lambda h, b: (b, 0, off + h))
    return pl.pallas_call(
        _mixer_b_kernel,
        out_shape=jax.ShapeDtypeStruct((bsz, s, n_heads * HEAD_B), BF16),
        grid=(n_heads, bsz),
        in_specs=[
            seq_spec(c0), seq_spec(c0 + n_heads), seq_spec(c0 + 2 * n_heads),
            pl.BlockSpec((2, HEAD_B), lambda h, b: (0, 0)),
            pl.BlockSpec((None, SEQ_BLOCK, s), lambda h, b: (h, 0, 0)),
        ],
        out_specs=pl.BlockSpec((None, s, HEAD_B), lambda h, b: (b, 0, h)),
        scratch_shapes=[pltpu.VMEM((s, HEAD_B), BF16), pltpu.VMEM((s, HEAD_B), BF16)],
        compiler_params=_cparams(("parallel", "parallel")),
        name="mixer_b",
    )(proj, proj, proj, qk_g, tbl)


def _mixer_c_kernel(gb_ref, q_ref, k_ref, v_ref, og_ref, cwq_ref, cwk_ref, cbq_ref, cbk_ref,
                    ig_ref, fg_ref, gain_ref, o_ref, qc_ref, kc_ref, c_ref, n_ref, m_ref):
    h = pl.program_id(1)
    n = pl.program_id(2)
    L = SEQ_BLOCK
    d = q_ref.shape[-1]

    @pl.when(n == 0)
    def _():
        qc_ref[...] = jax.nn.silu(_causal_conv(q_ref[...], cwq_ref[...], cbq_ref[...]))
        kc_ref[...] = jax.nn.silu(_causal_conv(k_ref[...], cwk_ref[...], cbk_ref[...])) * (d ** -0.5)
        c_ref[...] = jnp.zeros_like(c_ref)
        n_ref[...] = jnp.zeros_like(n_ref)
        m_ref[...] = jnp.zeros_like(m_ref)

    start = pl.multiple_of(n * L, L)
    qc = qc_ref[pl.ds(start, L), :]
    kc = kc_ref[pl.ds(start, L), :]
    vc = v_ref[...]
    qb = qc.astype(BF16)
    kb = kc.astype(BF16)
    vb = vc.astype(BF16)

    ig_row = ig_ref[pl.ds(n, 1), :] + gb_ref[0, h]
    f_row = fg_ref[pl.ds(n, 1), :] + gb_ref[1, h]
    ls_row = jnp.minimum(f_row, 0.0) - jnp.log1p(jnp.exp(-jnp.abs(f_row)))

    ri = lax.broadcasted_iota(jnp.int32, (L, L), 0)
    ci = lax.broadcasted_iota(jnp.int32, (L, L), 1)
    eye = ri == ci
    causal = ci <= ri

    def to_col(row):
        return jnp.sum(jnp.where(eye, row, 0.0), axis=1, keepdims=True)

    ls_col = to_col(ls_row)
    ig_col = to_col(ig_row)
    b_col = jnp.sum(jnp.where(causal, ls_row, 0.0), axis=1, keepdims=True)
    b_row = jnp.sum(jnp.where(ri <= ci, ls_col, 0.0), axis=0, keepdims=True)
    b_last = jnp.sum(ls_row, axis=1, keepdims=True)

    log_d = b_col - b_row + ig_row
    g_row = b_last - b_row + ig_row
    g_col = b_last - b_col + ig_col
    g_max = jnp.max(g_row, axis=1, keepdims=True)

    c_st = c_ref[...]
    n_st = n_ref[...]
    m_st = m_ref[...]

    inter = b_col + m_st
    m_t = jnp.maximum(inter, jnp.max(jnp.where(causal, log_d, -jnp.inf), axis=1, keepdims=True))
    s_qk = jnp.where(causal, _dot_nt(qb, kb) * jnp.exp(log_d - m_t), 0.0)
    inter_w = jnp.exp(inter - m_t)
    num = _dot(s_qk.astype(BF16), vb) + inter_w * _dot(qb, c_st.astype(BF16))
    qn = jnp.sum(qc * n_st, axis=1, keepdims=True)
    den = jnp.sum(s_qk, axis=1, keepdims=True) + inter_w * qn
    hout = num / jnp.maximum(jnp.abs(den), jnp.exp(-m_t))

    hn = _rms_rows(hout, gain_ref[...])
    o_ref[...] = (jax.nn.sigmoid(og_ref[...]) * hn).astype(o_ref.dtype)

    wk = jnp.exp(g_col - g_max) * kc
    c_loc = _dot_tn(wk.astype(BF16), vb)
    n_loc = jnp.sum(wk, axis=0, keepdims=True)
    m_new = jnp.maximum(b_last + m_st, g_max)
    decay = jnp.exp(b_last + m_st - m_new)
    fresh = jnp.exp(g_max - m_new)
    c_ref[...] = decay * c_st + fresh * c_loc
    n_ref[...] = decay * n_st + fresh * n_loc
    m_ref[...] = m_new


def _mixer_c(proj, gates, conv_w, conv_b, gate_b, h_gain, n_heads):
    bsz, s, _ = proj.shape
    d = HEAD_C
    L = SEQ_BLOCK
    nc = s // L
    g_rows = jnp.transpose(gates, (0, 2, 1)).reshape(bsz, 2 * n_heads, nc, L)
    cb = conv_b.reshape(1, -1)
    full = lambda off: pl.BlockSpec((None, s, d), lambda b, h, n: (b, 0, off + h))
    chunk = lambda off: pl.BlockSpec((None, L, d), lambda b, h, n: (b, n, off + h))
    row = lambda r, off: pl.BlockSpec((r, d), lambda b, h, n: (0, off + h))
    grow = lambda off: pl.BlockSpec((None, None, nc, L), lambda b, h, n: (b, off + h, 0, 0))
    return pl.pallas_call(
        _mixer_c_kernel,
        out_shape=jax.ShapeDtypeStruct((bsz, s, n_heads * d), BF16),
        grid=(bsz, n_heads, nc),
        in_specs=[
            pl.BlockSpec(memory_space=pltpu.SMEM),
            full(0), full(n_heads), chunk(2 * n_heads), chunk(3 * n_heads),
            row(CONV_W, 0), row(CONV_W, n_heads), row(1, 0), row(1, n_heads),
            grow(0), grow(n_heads), row(1, 0),
        ],
        out_specs=pl.BlockSpec((None, L, d), lambda b, h, n: (b, n, h)),
        scratch_shapes=[pltpu.VMEM((s, d), F32), pltpu.VMEM((s, d), F32),
                        pltpu.VMEM((d, d), F32), pltpu.VMEM((1, d), F32), pltpu.VMEM((1, 1), F32)],
        compiler_params=_cparams(("parallel", "parallel", "arbitrary")),
        name="mixer_c",
    )(gate_b, proj, proj, proj, proj, conv_w, conv_w, cb, cb, g_rows, g_rows,
      h_gain.reshape(1, -1))


def _mixer_d_kernel(q_ref, k_ref, v_ref, o_ref, kb_ref, vb_ref):
    s, hd = q_ref.shape
    B = SEQ_BLOCK
    nq = s // B
    kb_ref[...] = k_ref[...].astype(BF16)
    vb_ref[...] = v_ref[...].astype(BF16)
    ri = lax.broadcasted_iota(jnp.int32, (B, B), 0)
    ci = lax.broadcasted_iota(jnp.int32, (B, B), 1)
    strictly_before = ci < ri
    after = (ri > ci).astype(BF16)
    for qi in range(nq):
        qb = (q_ref[qi * B:(qi + 1) * B, :] * (hd ** -0.5)).astype(BF16)
        suffix = jnp.zeros((B, 1), F32)
        acc = jnp.zeros((B, hd), F32)
        for j in range(qi, -1, -1):
            keys = slice(j * B, (j + 1) * B)
            z = _dot_nt(qb, kb_ref[keys, :])
            sp = jnp.log1p(jnp.exp(-jnp.abs(z)))
            log_beta = jnp.minimum(z, 0.0) - sp
            log_stay = -jnp.maximum(z, 0.0) - sp
            if j == qi:
                log_stay = jnp.where(strictly_before, log_stay, 0.0)
            hi = log_stay.astype(BF16)
            lo = (log_stay - hi.astype(F32)).astype(BF16)
            later = _dot(hi, after) + _dot(lo, after) + suffix
            att = jnp.exp(log_beta + later)
            if j == qi:
                att = jnp.where(strictly_before, att, 0.0)
            acc = acc + _dot(att.astype(BF16), vb_ref[keys, :])
            suffix = suffix + jnp.sum(log_stay, axis=-1, keepdims=True)
        o_ref[qi * B:(qi + 1) * B, :] = acc.astype(o_ref.dtype)


def _mixer_d(proj, col0, n_heads):
    bsz, s, _ = proj.shape
    c0 = col0 // HEAD_D
    seq_spec = lambda off: pl.BlockSpec((None, s, HEAD_D), lambda b, h: (b, 0, off + h))
    return pl.pallas_call(
        _mixer_d_kernel,
        out_shape=jax.ShapeDtypeStruct((bsz, s, n_heads * HEAD_D), BF16),
        grid=(bsz, n_heads),
        in_specs=[seq_spec(c0), seq_spec(c0 + n_heads), seq_spec(c0 + 2 * n_heads)],
        out_specs=pl.BlockSpec((None, s, HEAD_D), lambda b, h: (b, 0, h)),
        scratch_shapes=[pltpu.VMEM((s, HEAD_D), BF16), pltpu.VMEM((s, HEAD_D), BF16)],
        compiler_params=_cparams(("parallel", "parallel")),
        name="mixer_d",
    )(proj, proj, proj)


def _layer_ab(x, bsz, s, g, w_in, conv_w, conv_b, w_r, b_r, w_i, b_i, lam, qk_g, rel_bias, w_out):
    d_rg = conv_w.shape[-1]
    n_heads = rel_bias.shape[-1]
    proj = _norm_matmul(x, g, w_in.astype(BF16)).reshape(bsz, s, -1)
    ya = _mixer_a(proj, conv_w, conv_b, w_r, b_r, w_i, b_i, lam, d_rg)
    yb = _mixer_b(proj, qk_g, rel_bias, 2 * d_rg, n_heads)
    n = bsz * s
    return _out_proj(x, ya.reshape(n, -1), yb.reshape(n, -1), w_out.astype(BF16))


def _layer_cd(x, bsz, s, g, w_in, conv_w, conv_b, gate_b, h_gain, w_out):
    n_heads_c = gate_b.shape[-1]
    dc = n_heads_c * HEAD_C
    d_mix = w_out.shape[0]
    dd = d_mix - dc
    n_heads_d = dd // HEAD_D
    n_gate = 2 * n_heads_c
    main = 4 * dc + 3 * dd
    width = -(-(main + n_gate) // 512) * 512
    w = jnp.concatenate([w_in[:, :4 * dc], w_in[:, 4 * dc + n_gate:], w_in[:, 4 * dc:4 * dc + n_gate],
                         jnp.zeros((w_in.shape[0], width - main - n_gate), w_in.dtype)], axis=1)
    proj = _norm_matmul(x, g, w.astype(BF16), tn=768).reshape(bsz, s, width)
    gates = proj[:, :, main:main + n_gate]
    yc = _mixer_c(proj, gates, conv_w, conv_b, gate_b, h_gain, n_heads_c)
    yd = _mixer_d(proj, 4 * dc, n_heads_d)
    n = bsz * s
    return _out_proj(x, yc.reshape(n, -1), yd.reshape(n, -1), w_out.astype(BF16))


def kernel(x, norm_g, ffn_w_in, ffn_w_out, ab_w_in, ab_conv_w, ab_conv_b, rg_w_r, rg_b_r, rg_w_i,
           rg_b_i, rg_lambda, qk_gain, rel_bias, ab_w_out, cd_w_in, cd_conv_w, cd_conv_b,
           mlstm_gate_bias, mlstm_h_gain, cd_w_out):
    bsz, s, d = x.shape
    depth = norm_g.shape[0]
    h = x.reshape(bsz * s, d)
    for layer in range(depth):
        j = layer // 2
        h = _ffn(h, norm_g[layer, 0], ffn_w_in[layer, 0].astype(BF16), ffn_w_out[layer, 0].astype(BF16))
        if layer % 2 == 0:
            h = _layer_ab(h, bsz, s, norm_g[layer, 1], ab_w_in[j], ab_conv_w[j], ab_conv_b[j],
                          rg_w_r[j], rg_b_r[j], rg_w_i[j], rg_b_i[j], rg_lambda[j], qk_gain[j],
                          rel_bias, ab_w_out[j])
        else:
            h = _layer_cd(h, bsz, s, norm_g[layer, 1], cd_w_in[j], cd_conv_w[j], cd_conv_b[j],
                          mlstm_gate_bias[j], mlstm_h_gain[j], cd_w_out[j])
        h = _ffn(h, norm_g[layer, 2], ffn_w_in[layer, 1].astype(BF16), ffn_w_out[layer, 1].astype(BF16))
    return h.reshape(bsz, s, d)
```

```python
import functools
import math

import numpy as np
import jax
import jax.numpy as jnp
from jax import lax
from jax.experimental import pallas as pl
from jax.experimental.pallas import tpu as pltpu

F32 = jnp.float32
BF16 = jnp.bfloat16

RMS_EPS = 1e-6
RG_C = 8.0
CONV_W = 4
RG_BLOCK = 128
HEAD_B = 128
HEAD_C = 256
HEAD_D = 128
SEQ_BLOCK = 128
DSW_PATTERNS = ((128, 1), (512, 4), (2048, 16))
N_BUCKETS = 32
MAX_DIST = 2048
NEG = -1e30
LOG2E = 1.4426950408889634
HI_MASK = np.uint32(0xFFFF0000)
SB_NEAR_BLOCKS = 3
SB_UNDERFLOW = -104.0
VMEM_LIMIT = 56 * 1024 * 1024


def _cparams(sem):
    return pltpu.CompilerParams(dimension_semantics=sem, vmem_limit_bytes=VMEM_LIMIT)


def _rms_rows(x, g):
    return x * lax.rsqrt(jnp.mean(x * x, axis=-1, keepdims=True) + RMS_EPS) * g


def _dot(a, b):
    return jnp.dot(a, b, preferred_element_type=F32)


def _dot_nt(a, b):
    return lax.dot_general(a, b, (((1,), (1,)), ((), ())), preferred_element_type=F32)


def _dot_tn(a, b):
    return lax.dot_general(a, b, (((0,), (0,)), ((), ())), preferred_element_type=F32)


def _pick_tile(n, pref):
    t = min(pref, n)
    while n % t:
        t //= 2
    return t


def _ffn_kernel(x_ref, g_ref, wg_ref, wu_ref, wo_ref, o_ref, xn_ref):
    j = pl.program_id(1)

    @pl.when(j == 0)
    def _():
        xn_ref[...] = _rms_rows(x_ref[...], g_ref[...]).astype(BF16)
        o_ref[...] = jnp.zeros_like(o_ref)

    xn = xn_ref[...]
    hg = _dot(xn, wg_ref[...])
    hu = _dot(xn, wu_ref[...])
    h = (hg * jax.nn.sigmoid(hg) * hu).astype(BF16)
    o_ref[...] += _dot(h, wo_ref[...])

    @pl.when(j == pl.num_programs(1) - 1)
    def _():
        o_ref[...] = x_ref[...] + 0.5 * o_ref[...]


def _ffn(x, g, w_in, w_out, tm=512, tf=512):
    n, d = x.shape
    d_ff = w_out.shape[0]
    tm = _pick_tile(n, tm)
    tf = _pick_tile(d_ff, tf)
    nj = d_ff // tf
    return pl.pallas_call(
        _ffn_kernel,
        out_shape=jax.ShapeDtypeStruct((n, d), F32),
        grid=(n // tm, nj),
        in_specs=[
            pl.BlockSpec((tm, d), lambda i, j: (i, 0)),
            pl.BlockSpec((1, d), lambda i, j: (0, 0)),
            pl.BlockSpec((d, tf), lambda i, j: (0, j)),
            pl.BlockSpec((d, tf), lambda i, j: (0, j + nj)),
            pl.BlockSpec((tf, d), lambda i, j: (j, 0)),
        ],
        out_specs=pl.BlockSpec((tm, d), lambda i, j: (i, 0)),
        scratch_shapes=[pltpu.VMEM((tm, d), BF16)],
        compiler_params=_cparams(("parallel", "arbitrary")),
        name="ffn",
    )(x, g.reshape(1, d), w_in, w_in, w_out)


def _norm_matmul_kernel(x_ref, g_ref, w_ref, o_ref, xn_ref):
    @pl.when(pl.program_id(1) == 0)
    def _():
        xn_ref[...] = _rms_rows(x_ref[...], g_ref[...]).astype(BF16)

    o_ref[...] = _dot(xn_ref[...], w_ref[...])


def _norm_matmul(x, g, w, tm=1024, tn=1024):
    n, d = x.shape
    wn = w.shape[1]
    tm = _pick_tile(n, tm)
    while wn % tn:
        tn -= 128
    return pl.pallas_call(
        _norm_matmul_kernel,
        out_shape=jax.ShapeDtypeStruct((n, wn), F32),
        grid=(n // tm, wn // tn),
        in_specs=[
            pl.BlockSpec((tm, d), lambda i, j: (i, 0)),
            pl.BlockSpec((1, d), lambda i, j: (0, 0)),
            pl.BlockSpec((d, tn), lambda i, j: (0, j)),
        ],
        out_specs=pl.BlockSpec((tm, tn), lambda i, j: (i, j)),
        scratch_shapes=[pltpu.VMEM((tm, d), BF16)],
        compiler_params=_cparams(("parallel", "arbitrary")),
        name="norm_matmul",
    )(x, g.reshape(1, d), w)


def _out_proj_kernel(x_ref, ya_ref, yb_ref, wa_ref, wb_ref, o_ref):
    o_ref[...] = x_ref[...] + _dot(ya_ref[...], wa_ref[...]) + _dot(yb_ref[...], wb_ref[...])


def _out_proj(x, ya, yb, w, tm=512):
    n, d = x.shape
    da = ya.shape[1]
    db = yb.shape[1]
    tm = _pick_tile(n, tm)
    return pl.pallas_call(
        _out_proj_kernel,
        out_shape=jax.ShapeDtypeStruct((n, d), F32),
        grid=(n // tm,),
        in_specs=[
            pl.BlockSpec((tm, d), lambda i: (i, 0)),
            pl.BlockSpec((tm, da), lambda i: (i, 0)),
            pl.BlockSpec((tm, db), lambda i: (i, 0)),
            pl.BlockSpec((da, d), lambda i: (0, 0)),
            pl.BlockSpec((db, d), lambda i: (0, 0)),
        ],
        out_specs=pl.BlockSpec((tm, d), lambda i: (i, 0)),
        compiler_params=_cparams(("parallel",)),
        name="out_proj",
    )(x, ya, yb, w[:da], w[da:])


def _causal_conv(x, w, b):
    t = lax.broadcasted_iota(jnp.int32, x.shape, 0)
    y = x * w[CONV_W - 1:CONV_W] + b
    for k in range(1, CONV_W):
        xs = jnp.where(t >= k, pltpu.roll(x, k, axis=0), 0.0)
        y = y + xs * w[CONV_W - 1 - k:CONV_W - k]
    return y


def _mixer_a_kernel(xr_ref, gate_ref, cw_ref, cb_ref, wr_ref, br_ref, wi_ref, bi_ref,
                    lam_ref, o_ref):
    s = xr_ref.shape[0]
    xc = _causal_conv(xr_ref[...], cw_ref[...], cb_ref[...])
    xb = xc.astype(BF16)
    r = jax.nn.sigmoid(_dot(xb, wr_ref[...]) + br_ref[...])
    i = jax.nn.sigmoid(_dot(xb, wi_ref[...]) + bi_ref[...])
    neg_lam = -lam_ref[...]
    softplus = jnp.maximum(neg_lam, 0.0) + jnp.log1p(jnp.exp(-jnp.abs(neg_lam)))
    log_a = -RG_C * r * softplus
    a = jnp.exp(log_a)
    u = jnp.sqrt(-jnp.tanh(log_a) * (a * a + 1.0)) * (i * xc)
    t = lax.broadcasted_iota(jnp.int32, a.shape, 0)
    k = 1
    while k < s:
        keep = t >= k
        a_sh = jnp.where(keep, pltpu.roll(a, k, axis=0), 1.0)
        u_sh = jnp.where(keep, pltpu.roll(u, k, axis=0), 0.0)
        u = a * u_sh + u
        a = a * a_sh
        k *= 2
    o_ref[...] = (jax.nn.gelu(gate_ref[...]) * u).astype(o_ref.dtype)


def _mixer_a(proj, conv_w, conv_b, w_r, b_r, w_i, b_i, lam, d_rg):
    bsz, s, _ = proj.shape
    nb = d_rg // RG_BLOCK
    vec = lambda v: v.reshape(1, d_rg)
    col = lambda off: pl.BlockSpec((None, s, RG_BLOCK), lambda b, g: (b, 0, off + g))
    row = lambda r: pl.BlockSpec((r, RG_BLOCK), lambda b, g: (0, g))
    blk = pl.BlockSpec((None, RG_BLOCK, RG_BLOCK), lambda b, g: (g, 0, 0))
    return pl.pallas_call(
        _mixer_a_kernel,
        out_shape=jax.ShapeDtypeStruct((bsz, s, d_rg), BF16),
        grid=(bsz, nb),
        in_specs=[col(nb), col(0), row(CONV_W), row(1), blk, row(1), blk, row(1), row(1)],
        out_specs=pl.BlockSpec((None, s, RG_BLOCK), lambda b, g: (b, 0, g)),
        compiler_params=_cparams(("parallel", "parallel")),
        name="mixer_a",
    )(proj, proj, conv_w, vec(conv_b), w_r.astype(BF16), vec(b_r), w_i.astype(BF16), vec(b_i),
      vec(lam))


def _t5_bucket_np(dist):
    max_exact = N_BUCKETS // 2
    d_f = np.maximum(dist, 1).astype(np.float32)
    large = max_exact + (np.log(d_f / np.float32(max_exact)) / np.float32(math.log(MAX_DIST / max_exact))
                         * np.float32(N_BUCKETS - max_exact)).astype(np.int32)
    large = np.minimum(large, N_BUCKETS - 1)
    return np.where(dist < max_exact, dist, large).astype(np.int32)


def _dsw_bias_rows(rel_bias, seq):
    dist = np.arange(seq)
    mult = np.zeros(seq, np.float32)
    for window, dilation in DSW_PATTERNS:
        mult += ((dist % dilation == 0) & (dist <= window)).astype(np.float32)
    logm = np.where(mult > 0, np.log(np.maximum(mult, 1.0)), NEG).astype(np.float32)
    f = rel_bias.astype(F32)[_t5_bucket_np(dist)] + logm[:, None]
    n_heads = f.shape[1]
    p = seq + SEQ_BLOCK
    hv = jnp.concatenate([f[::-1].T, jnp.full((n_heads, SEQ_BLOCK), NEG, F32)], axis=1)
    t = jnp.broadcast_to(hv[:, None, :], (n_heads, SEQ_BLOCK, p)).reshape(n_heads, SEQ_BLOCK * p)
    t = t[:, :SEQ_BLOCK * (p - 1)].reshape(n_heads, SEQ_BLOCK, p - 1)
    return t[:, :, SEQ_BLOCK - 1:SEQ_BLOCK - 1 + seq]


def _mixer_b_kernel(q_ref, k_ref, v_ref, g_ref, tbl_ref, o_ref, kn_ref, vb_ref):
    s, hd = q_ref.shape
    B = SEQ_BLOCK
    nq = s // B
    kn_ref[...] = _rms_rows(k_ref[...], g_ref[1:2]).astype(BF16)
    vb_ref[...] = v_ref[...].astype(BF16)
    for qi in range(nq):
        n_keys = (qi + 1) * B
        rows = slice(qi * B, n_keys)
        qn = (_rms_rows(q_ref[rows, :], g_ref[0:1]) * (1.0 / math.sqrt(hd))).astype(BF16)
        sc = _dot_nt(qn, kn_ref[0:n_keys, :]) + tbl_ref[:, s - n_keys:]
        m = jnp.max(sc, axis=-1, keepdims=True)
        p = jnp.exp(sc - m)
        l = jnp.sum(p, axis=-1, keepdims=True)
        o = _dot(p.astype(BF16), vb_ref[0:n_keys, :])
        o_ref[rows, :] = (o / l).astype(o_ref.dtype)


def _mixer_b(proj, qk_g, rel_bias, col0, n_heads):
    bsz, s, _ = proj.shape
    tbl = _dsw_bias_rows(rel_bias, s)
    c0 = col0 // HEAD_B
    seq_spec = lambda off: pl.BlockSpec((None, s, HEAD_B), lambda h, b: (b, 0, off + h))
    return pl.pallas_call(
        _mixer_b_kernel,
        out_shape=jax.ShapeDtypeStruct((bsz, s, n_heads * HEAD_B), BF16),
        grid=(n_heads, bsz),
        in_specs=[
            seq_spec(c0), seq_spec(c0 + n_heads), seq_spec(c0 + 2 * n_heads),
            pl.BlockSpec((2, HEAD_B), lambda h, b: (0, 0)),
            pl.BlockSpec((None, SEQ_BLOCK, s), lambda h, b: (h, 0, 0)),
        ],
        out_specs=pl.BlockSpec((None, s, HEAD_B), lambda h, b: (b, 0, h)),
        scratch_shapes=[pltpu.VMEM((s, HEAD_B), BF16), pltpu.VMEM((s, HEAD_B), BF16)],
        compiler_params=_cparams(("parallel", "parallel")),
        name="mixer_b",
    )(proj, proj, proj, qk_g, tbl)


def _mixer_c_kernel(gb_ref, q_ref, k_ref, v_ref, og_ref, cwq_ref, cwk_ref, cbq_ref, cbk_ref,
                    ig_ref, fg_ref, gain_ref, o_ref, qc_ref, kc_ref, c_ref, n_ref, m_ref):
    h = pl.program_id(1)
    n = pl.program_id(2)
    L = SEQ_BLOCK
    d = q_ref.shape[-1]

    @pl.when(n == 0)
    def _():
        qc_ref[...] = jax.nn.silu(_causal_conv(q_ref[...], cwq_ref[...], cbq_ref[...]))
        kc_ref[...] = jax.nn.silu(_causal_conv(k_ref[...], cwk_ref[...], cbk_ref[...])) * (d ** -0.5)
        c_ref[...] = jnp.zeros_like(c_ref)
        n_ref[...] = jnp.zeros_like(n_ref)
        m_ref[...] = jnp.zeros_like(m_ref)

    start = pl.multiple_of(n * L, L)
    qc = qc_ref[pl.ds(start, L), :]
    kc = kc_ref[pl.ds(start, L), :]
    vc = v_ref[...]
    qb = qc.astype(BF16)
    kb = kc.astype(BF16)
    vb = vc.astype(BF16)

    ig_row = ig_ref[pl.ds(n, 1), :] + gb_ref[0, h]
    f_row = fg_ref[pl.ds(n, 1), :] + gb_ref[1, h]
    ls_row = jnp.minimum(f_row, 0.0) - jnp.log1p(jnp.exp(-jnp.abs(f_row)))

    ri = lax.broadcasted_iota(jnp.int32, (L, L), 0)
    ci = lax.broadcasted_iota(jnp.int32, (L, L), 1)
    eye = ri == ci
    causal = ci <= ri

    def to_col(row):
        return jnp.sum(jnp.where(eye, row, 0.0), axis=1, keepdims=True)

    ls_col = to_col(ls_row)
    ig_col = to_col(ig_row)
    b_col = jnp.sum(jnp.where(causal, ls_row, 0.0), axis=1, keepdims=True)
    b_row = jnp.sum(jnp.where(ri <= ci, ls_col, 0.0), axis=0, keepdims=True)
    b_last = jnp.sum(ls_row, axis=1, keepdims=True)

    log_d = b_col - b_row + ig_row
    g_row = b_last - b_row + ig_row
    g_col = b_last - b_col + ig_col
    g_max = jnp.max(g_row, axis=1, keepdims=True)

    c_st = c_ref[...]
    n_st = n_ref[...]
    m_st = m_ref[...]

    inter = b_col + m_st
    m_t = jnp.maximum(inter, jnp.max(jnp.where(causal, log_d, -jnp.inf), axis=1, keepdims=True))
    s_qk = jnp.where(causal, _dot_nt(qb, kb) * jnp.exp(log_d - m_t), 0.0)
    inter_w = jnp.exp(inter - m_t)
    num = _dot(s_qk.astype(BF16), vb) + inter_w * _dot(qb, c_st.astype(BF16))
    qn = jnp.sum(qc * n_st, axis=1, keepdims=True)
    den = jnp.sum(s_qk, axis=1, keepdims=True) + inter_w * qn
    hout = num / jnp.maximum(jnp.abs(den), jnp.exp(-m_t))

    hn = _rms_rows(hout, gain_ref[...])
    o_ref[...] = (jax.nn.sigmoid(og_ref[...]) * hn).astype(o_ref.dtype)

    wk = jnp.exp(g_col - g_max) * kc
    c_loc = _dot_tn(wk.astype(BF16), vb)
    n_loc = jnp.sum(wk, axis=0, keepdims=True)
    m_new = jnp.maximum(b_last + m_st, g_max)
    decay = jnp.exp(b_last + m_st - m_new)
    fresh = jnp.exp(g_max - m_new)
    c_ref[...] = decay * c_st + fresh * c_loc
    n_ref[...] = decay * n_st + fresh * n_loc
    m_ref[...] = m_new


def _mixer_c(proj, gates, conv_w, conv_b, gate_b, h_gain, n_heads):
    bsz, s, _ = proj.shape
    d = HEAD_C
    L = SEQ_BLOCK
    nc = s // L
    g_rows = jnp.transpose(gates, (0, 2, 1)).reshape(bsz, 2 * n_heads, nc, L)
    cb = conv_b.reshape(1, -1)
    full = lambda off: pl.BlockSpec((None, s, d), lambda b, h, n: (b, 0, off + h))
    chunk = lambda off: pl.BlockSpec((None, L, d), lambda b, h, n: (b, n, off + h))
    row = lambda r, off: pl.BlockSpec((r, d), lambda b, h, n: (0, off + h))
    grow = lambda off: pl.BlockSpec((None, None, nc, L), lambda b, h, n: (b, off + h, 0, 0))
    return pl.pallas_call(
        _mixer_c_kernel,
        out_shape=jax.ShapeDtypeStruct((bsz, s, n_heads * d), BF16),
        grid=(bsz, n_heads, nc),
        in_specs=[
            pl.BlockSpec(memory_space=pltpu.SMEM),
            full(0), full(n_heads), chunk(2 * n_heads), chunk(3 * n_heads),
            row(CONV_W, 0), row(CONV_W, n_heads), row(1, 0), row(1, n_heads),
            grow(0), grow(n_heads), row(1, 0),
        ],
        out_specs=pl.BlockSpec((None, L, d), lambda b, h, n: (b, n, h)),
        scratch_shapes=[pltpu.VMEM((s, d), F32), pltpu.VMEM((s, d), F32),
                        pltpu.VMEM((d, d), F32), pltpu.VMEM((1, d), F32), pltpu.VMEM((1, 1), F32)],
        compiler_params=_cparams(("parallel", "parallel", "arbitrary")),
        name="mixer_c",
    )(gate_b, proj, proj, proj, proj, conv_w, conv_w, cb, cb, g_rows, g_rows,
      h_gain.reshape(1, -1))


def _mixer_d_kernel(q_ref, k_ref, v_ref, o_ref, kb_ref, vb_ref):
    s, hd = q_ref.shape
    B = SEQ_BLOCK
    nq = s // B
    kb_ref[...] = k_ref[...].astype(BF16)
    vb_ref[...] = v_ref[...].astype(BF16)
    ri = lax.broadcasted_iota(jnp.int32, (B, B), 0)
    ci = lax.broadcasted_iota(jnp.int32, (B, B), 1)
    strictly_before = ci < ri
    after = (ri > ci).astype(BF16)

    def span(qb, first, last, suffix, diag):
        n = last - first + 1
        keys = slice(first * B, (last + 1) * B)
        z = _dot_nt(qb, kb_ref[keys, :])
        sp = jnp.log(1.0 + jnp.exp2(jnp.abs(z) * (-LOG2E)))
        nm = jnp.minimum(z, 0.0)
        log_beta = nm - sp
        log_stay = (nm - z) - sp
        ls, later, tot = [], [], []
        for j in range(n):
            x = log_stay[:, j * B:(j + 1) * B]
            if diag and j == n - 1:
                x = jnp.where(strictly_before, x, 0.0)
            hi = lax.bitcast_convert_type(lax.bitcast_convert_type(x, jnp.uint32) & HI_MASK, F32)
            later.append(_dot(hi.astype(BF16), after) + _dot((x - hi).astype(BF16), after))
            tot.append(jnp.sum(x, axis=-1, keepdims=True))
            ls.append(x)
        att = [None] * n
        for j in range(n - 1, -1, -1):
            a = jnp.exp(log_beta[:, j * B:(j + 1) * B] + later[j] + suffix)
            if diag and j == n - 1:
                a = jnp.where(strictly_before, a, 0.0)
            att[j] = a.astype(BF16)
            suffix = suffix + tot[j]
        att = att[0] if n == 1 else jnp.concatenate(att, axis=1)
        return _dot(att, vb_ref[keys, :]), suffix

    for qi in range(nq):
        qb = (q_ref[qi * B:(qi + 1) * B, :] * (hd ** -0.5)).astype(BF16)
        first_near = max(qi - SB_NEAR_BLOCKS + 1, 0)
        acc, suffix = span(qb, first_near, qi, jnp.zeros((B, 1), F32), True)
        if first_near > 0:
            far = functools.partial(span, qb, 0, first_near - 1, suffix, False)
            acc = lax.cond(jnp.max(suffix) > SB_UNDERFLOW, lambda: acc + far()[0], lambda: acc)
        o_ref[qi * B:(qi + 1) * B, :] = acc.astype(o_ref.dtype)


def _mixer_d(proj, col0, n_heads):
    bsz, s, _ = proj.shape
    c0 = col0 // HEAD_D
    seq_spec = lambda off: pl.BlockSpec((None, s, HEAD_D), lambda b, h: (b, 0, off + h))
    return pl.pallas_call(
        _mixer_d_kernel,
        out_shape=jax.ShapeDtypeStruct((bsz, s, n_heads * HEAD_D), BF16),
        grid=(bsz, n_heads),
        in_specs=[seq_spec(c0), seq_spec(c0 + n_heads), seq_spec(c0 + 2 * n_heads)],
        out_specs=pl.BlockSpec((None, s, HEAD_D), lambda b, h: (b, 0, h)),
        scratch_shapes=[pltpu.VMEM((s, HEAD_D), BF16), pltpu.VMEM((s, HEAD_D), BF16)],
        compiler_params=_cparams(("parallel", "parallel")),
        name="mixer_d",
    )(proj, proj, proj)


def _layer_ab(x, bsz, s, g, w_in, conv_w, conv_b, w_r, b_r, w_i, b_i, lam, qk_g, rel_bias, w_out):
    d_rg = conv_w.shape[-1]
    n_heads = rel_bias.shape[-1]
    proj = _norm_matmul(x, g, w_in.astype(BF16)).reshape(bsz, s, -1)
    ya = _mixer_a(proj, conv_w, conv_b, w_r, b_r, w_i, b_i, lam, d_rg)
    yb = _mixer_b(proj, qk_g, rel_bias, 2 * d_rg, n_heads)
    n = bsz * s
    return _out_proj(x, ya.reshape(n, -1), yb.reshape(n, -1), w_out.astype(BF16))


def _layer_cd(x, bsz, s, g, w_in, conv_w, conv_b, gate_b, h_gain, w_out):
    n_heads_c = gate_b.shape[-1]
    dc = n_heads_c * HEAD_C
    d_mix = w_out.shape[0]
    dd = d_mix - dc
    n_heads_d = dd // HEAD_D
    n_gate = 2 * n_heads_c
    main = 4 * dc + 3 * dd
    width = -(-(main + n_gate) // 512) * 512
    w = jnp.concatenate([w_in[:, :4 * dc], w_in[:, 4 * dc + n_gate:], w_in[:, 4 * dc:4 * dc + n_gate],
                         jnp.zeros((w_in.shape[0], width - main - n_gate), w_in.dtype)], axis=1)
    proj = _norm_matmul(x, g, w.astype(BF16), tn=768).reshape(bsz, s, width)
    gates = proj[:, :, main:main + n_gate]
    yc = _mixer_c(proj, gates, conv_w, conv_b, gate_b, h_gain, n_heads_c)
    yd = _mixer_d(proj, 4 * dc, n_heads_d)
    n = bsz * s
    return _out_proj(x, yc.reshape(n, -1), yd.reshape(n, -1), w_out.astype(BF16))


def kernel(x, norm_g, ffn_w_in, ffn_w_out, ab_w_in, ab_conv_w, ab_conv_b, rg_w_r, rg_b_r, rg_w_i,
           rg_b_i, rg_lambda, qk_gain, rel_bias, ab_w_out, cd_w_in, cd_conv_w, cd_conv_b,
           mlstm_gate_bias, mlstm_h_gain, cd_w_out):
    bsz, s, d = x.shape
    depth = norm_g.shape[0]
    h = x.reshape(bsz * s, d)
    for layer in range(depth):
        j = layer // 2
        h = _ffn(h, norm_g[layer, 0], ffn_w_in[layer, 0].astype(BF16), ffn_w_out[layer, 0].astype(BF16))
        if layer % 2 == 0:
            h = _layer_ab(h, bsz, s, norm_g[layer, 1], ab_w_in[j], ab_conv_w[j], ab_conv_b[j],
                          rg_w_r[j], rg_b_r[j], rg_w_i[j], rg_b_i[j], rg_lambda[j], qk_gain[j],
                          rel_bias, ab_w_out[j])
        else:
            h = _layer_cd(h, bsz, s, norm_g[layer, 1], cd_w_in[j], cd_conv_w[j], cd_conv_b[j],
                          mlstm_gate_bias[j], mlstm_h_gain[j], cd_w_out[j])
        h = _ffn(h, norm_g[layer, 2], ffn_w_in[layer, 1].astype(BF16), ffn_w_out[layer, 1].astype(BF16))
    return h.reshape(bsz, s, d)
```

```python
import functools
import math

import numpy as np
import jax
import jax.numpy as jnp
from jax import lax
from jax.experimental import pallas as pl
from jax.experimental.pallas import tpu as pltpu

F32 = jnp.float32
BF16 = jnp.bfloat16

RMS_EPS = 1e-6
RG_C = 8.0
CONV_W = 4
RG_BLOCK = 128
HEAD_B = 128
HEAD_C = 256
HEAD_D = 128
SEQ_BLOCK = 128
DSW_PATTERNS = ((128, 1), (512, 4), (2048, 16))
N_BUCKETS = 32
MAX_DIST = 2048
NEG = -1e30
LOG2E = 1.4426950408889634
HI_MASK = np.uint32(0xFFFF0000)
SB_NEAR_BLOCKS = 3
SB_UNDERFLOW = -104.0
VMEM_LIMIT = 56 * 1024 * 1024


def _cparams(sem):
    return pltpu.CompilerParams(dimension_semantics=sem, vmem_limit_bytes=VMEM_LIMIT)


def _rms_rows(x, g):
    return x * lax.rsqrt(jnp.mean(x * x, axis=-1, keepdims=True) + RMS_EPS) * g


def _dot(a, b):
    return jnp.dot(a, b, preferred_element_type=F32)


def _dot_nt(a, b):
    return lax.dot_general(a, b, (((1,), (1,)), ((), ())), preferred_element_type=F32)


def _dot_tn(a, b):
    return lax.dot_general(a, b, (((0,), (0,)), ((), ())), preferred_element_type=F32)


def _pick_tile(n, pref):
    t = min(pref, n)
    while n % t:
        t //= 2
    return t


def _ffn_kernel(x_ref, g_ref, wg_ref, wu_ref, wo_ref, o_ref, xn_ref):
    j = pl.program_id(1)

    @pl.when(j == 0)
    def _():
        xn_ref[...] = _rms_rows(x_ref[...], g_ref[...]).astype(BF16)
        o_ref[...] = jnp.zeros_like(o_ref)

    xn = xn_ref[...]
    hg = _dot(xn, wg_ref[...])
    hu = _dot(xn, wu_ref[...])
    h = (hg * jax.nn.sigmoid(hg) * hu).astype(BF16)
    o_ref[...] += _dot(h, wo_ref[...])

    @pl.when(j == pl.num_programs(1) - 1)
    def _():
        o_ref[...] = x_ref[...] + 0.5 * o_ref[...]


def _ffn(x, g, w_in, w_out, layer, half, tm=512, tf=512):
    n, d = x.shape
    d_ff = w_out.shape[2]
    tm = _pick_tile(n, tm)
    tf = _pick_tile(d_ff, tf)
    nj = d_ff // tf
    return pl.pallas_call(
        _ffn_kernel,
        out_shape=jax.ShapeDtypeStruct((n, d), F32),
        grid=(n // tm, nj),
        in_specs=[
            pl.BlockSpec((tm, d), lambda i, j: (i, 0)),
            pl.BlockSpec((1, d), lambda i, j: (0, 0)),
            pl.BlockSpec((None, None, d, tf), lambda i, j: (layer, half, 0, j)),
            pl.BlockSpec((None, None, d, tf), lambda i, j: (layer, half, 0, j + nj)),
            pl.BlockSpec((None, None, tf, d), lambda i, j: (layer, half, j, 0)),
        ],
        out_specs=pl.BlockSpec((tm, d), lambda i, j: (i, 0)),
        scratch_shapes=[pltpu.VMEM((tm, d), BF16)],
        compiler_params=_cparams(("parallel", "arbitrary")),
        name="ffn",
    )(x, g.reshape(1, d), w_in, w_in, w_out)


def _norm_matmul_kernel(x_ref, g_ref, w_ref, o_ref, xn_ref):
    @pl.when(pl.program_id(1) == 0)
    def _():
        xn_ref[...] = _rms_rows(x_ref[...], g_ref[...]).astype(BF16)

    o_ref[...] = _dot(xn_ref[...], w_ref[...])


def _norm_matmul(x, g, w, tm=1024, tn=1024):
    n, d = x.shape
    wn = w.shape[1]
    tm = _pick_tile(n, tm)
    while wn % tn:
        tn -= 128
    return pl.pallas_call(
        _norm_matmul_kernel,
        out_shape=jax.ShapeDtypeStruct((n, wn), F32),
        grid=(n // tm, wn // tn),
        in_specs=[
            pl.BlockSpec((tm, d), lambda i, j: (i, 0)),
            pl.BlockSpec((1, d), lambda i, j: (0, 0)),
            pl.BlockSpec((d, tn), lambda i, j: (0, j)),
        ],
        out_specs=pl.BlockSpec((tm, tn), lambda i, j: (i, j)),
        scratch_shapes=[pltpu.VMEM((tm, d), BF16)],
        compiler_params=_cparams(("parallel", "arbitrary")),
        name="norm_matmul",
    )(x, g.reshape(1, d), w)


def _out_proj_kernel(x_ref, ya_ref, yb_ref, wa_ref, wb_ref, o_ref):
    o_ref[...] = x_ref[...] + _dot(ya_ref[...], wa_ref[...]) + _dot(yb_ref[...], wb_ref[...])


def _out_proj(x, ya, yb, w, tm=512):
    n, d = x.shape
    da = ya.shape[1]
    db = yb.shape[1]
    assert da == db and w.shape[0] == da + db
    tm = _pick_tile(n, tm)
    return pl.pallas_call(
        _out_proj_kernel,
        out_shape=jax.ShapeDtypeStruct((n, d), F32),
        grid=(n // tm,),
        in_specs=[
            pl.BlockSpec((tm, d), lambda i: (i, 0)),
            pl.BlockSpec((tm, da), lambda i: (i, 0)),
            pl.BlockSpec((tm, db), lambda i: (i, 0)),
            pl.BlockSpec((da, d), lambda i: (0, 0)),
            pl.BlockSpec((db, d), lambda i: (1, 0)),
        ],
        out_specs=pl.BlockSpec((tm, d), lambda i: (i, 0)),
        compiler_params=_cparams(("parallel",)),
        name="out_proj",
    )(x, ya, yb, w, w)


def _causal_conv(x, w, b):
    t = lax.broadcasted_iota(jnp.int32, x.shape, 0)
    y = x * w[CONV_W - 1:CONV_W] + b
    for k in range(1, CONV_W):
        xs = jnp.where(t >= k, pltpu.roll(x, k, axis=0), 0.0)
        y = y + xs * w[CONV_W - 1 - k:CONV_W - k]
    return y


def _mixer_a_kernel(xr_ref, gate_ref, cw_ref, cb_ref, wr_ref, br_ref, wi_ref, bi_ref,
                    lam_ref, o_ref):
    s = xr_ref.shape[0]
    xc = _causal_conv(xr_ref[...], cw_ref[...], cb_ref[...])
    xb = xc.astype(BF16)
    r = jax.nn.sigmoid(_dot(xb, wr_ref[...]) + br_ref[...])
    i = jax.nn.sigmoid(_dot(xb, wi_ref[...]) + bi_ref[...])
    neg_lam = -lam_ref[...]
    softplus = jnp.maximum(neg_lam, 0.0) + jnp.log1p(jnp.exp(-jnp.abs(neg_lam)))
    log_a = -RG_C * r * softplus
    a = jnp.exp(log_a)
    u = jnp.sqrt(-jnp.tanh(log_a) * (a * a + 1.0)) * (i * xc)
    t = lax.broadcasted_iota(jnp.int32, a.shape, 0)
    k = 1
    while k < s:
        keep = t >= k
        a_sh = jnp.where(keep, pltpu.roll(a, k, axis=0), 1.0)
        u_sh = jnp.where(keep, pltpu.roll(u, k, axis=0), 0.0)
        u = a * u_sh + u
        a = a * a_sh
        k *= 2
    o_ref[...] = (jax.nn.gelu(gate_ref[...]) * u).astype(o_ref.dtype)


def _mixer_a(proj, conv_w, conv_b, w_r, b_r, w_i, b_i, lam, d_rg):
    bsz, s, _ = proj.shape
    nb = d_rg // RG_BLOCK
    vec = lambda v: v.reshape(1, d_rg)
    col = lambda off: pl.BlockSpec((None, s, RG_BLOCK), lambda b, g: (b, 0, off + g))
    row = lambda r: pl.BlockSpec((r, RG_BLOCK), lambda b, g: (0, g))
    blk = pl.BlockSpec((None, RG_BLOCK, RG_BLOCK), lambda b, g: (g, 0, 0))
    return pl.pallas_call(
        _mixer_a_kernel,
        out_shape=jax.ShapeDtypeStruct((bsz, s, d_rg), BF16),
        grid=(bsz, nb),
        in_specs=[col(nb), col(0), row(CONV_W), row(1), blk, row(1), blk, row(1), row(1)],
        out_specs=pl.BlockSpec((None, s, RG_BLOCK), lambda b, g: (b, 0, g)),
        compiler_params=_cparams(("parallel", "parallel")),
        name="mixer_a",
    )(proj, proj, conv_w, vec(conv_b), w_r.astype(BF16), vec(b_r), w_i.astype(BF16), vec(b_i),
      vec(lam))


def _t5_bucket_np(dist):
    max_exact = N_BUCKETS // 2
    d_f = np.maximum(dist, 1).astype(np.float32)
    large = max_exact + (np.log(d_f / np.float32(max_exact)) / np.float32(math.log(MAX_DIST / max_exact))
                         * np.float32(N_BUCKETS - max_exact)).astype(np.int32)
    large = np.minimum(large, N_BUCKETS - 1)
    return np.where(dist < max_exact, dist, large).astype(np.int32)


def _dsw_bias_rows(rel_bias, seq):
    dist = np.arange(seq)
    mult = np.zeros(seq, np.float32)
    for window, dilation in DSW_PATTERNS:
        mult += ((dist % dilation == 0) & (dist <= window)).astype(np.float32)
    logm = np.where(mult > 0, np.log(np.maximum(mult, 1.0)), NEG).astype(np.float32)
    f = rel_bias.astype(F32)[_t5_bucket_np(dist)] + logm[:, None]
    n_heads = f.shape[1]
    p = seq + SEQ_BLOCK
    hv = jnp.concatenate([f[::-1].T, jnp.full((n_heads, SEQ_BLOCK), NEG, F32)], axis=1)
    t = jnp.broadcast_to(hv[:, None, :], (n_heads, SEQ_BLOCK, p)).reshape(n_heads, SEQ_BLOCK * p)
    t = t[:, :SEQ_BLOCK * (p - 1)].reshape(n_heads, SEQ_BLOCK, p - 1)
    return t[:, :, SEQ_BLOCK - 1:SEQ_BLOCK - 1 + seq]


def _mixer_b_kernel(q_ref, k_ref, v_ref, g_ref, tbl_ref, o_ref, kn_ref, vb_ref):
    s, hd = q_ref.shape
    B = SEQ_BLOCK
    nq = s // B
    kn_ref[...] = _rms_rows(k_ref[...], g_ref[1:2]).astype(BF16)
    vb_ref[...] = v_ref[...].astype(BF16)
    for qi in range(nq):
        n_keys = (qi + 1) * B
        rows = slice(qi * B, n_keys)
        qn = (_rms_rows(q_ref[rows, :], g_ref[0:1]) * (1.0 / math.sqrt(hd))).astype(BF16)
        sc = _dot_nt(qn, kn_ref[0:n_keys, :]) + tbl_ref[:, s - n_keys:]
        m = jnp.max(sc, axis=-1, keepdims=True)
        p = jnp.exp(sc - m)
        l = jnp.sum(p, axis=-1, keepdims=True)
        o = _dot(p.astype(BF16), vb_ref[0:n_keys, :])
        o_ref[rows, :] = (o / l).astype(o_ref.dtype)


def _mixer_b(proj, qk_g, rel_bias, col0, n_heads):
    bsz, s, _ = proj.shape
    tbl = _dsw_bias_rows(rel_bias, s)
    c0 = col0 // HEAD_B
    seq_spec = lambda off: pl.BlockSpec((None, s, HEAD_B), lambda h, b: (b, 0, off + h))
    return pl.pallas_call(
        _mixer_b_kernel,
        out_shape=jax.ShapeDtypeStruct((bsz, s, n_heads * HEAD_B), BF16),
        grid=(n_heads, bsz),
        in_specs=[
            seq_spec(c0), seq_spec(c0 + n_heads), seq_spec(c0 + 2 * n_heads),
            pl.BlockSpec((2, HEAD_B), lambda h, b: (0, 0)),
            pl.BlockSpec((None, SEQ_BLOCK, s), lambda h, b: (h, 0, 0)),
        ],
        out_specs=pl.BlockSpec((None, s, HEAD_B), lambda h, b: (b, 0, h)),
        scratch_shapes=[pltpu.VMEM((s, HEAD_B), BF16), pltpu.VMEM((s, HEAD_B), BF16)],
        compiler_params=_cparams(("parallel", "parallel")),
        name="mixer_b",
    )(proj, proj, proj, qk_g, tbl)


def _mixer_c_kernel(gb_ref, q_ref, k_ref, v_ref, og_ref, cwq_ref, cwk_ref, cbq_ref, cbk_ref,
                    ig_ref, fg_ref, gain_ref, o_ref, qc_ref, kc_ref, c_ref, n_ref, m_ref):
    h = pl.program_id(1)
    L = SEQ_BLOCK
    s, d = q_ref.shape
    nc = s // L

    qc_ref[...] = jax.nn.silu(_causal_conv(q_ref[...], cwq_ref[...], cbq_ref[...]))
    kc_ref[...] = jax.nn.silu(_causal_conv(k_ref[...], cwk_ref[...], cbk_ref[...])) * (d ** -0.5)
    c_ref[...] = jnp.zeros_like(c_ref)
    n_ref[...] = jnp.zeros_like(n_ref)
    m_ref[...] = jnp.zeros_like(m_ref)

    ri = lax.broadcasted_iota(jnp.int32, (L, L), 0)
    ci = lax.broadcasted_iota(jnp.int32, (L, L), 1)
    eye = ri == ci
    causal = ci <= ri

    def to_col(row):
        return jnp.sum(jnp.where(eye, row, 0.0), axis=1, keepdims=True)

    ig_all = ig_ref[...] + gb_ref[0, h]
    f_all = fg_ref[...] + gb_ref[1, h]
    ls_all = jnp.minimum(f_all, 0.0) - jnp.log1p(jnp.exp(-jnp.abs(f_all)))

    for n in range(nc):
        rows = slice(n * L, (n + 1) * L)
        qc = qc_ref[rows, :]
        kc = kc_ref[rows, :]
        qb = qc.astype(BF16)
        kb = kc.astype(BF16)
        vb = v_ref[rows, :].astype(BF16)

        ig_row = ig_all[n:n + 1]
        ls_row = ls_all[n:n + 1]
        ls_col = to_col(ls_row)
        ig_col = to_col(ig_row)
        b_col = jnp.sum(jnp.where(causal, ls_row, 0.0), axis=1, keepdims=True)
        b_row = jnp.sum(jnp.where(ri <= ci, ls_col, 0.0), axis=0, keepdims=True)
        b_last = jnp.sum(ls_row, axis=1, keepdims=True)

        log_d = b_col - b_row + ig_row
        g_row = b_last - b_row + ig_row
        g_col = b_last - b_col + ig_col
        g_max = jnp.max(g_row, axis=1, keepdims=True)
        d_max = jnp.max(jnp.where(causal, log_d, -jnp.inf), axis=1, keepdims=True)
        qk = _dot_nt(qb, kb)

        c_st = c_ref[...]
        n_st = n_ref[...]
        m_st = m_ref[...]

        inter = b_col + m_st
        m_t = jnp.maximum(inter, d_max)
        s_qk = jnp.where(causal, qk * jnp.exp(log_d - m_t), 0.0)
        inter_w = jnp.exp(inter - m_t)
        num = _dot(s_qk.astype(BF16), vb) + inter_w * _dot(qb, c_st.astype(BF16))
        qn = jnp.sum(qc * n_st, axis=1, keepdims=True)
        den = jnp.sum(s_qk, axis=1, keepdims=True) + inter_w * qn
        hout = num / jnp.maximum(jnp.abs(den), jnp.exp(-m_t))

        hn = _rms_rows(hout, gain_ref[...])
        o_ref[rows, :] = (jax.nn.sigmoid(og_ref[rows, :]) * hn).astype(o_ref.dtype)

        wk = jnp.exp(g_col - g_max) * kc
        c_loc = _dot_tn(wk.astype(BF16), vb)
        n_loc = jnp.sum(wk, axis=0, keepdims=True)
        m_new = jnp.maximum(b_last + m_st, g_max)
        decay = jnp.exp(b_last + m_st - m_new)
        fresh = jnp.exp(g_max - m_new)
        c_ref[...] = decay * c_st + fresh * c_loc
        n_ref[...] = decay * n_st + fresh * n_loc
        m_ref[...] = m_new


def _mixer_c(proj, gates, conv_w, conv_b, gate_b, h_gain, n_heads):
    bsz, s, _ = proj.shape
    d = HEAD_C
    L = SEQ_BLOCK
    nc = s // L
    g_rows = jnp.transpose(gates, (0, 2, 1)).reshape(bsz, 2 * n_heads, nc, L)
    cb = conv_b.reshape(1, -1)
    full = lambda off: pl.BlockSpec((None, s, d), lambda b, h: (b, 0, off + h))
    row = lambda r, off: pl.BlockSpec((r, d), lambda b, h: (0, off + h))
    grow = lambda off: pl.BlockSpec((None, None, nc, L), lambda b, h: (b, off + h, 0, 0))
    return pl.pallas_call(
        _mixer_c_kernel,
        out_shape=jax.ShapeDtypeStruct((bsz, s, n_heads * d), BF16),
        grid=(bsz, n_heads),
        in_specs=[
            pl.BlockSpec(memory_space=pltpu.SMEM),
            full(0), full(n_heads), full(2 * n_heads), full(3 * n_heads),
            row(CONV_W, 0), row(CONV_W, n_heads), row(1, 0), row(1, n_heads),
            grow(0), grow(n_heads), row(1, 0),
        ],
        out_specs=pl.BlockSpec((None, s, d), lambda b, h: (b, 0, h)),
        scratch_shapes=[pltpu.VMEM((s, d), F32), pltpu.VMEM((s, d), F32),
                        pltpu.VMEM((d, d), F32), pltpu.VMEM((1, d), F32), pltpu.VMEM((1, 1), F32)],
        compiler_params=_cparams(("parallel", "parallel")),
        name="mixer_c",
    )(gate_b, proj, proj, proj, proj, conv_w, conv_w, cb, cb, g_rows, g_rows,
      h_gain.reshape(1, -1))


def _mixer_d_kernel(q_ref, k_ref, v_ref, o_ref, kb_ref, vb_ref):
    s, hd = q_ref.shape
    B = SEQ_BLOCK
    nq = s // B
    kb_ref[...] = k_ref[...].astype(BF16)
    vb_ref[...] = v_ref[...].astype(BF16)
    ri = lax.broadcasted_iota(jnp.int32, (B, B), 0)
    ci = lax.broadcasted_iota(jnp.int32, (B, B), 1)
    strictly_before = ci < ri
    after = (ri > ci).astype(BF16)

    def span(qb, first, last, suffix, diag):
        n = last - first + 1
        keys = slice(first * B, (last + 1) * B)
        z = _dot_nt(qb, kb_ref[keys, :])
        sp = jnp.log(1.0 + jnp.exp2(jnp.abs(z) * (-LOG2E)))
        nm = jnp.minimum(z, 0.0)
        log_beta = nm - sp
        log_stay = (nm - z) - sp
        ls, later, tot = [], [], []
        for j in range(n):
            x = log_stay[:, j * B:(j + 1) * B]
            if diag and j == n - 1:
                x = jnp.where(strictly_before, x, 0.0)
            hi = lax.bitcast_convert_type(lax.bitcast_convert_type(x, jnp.uint32) & HI_MASK, F32)
            later.append(_dot(hi.astype(BF16), after) + _dot((x - hi).astype(BF16), after))
            tot.append(jnp.sum(x, axis=-1, keepdims=True))
            ls.append(x)
        att = [None] * n
        for j in range(n - 1, -1, -1):
            a = jnp.exp(log_beta[:, j * B:(j + 1) * B] + later[j] + suffix)
            if diag and j == n - 1:
                a = jnp.where(strictly_before, a, 0.0)
            att[j] = a.astype(BF16)
            suffix = suffix + tot[j]
        att = att[0] if n == 1 else jnp.concatenate(att, axis=1)
        return _dot(att, vb_ref[keys, :]), suffix

    for qi in range(nq):
        qb = (q_ref[qi * B:(qi + 1) * B, :] * (hd ** -0.5)).astype(BF16)
        first_near = max(qi - SB_NEAR_BLOCKS + 1, 0)
        acc, suffix = span(qb, first_near, qi, jnp.zeros((B, 1), F32), True)
        if first_near > 0:
            far = functools.partial(span, qb, 0, first_near - 1, suffix, False)
            acc = lax.cond(jnp.max(suffix) > SB_UNDERFLOW, lambda: acc + far()[0], lambda: acc)
        o_ref[qi * B:(qi + 1) * B, :] = acc.astype(o_ref.dtype)


def _mixer_d(proj, col0, n_heads):
    bsz, s, _ = proj.shape
    c0 = col0 // HEAD_D
    seq_spec = lambda off: pl.BlockSpec((None, s, HEAD_D), lambda b, h: (b, 0, off + h))
    return pl.pallas_call(
        _mixer_d_kernel,
        out_shape=jax.ShapeDtypeStruct((bsz, s, n_heads * HEAD_D), BF16),
        grid=(bsz, n_heads),
        in_specs=[seq_spec(c0), seq_spec(c0 + n_heads), seq_spec(c0 + 2 * n_heads)],
        out_specs=pl.BlockSpec((None, s, HEAD_D), lambda b, h: (b, 0, h)),
        scratch_shapes=[pltpu.VMEM((s, HEAD_D), BF16), pltpu.VMEM((s, HEAD_D), BF16)],
        compiler_params=_cparams(("parallel", "parallel")),
        name="mixer_d",
    )(proj, proj, proj)


def _layer_ab(x, bsz, s, g, w_in, conv_w, conv_b, w_r, b_r, w_i, b_i, lam, qk_g, rel_bias, w_out):
    d_rg = conv_w.shape[-1]
    n_heads = rel_bias.shape[-1]
    proj = _norm_matmul(x, g, w_in.astype(BF16)).reshape(bsz, s, -1)
    ya = _mixer_a(proj, conv_w, conv_b, w_r, b_r, w_i, b_i, lam, d_rg)
    yb = _mixer_b(proj, qk_g, rel_bias, 2 * d_rg, n_heads)
    n = bsz * s
    return _out_proj(x, ya.reshape(n, -1), yb.reshape(n, -1), w_out.astype(BF16))


def _layer_cd(x, bsz, s, g, w_in, conv_w, conv_b, gate_b, h_gain, w_out):
    n_heads_c = gate_b.shape[-1]
    dc = n_heads_c * HEAD_C
    d_mix = w_out.shape[0]
    dd = d_mix - dc
    n_heads_d = dd // HEAD_D
    n_gate = 2 * n_heads_c
    main = 4 * dc + 3 * dd
    width = -(-(main + n_gate) // 512) * 512
    w = jnp.concatenate([w_in[:, :4 * dc], w_in[:, 4 * dc + n_gate:], w_in[:, 4 * dc:4 * dc + n_gate],
                         jnp.zeros((w_in.shape[0], width - main - n_gate), w_in.dtype)], axis=1)
    proj = _norm_matmul(x, g, w.astype(BF16), tn=768).reshape(bsz, s, width)
    gates = proj[:, :, main:main + n_gate]
    yc = _mixer_c(proj, gates, conv_w, conv_b, gate_b, h_gain, n_heads_c)
    yd = _mixer_d(proj, 4 * dc, n_heads_d)
    n = bsz * s
    return _out_proj(x, yc.reshape(n, -1), yd.reshape(n, -1), w_out.astype(BF16))


def kernel(x, norm_g, ffn_w_in, ffn_w_out, ab_w_in, ab_conv_w, ab_conv_b, rg_w_r, rg_b_r, rg_w_i,
           rg_b_i, rg_lambda, qk_gain, rel_bias, ab_w_out, cd_w_in, cd_conv_w, cd_conv_b,
           mlstm_gate_bias, mlstm_h_gain, cd_w_out):
    bsz, s, d = x.shape
    depth = norm_g.shape[0]
    h = x.reshape(bsz * s, d)
    ffn_w_in = ffn_w_in.astype(BF16)
    ffn_w_out = ffn_w_out.astype(BF16)
    for layer in range(depth):
        j = layer // 2
        h = _ffn(h, norm_g[layer, 0], ffn_w_in, ffn_w_out, layer, 0)
        if layer % 2 == 0:
            h = _layer_ab(h, bsz, s, norm_g[layer, 1], ab_w_in[j], ab_conv_w[j], ab_conv_b[j],
                          rg_w_r[j], rg_b_r[j], rg_w_i[j], rg_b_i[j], rg_lambda[j], qk_gain[j],
                          rel_bias, ab_w_out[j])
        else:
            h = _layer_cd(h, bsz, s, norm_g[layer, 1], cd_w_in[j], cd_conv_w[j], cd_conv_b[j],
                          mlstm_gate_bias[j], mlstm_h_gain[j], cd_w_out[j])
        h = _ffn(h, norm_g[layer, 2], ffn_w_in, ffn_w_out, layer, 1)
    return h.reshape(bsz, s, d)
```

```python
import functools
import math

import numpy as np
import jax
import jax.numpy as jnp
from jax import lax
from jax.experimental import pallas as pl
from jax.experimental.pallas import tpu as pltpu

F32 = jnp.float32
BF16 = jnp.bfloat16

RMS_EPS = 1e-6
RG_C = 8.0
CONV_W = 4
RG_BLOCK = 128
HEAD_B = 128
HEAD_C = 256
HEAD_D = 128
SEQ_BLOCK = 128
DSW_PATTERNS = ((128, 1), (512, 4), (2048, 16))
N_BUCKETS = 32
MAX_DIST = 2048
NEG = -1e30
LOG2E = 1.4426950408889634
HI_MASK = np.uint32(0xFFFF0000)
SB_NEAR_BLOCKS = 3
SB_UNDERFLOW = -104.0
VMEM_LIMIT = 56 * 1024 * 1024


def _cparams(sem):
    return pltpu.CompilerParams(dimension_semantics=sem, vmem_limit_bytes=VMEM_LIMIT)


def _rms_rows(x, g):
    return x * lax.rsqrt(jnp.mean(x * x, axis=-1, keepdims=True) + RMS_EPS) * g


def _dot(a, b):
    return jnp.dot(a, b, preferred_element_type=F32)


def _dot_nt(a, b):
    return lax.dot_general(a, b, (((1,), (1,)), ((), ())), preferred_element_type=F32)


def _dot_tn(a, b):
    return lax.dot_general(a, b, (((0,), (0,)), ((), ())), preferred_element_type=F32)


def _pick_tile(n, pref):
    t = min(pref, n)
    while n % t:
        t //= 2
    return t


def _ffn_kernel(x_ref, g_ref, wg_ref, wu_ref, wo_ref, o_ref, xn_ref):
    j = pl.program_id(1)

    @pl.when(j == 0)
    def _():
        xn_ref[...] = _rms_rows(x_ref[...], g_ref[...]).astype(BF16)
        o_ref[...] = jnp.zeros_like(o_ref)

    xn = xn_ref[...]
    hg = _dot(xn, wg_ref[...])
    hu = _dot(xn, wu_ref[...])
    h = (hg * jax.nn.sigmoid(hg) * hu).astype(BF16)
    o_ref[...] += _dot(h, wo_ref[...])

    @pl.when(j == pl.num_programs(1) - 1)
    def _():
        o_ref[...] = x_ref[...] + 0.5 * o_ref[...]


def _ffn(x, g, w_in, w_out, layer, half, tm=1024, tf=512):
    n, d = x.shape
    d_ff = w_out.shape[2]
    tm = _pick_tile(n, tm)
    tf = _pick_tile(d_ff, tf)
    nj = d_ff // tf
    return pl.pallas_call(
        _ffn_kernel,
        out_shape=jax.ShapeDtypeStruct((n, d), F32),
        grid=(n // tm, nj),
        in_specs=[
            pl.BlockSpec((tm, d), lambda i, j: (i, 0)),
            pl.BlockSpec((1, d), lambda i, j: (0, 0)),
            pl.BlockSpec((None, None, d, tf), lambda i, j: (layer, half, 0, j)),
            pl.BlockSpec((None, None, d, tf), lambda i, j: (layer, half, 0, j + nj)),
            pl.BlockSpec((None, None, tf, d), lambda i, j: (layer, half, j, 0)),
        ],
        out_specs=pl.BlockSpec((tm, d), lambda i, j: (i, 0)),
        scratch_shapes=[pltpu.VMEM((tm, d), BF16)],
        compiler_params=_cparams(("parallel", "arbitrary")),
        name="ffn",
    )(x, g.reshape(1, d), w_in, w_in, w_out)


def _norm_matmul_kernel(x_ref, g_ref, w_ref, o_ref, xn_ref):
    @pl.when(pl.program_id(1) == 0)
    def _():
        xn_ref[...] = _rms_rows(x_ref[...], g_ref[...]).astype(BF16)

    o_ref[...] = _dot(xn_ref[...], w_ref[...])


def _norm_matmul(x, g, w, tm=1024, tn=1024):
    n, d = x.shape
    wn = w.shape[1]
    tm = _pick_tile(n, tm)
    while wn % tn:
        tn -= 128
    return pl.pallas_call(
        _norm_matmul_kernel,
        out_shape=jax.ShapeDtypeStruct((n, wn), F32),
        grid=(n // tm, wn // tn),
        in_specs=[
            pl.BlockSpec((tm, d), lambda i, j: (i, 0)),
            pl.BlockSpec((1, d), lambda i, j: (0, 0)),
            pl.BlockSpec((d, tn), lambda i, j: (0, j)),
        ],
        out_specs=pl.BlockSpec((tm, tn), lambda i, j: (i, j)),
        scratch_shapes=[pltpu.VMEM((tm, d), BF16)],
        compiler_params=_cparams(("parallel", "arbitrary")),
        name="norm_matmul",
    )(x, g.reshape(1, d), w)


def _out_proj_kernel(x_ref, ya_ref, yb_ref, wa_ref, wb_ref, o_ref):
    o_ref[...] = x_ref[...] + _dot(ya_ref[...], wa_ref[...]) + _dot(yb_ref[...], wb_ref[...])


def _out_proj(x, ya, yb, w, tm=512):
    n, d = x.shape
    da = ya.shape[1]
    db = yb.shape[1]
    assert da == db and w.shape[0] == da + db
    tm = _pick_tile(n, tm)
    return pl.pallas_call(
        _out_proj_kernel,
        out_shape=jax.ShapeDtypeStruct((n, d), F32),
        grid=(n // tm,),
        in_specs=[
            pl.BlockSpec((tm, d), lambda i: (i, 0)),
            pl.BlockSpec((tm, da), lambda i: (i, 0)),
            pl.BlockSpec((tm, db), lambda i: (i, 0)),
            pl.BlockSpec((da, d), lambda i: (0, 0)),
            pl.BlockSpec((db, d), lambda i: (1, 0)),
        ],
        out_specs=pl.BlockSpec((tm, d), lambda i: (i, 0)),
        compiler_params=_cparams(("parallel",)),
        name="out_proj",
    )(x, ya, yb, w, w)


def _causal_conv(x, w, b):
    t = lax.broadcasted_iota(jnp.int32, x.shape, 0)
    y = x * w[CONV_W - 1:CONV_W] + b
    for k in range(1, CONV_W):
        xs = jnp.where(t >= k, pltpu.roll(x, k, axis=0), 0.0)
        y = y + xs * w[CONV_W - 1 - k:CONV_W - k]
    return y


def _mixer_a_kernel(xr_ref, gate_ref, cw_ref, cb_ref, wr_ref, br_ref, wi_ref, bi_ref,
                    lam_ref, o_ref):
    s = xr_ref.shape[0]
    xc = _causal_conv(xr_ref[...], cw_ref[...], cb_ref[...])
    xb = xc.astype(BF16)
    r = jax.nn.sigmoid(_dot(xb, wr_ref[...]) + br_ref[...])
    i = jax.nn.sigmoid(_dot(xb, wi_ref[...]) + bi_ref[...])
    neg_lam = -lam_ref[...]
    softplus = jnp.maximum(neg_lam, 0.0) + jnp.log1p(jnp.exp(-jnp.abs(neg_lam)))
    log_a = -RG_C * r * softplus
    a = jnp.exp(log_a)
    u = jnp.sqrt(-jnp.tanh(log_a) * (a * a + 1.0)) * (i * xc)
    t = lax.broadcasted_iota(jnp.int32, a.shape, 0)
    k = 1
    while k < s:
        keep = t >= k
        a_sh = jnp.where(keep, pltpu.roll(a, k, axis=0), 1.0)
        u_sh = jnp.where(keep, pltpu.roll(u, k, axis=0), 0.0)
        u = a * u_sh + u
        a = a * a_sh
        k *= 2
    o_ref[...] = (jax.nn.gelu(gate_ref[...]) * u).astype(o_ref.dtype)


def _mixer_a(proj, conv_w, conv_b, w_r, b_r, w_i, b_i, lam, d_rg):
    bsz, s, _ = proj.shape
    nb = d_rg // RG_BLOCK
    vec = lambda v: v.reshape(1, d_rg)
    col = lambda off: pl.BlockSpec((None, s, RG_BLOCK), lambda b, g: (b, 0, off + g))
    row = lambda r: pl.BlockSpec((r, RG_BLOCK), lambda b, g: (0, g))
    blk = pl.BlockSpec((None, RG_BLOCK, RG_BLOCK), lambda b, g: (g, 0, 0))
    return pl.pallas_call(
        _mixer_a_kernel,
        out_shape=jax.ShapeDtypeStruct((bsz, s, d_rg), BF16),
        grid=(bsz, nb),
        in_specs=[col(nb), col(0), row(CONV_W), row(1), blk, row(1), blk, row(1), row(1)],
        out_specs=pl.BlockSpec((None, s, RG_BLOCK), lambda b, g: (b, 0, g)),
        compiler_params=_cparams(("parallel", "parallel")),
        name="mixer_a",
    )(proj, proj, conv_w, vec(conv_b), w_r.astype(BF16), vec(b_r), w_i.astype(BF16), vec(b_i),
      vec(lam))


def _t5_bucket_np(dist):
    max_exact = N_BUCKETS // 2
    d_f = np.maximum(dist, 1).astype(np.float32)
    large = max_exact + (np.log(d_f / np.float32(max_exact)) / np.float32(math.log(MAX_DIST / max_exact))
                         * np.float32(N_BUCKETS - max_exact)).astype(np.int32)
    large = np.minimum(large, N_BUCKETS - 1)
    return np.where(dist < max_exact, dist, large).astype(np.int32)


def _dsw_bias_rows(rel_bias, seq):
    dist = np.arange(seq)
    mult = np.zeros(seq, np.float32)
    for window, dilation in DSW_PATTERNS:
        mult += ((dist % dilation == 0) & (dist <= window)).astype(np.float32)
    logm = np.where(mult > 0, np.log(np.maximum(mult, 1.0)), NEG).astype(np.float32)
    f = rel_bias.astype(F32)[_t5_bucket_np(dist)] + logm[:, None]
    n_heads = f.shape[1]
    p = seq + SEQ_BLOCK
    hv = jnp.concatenate([f[::-1].T, jnp.full((n_heads, SEQ_BLOCK), NEG, F32)], axis=1)
    t = jnp.broadcast_to(hv[:, None, :], (n_heads, SEQ_BLOCK, p)).reshape(n_heads, SEQ_BLOCK * p)
    t = t[:, :SEQ_BLOCK * (p - 1)].reshape(n_heads, SEQ_BLOCK, p - 1)
    return t[:, :, SEQ_BLOCK - 1:SEQ_BLOCK - 1 + seq]


def _mixer_b_kernel(q_ref, k_ref, v_ref, g_ref, tbl_ref, o_ref, kn_ref, vb_ref):
    s, hd = q_ref.shape
    B = SEQ_BLOCK
    nq = s // B
    kn_ref[...] = _rms_rows(k_ref[...], g_ref[1:2]).astype(BF16)
    vb_ref[...] = v_ref[...].astype(BF16)
    for qi in range(nq):
        n_keys = (qi + 1) * B
        rows = slice(qi * B, n_keys)
        qn = (_rms_rows(q_ref[rows, :], g_ref[0:1]) * (1.0 / math.sqrt(hd))).astype(BF16)
        sc = _dot_nt(qn, kn_ref[0:n_keys, :]) + tbl_ref[:, s - n_keys:]
        m = jnp.max(sc, axis=-1, keepdims=True)
        p = jnp.exp(sc - m)
        l = jnp.sum(p, axis=-1, keepdims=True)
        o = _dot(p.astype(BF16), vb_ref[0:n_keys, :])
        o_ref[rows, :] = (o / l).astype(o_ref.dtype)


def _mixer_b(proj, qk_g, rel_bias, col0, n_heads):
    bsz, s, _ = proj.shape
    tbl = _dsw_bias_rows(rel_bias, s)
    c0 = col0 // HEAD_B
    seq_spec = lambda off: pl.BlockSpec((None, s, HEAD_B), lambda h, b: (b, 0, off + h))
    return pl.pallas_call(
        _mixer_b_kernel,
        out_shape=jax.ShapeDtypeStruct((bsz, s, n_heads * HEAD_B), BF16),
        grid=(n_heads, bsz),
        in_specs=[
            seq_spec(c0), seq_spec(c0 + n_heads), seq_spec(c0 + 2 * n_heads),
            pl.BlockSpec((2, HEAD_B), lambda h, b: (0, 0)),
            pl.BlockSpec((None, SEQ_BLOCK, s), lambda h, b: (h, 0, 0)),
        ],
        out_specs=pl.BlockSpec((None, s, HEAD_B), lambda h, b: (b, 0, h)),
        scratch_shapes=[pltpu.VMEM((s, HEAD_B), BF16), pltpu.VMEM((s, HEAD_B), BF16)],
        compiler_params=_cparams(("parallel", "parallel")),
        name="mixer_b",
    )(proj, proj, proj, qk_g, tbl)


def _mixer_c_kernel(gb_ref, q_ref, k_ref, v_ref, og_ref, cwq_ref, cwk_ref, cbq_ref, cbk_ref,
                    ig_ref, fg_ref, gain_ref, o_ref, qc_ref, kc_ref, c_ref, n_ref, m_ref):
    h = pl.program_id(1)
    L = SEQ_BLOCK
    s, d = q_ref.shape
    nc = s // L

    qc_ref[...] = jax.nn.silu(_causal_conv(q_ref[...], cwq_ref[...], cbq_ref[...]))
    kc_ref[...] = jax.nn.silu(_causal_conv(k_ref[...], cwk_ref[...], cbk_ref[...])) * (d ** -0.5)
    c_ref[...] = jnp.zeros_like(c_ref)
    n_ref[...] = jnp.zeros_like(n_ref)
    m_ref[...] = jnp.zeros_like(m_ref)

    ri = lax.broadcasted_iota(jnp.int32, (L, L), 0)
    ci = lax.broadcasted_iota(jnp.int32, (L, L), 1)
    eye = ri == ci
    causal = ci <= ri

    def to_col(row):
        return jnp.sum(jnp.where(eye, row, 0.0), axis=1, keepdims=True)

    ig_all = ig_ref[...] + gb_ref[0, h]
    f_all = fg_ref[...] + gb_ref[1, h]
    ls_all = jnp.minimum(f_all, 0.0) - jnp.log1p(jnp.exp(-jnp.abs(f_all)))

    for n in range(nc):
        rows = slice(n * L, (n + 1) * L)
        qc = qc_ref[rows, :]
        kc = kc_ref[rows, :]
        qb = qc.astype(BF16)
        kb = kc.astype(BF16)
        vb = v_ref[rows, :].astype(BF16)

        ig_row = ig_all[n:n + 1]
        ls_row = ls_all[n:n + 1]
        ls_col = to_col(ls_row)
        ig_col = to_col(ig_row)
        b_col = jnp.sum(jnp.where(causal, ls_row, 0.0), axis=1, keepdims=True)
        b_row = jnp.sum(jnp.where(ri <= ci, ls_col, 0.0), axis=0, keepdims=True)
        b_last = jnp.sum(ls_row, axis=1, keepdims=True)

        log_d = b_col - b_row + ig_row
        g_row = b_last - b_row + ig_row
        g_col = b_last - b_col + ig_col
        g_max = jnp.max(g_row, axis=1, keepdims=True)
        d_max = jnp.max(jnp.where(causal, log_d, -jnp.inf), axis=1, keepdims=True)
        qk = _dot_nt(qb, kb)

        c_st = c_ref[...]
        n_st = n_ref[...]
        m_st = m_ref[...]

        inter = b_col + m_st
        m_t = jnp.maximum(inter, d_max)
        s_qk = jnp.where(causal, qk * jnp.exp(log_d - m_t), 0.0)
        inter_w = jnp.exp(inter - m_t)
        num = _dot(s_qk.astype(BF16), vb) + inter_w * _dot(qb, c_st.astype(BF16))
        qn = jnp.sum(qc * n_st, axis=1, keepdims=True)
        den = jnp.sum(s_qk, axis=1, keepdims=True) + inter_w * qn
        hout = num / jnp.maximum(jnp.abs(den), jnp.exp(-m_t))

        hn = _rms_rows(hout, gain_ref[...])
        o_ref[rows, :] = (jax.nn.sigmoid(og_ref[rows, :]) * hn).astype(o_ref.dtype)

        wk = jnp.exp(g_col - g_max) * kc
        c_loc = _dot_tn(wk.astype(BF16), vb)
        n_loc = jnp.sum(wk, axis=0, keepdims=True)
        m_new = jnp.maximum(b_last + m_st, g_max)
        decay = jnp.exp(b_last + m_st - m_new)
        fresh = jnp.exp(g_max - m_new)
        c_ref[...] = decay * c_st + fresh * c_loc
        n_ref[...] = decay * n_st + fresh * n_loc
        m_ref[...] = m_new


def _mixer_c(proj, gates, conv_w, conv_b, gate_b, h_gain, n_heads):
    bsz, s, _ = proj.shape
    d = HEAD_C
    L = SEQ_BLOCK
    nc = s // L
    g_rows = jnp.transpose(gates, (0, 2, 1)).reshape(bsz, 2 * n_heads, nc, L)
    cb = conv_b.reshape(1, -1)
    full = lambda off: pl.BlockSpec((None, s, d), lambda b, h: (b, 0, off + h))
    row = lambda r, off: pl.BlockSpec((r, d), lambda b, h: (0, off + h))
    grow = lambda off: pl.BlockSpec((None, None, nc, L), lambda b, h: (b, off + h, 0, 0))
    return pl.pallas_call(
        _mixer_c_kernel,
        out_shape=jax.ShapeDtypeStruct((bsz, s, n_heads * d), BF16),
        grid=(bsz, n_heads),
        in_specs=[
            pl.BlockSpec(memory_space=pltpu.SMEM),
            full(0), full(n_heads), full(2 * n_heads), full(3 * n_heads),
            row(CONV_W, 0), row(CONV_W, n_heads), row(1, 0), row(1, n_heads),
            grow(0), grow(n_heads), row(1, 0),
        ],
        out_specs=pl.BlockSpec((None, s, d), lambda b, h: (b, 0, h)),
        scratch_shapes=[pltpu.VMEM((s, d), F32), pltpu.VMEM((s, d), F32),
                        pltpu.VMEM((d, d), F32), pltpu.VMEM((1, d), F32), pltpu.VMEM((1, 1), F32)],
        compiler_params=_cparams(("parallel", "parallel")),
        name="mixer_c",
    )(gate_b, proj, proj, proj, proj, conv_w, conv_w, cb, cb, g_rows, g_rows,
      h_gain.reshape(1, -1))


def _mixer_d_kernel(q_ref, k_ref, v_ref, o_ref, kb_ref, vb_ref, acc_ref, suf_ref):
    s, hd = q_ref.shape
    B = SEQ_BLOCK
    nq = s // B
    kb_ref[...] = k_ref[...].astype(BF16)
    vb_ref[...] = v_ref[...].astype(BF16)
    ri = lax.broadcasted_iota(jnp.int32, (B, B), 0)
    ci = lax.broadcasted_iota(jnp.int32, (B, B), 1)
    strictly_before = ci < ri
    after = (ri > ci).astype(BF16)

    def span(qb, first, last, suffix, diag):
        n = last - first + 1
        keys = slice(first * B, (last + 1) * B)
        z = _dot_nt(qb, kb_ref[keys, :])
        sp = jnp.log(1.0 + jnp.exp2(jnp.abs(z) * (-LOG2E)))
        nm = jnp.minimum(z, 0.0)
        log_beta = nm - sp
        log_stay = (nm - z) - sp
        ls, later, tot = [], [], []
        for j in range(n):
            x = log_stay[:, j * B:(j + 1) * B]
            if diag and j == n - 1:
                x = jnp.where(strictly_before, x, 0.0)
            hi = lax.bitcast_convert_type(lax.bitcast_convert_type(x, jnp.uint32) & HI_MASK, F32)
            later.append(_dot(hi.astype(BF16), after) + _dot((x - hi).astype(BF16), after))
            tot.append(jnp.sum(x, axis=-1, keepdims=True))
            ls.append(x)
        att = [None] * n
        for j in range(n - 1, -1, -1):
            a = jnp.exp(log_beta[:, j * B:(j + 1) * B] + later[j] + suffix)
            if diag and j == n - 1:
                a = jnp.where(strictly_before, a, 0.0)
            att[j] = a.astype(BF16)
            suffix = suffix + tot[j]
        att = att[0] if n == 1 else jnp.concatenate(att, axis=1)
        return _dot(att, vb_ref[keys, :]), suffix

    def q_block(qi):
        return (q_ref[qi * B:(qi + 1) * B, :] * (hd ** -0.5)).astype(BF16)

    worst = None
    for qi in range(nq):
        rows = slice(qi * B, (qi + 1) * B)
        first_near = max(qi - SB_NEAR_BLOCKS + 1, 0)
        acc, suffix = span(q_block(qi), first_near, qi, jnp.zeros((B, 1), F32), True)
        o_ref[rows, :] = acc.astype(o_ref.dtype)
        if first_near > 0:
            acc_ref[rows, :] = acc
            suf_ref[rows, :] = suffix
            worst = suffix if worst is None else jnp.maximum(worst, suffix)

    if worst is not None:
        @pl.when(jnp.max(worst) > SB_UNDERFLOW)
        def _():
            for qi in range(SB_NEAR_BLOCKS, nq):
                rows = slice(qi * B, (qi + 1) * B)
                suffix = suf_ref[rows, :]

                @pl.when(jnp.max(suffix) > SB_UNDERFLOW)
                def _():
                    far, _ = span(q_block(qi), 0, qi - SB_NEAR_BLOCKS, suffix, False)
                    o_ref[rows, :] = (acc_ref[rows, :] + far).astype(o_ref.dtype)


def _mixer_d(proj, col0, n_heads):
    bsz, s, _ = proj.shape
    c0 = col0 // HEAD_D
    seq_spec = lambda off: pl.BlockSpec((None, s, HEAD_D), lambda b, h: (b, 0, off + h))
    return pl.pallas_call(
        _mixer_d_kernel,
        out_shape=jax.ShapeDtypeStruct((bsz, s, n_heads * HEAD_D), BF16),
        grid=(bsz, n_heads),
        in_specs=[seq_spec(c0), seq_spec(c0 + n_heads), seq_spec(c0 + 2 * n_heads)],
        out_specs=pl.BlockSpec((None, s, HEAD_D), lambda b, h: (b, 0, h)),
        scratch_shapes=[pltpu.VMEM((s, HEAD_D), BF16), pltpu.VMEM((s, HEAD_D), BF16),
                        pltpu.VMEM((s, HEAD_D), F32), pltpu.VMEM((s, 1), F32)],
        compiler_params=_cparams(("parallel", "parallel")),
        name="mixer_d",
    )(proj, proj, proj)


def _layer_ab(x, bsz, s, g, w_in, conv_w, conv_b, w_r, b_r, w_i, b_i, lam, qk_g, rel_bias, w_out):
    d_rg = conv_w.shape[-1]
    n_heads = rel_bias.shape[-1]
    proj = _norm_matmul(x, g, w_in.astype(BF16)).reshape(bsz, s, -1)
    ya = _mixer_a(proj, conv_w, conv_b, w_r, b_r, w_i, b_i, lam, d_rg)
    yb = _mixer_b(proj, qk_g, rel_bias, 2 * d_rg, n_heads)
    n = bsz * s
    return _out_proj(x, ya.reshape(n, -1), yb.reshape(n, -1), w_out.astype(BF16))


def _layer_cd(x, bsz, s, g, w_in, conv_w, conv_b, gate_b, h_gain, w_out):
    n_heads_c = gate_b.shape[-1]
    dc = n_heads_c * HEAD_C
    d_mix = w_out.shape[0]
    dd = d_mix - dc
    n_heads_d = dd // HEAD_D
    n_gate = 2 * n_heads_c
    main = 4 * dc + 3 * dd
    width = -(-(main + n_gate) // 512) * 512
    w = jnp.concatenate([w_in[:, :4 * dc], w_in[:, 4 * dc + n_gate:], w_in[:, 4 * dc:4 * dc + n_gate],
                         jnp.zeros((w_in.shape[0], width - main - n_gate), w_in.dtype)], axis=1)
    proj = _norm_matmul(x, g, w.astype(BF16), tn=768).reshape(bsz, s, width)
    gates = proj[:, :, main:main + n_gate]
    yc = _mixer_c(proj, gates, conv_w, conv_b, gate_b, h_gain, n_heads_c)
    yd = _mixer_d(proj, 4 * dc, n_heads_d)
    n = bsz * s
    return _out_proj(x, yc.reshape(n, -1), yd.reshape(n, -1), w_out.astype(BF16))


def kernel(x, norm_g, ffn_w_in, ffn_w_out, ab_w_in, ab_conv_w, ab_conv_b, rg_w_r, rg_b_r, rg_w_i,
           rg_b_i, rg_lambda, qk_gain, rel_bias, ab_w_out, cd_w_in, cd_conv_w, cd_conv_b,
           mlstm_gate_bias, mlstm_h_gain, cd_w_out):
    bsz, s, d = x.shape
    depth = norm_g.shape[0]
    h = x.reshape(bsz * s, d)
    ffn_w_in = ffn_w_in.astype(BF16)
    ffn_w_out = ffn_w_out.astype(BF16)
    for layer in range(depth):
        j = layer // 2
        h = _ffn(h, norm_g[layer, 0], ffn_w_in, ffn_w_out, layer, 0)
        if layer % 2 == 0:
            h = _layer_ab(h, bsz, s, norm_g[layer, 1], ab_w_in[j], ab_conv_w[j], ab_conv_b[j],
                          rg_w_r[j], rg_b_r[j], rg_w_i[j], rg_b_i[j], rg_lambda[j], qk_gain[j],
                          rel_bias, ab_w_out[j])
        else:
            h = _layer_cd(h, bsz, s, norm_g[layer, 1], cd_w_in[j], cd_conv_w[j], cd_conv_b[j],
                          mlstm_gate_bias[j], mlstm_h_gain[j], cd_w_out[j])
        h = _ffn(h, norm_g[layer, 2], ffn_w_in, ffn_w_out, layer, 1)
    return h.reshape(bsz, s, d)
```

```python
import functools
import math

import numpy as np
import jax
import jax.numpy as jnp
from jax import lax
from jax.experimental import pallas as pl
from jax.experimental.pallas import tpu as pltpu

F32 = jnp.float32
BF16 = jnp.bfloat16

RMS_EPS = 1e-6
RG_C = 8.0
CONV_W = 4
RG_BLOCK = 128
HEAD_B = 128
HEAD_C = 256
HEAD_D = 128
SEQ_BLOCK = 128
SUBLANES = 8
BF16_SUBLANES = 16
DSW_PATTERNS = ((128, 1), (512, 4), (2048, 16))
N_BUCKETS = 32
MAX_DIST = 2048
NEG = -1e30
LOG2E = 1.4426950408889634
HI_MASK = np.uint32(0xFFFF0000)
SB_NEAR_BLOCKS = 3
SB_UNDERFLOW = -104.0
VMEM_LIMIT = 56 * 1024 * 1024


def _cparams(sem):
    return pltpu.CompilerParams(dimension_semantics=sem, vmem_limit_bytes=VMEM_LIMIT)


def _rms_rows(x, g):
    return x * lax.rsqrt(jnp.mean(x * x, axis=-1, keepdims=True) + RMS_EPS) * g


def _dot(a, b):
    return jnp.dot(a, b, preferred_element_type=F32)


def _dot_nt(a, b):
    return lax.dot_general(a, b, (((1,), (1,)), ((), ())), preferred_element_type=F32)


def _dot_tn(a, b):
    return lax.dot_general(a, b, (((0,), (0,)), ((), ())), preferred_element_type=F32)


def _pick_tile(n, pref):
    t = min(pref, n)
    while n % t:
        t //= 2
    return t


def _ffn_cast_jobs(w_in, w_out, layer, half, grid):
    n_steps = math.prod(grid)

    def step(*idx):
        t = idx[0]
        for i, g in zip(idx[1:], grid[1:]):
            t = t * g + i
        return t

    jobs = []
    for w in (w_in, w_out):
        rows, cols = w.shape[2:]
        n = n_steps
        while rows % n or (rows // n) % BF16_SUBLANES:
            n -= 1
        slab = rows // n
        blk = lambda *idx, n=n: jnp.minimum(step(*idx), n - 1)
        jobs.append((
            w,
            pl.BlockSpec((None, None, slab, cols), lambda *idx, blk=blk: (layer, half, blk(*idx), 0)),
            pl.BlockSpec((slab, cols), lambda *idx, blk=blk: (blk(*idx), 0)),
            jax.ShapeDtypeStruct((rows, cols), BF16),
        ))
    return jobs


def _call_with_casts(body, jobs, *, out_shape, grid, in_specs, out_specs, scratch_shapes=(),
                     semantics, name, operands):
    n_in = len(in_specs)
    n_jobs = len(jobs)

    def kernel(*refs):
        for i in range(n_jobs):
            dst = refs[n_in + n_jobs + 1 + i]
            dst[...] = refs[n_in + i][...].astype(dst.dtype)
        body(*refs[:n_in], refs[n_in + n_jobs], *refs[n_in + 2 * n_jobs + 1:])

    outs = pl.pallas_call(
        kernel,
        out_shape=[out_shape] + [j[3] for j in jobs],
        grid=grid,
        in_specs=list(in_specs) + [j[1] for j in jobs],
        out_specs=[out_specs] + [j[2] for j in jobs],
        scratch_shapes=list(scratch_shapes),
        compiler_params=_cparams(semantics),
        name=name,
    )(*operands, *[j[0] for j in jobs])
    return outs[0], list(outs[1:])


def _ffn_kernel(x_ref, g_ref, wg_ref, wu_ref, wo_ref, o_ref, xn_ref):
    j = pl.program_id(1)

    @pl.when(j == 0)
    def _():
        x = x_ref[...]
        xn_ref[...] = _rms_rows(x, g_ref[...]).astype(BF16)
        o_ref[...] = x

    xn = xn_ref[...]
    hg = _dot(xn, wg_ref[...])
    hu = _dot(xn, wu_ref[...])
    h = (0.5 * hg * jax.nn.sigmoid(hg) * hu).astype(BF16)
    o_ref[...] += _dot(h, wo_ref[...])


def _ffn(x, g, w_in, w_out, tm=1024, tf=512):
    n, d = x.shape
    d_ff = w_out.shape[0]
    tm = _pick_tile(n, tm)
    tf = _pick_tile(d_ff, tf)
    nj = d_ff // tf
    return pl.pallas_call(
        _ffn_kernel,
        out_shape=jax.ShapeDtypeStruct((n, d), F32),
        grid=(n // tm, nj),
        in_specs=[
            pl.BlockSpec((tm, d), lambda i, j: (i, 0)),
            pl.BlockSpec((1, d), lambda i, j: (0, 0)),
            pl.BlockSpec((d, tf), lambda i, j: (0, j)),
            pl.BlockSpec((d, tf), lambda i, j: (0, j + nj)),
            pl.BlockSpec((tf, d), lambda i, j: (j, 0)),
        ],
        out_specs=pl.BlockSpec((tm, d), lambda i, j: (i, 0)),
        scratch_shapes=[pltpu.VMEM((tm, d), BF16)],
        compiler_params=_cparams(("parallel", "arbitrary")),
        name="ffn",
    )(x, g.reshape(1, d), w_in, w_in, w_out)


def _norm_matmul_kernel(x_ref, g_ref, w_ref, o_ref, xn_ref):
    @pl.when(pl.program_id(1) == 0)
    def _():
        xn_ref[...] = _rms_rows(x_ref[...], g_ref[...]).astype(BF16)

    o_ref[...] = _dot(xn_ref[...], w_ref[...])


def _norm_matmul(x, g, w, tm=1024, tn=1024):
    n, d = x.shape
    wn = w.shape[1]
    tm = _pick_tile(n, tm)
    while wn % tn:
        tn -= 128
    return pl.pallas_call(
        _norm_matmul_kernel,
        out_shape=jax.ShapeDtypeStruct((n, wn), F32),
        grid=(n // tm, wn // tn),
        in_specs=[
            pl.BlockSpec((tm, d), lambda i, j: (i, 0)),
            pl.BlockSpec((1, d), lambda i, j: (0, 0)),
            pl.BlockSpec((d, tn), lambda i, j: (0, j)),
        ],
        out_specs=pl.BlockSpec((tm, tn), lambda i, j: (i, j)),
        scratch_shapes=[pltpu.VMEM((tm, d), BF16)],
        compiler_params=_cparams(("parallel", "arbitrary")),
        name="norm_matmul",
    )(x, g.reshape(1, d), w)


def _out_proj_kernel(x_ref, ya_ref, yb_ref, wa_ref, wb_ref, o_ref):
    o_ref[...] = x_ref[...] + _dot(ya_ref[...], wa_ref[...]) + _dot(yb_ref[...], wb_ref[...])


def _out_proj(x, ya, yb, w, tm=512):
    n, d = x.shape
    da = ya.shape[1]
    db = yb.shape[1]
    assert da == db and w.shape[0] == da + db
    tm = _pick_tile(n, tm)
    return pl.pallas_call(
        _out_proj_kernel,
        out_shape=jax.ShapeDtypeStruct((n, d), F32),
        grid=(n // tm,),
        in_specs=[
            pl.BlockSpec((tm, d), lambda i: (i, 0)),
            pl.BlockSpec((tm, da), lambda i: (i, 0)),
            pl.BlockSpec((tm, db), lambda i: (i, 0)),
            pl.BlockSpec((da, d), lambda i: (0, 0)),
            pl.BlockSpec((db, d), lambda i: (1, 0)),
        ],
        out_specs=pl.BlockSpec((tm, d), lambda i: (i, 0)),
        compiler_params=_cparams(("parallel",)),
        name="out_proj",
    )(x, ya, yb, w, w)


def _causal_conv(x, w, b):
    y = x * w[CONV_W - 1:CONV_W] + b
    for k in range(1, CONV_W):
        y = y + _shift_rows(x, k, 0.0) * w[CONV_W - 1 - k:CONV_W - k]
    return y


def _shift_rows(x, k, fill):
    assert 0 < k < SUBLANES
    rolled = pltpu.roll(x, k, axis=0)
    t = lax.broadcasted_iota(jnp.int32, (SUBLANES, x.shape[1]), 0)
    head = jnp.where(t >= k, rolled[:SUBLANES], fill)
    return jnp.concatenate([head, rolled[SUBLANES:]], axis=0)


def _mixer_a_kernel(xr_ref, gate_ref, cw_ref, cb_ref, wr_ref, br_ref, wi_ref, bi_ref,
                    lam_ref, o_ref):
    s = xr_ref.shape[0]
    xc = _causal_conv(xr_ref[...], cw_ref[...], cb_ref[...])
    xb = xc.astype(BF16)
    r = jax.nn.sigmoid(_dot(xb, wr_ref[...]) + br_ref[...])
    i = jax.nn.sigmoid(_dot(xb, wi_ref[...]) + bi_ref[...])
    neg_lam = -lam_ref[...]
    softplus = jnp.maximum(neg_lam, 0.0) + jnp.log1p(jnp.exp(-jnp.abs(neg_lam)))
    log_a = -RG_C * r * softplus
    a = jnp.exp(log_a)
    u = jnp.sqrt(-jnp.tanh(log_a) * (a * a + 1.0)) * (i * xc)
    k = 1
    while k < s:
        if k < SUBLANES:
            u = a * _shift_rows(u, k, 0.0) + u
            a = a * _shift_rows(a, k, 1.0)
        else:
            u = jnp.concatenate([u[:k], a[k:] * u[:s - k] + u[k:]], axis=0)
            a = jnp.concatenate([a[:k], a[k:] * a[:s - k]], axis=0)
        k *= 2
    o_ref[...] = (jax.nn.gelu(gate_ref[...]) * u).astype(o_ref.dtype)


def _mixer_a(proj, conv_w, conv_b, w_r, b_r, w_i, b_i, lam, d_rg, cast_jobs):
    bsz, s, _ = proj.shape
    nb = d_rg // RG_BLOCK
    grid = (bsz, nb)
    vec = lambda v: v.reshape(1, d_rg)
    col = lambda off: pl.BlockSpec((None, s, RG_BLOCK), lambda b, g: (b, 0, off + g))
    row = lambda r: pl.BlockSpec((r, RG_BLOCK), lambda b, g: (0, g))
    blk = pl.BlockSpec((None, RG_BLOCK, RG_BLOCK), lambda b, g: (g, 0, 0))
    return _call_with_casts(
        _mixer_a_kernel, cast_jobs(grid),
        out_shape=jax.ShapeDtypeStruct((bsz, s, d_rg), BF16),
        grid=grid,
        in_specs=[col(nb), col(0), row(CONV_W), row(1), blk, row(1), blk, row(1), row(1)],
        out_specs=pl.BlockSpec((None, s, RG_BLOCK), lambda b, g: (b, 0, g)),
        semantics=("arbitrary", "arbitrary"),
        name="mixer_a",
        operands=(proj, proj, conv_w, vec(conv_b), w_r.astype(BF16), vec(b_r), w_i.astype(BF16),
                  vec(b_i), vec(lam)))


def _t5_bucket_np(dist):
    max_exact = N_BUCKETS // 2
    d_f = np.maximum(dist, 1).astype(np.float32)
    large = max_exact + (np.log(d_f / np.float32(max_exact)) / np.float32(math.log(MAX_DIST / max_exact))
                         * np.float32(N_BUCKETS - max_exact)).astype(np.int32)
    large = np.minimum(large, N_BUCKETS - 1)
    return np.where(dist < max_exact, dist, large).astype(np.int32)


def _dsw_bias_rows(rel_bias, seq):
    dist = np.arange(seq)
    mult = np.zeros(seq, np.float32)
    for window, dilation in DSW_PATTERNS:
        mult += ((dist % dilation == 0) & (dist <= window)).astype(np.float32)
    logm = np.where(mult > 0, np.log(np.maximum(mult, 1.0)), NEG).astype(np.float32)
    f = (rel_bias.astype(F32)[_t5_bucket_np(dist)] + logm[:, None]) * LOG2E
    n_heads = f.shape[1]
    p = seq + SEQ_BLOCK
    hv = jnp.concatenate([f[::-1].T, jnp.full((n_heads, SEQ_BLOCK), NEG, F32)], axis=1)
    t = jnp.broadcast_to(hv[:, None, :], (n_heads, SEQ_BLOCK, p)).reshape(n_heads, SEQ_BLOCK * p)
    t = t[:, :SEQ_BLOCK * (p - 1)].reshape(n_heads, SEQ_BLOCK, p - 1)
    return t[:, :, SEQ_BLOCK - 1:SEQ_BLOCK - 1 + seq]


def _mixer_b_kernel(q_ref, k_ref, v_ref, g_ref, tbl_ref, o_ref, kn_ref, vb_ref):
    s, hd = q_ref.shape
    B = SEQ_BLOCK
    nq = s // B
    kn_ref[...] = _rms_rows(k_ref[...], g_ref[1:2]).astype(BF16)
    vb_ref[...] = v_ref[...].astype(BF16)
    for qi in range(nq):
        n_keys = (qi + 1) * B
        rows = slice(qi * B, n_keys)
        qn = (_rms_rows(q_ref[rows, :], g_ref[0:1]) * (LOG2E / math.sqrt(hd))).astype(BF16)
        sc = _dot_nt(qn, kn_ref[0:n_keys, :]) + tbl_ref[:, s - n_keys:]
        m = jnp.max(sc, axis=-1, keepdims=True)
        p = jnp.exp2(sc - m)
        l = jnp.sum(p, axis=-1, keepdims=True)
        o = _dot(p.astype(BF16), vb_ref[0:n_keys, :])
        o_ref[rows, :] = (o / l).astype(o_ref.dtype)


def _mixer_b(proj, qk_g, rel_bias, col0, n_heads, cast_jobs):
    bsz, s, _ = proj.shape
    tbl = _dsw_bias_rows(rel_bias, s)
    c0 = col0 // HEAD_B
    grid = (n_heads, bsz)
    seq_spec = lambda off: pl.BlockSpec((None, s, HEAD_B), lambda h, b: (b, 0, off + h))
    return _call_with_casts(
        _mixer_b_kernel, cast_jobs(grid),
        out_shape=jax.ShapeDtypeStruct((bsz, s, n_heads * HEAD_B), BF16),
        grid=grid,
        in_specs=[
            seq_spec(c0), seq_spec(c0 + n_heads), seq_spec(c0 + 2 * n_heads),
            pl.BlockSpec((2, HEAD_B), lambda h, b: (0, 0)),
            pl.BlockSpec((None, SEQ_BLOCK, s), lambda h, b: (h, 0, 0)),
        ],
        out_specs=pl.BlockSpec((None, s, HEAD_B), lambda h, b: (b, 0, h)),
        scratch_shapes=[pltpu.VMEM((s, HEAD_B), BF16), pltpu.VMEM((s, HEAD_B), BF16)],
        semantics=("arbitrary", "arbitrary"),
        name="mixer_b",
        operands=(proj, proj, proj, qk_g, tbl))


def _mixer_c_kernel(gb_ref, q_ref, k_ref, v_ref, og_ref, cwq_ref, cwk_ref, cbq_ref, cbk_ref,
                    ig_ref, fg_ref, gain_ref, o_ref, qc_ref, kc_ref, c_ref, n_ref, m_ref):
    h = pl.program_id(1)
    L = SEQ_BLOCK
    s, d = q_ref.shape
    nc = s // L

    qc_ref[...] = jax.nn.silu(_causal_conv(q_ref[...], cwq_ref[...], cbq_ref[...]))
    kc_ref[...] = jax.nn.silu(_causal_conv(k_ref[...], cwk_ref[...], cbk_ref[...])) * (d ** -0.5)
    c_ref[...] = jnp.zeros_like(c_ref)
    n_ref[...] = jnp.zeros_like(n_ref)
    m_ref[...] = jnp.zeros_like(m_ref)

    ri = lax.broadcasted_iota(jnp.int32, (L, L), 0)
    ci = lax.broadcasted_iota(jnp.int32, (L, L), 1)
    eye = ri == ci
    causal = ci <= ri

    def to_col(row):
        return jnp.sum(jnp.where(eye, row, 0.0), axis=1, keepdims=True)

    ig_all = ig_ref[...] + gb_ref[0, h]
    f_all = fg_ref[...] + gb_ref[1, h]
    ls_all = jnp.minimum(f_all, 0.0) - jnp.log1p(jnp.exp(-jnp.abs(f_all)))

    for n in range(nc):
        rows = slice(n * L, (n + 1) * L)
        qc = qc_ref[rows, :]
        kc = kc_ref[rows, :]
        qb = qc.astype(BF16)
        kb = kc.astype(BF16)
        vb = v_ref[rows, :].astype(BF16)

        ig_row = ig_all[n:n + 1]
        ls_row = ls_all[n:n + 1]
        ls_col = to_col(ls_row)
        ig_col = to_col(ig_row)
        b_col = jnp.sum(jnp.where(causal, ls_row, 0.0), axis=1, keepdims=True)
        b_row = jnp.sum(jnp.where(ri <= ci, ls_col, 0.0), axis=0, keepdims=True)
        b_last = jnp.sum(ls_row, axis=1, keepdims=True)

        log_d = b_col - b_row + ig_row
        g_row = b_last - b_row + ig_row
        g_col = b_last - b_col + ig_col
        g_max = jnp.max(g_row, axis=1, keepdims=True)
        d_max = jnp.max(jnp.where(causal, log_d, -jnp.inf), axis=1, keepdims=True)
        qk = _dot_nt(qb, kb)

        c_st = c_ref[...]
        n_st = n_ref[...]
        m_st = m_ref[...]

        inter = b_col + m_st
        m_t = jnp.maximum(inter, d_max)
        s_qk = jnp.where(causal, qk * jnp.exp(log_d - m_t), 0.0)
        inter_w = jnp.exp(inter - m_t)
        num = _dot(s_qk.astype(BF16), vb) + inter_w * _dot(qb, c_st.astype(BF16))
        qn = jnp.sum(qc * n_st, axis=1, keepdims=True)
        den = jnp.sum(s_qk, axis=1, keepdims=True) + inter_w * qn
        hout = num / jnp.maximum(jnp.abs(den), jnp.exp(-m_t))

        hn = _rms_rows(hout, gain_ref[...])
        o_ref[rows, :] = (jax.nn.sigmoid(og_ref[rows, :]) * hn).astype(o_ref.dtype)

        wk = jnp.exp(g_col - g_max) * kc
        c_loc = _dot_tn(wk.astype(BF16), vb)
        n_loc = jnp.sum(wk, axis=0, keepdims=True)
        m_new = jnp.maximum(b_last + m_st, g_max)
        decay = jnp.exp(b_last + m_st - m_new)
        fresh = jnp.exp(g_max - m_new)
        c_ref[...] = decay * c_st + fresh * c_loc
        n_ref[...] = decay * n_st + fresh * n_loc
        m_ref[...] = m_new


def _mixer_c(proj, gates, conv_w, conv_b, gate_b, h_gain, n_heads, cast_jobs):
    bsz, s, _ = proj.shape
    d = HEAD_C
    L = SEQ_BLOCK
    nc = s // L
    grid = (bsz, n_heads)
    g_rows = jnp.transpose(gates, (0, 2, 1)).reshape(bsz, 2 * n_heads, nc, L)
    cb = conv_b.reshape(1, -1)
    full = lambda off: pl.BlockSpec((None, s, d), lambda b, h: (b, 0, off + h))
    row = lambda r, off: pl.BlockSpec((r, d), lambda b, h: (0, off + h))
    grow = lambda off: pl.BlockSpec((None, None, nc, L), lambda b, h: (b, off + h, 0, 0))
    return _call_with_casts(
        _mixer_c_kernel, cast_jobs(grid),
        out_shape=jax.ShapeDtypeStruct((bsz, s, n_heads * d), BF16),
        grid=grid,
        in_specs=[
            pl.BlockSpec(memory_space=pltpu.SMEM),
            full(0), full(n_heads), full(2 * n_heads), full(3 * n_heads),
            row(CONV_W, 0), row(CONV_W, n_heads), row(1, 0), row(1, n_heads),
            grow(0), grow(n_heads), row(1, 0),
        ],
        out_specs=pl.BlockSpec((None, s, d), lambda b, h: (b, 0, h)),
        scratch_shapes=[pltpu.VMEM((s, d), F32), pltpu.VMEM((s, d), F32),
                        pltpu.VMEM((d, d), F32), pltpu.VMEM((1, d), F32), pltpu.VMEM((1, 1), F32)],
        semantics=("arbitrary", "arbitrary"),
        name="mixer_c",
        operands=(gate_b, proj, proj, proj, proj, conv_w, conv_w, cb, cb, g_rows, g_rows,
                  h_gain.reshape(1, -1)))


def _mixer_d_kernel(q_ref, k_ref, v_ref, o_ref, kb_ref, vb_ref, acc_ref, suf_ref):
    s, hd = q_ref.shape
    B = SEQ_BLOCK
    nq = s // B
    kb_ref[...] = k_ref[...].astype(BF16)
    vb_ref[...] = v_ref[...].astype(BF16)
    ri = lax.broadcasted_iota(jnp.int32, (B, B), 0)
    ci = lax.broadcasted_iota(jnp.int32, (B, B), 1)
    strictly_before = ci < ri
    after = (ri > ci).astype(BF16)

    def span(qb, first, last, suffix, diag):
        n = last - first + 1
        keys = slice(first * B, (last + 1) * B)
        z = _dot_nt(qb, kb_ref[keys, :])
        sp = jnp.log(1.0 + jnp.exp2(jnp.abs(z) * (-LOG2E)))
        nm = jnp.minimum(z, 0.0)
        log_beta = nm - sp
        log_stay = (nm - z) - sp
        later, tot = [], []
        for j in range(n):
            x = log_stay[:, j * B:(j + 1) * B]
            if diag and j == n - 1:
                x = jnp.where(strictly_before, x, 0.0)
            hi = lax.bitcast_convert_type(lax.bitcast_convert_type(x, jnp.uint32) & HI_MASK, F32)
            later.append(_dot(hi.astype(BF16), after) + _dot((x - hi).astype(BF16), after))
            tot.append(jnp.sum(x, axis=-1, keepdims=True))
        att = [None] * n
        for j in range(n - 1, -1, -1):
            a = jnp.exp(log_beta[:, j * B:(j + 1) * B] + later[j] + suffix)
            if diag and j == n - 1:
                a = jnp.where(strictly_before, a, 0.0)
            att[j] = a.astype(BF16)
            suffix = suffix + tot[j]
        att = att[0] if n == 1 else jnp.concatenate(att, axis=1)
        return _dot(att, vb_ref[keys, :]), suffix

    def q_block(qi):
        return (q_ref[qi * B:(qi + 1) * B, :] * (hd ** -0.5)).astype(BF16)

    worst = None
    for qi in range(nq):
        rows = slice(qi * B, (qi + 1) * B)
        first_near = max(qi - SB_NEAR_BLOCKS + 1, 0)
        acc, suffix = span(q_block(qi), first_near, qi, jnp.zeros((B, 1), F32), True)
        o_ref[rows, :] = acc.astype(o_ref.dtype)
        if first_near > 0:
            acc_ref[rows, :] = acc
            suf_ref[rows, :] = suffix
            worst = suffix if worst is None else jnp.maximum(worst, suffix)

    if worst is not None:
        @pl.when(jnp.max(worst) > SB_UNDERFLOW)
        def _():
            for qi in range(SB_NEAR_BLOCKS, nq):
                rows = slice(qi * B, (qi + 1) * B)
                suffix = suf_ref[rows, :]

                @pl.when(jnp.max(suffix) > SB_UNDERFLOW)
                def _():
                    far, _ = span(q_block(qi), 0, qi - SB_NEAR_BLOCKS, suffix, False)
                    o_ref[rows, :] = (acc_ref[rows, :] + far).astype(o_ref.dtype)


def _mixer_d(proj, col0, n_heads, cast_jobs):
    bsz, s, _ = proj.shape
    c0 = col0 // HEAD_D
    grid = (bsz, n_heads)
    seq_spec = lambda off: pl.BlockSpec((None, s, HEAD_D), lambda b, h: (b, 0, off + h))
    return _call_with_casts(
        _mixer_d_kernel, cast_jobs(grid),
        out_shape=jax.ShapeDtypeStruct((bsz, s, n_heads * HEAD_D), BF16),
        grid=grid,
        in_specs=[seq_spec(c0), seq_spec(c0 + n_heads), seq_spec(c0 + 2 * n_heads)],
        out_specs=pl.BlockSpec((None, s, HEAD_D), lambda b, h: (b, 0, h)),
        scratch_shapes=[pltpu.VMEM((s, HEAD_D), BF16), pltpu.VMEM((s, HEAD_D), BF16),
                        pltpu.VMEM((s, HEAD_D), F32), pltpu.VMEM((s, 1), F32)],
        semantics=("arbitrary", "arbitrary"),
        name="mixer_d",
        operands=(proj, proj, proj))


def _layer_ab(x, bsz, s, g, w_in, conv_w, conv_b, w_r, b_r, w_i, b_i, lam, qk_g, rel_bias, w_out,
              casts):
    d_rg = conv_w.shape[-1]
    n_heads = rel_bias.shape[-1]
    proj = _norm_matmul(x, g, w_in.astype(BF16)).reshape(bsz, s, -1)
    ya, cast_a = _mixer_a(proj, conv_w, conv_b, w_r, b_r, w_i, b_i, lam, d_rg, casts[0])
    yb, cast_b = _mixer_b(proj, qk_g, rel_bias, 2 * d_rg, n_heads, casts[1])
    n = bsz * s
    return _out_proj(x, ya.reshape(n, -1), yb.reshape(n, -1), w_out.astype(BF16)), cast_a + cast_b


def _cd_weight_kernel(a_ref, b_ref, o_ref, *, n_aligned, n_shifted, n_gate):
    c = pl.program_id(0)
    w = o_ref.shape[1]
    lane = lax.broadcasted_iota(jnp.int32, o_ref.shape, 1)

    @pl.when(c < n_aligned)
    def _():
        o_ref[...] = a_ref[...].astype(o_ref.dtype)

    @pl.when((c >= n_aligned) & (c < n_aligned + n_shifted))
    def _():
        a = pltpu.roll(a_ref[...], w - n_gate, axis=1)
        b = pltpu.roll(b_ref[...], w - n_gate, axis=1)
        o_ref[...] = jnp.where(lane < w - n_gate, a, b).astype(o_ref.dtype)

    @pl.when(c >= n_aligned + n_shifted)
    def _():
        o_ref[...] = jnp.where(lane < n_gate, a_ref[...], 0.0).astype(o_ref.dtype)


def _cd_weight(w_in, j, gate_col, n_gate, width, tw=512):
    _, d, cols = w_in.shape
    main = cols - n_gate
    assert gate_col % tw == 0 and (main - gate_col) % tw == 0 and width == main + tw
    n_aligned = gate_col // tw
    n_shifted = (main - gate_col) // tw
    last = (cols - 1) // tw
    return pl.pallas_call(
        functools.partial(_cd_weight_kernel, n_aligned=n_aligned, n_shifted=n_shifted, n_gate=n_gate),
        out_shape=jax.ShapeDtypeStruct((d, width), BF16),
        grid=(width // tw,),
        in_specs=[
            pl.BlockSpec((None, d, tw),
                         lambda c: (j, 0, jnp.where(c < n_aligned + n_shifted, c, n_aligned))),
            pl.BlockSpec((None, d, tw), lambda c: (j, 0, jnp.minimum(c + 1, last))),
        ],
        out_specs=pl.BlockSpec((d, tw), lambda c: (0, c)),
        compiler_params=_cparams(("arbitrary",)),
        name="cd_weight",
    )(w_in, w_in)


def _layer_cd(x, bsz, s, g, w_in_all, j, conv_w, conv_b, gate_b, h_gain, w_out, casts):
    n_heads_c = gate_b.shape[-1]
    dc = n_heads_c * HEAD_C
    d_mix = w_out.shape[0]
    dd = d_mix - dc
    n_heads_d = dd // HEAD_D
    n_gate = 2 * n_heads_c
    main = 4 * dc + 3 * dd
    width = -(-(main + n_gate) // 512) * 512
    w = _cd_weight(w_in_all, j, 4 * dc, n_gate, width)
    proj = _norm_matmul(x, g, w, tn=768).reshape(bsz, s, width)
    gates = proj[:, :, main:main + n_gate]
    yd, cast_d = _mixer_d(proj, 4 * dc, n_heads_d, casts[0])
    yc, cast_c = _mixer_c(proj, gates, conv_w, conv_b, gate_b, h_gain, n_heads_c, casts[1])
    n = bsz * s
    return _out_proj(x, yc.reshape(n, -1), yd.reshape(n, -1), w_out.astype(BF16)), cast_d + cast_c


def kernel(x, norm_g, ffn_w_in, ffn_w_out, ab_w_in, ab_conv_w, ab_conv_b, rg_w_r, rg_b_r, rg_w_i,
           rg_b_i, rg_lambda, qk_gain, rel_bias, ab_w_out, cd_w_in, cd_conv_w, cd_conv_b,
           mlstm_gate_bias, mlstm_h_gain, cd_w_out):
    bsz, s, d = x.shape
    depth = norm_g.shape[0]
    h = x.reshape(bsz * s, d)
    ffn_w = {(0, 0): (ffn_w_in[0, 0].astype(BF16), ffn_w_out[0, 0].astype(BF16))}
    for layer in range(depth):
        j = layer // 2
        h = _ffn(h, norm_g[layer, 0], *ffn_w[(layer, 0)])
        targets = [(layer, 1)] + ([(layer + 1, 0)] if layer + 1 < depth else [])
        casts = [functools.partial(_ffn_cast_jobs, ffn_w_in, ffn_w_out, *t) for t in targets]
        casts += [lambda grid: []] * (2 - len(casts))
        if layer % 2 == 0:
            h, cast = _layer_ab(h, bsz, s, norm_g[layer, 1], ab_w_in[j], ab_conv_w[j], ab_conv_b[j],
                                rg_w_r[j], rg_b_r[j], rg_w_i[j], rg_b_i[j], rg_lambda[j], qk_gain[j],
                                rel_bias, ab_w_out[j], casts)
        else:
            h, cast = _layer_cd(h, bsz, s, norm_g[layer, 1], cd_w_in, j, cd_conv_w[j], cd_conv_b[j],
                                mlstm_gate_bias[j], mlstm_h_gain[j], cd_w_out[j], casts)
        for i, t in enumerate(targets):
            ffn_w[t] = (cast[2 * i], cast[2 * i + 1])
        h = _ffn(h, norm_g[layer, 2], *ffn_w[(layer, 1)])
    return h.reshape(bsz, s, d)
```

```python
import functools
import math

import numpy as np
import jax
import jax.numpy as jnp
from jax import lax
from jax.experimental import pallas as pl
from jax.experimental.pallas import tpu as pltpu

F32 = jnp.float32
BF16 = jnp.bfloat16

RMS_EPS = 1e-6
RG_C = 8.0
CONV_W = 4
RG_BLOCK = 128
HEAD_B = 128
HEAD_C = 256
HEAD_D = 128
SEQ_BLOCK = 128
SUBLANES = 8
BF16_SUBLANES = 16
DSW_PATTERNS = ((128, 1), (512, 4), (2048, 16))
N_BUCKETS = 32
MAX_DIST = 2048
NEG = -1e30
LOG2E = 1.4426950408889634
HI_MASK = np.uint32(0xFFFF0000)
SB_NEAR_BLOCKS = 3
SB_QUERY_ROWS = 256
SB_UNDERFLOW = -104.0
VMEM_LIMIT = 56 * 1024 * 1024


def _cparams(sem):
    return pltpu.CompilerParams(dimension_semantics=sem, vmem_limit_bytes=VMEM_LIMIT)


def _rms_rows(x, g):
    return x * lax.rsqrt(jnp.mean(x * x, axis=-1, keepdims=True) + RMS_EPS) * g


def _dot(a, b):
    return jnp.dot(a, b, preferred_element_type=F32)


def _dot_nt(a, b):
    return lax.dot_general(a, b, (((1,), (1,)), ((), ())), preferred_element_type=F32)


def _dot_tn(a, b):
    return lax.dot_general(a, b, (((0,), (0,)), ((), ())), preferred_element_type=F32)


def _pick_tile(n, pref):
    t = min(pref, n)
    while n % t:
        t //= 2
    return t


def _ffn_cast_jobs(w_in, w_out, layer, half, grid):
    n_steps = math.prod(grid)

    def step(*idx):
        t = idx[0]
        for i, g in zip(idx[1:], grid[1:]):
            t = t * g + i
        return t

    jobs = []
    for w in (w_in, w_out):
        rows, cols = w.shape[2:]
        n = n_steps
        while rows % n or (rows // n) % BF16_SUBLANES:
            n -= 1
        slab = rows // n
        blk = lambda *idx, n=n: jnp.minimum(step(*idx), n - 1)
        jobs.append((
            w,
            pl.BlockSpec((None, None, slab, cols), lambda *idx, blk=blk: (layer, half, blk(*idx), 0)),
            pl.BlockSpec((slab, cols), lambda *idx, blk=blk: (blk(*idx), 0)),
            jax.ShapeDtypeStruct((rows, cols), BF16),
        ))
    return jobs


def _call_with_casts(body, jobs, *, out_shape, grid, in_specs, out_specs, scratch_shapes=(),
                     semantics, name, operands):
    n_in = len(in_specs)
    n_jobs = len(jobs)

    def kernel(*refs):
        for i in range(n_jobs):
            dst = refs[n_in + n_jobs + 1 + i]
            dst[...] = refs[n_in + i][...].astype(dst.dtype)
        body(*refs[:n_in], refs[n_in + n_jobs], *refs[n_in + 2 * n_jobs + 1:])

    outs = pl.pallas_call(
        kernel,
        out_shape=[out_shape] + [j[3] for j in jobs],
        grid=grid,
        in_specs=list(in_specs) + [j[1] for j in jobs],
        out_specs=[out_specs] + [j[2] for j in jobs],
        scratch_shapes=list(scratch_shapes),
        compiler_params=_cparams(semantics),
        name=name,
    )(*operands, *[j[0] for j in jobs])
    return outs[0], list(outs[1:])


def _ffn_kernel(x_ref, g_ref, wg_ref, wu_ref, wo_ref, o_ref, xn_ref):
    j = pl.program_id(1)

    @pl.when(j == 0)
    def _():
        x = x_ref[...]
        xn_ref[...] = _rms_rows(x, g_ref[...]).astype(BF16)
        o_ref[...] = x

    xn = xn_ref[...]
    hg = _dot(xn, wg_ref[...])
    hu = _dot(xn, wu_ref[...])
    h = (0.5 * hg * jax.nn.sigmoid(hg) * hu).astype(BF16)
    o_ref[...] += _dot(h, wo_ref[...])


def _ffn(x, g, w_in, w_out, tm=1024, tf=512):
    n, d = x.shape
    d_ff = w_out.shape[0]
    tm = _pick_tile(n, tm)
    tf = _pick_tile(d_ff, tf)
    nj = d_ff // tf
    return pl.pallas_call(
        _ffn_kernel,
        out_shape=jax.ShapeDtypeStruct((n, d), F32),
        grid=(n // tm, nj),
        in_specs=[
            pl.BlockSpec((tm, d), lambda i, j: (i, 0)),
            pl.BlockSpec((1, d), lambda i, j: (0, 0)),
            pl.BlockSpec((d, tf), lambda i, j: (0, j)),
            pl.BlockSpec((d, tf), lambda i, j: (0, j + nj)),
            pl.BlockSpec((tf, d), lambda i, j: (j, 0)),
        ],
        out_specs=pl.BlockSpec((tm, d), lambda i, j: (i, 0)),
        scratch_shapes=[pltpu.VMEM((tm, d), BF16)],
        compiler_params=_cparams(("parallel", "arbitrary")),
        name="ffn",
    )(x, g.reshape(1, d), w_in, w_in, w_out)


def _norm_matmul_kernel(x_ref, g_ref, w_ref, o_ref, xn_ref):
    @pl.when(pl.program_id(1) == 0)
    def _():
        xn_ref[...] = _rms_rows(x_ref[...], g_ref[...]).astype(BF16)

    o_ref[...] = _dot(xn_ref[...], w_ref[...])


def _norm_matmul(x, g, w, tm=1024, tn=1024):
    n, d = x.shape
    wn = w.shape[1]
    tm = _pick_tile(n, tm)
    while wn % tn:
        tn -= 128
    return pl.pallas_call(
        _norm_matmul_kernel,
        out_shape=jax.ShapeDtypeStruct((n, wn), F32),
        grid=(n // tm, wn // tn),
        in_specs=[
            pl.BlockSpec((tm, d), lambda i, j: (i, 0)),
            pl.BlockSpec((1, d), lambda i, j: (0, 0)),
            pl.BlockSpec((d, tn), lambda i, j: (0, j)),
        ],
        out_specs=pl.BlockSpec((tm, tn), lambda i, j: (i, j)),
        scratch_shapes=[pltpu.VMEM((tm, d), BF16)],
        compiler_params=_cparams(("parallel", "arbitrary")),
        name="norm_matmul",
    )(x, g.reshape(1, d), w)


def _out_proj_kernel(x_ref, ya_ref, yb_ref, wa_ref, wb_ref, o_ref):
    o_ref[...] = x_ref[...] + _dot(ya_ref[...], wa_ref[...]) + _dot(yb_ref[...], wb_ref[...])


def _out_proj(x, ya, yb, w, tm=512):
    n, d = x.shape
    da = ya.shape[1]
    db = yb.shape[1]
    assert da == db and w.shape[0] == da + db
    tm = _pick_tile(n, tm)
    return pl.pallas_call(
        _out_proj_kernel,
        out_shape=jax.ShapeDtypeStruct((n, d), F32),
        grid=(n // tm,),
        in_specs=[
            pl.BlockSpec((tm, d), lambda i: (i, 0)),
            pl.BlockSpec((tm, da), lambda i: (i, 0)),
            pl.BlockSpec((tm, db), lambda i: (i, 0)),
            pl.BlockSpec((da, d), lambda i: (0, 0)),
            pl.BlockSpec((db, d), lambda i: (1, 0)),
        ],
        out_specs=pl.BlockSpec((tm, d), lambda i: (i, 0)),
        compiler_params=_cparams(("parallel",)),
        name="out_proj",
    )(x, ya, yb, w, w)


def _causal_conv(x, w, b):
    y = x * w[CONV_W - 1:CONV_W] + b
    for k in range(1, CONV_W):
        y = y + _shift_rows(x, k, 0.0) * w[CONV_W - 1 - k:CONV_W - k]
    return y


def _shift_rows(x, k, fill):
    assert 0 < k < SUBLANES
    rolled = pltpu.roll(x, k, axis=0)
    t = lax.broadcasted_iota(jnp.int32, (SUBLANES, x.shape[1]), 0)
    head = jnp.where(t >= k, rolled[:SUBLANES], fill)
    return jnp.concatenate([head, rolled[SUBLANES:]], axis=0)


def _mixer_a_kernel(xr_ref, gate_ref, cw_ref, cb_ref, wr_ref, br_ref, wi_ref, bi_ref,
                    lam_ref, o_ref):
    s = xr_ref.shape[0]
    xc = _causal_conv(xr_ref[...], cw_ref[...], cb_ref[...])
    xb = xc.astype(BF16)
    r = jax.nn.sigmoid(_dot(xb, wr_ref[...]) + br_ref[...])
    i = jax.nn.sigmoid(_dot(xb, wi_ref[...]) + bi_ref[...])
    neg_lam = -lam_ref[...]
    softplus = jnp.maximum(neg_lam, 0.0) + jnp.log1p(jnp.exp(-jnp.abs(neg_lam)))
    log_a = -RG_C * r * softplus
    a = jnp.exp(log_a)
    u = jnp.sqrt(-jnp.tanh(log_a) * (a * a + 1.0)) * (i * xc)
    k = 1
    while k < s:
        if k < SUBLANES:
            u = a * _shift_rows(u, k, 0.0) + u
            a = a * _shift_rows(a, k, 1.0)
        else:
            u = jnp.concatenate([u[:k], a[k:] * u[:s - k] + u[k:]], axis=0)
            a = jnp.concatenate([a[:k], a[k:] * a[:s - k]], axis=0)
        k *= 2
    o_ref[...] = (jax.nn.gelu(gate_ref[...]) * u).astype(o_ref.dtype)


def _mixer_a(proj, conv_w, conv_b, w_r, b_r, w_i, b_i, lam, d_rg, cast_jobs):
    bsz, s, _ = proj.shape
    nb = d_rg // RG_BLOCK
    grid = (bsz, nb)
    vec = lambda v: v.reshape(1, d_rg)
    col = lambda off: pl.BlockSpec((None, s, RG_BLOCK), lambda b, g: (b, 0, off + g))
    row = lambda r: pl.BlockSpec((r, RG_BLOCK), lambda b, g: (0, g))
    blk = pl.BlockSpec((None, RG_BLOCK, RG_BLOCK), lambda b, g: (g, 0, 0))
    return _call_with_casts(
        _mixer_a_kernel, cast_jobs(grid),
        out_shape=jax.ShapeDtypeStruct((bsz, s, d_rg), BF16),
        grid=grid,
        in_specs=[col(nb), col(0), row(CONV_W), row(1), blk, row(1), blk, row(1), row(1)],
        out_specs=pl.BlockSpec((None, s, RG_BLOCK), lambda b, g: (b, 0, g)),
        semantics=("arbitrary", "arbitrary"),
        name="mixer_a",
        operands=(proj, proj, conv_w, vec(conv_b), w_r.astype(BF16), vec(b_r), w_i.astype(BF16),
                  vec(b_i), vec(lam)))


def _t5_bucket_np(dist):
    max_exact = N_BUCKETS // 2
    d_f = np.maximum(dist, 1).astype(np.float32)
    large = max_exact + (np.log(d_f / np.float32(max_exact)) / np.float32(math.log(MAX_DIST / max_exact))
                         * np.float32(N_BUCKETS - max_exact)).astype(np.int32)
    large = np.minimum(large, N_BUCKETS - 1)
    return np.where(dist < max_exact, dist, large).astype(np.int32)


def _dsw_bias_rows(rel_bias, seq, qb):
    dist = np.arange(seq)
    mult = np.zeros(seq, np.float32)
    for window, dilation in DSW_PATTERNS:
        mult += ((dist % dilation == 0) & (dist <= window)).astype(np.float32)
    logm = np.where(mult > 0, np.log(np.maximum(mult, 1.0)), NEG).astype(np.float32)
    f = (rel_bias.astype(F32)[_t5_bucket_np(dist)] + logm[:, None]) * LOG2E
    n_heads = f.shape[1]
    p = seq + qb
    hv = jnp.concatenate([f[::-1].T, jnp.full((n_heads, qb), NEG, F32)], axis=1)
    t = jnp.broadcast_to(hv[:, None, :], (n_heads, qb, p)).reshape(n_heads, qb * p)
    t = t[:, :qb * (p - 1)].reshape(n_heads, qb, p - 1)
    return t[:, :, qb - 1:qb - 1 + seq]


def _mixer_b_kernel(q_ref, k_ref, v_ref, g_ref, tbl_ref, o_ref, kn_ref, vb_ref):
    s, hd = q_ref.shape
    qb = tbl_ref.shape[0]
    kn_ref[...] = _rms_rows(k_ref[...], g_ref[1:2]).astype(BF16)
    vb_ref[...] = v_ref[...].astype(BF16)
    for gi in range(s // qb):
        n_keys = (gi + 1) * qb
        rows = slice(gi * qb, n_keys)
        qn = (_rms_rows(q_ref[rows, :], g_ref[0:1]) * (LOG2E / math.sqrt(hd))).astype(BF16)
        sc = _dot_nt(qn, kn_ref[0:n_keys, :]) + tbl_ref[:, s - n_keys:]
        m = jnp.max(sc, axis=-1, keepdims=True)
        p = jnp.exp2(sc - m)
        l = jnp.sum(p, axis=-1, keepdims=True)
        o = _dot(p.astype(BF16), vb_ref[0:n_keys, :])
        o_ref[rows, :] = (o / l).astype(o_ref.dtype)


def _mixer_b(proj, qk_g, rel_bias, col0, n_heads, cast_jobs, qb=256):
    bsz, s, _ = proj.shape
    qb = _pick_tile(s, qb)
    tbl = _dsw_bias_rows(rel_bias, s, qb)
    c0 = col0 // HEAD_B
    grid = (n_heads, bsz)
    seq_spec = lambda off: pl.BlockSpec((None, s, HEAD_B), lambda h, b: (b, 0, off + h))
    return _call_with_casts(
        _mixer_b_kernel, cast_jobs(grid),
        out_shape=jax.ShapeDtypeStruct((bsz, s, n_heads * HEAD_B), BF16),
        grid=grid,
        in_specs=[
            seq_spec(c0), seq_spec(c0 + n_heads), seq_spec(c0 + 2 * n_heads),
            pl.BlockSpec((2, HEAD_B), lambda h, b: (0, 0)),
            pl.BlockSpec((None, qb, s), lambda h, b: (h, 0, 0)),
        ],
        out_specs=pl.BlockSpec((None, s, HEAD_B), lambda h, b: (b, 0, h)),
        scratch_shapes=[pltpu.VMEM((s, HEAD_B), BF16), pltpu.VMEM((s, HEAD_B), BF16)],
        semantics=("arbitrary", "arbitrary"),
        name="mixer_b",
        operands=(proj, proj, proj, qk_g, tbl))


def _mixer_c_kernel(gb_ref, q_ref, k_ref, v_ref, og_ref, cwq_ref, cwk_ref, cbq_ref, cbk_ref,
                    ig_ref, fg_ref, gain_ref, o_ref, qc_ref, kc_ref, c_ref, n_ref, m_ref):
    h = pl.program_id(1)
    L = SEQ_BLOCK
    s, d = q_ref.shape
    nc = s // L

    qc_ref[...] = jax.nn.silu(_causal_conv(q_ref[...], cwq_ref[...], cbq_ref[...]))
    kc_ref[...] = jax.nn.silu(_causal_conv(k_ref[...], cwk_ref[...], cbk_ref[...])) * (d ** -0.5)
    c_ref[...] = jnp.zeros_like(c_ref)
    n_ref[...] = jnp.zeros_like(n_ref)
    m_ref[...] = jnp.zeros_like(m_ref)

    ri = lax.broadcasted_iota(jnp.int32, (L, L), 0)
    ci = lax.broadcasted_iota(jnp.int32, (L, L), 1)
    eye = ri == ci
    causal = ci <= ri

    def to_col(row):
        return jnp.sum(jnp.where(eye, row, 0.0), axis=1, keepdims=True)

    ig_all = ig_ref[...] + gb_ref[0, h]
    f_all = fg_ref[...] + gb_ref[1, h]
    ls_all = jnp.minimum(f_all, 0.0) - jnp.log1p(jnp.exp(-jnp.abs(f_all)))

    for n in range(nc):
        rows = slice(n * L, (n + 1) * L)
        qc = qc_ref[rows, :]
        kc = kc_ref[rows, :]
        qb = qc.astype(BF16)
        kb = kc.astype(BF16)
        vb = v_ref[rows, :].astype(BF16)

        ig_row = ig_all[n:n + 1]
        ls_row = ls_all[n:n + 1]
        ls_col = to_col(ls_row)
        ig_col = to_col(ig_row)
        b_col = jnp.sum(jnp.where(causal, ls_row, 0.0), axis=1, keepdims=True)
        b_row = jnp.sum(jnp.where(ri <= ci, ls_col, 0.0), axis=0, keepdims=True)
        b_last = jnp.sum(ls_row, axis=1, keepdims=True)

        log_d = b_col - b_row + ig_row
        g_row = b_last - b_row + ig_row
        g_col = b_last - b_col + ig_col
        g_max = jnp.max(g_row, axis=1, keepdims=True)
        d_max = jnp.max(jnp.where(causal, log_d, -jnp.inf), axis=1, keepdims=True)
        qk = _dot_nt(qb, kb)

        c_st = c_ref[...]
        n_st = n_ref[...]
        m_st = m_ref[...]

        inter = b_col + m_st
        m_t = jnp.maximum(inter, d_max)
        s_qk = jnp.where(causal, qk * jnp.exp(log_d - m_t), 0.0)
        inter_w = jnp.exp(inter - m_t)
        num = _dot(s_qk.astype(BF16), vb) + inter_w * _dot(qb, c_st.astype(BF16))
        qn = jnp.sum(qc * n_st, axis=1, keepdims=True)
        den = jnp.sum(s_qk, axis=1, keepdims=True) + inter_w * qn
        hout = num / jnp.maximum(jnp.abs(den), jnp.exp(-m_t))

        hn = _rms_rows(hout, gain_ref[...])
        o_ref[rows, :] = (jax.nn.sigmoid(og_ref[rows, :]) * hn).astype(o_ref.dtype)

        wk = jnp.exp(g_col - g_max) * kc
        c_loc = _dot_tn(wk.astype(BF16), vb)
        n_loc = jnp.sum(wk, axis=0, keepdims=True)
        m_new = jnp.maximum(b_last + m_st, g_max)
        decay = jnp.exp(b_last + m_st - m_new)
        fresh = jnp.exp(g_max - m_new)
        c_ref[...] = decay * c_st + fresh * c_loc
        n_ref[...] = decay * n_st + fresh * n_loc
        m_ref[...] = m_new


def _mixer_c(proj, gates, conv_w, conv_b, gate_b, h_gain, n_heads, cast_jobs):
    bsz, s, _ = proj.shape
    d = HEAD_C
    L = SEQ_BLOCK
    nc = s // L
    grid = (bsz, n_heads)
    g_rows = jnp.transpose(gates, (0, 2, 1)).reshape(bsz, 2 * n_heads, nc, L)
    cb = conv_b.reshape(1, -1)
    full = lambda off: pl.BlockSpec((None, s, d), lambda b, h: (b, 0, off + h))
    row = lambda r, off: pl.BlockSpec((r, d), lambda b, h: (0, off + h))
    grow = lambda off: pl.BlockSpec((None, None, nc, L), lambda b, h: (b, off + h, 0, 0))
    return _call_with_casts(
        _mixer_c_kernel, cast_jobs(grid),
        out_shape=jax.ShapeDtypeStruct((bsz, s, n_heads * d), BF16),
        grid=grid,
        in_specs=[
            pl.BlockSpec(memory_space=pltpu.SMEM),
            full(0), full(n_heads), full(2 * n_heads), full(3 * n_heads),
            row(CONV_W, 0), row(CONV_W, n_heads), row(1, 0), row(1, n_heads),
            grow(0), grow(n_heads), row(1, 0),
        ],
        out_specs=pl.BlockSpec((None, s, d), lambda b, h: (b, 0, h)),
        scratch_shapes=[pltpu.VMEM((s, d), F32), pltpu.VMEM((s, d), F32),
                        pltpu.VMEM((d, d), F32), pltpu.VMEM((1, d), F32), pltpu.VMEM((1, 1), F32)],
        semantics=("arbitrary", "arbitrary"),
        name="mixer_c",
        operands=(gate_b, proj, proj, proj, proj, conv_w, conv_w, cb, cb, g_rows, g_rows,
                  h_gain.reshape(1, -1)))


def _mixer_d_kernel(q_ref, k_ref, v_ref, o_ref, kb_ref, vb_ref, acc_ref, suf_ref):
    s, hd = q_ref.shape
    B = SEQ_BLOCK
    G = min(SB_QUERY_ROWS, s)
    kb_ref[...] = k_ref[...].astype(BF16)
    vb_ref[...] = v_ref[...].astype(BF16)
    ri = lax.broadcasted_iota(jnp.int32, (B, B), 0)
    ci = lax.broadcasted_iota(jnp.int32, (B, B), 1)
    after = (ri > ci).astype(BF16)
    after2 = jnp.concatenate([after, after], axis=0)
    row_g = lax.broadcasted_iota(jnp.int32, (G, B), 0)
    col_g = lax.broadcasted_iota(jnp.int32, (G, B), 1)

    def span(q0, first, last, suffix):
        n = last - first + 1
        keys = slice(first * B, (last + 1) * B)
        qb = (q_ref[q0:q0 + G, :] * (hd ** -0.5)).astype(BF16)
        z = _dot_nt(qb, kb_ref[keys, :])
        sp = jnp.log(1.0 + jnp.exp2(jnp.abs(z) * (-LOG2E)))
        nm = jnp.minimum(z, 0.0)
        log_beta = nm - sp
        log_stay = (nm - z) - sp
        before = [col_g + ((first + j) * B - q0) < row_g if (first + j + 1) * B > q0 else None
                  for j in range(n)]
        later, tot = [], []
        for j in range(n):
            x = log_stay[:, j * B:(j + 1) * B]
            if before[j] is not None:
                x = jnp.where(before[j], x, 0.0)
            hi = lax.bitcast_convert_type(lax.bitcast_convert_type(x, jnp.uint32) & HI_MASK, F32)
            hl = jnp.concatenate([hi.astype(BF16), (x - hi).astype(BF16)], axis=1)
            later.append(_dot(hl, after2))
            tot.append(jnp.sum(x, axis=-1, keepdims=True))
        att = [None] * n
        for j in range(n - 1, -1, -1):
            a = jnp.exp(log_beta[:, j * B:(j + 1) * B] + later[j] + suffix)
            if before[j] is not None:
                a = jnp.where(before[j], a, 0.0)
            att[j] = a.astype(BF16)
            suffix = suffix + tot[j]
        att = att[0] if n == 1 else jnp.concatenate(att, axis=1)
        return _dot(att, vb_ref[keys, :]), suffix

    def first_near(q0):
        return max(q0 // B - SB_NEAR_BLOCKS + 1, 0)

    worst = None
    for q0 in range(0, s, G):
        rows = slice(q0, q0 + G)
        acc, suffix = span(q0, first_near(q0), (q0 + G) // B - 1, jnp.zeros((G, 1), F32))
        o_ref[rows, :] = acc.astype(o_ref.dtype)
        if first_near(q0) > 0:
            acc_ref[rows, :] = acc
            suf_ref[rows, :] = suffix
            worst = suffix if worst is None else jnp.maximum(worst, suffix)

    if worst is not None:
        @pl.when(jnp.max(worst) > SB_UNDERFLOW)
        def _():
            for q0 in range(0, s, G):
                if first_near(q0) == 0:
                    continue
                rows = slice(q0, q0 + G)
                suffix = suf_ref[rows, :]

                @pl.when(jnp.max(suffix) > SB_UNDERFLOW)
                def _():
                    far, _ = span(q0, 0, first_near(q0) - 1, suffix)
                    o_ref[rows, :] = (acc_ref[rows, :] + far).astype(o_ref.dtype)


def _mixer_d(proj, col0, n_heads, cast_jobs):
    bsz, s, _ = proj.shape
    c0 = col0 // HEAD_D
    grid = (bsz, n_heads)
    seq_spec = lambda off: pl.BlockSpec((None, s, HEAD_D), lambda b, h: (b, 0, off + h))
    return _call_with_casts(
        _mixer_d_kernel, cast_jobs(grid),
        out_shape=jax.ShapeDtypeStruct((bsz, s, n_heads * HEAD_D), BF16),
        grid=grid,
        in_specs=[seq_spec(c0), seq_spec(c0 + n_heads), seq_spec(c0 + 2 * n_heads)],
        out_specs=pl.BlockSpec((None, s, HEAD_D), lambda b, h: (b, 0, h)),
        scratch_shapes=[pltpu.VMEM((s, HEAD_D), BF16), pltpu.VMEM((s, HEAD_D), BF16),
                        pltpu.VMEM((s, HEAD_D), F32), pltpu.VMEM((s, 1), F32)],
        semantics=("arbitrary", "arbitrary"),
        name="mixer_d",
        operands=(proj, proj, proj))


def _layer_ab(x, bsz, s, g, w_in, conv_w, conv_b, w_r, b_r, w_i, b_i, lam, qk_g, rel_bias, w_out,
              casts):
    d_rg = conv_w.shape[-1]
    n_heads = rel_bias.shape[-1]
    proj = _norm_matmul(x, g, w_in.astype(BF16)).reshape(bsz, s, -1)
    ya, cast_a = _mixer_a(proj, conv_w, conv_b, w_r, b_r, w_i, b_i, lam, d_rg, casts[0])
    yb, cast_b = _mixer_b(proj, qk_g, rel_bias, 2 * d_rg, n_heads, casts[1])
    n = bsz * s
    return _out_proj(x, ya.reshape(n, -1), yb.reshape(n, -1), w_out.astype(BF16)), cast_a + cast_b


def _cd_weight_kernel(a_ref, b_ref, o_ref, *, n_aligned, n_shifted, n_gate):
    c = pl.program_id(0)
    w = o_ref.shape[1]
    lane = lax.broadcasted_iota(jnp.int32, o_ref.shape, 1)

    @pl.when(c < n_aligned)
    def _():
        o_ref[...] = a_ref[...].astype(o_ref.dtype)

    @pl.when((c >= n_aligned) & (c < n_aligned + n_shifted))
    def _():
        a = pltpu.roll(a_ref[...], w - n_gate, axis=1)
        b = pltpu.roll(b_ref[...], w - n_gate, axis=1)
        o_ref[...] = jnp.where(lane < w - n_gate, a, b).astype(o_ref.dtype)

    @pl.when(c >= n_aligned + n_shifted)
    def _():
        o_ref[...] = jnp.where(lane < n_gate, a_ref[...], 0.0).astype(o_ref.dtype)


def _cd_weight(w_in, j, gate_col, n_gate, width, tw=512):
    _, d, cols = w_in.shape
    main = cols - n_gate
    assert gate_col % tw == 0 and (main - gate_col) % tw == 0 and width == main + tw
    n_aligned = gate_col // tw
    n_shifted = (main - gate_col) // tw
    last = (cols - 1) // tw
    return pl.pallas_call(
        functools.partial(_cd_weight_kernel, n_aligned=n_aligned, n_shifted=n_shifted, n_gate=n_gate),
        out_shape=jax.ShapeDtypeStruct((d, width), BF16),
        grid=(width // tw,),
        in_specs=[
            pl.BlockSpec((None, d, tw),
                         lambda c: (j, 0, jnp.where(c < n_aligned + n_shifted, c, n_aligned))),
            pl.BlockSpec((None, d, tw), lambda c: (j, 0, jnp.minimum(c + 1, last))),
        ],
        out_specs=pl.BlockSpec((d, tw), lambda c: (0, c)),
        compiler_params=_cparams(("arbitrary",)),
        name="cd_weight",
    )(w_in, w_in)


def _layer_cd(x, bsz, s, g, w_in_all, j, conv_w, conv_b, gate_b, h_gain, w_out, casts):
    n_heads_c = gate_b.shape[-1]
    dc = n_heads_c * HEAD_C
    d_mix = w_out.shape[0]
    dd = d_mix - dc
    n_heads_d = dd // HEAD_D
    n_gate = 2 * n_heads_c
    main = 4 * dc + 3 * dd
    width = -(-(main + n_gate) // 512) * 512
    w = _cd_weight(w_in_all, j, 4 * dc, n_gate, width)
    proj = _norm_matmul(x, g, w, tn=768).reshape(bsz, s, width)
    gates = proj[:, :, main:main + n_gate]
    yd, cast_d = _mixer_d(proj, 4 * dc, n_heads_d, casts[0])
    yc, cast_c = _mixer_c(proj, gates, conv_w, conv_b, gate_b, h_gain, n_heads_c, casts[1])
    n = bsz * s
    return _out_proj(x, yc.reshape(n, -1), yd.reshape(n, -1), w_out.astype(BF16)), cast_d + cast_c


def kernel(x, norm_g, ffn_w_in, ffn_w_out, ab_w_in, ab_conv_w, ab_conv_b, rg_w_r, rg_b_r, rg_w_i,
           rg_b_i, rg_lambda, qk_gain, rel_bias, ab_w_out, cd_w_in, cd_conv_w, cd_conv_b,
           mlstm_gate_bias, mlstm_h_gain, cd_w_out):
    bsz, s, d = x.shape
    depth = norm_g.shape[0]
    h = x.reshape(bsz * s, d)
    ffn_w = {(0, 0): (ffn_w_in[0, 0].astype(BF16), ffn_w_out[0, 0].astype(BF16))}
    for layer in range(depth):
        j = layer // 2
        h = _ffn(h, norm_g[layer, 0], *ffn_w[(layer, 0)])
        targets = [(layer, 1)] + ([(layer + 1, 0)] if layer + 1 < depth else [])
        casts = [functools.partial(_ffn_cast_jobs, ffn_w_in, ffn_w_out, *t) for t in targets]
        casts += [lambda grid: []] * (2 - len(casts))
        if layer % 2 == 0:
            h, cast = _layer_ab(h, bsz, s, norm_g[layer, 1], ab_w_in[j], ab_conv_w[j], ab_conv_b[j],
                                rg_w_r[j], rg_b_r[j], rg_w_i[j], rg_b_i[j], rg_lambda[j], qk_gain[j],
                                rel_bias, ab_w_out[j], casts)
        else:
            h, cast = _layer_cd(h, bsz, s, norm_g[layer, 1], cd_w_in, j, cd_conv_w[j], cd_conv_b[j],
                                mlstm_gate_bias[j], mlstm_h_gain[j], cd_w_out[j], casts)
        for i, t in enumerate(targets):
            ffn_w[t] = (cast[2 * i], cast[2 * i + 1])
        h = _ffn(h, norm_g[layer, 2], *ffn_w[(layer, 1)])
    return h.reshape(bsz, s, d)
```

```python
import functools
import math

import numpy as np
import jax
import jax.numpy as jnp
from jax import lax
from jax.experimental import pallas as pl
from jax.experimental.pallas import tpu as pltpu

F32 = jnp.float32
BF16 = jnp.bfloat16

RMS_EPS = 1e-6
RG_C = 8.0
CONV_W = 4
RG_BLOCK = 128
HEAD_B = 128
HEAD_C = 256
HEAD_D = 128
SEQ_BLOCK = 128
SUBLANES = 8
BF16_SUBLANES = 16
DSW_PATTERNS = ((128, 1), (512, 4), (2048, 16))
N_BUCKETS = 32
MAX_DIST = 2048
NEG = -1e30
LOG2E = 1.4426950408889634
HI_MASK = np.uint32(0xFFFF0000)
SB_NEAR_BLOCKS = 3
SB_QUERY_ROWS = 256
SB_UNDERFLOW = -104.0
VMEM_LIMIT = 56 * 1024 * 1024


def _cparams(sem):
    return pltpu.CompilerParams(dimension_semantics=sem, vmem_limit_bytes=VMEM_LIMIT)


def _rms_rows(x, g):
    return x * lax.rsqrt(jnp.mean(x * x, axis=-1, keepdims=True) + RMS_EPS) * g


def _dot(a, b):
    return jnp.dot(a, b, preferred_element_type=F32)


def _dot_nt(a, b):
    return lax.dot_general(a, b, (((1,), (1,)), ((), ())), preferred_element_type=F32)


def _dot_tn(a, b):
    return lax.dot_general(a, b, (((0,), (0,)), ((), ())), preferred_element_type=F32)


def _pick_tile(n, pref):
    t = min(pref, n)
    while n % t:
        t //= 2
    return t


def _ffn_cast_jobs(w_in, w_out, layer, half, grid):
    n_steps = math.prod(grid)

    def step(*idx):
        t = idx[0]
        for i, g in zip(idx[1:], grid[1:]):
            t = t * g + i
        return t

    jobs = []
    for w in (w_in, w_out):
        rows, cols = w.shape[2:]
        n = n_steps
        while rows % n or (rows // n) % BF16_SUBLANES:
            n -= 1
        slab = rows // n
        blk = lambda *idx, n=n: jnp.minimum(step(*idx), n - 1)
        jobs.append((
            w,
            pl.BlockSpec((None, None, slab, cols), lambda *idx, blk=blk: (layer, half, blk(*idx), 0)),
            pl.BlockSpec((slab, cols), lambda *idx, blk=blk: (blk(*idx), 0)),
            jax.ShapeDtypeStruct((rows, cols), BF16),
        ))
    return jobs


def _call_with_casts(body, jobs, *, out_shape, grid, in_specs, out_specs, scratch_shapes=(),
                     semantics, name, operands):
    n_in = len(in_specs)
    n_jobs = len(jobs)

    def kernel(*refs):
        for i in range(n_jobs):
            dst = refs[n_in + n_jobs + 1 + i]
            dst[...] = refs[n_in + i][...].astype(dst.dtype)
        body(*refs[:n_in], refs[n_in + n_jobs], *refs[n_in + 2 * n_jobs + 1:])

    outs = pl.pallas_call(
        kernel,
        out_shape=[out_shape] + [j[3] for j in jobs],
        grid=grid,
        in_specs=list(in_specs) + [j[1] for j in jobs],
        out_specs=[out_specs] + [j[2] for j in jobs],
        scratch_shapes=list(scratch_shapes),
        compiler_params=_cparams(semantics),
        name=name,
    )(*operands, *[j[0] for j in jobs])
    return outs[0], list(outs[1:])


def _ffn_kernel(x_ref, g_ref, wg_ref, wu_ref, wo_ref, o_ref, xn_ref):
    j = pl.program_id(1)

    @pl.when(j == 0)
    def _():
        x = x_ref[...]
        xn_ref[...] = _rms_rows(x, g_ref[...]).astype(BF16)
        o_ref[...] = x

    xn = xn_ref[...]
    hg = _dot(xn, wg_ref[...])
    hu = _dot(xn, wu_ref[...])
    h = (0.5 * hg * jax.nn.sigmoid(hg) * hu).astype(BF16)
    o_ref[...] += _dot(h, wo_ref[...])


def _ffn(x, g, w_in, w_out, tm=1024, tf=512):
    n, d = x.shape
    d_ff = w_out.shape[0]
    tm = _pick_tile(n, tm)
    tf = _pick_tile(d_ff, tf)
    nj = d_ff // tf
    return pl.pallas_call(
        _ffn_kernel,
        out_shape=jax.ShapeDtypeStruct((n, d), F32),
        grid=(n // tm, nj),
        in_specs=[
            pl.BlockSpec((tm, d), lambda i, j: (i, 0)),
            pl.BlockSpec((1, d), lambda i, j: (0, 0)),
            pl.BlockSpec((d, tf), lambda i, j: (0, j)),
            pl.BlockSpec((d, tf), lambda i, j: (0, j + nj)),
            pl.BlockSpec((tf, d), lambda i, j: (j, 0)),
        ],
        out_specs=pl.BlockSpec((tm, d), lambda i, j: (i, 0)),
        scratch_shapes=[pltpu.VMEM((tm, d), BF16)],
        compiler_params=_cparams(("parallel", "arbitrary")),
        name="ffn",
    )(x, g.reshape(1, d), w_in, w_in, w_out)


def _norm_matmul_kernel(x_ref, g_ref, w_ref, o_ref, xn_ref):
    @pl.when(pl.program_id(1) == 0)
    def _():
        xn_ref[...] = _rms_rows(x_ref[...], g_ref[...]).astype(BF16)

    o_ref[...] = _dot(xn_ref[...], w_ref[...])


def _norm_matmul(x, g, w, tm=1024, tn=1024):
    n, d = x.shape
    wn = w.shape[1]
    tm = _pick_tile(n, tm)
    while wn % tn:
        tn -= 128
    return pl.pallas_call(
        _norm_matmul_kernel,
        out_shape=jax.ShapeDtypeStruct((n, wn), F32),
        grid=(n // tm, wn // tn),
        in_specs=[
            pl.BlockSpec((tm, d), lambda i, j: (i, 0)),
            pl.BlockSpec((1, d), lambda i, j: (0, 0)),
            pl.BlockSpec((d, tn), lambda i, j: (0, j)),
        ],
        out_specs=pl.BlockSpec((tm, tn), lambda i, j: (i, j)),
        scratch_shapes=[pltpu.VMEM((tm, d), BF16)],
        compiler_params=_cparams(("parallel", "arbitrary")),
        name="norm_matmul",
    )(x, g.reshape(1, d), w)


def _out_proj_kernel(x_ref, ya_ref, yb_ref, wa_ref, wb_ref, o_ref):
    o_ref[...] = x_ref[...] + _dot(ya_ref[...], wa_ref[...]) + _dot(yb_ref[...], wb_ref[...])


def _out_proj(x, ya, yb, w, tm=512):
    n, d = x.shape
    da = ya.shape[1]
    db = yb.shape[1]
    assert da == db and w.shape[0] == da + db
    tm = _pick_tile(n, tm)
    return pl.pallas_call(
        _out_proj_kernel,
        out_shape=jax.ShapeDtypeStruct((n, d), F32),
        grid=(n // tm,),
        in_specs=[
            pl.BlockSpec((tm, d), lambda i: (i, 0)),
            pl.BlockSpec((tm, da), lambda i: (i, 0)),
            pl.BlockSpec((tm, db), lambda i: (i, 0)),
            pl.BlockSpec((da, d), lambda i: (0, 0)),
            pl.BlockSpec((db, d), lambda i: (1, 0)),
        ],
        out_specs=pl.BlockSpec((tm, d), lambda i: (i, 0)),
        compiler_params=_cparams(("parallel",)),
        name="out_proj",
    )(x, ya, yb, w, w)


def _causal_conv(x, w, b):
    y = x * w[CONV_W - 1:CONV_W] + b
    for k in range(1, CONV_W):
        y = y + _shift_rows(x, k, 0.0) * w[CONV_W - 1 - k:CONV_W - k]
    return y


def _shift_rows(x, k, fill):
    assert 0 < k < SUBLANES
    rolled = pltpu.roll(x, k, axis=0)
    t = lax.broadcasted_iota(jnp.int32, (SUBLANES, x.shape[1]), 0)
    head = jnp.where(t >= k, rolled[:SUBLANES], fill)
    return jnp.concatenate([head, rolled[SUBLANES:]], axis=0)


def _mixer_a_kernel(xr_ref, gate_ref, cw_ref, cb_ref, wr_ref, br_ref, wi_ref, bi_ref,
                    lam_ref, o_ref):
    s = xr_ref.shape[0]
    xc = _causal_conv(xr_ref[...], cw_ref[...], cb_ref[...])
    xb = xc.astype(BF16)
    r = jax.nn.sigmoid(_dot(xb, wr_ref[...]) + br_ref[...])
    i = jax.nn.sigmoid(_dot(xb, wi_ref[...]) + bi_ref[...])
    neg_lam = -lam_ref[...]
    softplus = jnp.maximum(neg_lam, 0.0) + jnp.log1p(jnp.exp(-jnp.abs(neg_lam)))
    log_a = -RG_C * r * softplus
    a = jnp.exp(log_a)
    u = jnp.sqrt(-jnp.tanh(log_a) * (a * a + 1.0)) * (i * xc)
    k = 1
    while k < s:
        if k < SUBLANES:
            u = a * _shift_rows(u, k, 0.0) + u
            a = a * _shift_rows(a, k, 1.0)
        else:
            u = jnp.concatenate([u[:k], a[k:] * u[:s - k] + u[k:]], axis=0)
            a = jnp.concatenate([a[:k], a[k:] * a[:s - k]], axis=0)
        k *= 2
    o_ref[...] = (jax.nn.gelu(gate_ref[...]) * u).astype(o_ref.dtype)


def _mixer_a(proj, conv_w, conv_b, w_r, b_r, w_i, b_i, lam, d_rg, cast_jobs):
    bsz, s, _ = proj.shape
    nb = d_rg // RG_BLOCK
    grid = (bsz, nb)
    vec = lambda v: v.reshape(1, d_rg)
    col = lambda off: pl.BlockSpec((None, s, RG_BLOCK), lambda b, g: (b, 0, off + g))
    row = lambda r: pl.BlockSpec((r, RG_BLOCK), lambda b, g: (0, g))
    blk = pl.BlockSpec((None, RG_BLOCK, RG_BLOCK), lambda b, g: (g, 0, 0))
    return _call_with_casts(
        _mixer_a_kernel, cast_jobs(grid),
        out_shape=jax.ShapeDtypeStruct((bsz, s, d_rg), BF16),
        grid=grid,
        in_specs=[col(nb), col(0), row(CONV_W), row(1), blk, row(1), blk, row(1), row(1)],
        out_specs=pl.BlockSpec((None, s, RG_BLOCK), lambda b, g: (b, 0, g)),
        semantics=("arbitrary", "arbitrary"),
        name="mixer_a",
        operands=(proj, proj, conv_w, vec(conv_b), w_r.astype(BF16), vec(b_r), w_i.astype(BF16),
                  vec(b_i), vec(lam)))


def _t5_bucket_np(dist):
    max_exact = N_BUCKETS // 2
    d_f = np.maximum(dist, 1).astype(np.float32)
    large = max_exact + (np.log(d_f / np.float32(max_exact)) / np.float32(math.log(MAX_DIST / max_exact))
                         * np.float32(N_BUCKETS - max_exact)).astype(np.int32)
    large = np.minimum(large, N_BUCKETS - 1)
    return np.where(dist < max_exact, dist, large).astype(np.int32)


def _dsw_bias_vector(rel_bias, seq, qb):
    dist = np.arange(seq)
    mult = np.zeros(seq, np.float32)
    for window, dilation in DSW_PATTERNS:
        mult += ((dist % dilation == 0) & (dist <= window)).astype(np.float32)
    logm = np.where(mult > 0, np.log(np.maximum(mult, 1.0)), NEG).astype(np.float32)
    f = (rel_bias.astype(F32)[_t5_bucket_np(dist)] + logm[:, None]) * LOG2E
    n_heads = f.shape[1]
    hv = jnp.concatenate([f[::-1].T, jnp.full((n_heads, qb), NEG, F32)], axis=1)
    return hv[:, None, :]


def _mixer_b_kernel(q_ref, k_ref, v_ref, g_ref, hv_ref, o_ref, kn_ref, vb_ref, tbl_ref):
    s, hd = q_ref.shape
    qb = tbl_ref.shape[0]

    @pl.when(pl.program_id(1) == 0)
    def _():
        skew = pltpu.roll(jnp.broadcast_to(hv_ref[...], (qb, s + qb)), 1, axis=1, stride=1, stride_axis=0)
        tbl_ref[...] = skew[:, qb:]

    kn_ref[...] = _rms_rows(k_ref[...], g_ref[1:2]).astype(BF16)
    vb_ref[...] = v_ref[...].astype(BF16)
    for gi in range(s // qb):
        n_keys = (gi + 1) * qb
        rows = slice(gi * qb, n_keys)
        qn = (_rms_rows(q_ref[rows, :], g_ref[0:1]) * (LOG2E / math.sqrt(hd))).astype(BF16)
        sc = _dot_nt(qn, kn_ref[0:n_keys, :]) + tbl_ref[:, s - n_keys:]
        m = jnp.max(sc, axis=-1, keepdims=True)
        p = jnp.exp2(sc - m)
        l = jnp.sum(p, axis=-1, keepdims=True)
        o = _dot(p.astype(BF16), vb_ref[0:n_keys, :])
        o_ref[rows, :] = (o / l).astype(o_ref.dtype)


def _mixer_b(proj, qk_g, rel_bias, col0, n_heads, cast_jobs, qb=256):
    bsz, s, _ = proj.shape
    qb = _pick_tile(s, qb)
    hv = _dsw_bias_vector(rel_bias, s, qb)
    c0 = col0 // HEAD_B
    grid = (n_heads, bsz)
    seq_spec = lambda off: pl.BlockSpec((None, s, HEAD_B), lambda h, b: (b, 0, off + h))
    return _call_with_casts(
        _mixer_b_kernel, cast_jobs(grid),
        out_shape=jax.ShapeDtypeStruct((bsz, s, n_heads * HEAD_B), BF16),
        grid=grid,
        in_specs=[
            seq_spec(c0), seq_spec(c0 + n_heads), seq_spec(c0 + 2 * n_heads),
            pl.BlockSpec((2, HEAD_B), lambda h, b: (0, 0)),
            pl.BlockSpec((None, 1, s + qb), lambda h, b: (h, 0, 0)),
        ],
        out_specs=pl.BlockSpec((None, s, HEAD_B), lambda h, b: (b, 0, h)),
        scratch_shapes=[pltpu.VMEM((s, HEAD_B), BF16), pltpu.VMEM((s, HEAD_B), BF16),
                        pltpu.VMEM((qb, s), F32)],
        semantics=("arbitrary", "arbitrary"),
        name="mixer_b",
        operands=(proj, proj, proj, qk_g, hv))


def _mixer_c_kernel(gb_ref, q_ref, k_ref, v_ref, og_ref, cwq_ref, cwk_ref, cbq_ref, cbk_ref,
                    ig_ref, fg_ref, gain_ref, o_ref, qc_ref, kc_ref, c_ref, n_ref, m_ref):
    h = pl.program_id(1)
    L = SEQ_BLOCK
    s, d = q_ref.shape
    nc = s // L

    qc_ref[...] = jax.nn.silu(_causal_conv(q_ref[...], cwq_ref[...], cbq_ref[...]))
    kc_ref[...] = jax.nn.silu(_causal_conv(k_ref[...], cwk_ref[...], cbk_ref[...])) * (d ** -0.5)
    c_ref[...] = jnp.zeros_like(c_ref)
    n_ref[...] = jnp.zeros_like(n_ref)
    m_ref[...] = jnp.zeros_like(m_ref)

    ri = lax.broadcasted_iota(jnp.int32, (L, L), 0)
    ci = lax.broadcasted_iota(jnp.int32, (L, L), 1)
    eye = ri == ci
    causal = ci <= ri

    def to_col(row):
        return jnp.sum(jnp.where(eye, row, 0.0), axis=1, keepdims=True)

    ig_all = ig_ref[...] + gb_ref[0, h]
    f_all = fg_ref[...] + gb_ref[1, h]
    ls_all = jnp.minimum(f_all, 0.0) - jnp.log1p(jnp.exp(-jnp.abs(f_all)))

    for n in range(nc):
        rows = slice(n * L, (n + 1) * L)
        qc = qc_ref[rows, :]
        kc = kc_ref[rows, :]
        qb = qc.astype(BF16)
        kb = kc.astype(BF16)
        vb = v_ref[rows, :].astype(BF16)

        ig_row = ig_all[n:n + 1]
        ls_row = ls_all[n:n + 1]
        ls_col = to_col(ls_row)
        ig_col = to_col(ig_row)
        b_col = jnp.sum(jnp.where(causal, ls_row, 0.0), axis=1, keepdims=True)
        b_row = jnp.sum(jnp.where(ri <= ci, ls_col, 0.0), axis=0, keepdims=True)
        b_last = jnp.sum(ls_row, axis=1, keepdims=True)

        log_d = b_col - b_row + ig_row
        g_row = b_last - b_row + ig_row
        g_col = b_last - b_col + ig_col
        g_max = jnp.max(g_row, axis=1, keepdims=True)
        d_max = jnp.max(jnp.where(causal, log_d, -jnp.inf), axis=1, keepdims=True)
        qk = _dot_nt(qb, kb)

        c_st = c_ref[...]
        n_st = n_ref[...]
        m_st = m_ref[...]

        inter = b_col + m_st
        m_t = jnp.maximum(inter, d_max)
        s_qk = jnp.where(causal, qk * jnp.exp(log_d - m_t), 0.0)
        inter_w = jnp.exp(inter - m_t)
        num = _dot(s_qk.astype(BF16), vb) + inter_w * _dot(qb, c_st.astype(BF16))
        qn = jnp.sum(qc * n_st, axis=1, keepdims=True)
        den = jnp.sum(s_qk, axis=1, keepdims=True) + inter_w * qn
        hout = num / jnp.maximum(jnp.abs(den), jnp.exp(-m_t))

        hn = _rms_rows(hout, gain_ref[...])
        o_ref[rows, :] = (jax.nn.sigmoid(og_ref[rows, :]) * hn).astype(o_ref.dtype)

        wk = jnp.exp(g_col - g_max) * kc
        c_loc = _dot_tn(wk.astype(BF16), vb)
        n_loc = jnp.sum(wk, axis=0, keepdims=True)
        m_new = jnp.maximum(b_last + m_st, g_max)
        decay = jnp.exp(b_last + m_st - m_new)
        fresh = jnp.exp(g_max - m_new)
        c_ref[...] = decay * c_st + fresh * c_loc
        n_ref[...] = decay * n_st + fresh * n_loc
        m_ref[...] = m_new


def _mixer_c(proj, gates, conv_w, conv_b, gate_b, h_gain, n_heads, cast_jobs):
    bsz, s, _ = proj.shape
    d = HEAD_C
    L = SEQ_BLOCK
    nc = s // L
    grid = (bsz, n_heads)
    g_rows = jnp.transpose(gates, (0, 2, 1)).reshape(bsz, 2 * n_heads, nc, L)
    cb = conv_b.reshape(1, -1)
    full = lambda off: pl.BlockSpec((None, s, d), lambda b, h: (b, 0, off + h))
    row = lambda r, off: pl.BlockSpec((r, d), lambda b, h: (0, off + h))
    grow = lambda off: pl.BlockSpec((None, None, nc, L), lambda b, h: (b, off + h, 0, 0))
    return _call_with_casts(
        _mixer_c_kernel, cast_jobs(grid),
        out_shape=jax.ShapeDtypeStruct((bsz, s, n_heads * d), BF16),
        grid=grid,
        in_specs=[
            pl.BlockSpec(memory_space=pltpu.SMEM),
            full(0), full(n_heads), full(2 * n_heads), full(3 * n_heads),
            row(CONV_W, 0), row(CONV_W, n_heads), row(1, 0), row(1, n_heads),
            grow(0), grow(n_heads), row(1, 0),
        ],
        out_specs=pl.BlockSpec((None, s, d), lambda b, h: (b, 0, h)),
        scratch_shapes=[pltpu.VMEM((s, d), F32), pltpu.VMEM((s, d), F32),
                        pltpu.VMEM((d, d), F32), pltpu.VMEM((1, d), F32), pltpu.VMEM((1, 1), F32)],
        semantics=("arbitrary", "arbitrary"),
        name="mixer_c",
        operands=(gate_b, proj, proj, proj, proj, conv_w, conv_w, cb, cb, g_rows, g_rows,
                  h_gain.reshape(1, -1)))


def _mixer_d_kernel(q_ref, k_ref, v_ref, o_ref, kb_ref, vb_ref, acc_ref, suf_ref):
    s, hd = q_ref.shape
    B = SEQ_BLOCK
    G = min(SB_QUERY_ROWS, s)
    kb_ref[...] = k_ref[...].astype(BF16)
    vb_ref[...] = v_ref[...].astype(BF16)
    ri = lax.broadcasted_iota(jnp.int32, (B, B), 0)
    ci = lax.broadcasted_iota(jnp.int32, (B, B), 1)
    after = (ri > ci).astype(BF16)
    after2 = jnp.concatenate([after, after], axis=0)
    row_g = lax.broadcasted_iota(jnp.int32, (G, B), 0)
    col_g = lax.broadcasted_iota(jnp.int32, (G, B), 1)

    def span(q0, first, last, suffix):
        n = last - first + 1
        keys = slice(first * B, (last + 1) * B)
        qb = (q_ref[q0:q0 + G, :] * (hd ** -0.5)).astype(BF16)
        z = _dot_nt(qb, kb_ref[keys, :])
        sp = jnp.log(1.0 + jnp.exp2(jnp.abs(z) * (-LOG2E)))
        nm = jnp.minimum(z, 0.0)
        log_beta = nm - sp
        log_stay = (nm - z) - sp
        before = [col_g + ((first + j) * B - q0) < row_g if (first + j + 1) * B > q0 else None
                  for j in range(n)]
        later, tot = [], []
        for j in range(n):
            x = log_stay[:, j * B:(j + 1) * B]
            if before[j] is not None:
                x = jnp.where(before[j], x, 0.0)
            hi = lax.bitcast_convert_type(lax.bitcast_convert_type(x, jnp.uint32) & HI_MASK, F32)
            hl = jnp.concatenate([hi.astype(BF16), (x - hi).astype(BF16)], axis=1)
            later.append(_dot(hl, after2))
            tot.append(jnp.sum(x, axis=-1, keepdims=True))
        att = [None] * n
        for j in range(n - 1, -1, -1):
            a = jnp.exp(log_beta[:, j * B:(j + 1) * B] + later[j] + suffix)
            if before[j] is not None:
                a = jnp.where(before[j], a, 0.0)
            att[j] = a.astype(BF16)
            suffix = suffix + tot[j]
        att = att[0] if n == 1 else jnp.concatenate(att, axis=1)
        return _dot(att, vb_ref[keys, :]), suffix

    def first_near(q0):
        return max(q0 // B - SB_NEAR_BLOCKS + 1, 0)

    worst = None
    for q0 in range(0, s, G):
        rows = slice(q0, q0 + G)
        acc, suffix = span(q0, first_near(q0), (q0 + G) // B - 1, jnp.zeros((G, 1), F32))
        o_ref[rows, :] = acc.astype(o_ref.dtype)
        if first_near(q0) > 0:
            acc_ref[rows, :] = acc
            suf_ref[rows, :] = suffix
            worst = suffix if worst is None else jnp.maximum(worst, suffix)

    if worst is not None:
        @pl.when(jnp.max(worst) > SB_UNDERFLOW)
        def _():
            for q0 in range(0, s, G):
                if first_near(q0) == 0:
                    continue
                rows = slice(q0, q0 + G)
                suffix = suf_ref[rows, :]

                @pl.when(jnp.max(suffix) > SB_UNDERFLOW)
                def _():
                    far, _ = span(q0, 0, first_near(q0) - 1, suffix)
                    o_ref[rows, :] = (acc_ref[rows, :] + far).astype(o_ref.dtype)


def _mixer_d(proj, col0, n_heads, cast_jobs):
    bsz, s, _ = proj.shape
    c0 = col0 // HEAD_D
    grid = (bsz, n_heads)
    seq_spec = lambda off: pl.BlockSpec((None, s, HEAD_D), lambda b, h: (b, 0, off + h))
    return _call_with_casts(
        _mixer_d_kernel, cast_jobs(grid),
        out_shape=jax.ShapeDtypeStruct((bsz, s, n_heads * HEAD_D), BF16),
        grid=grid,
        in_specs=[seq_spec(c0), seq_spec(c0 + n_heads), seq_spec(c0 + 2 * n_heads)],
        out_specs=pl.BlockSpec((None, s, HEAD_D), lambda b, h: (b, 0, h)),
        scratch_shapes=[pltpu.VMEM((s, HEAD_D), BF16), pltpu.VMEM((s, HEAD_D), BF16),
                        pltpu.VMEM((s, HEAD_D), F32), pltpu.VMEM((s, 1), F32)],
        semantics=("arbitrary", "arbitrary"),
        name="mixer_d",
        operands=(proj, proj, proj))


def _layer_ab(x, bsz, s, g, w_in, conv_w, conv_b, w_r, b_r, w_i, b_i, lam, qk_g, rel_bias, w_out,
              casts):
    d_rg = conv_w.shape[-1]
    n_heads = rel_bias.shape[-1]
    proj = _norm_matmul(x, g, w_in.astype(BF16)).reshape(bsz, s, -1)
    ya, cast_a = _mixer_a(proj, conv_w, conv_b, w_r, b_r, w_i, b_i, lam, d_rg, casts[0])
    yb, cast_b = _mixer_b(proj, qk_g, rel_bias, 2 * d_rg, n_heads, casts[1])
    n = bsz * s
    return _out_proj(x, ya.reshape(n, -1), yb.reshape(n, -1), w_out.astype(BF16)), cast_a + cast_b


def _cd_weight_kernel(a_ref, b_ref, o_ref, *, n_aligned, n_shifted, n_gate):
    c = pl.program_id(0)
    w = o_ref.shape[1]
    lane = lax.broadcasted_iota(jnp.int32, o_ref.shape, 1)

    @pl.when(c < n_aligned)
    def _():
        o_ref[...] = a_ref[...].astype(o_ref.dtype)

    @pl.when((c >= n_aligned) & (c < n_aligned + n_shifted))
    def _():
        a = pltpu.roll(a_ref[...], w - n_gate, axis=1)
        b = pltpu.roll(b_ref[...], w - n_gate, axis=1)
        o_ref[...] = jnp.where(lane < w - n_gate, a, b).astype(o_ref.dtype)

    @pl.when(c >= n_aligned + n_shifted)
    def _():
        o_ref[...] = jnp.where(lane < n_gate, a_ref[...], 0.0).astype(o_ref.dtype)


def _cd_weight(w_in, j, gate_col, n_gate, width, tw=512):
    _, d, cols = w_in.shape
    main = cols - n_gate
    assert gate_col % tw == 0 and (main - gate_col) % tw == 0 and width == main + tw
    n_aligned = gate_col // tw
    n_shifted = (main - gate_col) // tw
    last = (cols - 1) // tw
    return pl.pallas_call(
        functools.partial(_cd_weight_kernel, n_aligned=n_aligned, n_shifted=n_shifted, n_gate=n_gate),
        out_shape=jax.ShapeDtypeStruct((d, width), BF16),
        grid=(width // tw,),
        in_specs=[
            pl.BlockSpec((None, d, tw),
                         lambda c: (j, 0, jnp.where(c < n_aligned + n_shifted, c, n_aligned))),
            pl.BlockSpec((None, d, tw), lambda c: (j, 0, jnp.minimum(c + 1, last))),
        ],
        out_specs=pl.BlockSpec((d, tw), lambda c: (0, c)),
        compiler_params=_cparams(("arbitrary",)),
        name="cd_weight",
    )(w_in, w_in)


def _layer_cd(x, bsz, s, g, w_in_all, j, conv_w, conv_b, gate_b, h_gain, w_out, casts):
    n_heads_c = gate_b.shape[-1]
    dc = n_heads_c * HEAD_C
    d_mix = w_out.shape[0]
    dd = d_mix - dc
    n_heads_d = dd // HEAD_D
    n_gate = 2 * n_heads_c
    main = 4 * dc + 3 * dd
    width = -(-(main + n_gate) // 512) * 512
    w = _cd_weight(w_in_all, j, 4 * dc, n_gate, width)
    proj = _norm_matmul(x, g, w, tn=768).reshape(bsz, s, width)
    gates = proj[:, :, main:main + n_gate]
    yd, cast_d = _mixer_d(proj, 4 * dc, n_heads_d, casts[0])
    yc, cast_c = _mixer_c(proj, gates, conv_w, conv_b, gate_b, h_gain, n_heads_c, casts[1])
    n = bsz * s
    return _out_proj(x, yc.reshape(n, -1), yd.reshape(n, -1), w_out.astype(BF16)), cast_d + cast_c


def kernel(x, norm_g, ffn_w_in, ffn_w_out, ab_w_in, ab_conv_w, ab_conv_b, rg_w_r, rg_b_r, rg_w_i,
           rg_b_i, rg_lambda, qk_gain, rel_bias, ab_w_out, cd_w_in, cd_conv_w, cd_conv_b,
           mlstm_gate_bias, mlstm_h_gain, cd_w_out):
    bsz, s, d = x.shape
    depth = norm_g.shape[0]
    h = x.reshape(bsz * s, d)
    ffn_w = {(0, 0): (ffn_w_in[0, 0].astype(BF16), ffn_w_out[0, 0].astype(BF16))}
    for layer in range(depth):
        j = layer // 2
        h = _ffn(h, norm_g[layer, 0], *ffn_w[(layer, 0)])
        targets = [(layer, 1)] + ([(layer + 1, 0)] if layer + 1 < depth else [])
        casts = [functools.partial(_ffn_cast_jobs, ffn_w_in, ffn_w_out, *t) for t in targets]
        casts += [lambda grid: []] * (2 - len(casts))
        if layer % 2 == 0:
            h, cast = _layer_ab(h, bsz, s, norm_g[layer, 1], ab_w_in[j], ab_conv_w[j], ab_conv_b[j],
                                rg_w_r[j], rg_b_r[j], rg_w_i[j], rg_b_i[j], rg_lambda[j], qk_gain[j],
                                rel_bias, ab_w_out[j], casts)
        else:
            h, cast = _layer_cd(h, bsz, s, norm_g[layer, 1], cd_w_in, j, cd_conv_w[j], cd_conv_b[j],
                                mlstm_gate_bias[j], mlstm_h_gain[j], cd_w_out[j], casts)
        for i, t in enumerate(targets):
            ffn_w[t] = (cast[2 * i], cast[2 * i + 1])
        h = _ffn(h, norm_g[layer, 2], *ffn_w[(layer, 1)])
    return h.reshape(bsz, s, d)
```

```python
import functools
import math

import numpy as np
import jax
import jax.numpy as jnp
from jax import lax
from jax.experimental import pallas as pl
from jax.experimental.pallas import tpu as pltpu

F32 = jnp.float32
BF16 = jnp.bfloat16

RMS_EPS = 1e-6
RG_C = 8.0
CONV_W = 4
RG_BLOCK = 128
HEAD_B = 128
HEAD_C = 256
HEAD_D = 128
SEQ_BLOCK = 128
SUBLANES = 8
BF16_SUBLANES = 16
DSW_PATTERNS = ((128, 1), (512, 4), (2048, 16))
N_BUCKETS = 32
MAX_DIST = 2048
NEG = -1e30
LOG2E = 1.4426950408889634
HI_MASK = np.uint32(0xFFFF0000)
SB_NEAR_BLOCKS = 3
SB_QUERY_ROWS = 256
SB_UNDERFLOW = -104.0
VMEM_LIMIT = 56 * 1024 * 1024


def _cparams(sem):
    return pltpu.CompilerParams(dimension_semantics=sem, vmem_limit_bytes=VMEM_LIMIT)


def _rms_rows(x, g):
    return x * lax.rsqrt(jnp.mean(x * x, axis=-1, keepdims=True) + RMS_EPS) * g


def _dot(a, b):
    return jnp.dot(a, b, preferred_element_type=F32)


def _dot_nt(a, b):
    return lax.dot_general(a, b, (((1,), (1,)), ((), ())), preferred_element_type=F32)


def _dot_tn(a, b):
    return lax.dot_general(a, b, (((0,), (0,)), ((), ())), preferred_element_type=F32)


def _pick_tile(n, pref):
    t = min(pref, n)
    while n % t:
        t //= 2
    return t


def _ffn_cast_jobs(w_in, w_out, layer, half, grid):
    n_steps = math.prod(grid)

    def step(*idx):
        t = idx[0]
        for i, g in zip(idx[1:], grid[1:]):
            t = t * g + i
        return t

    jobs = []
    for w in (w_in, w_out):
        rows, cols = w.shape[2:]
        n = n_steps
        while rows % n or (rows // n) % BF16_SUBLANES:
            n -= 1
        slab = rows // n
        blk = lambda *idx, n=n: jnp.minimum(step(*idx), n - 1)
        jobs.append((
            w,
            pl.BlockSpec((None, None, slab, cols), lambda *idx, blk=blk: (layer, half, blk(*idx), 0)),
            pl.BlockSpec((slab, cols), lambda *idx, blk=blk: (blk(*idx), 0)),
            jax.ShapeDtypeStruct((rows, cols), BF16),
        ))
    return jobs


def _call_with_casts(body, jobs, *, out_shape, grid, in_specs, out_specs, scratch_shapes=(),
                     semantics, name, operands):
    n_in = len(in_specs)
    n_jobs = len(jobs)

    def kernel(*refs):
        for i in range(n_jobs):
            dst = refs[n_in + n_jobs + 1 + i]
            dst[...] = refs[n_in + i][...].astype(dst.dtype)
        body(*refs[:n_in], refs[n_in + n_jobs], *refs[n_in + 2 * n_jobs + 1:])

    outs = pl.pallas_call(
        kernel,
        out_shape=[out_shape] + [j[3] for j in jobs],
        grid=grid,
        in_specs=list(in_specs) + [j[1] for j in jobs],
        out_specs=[out_specs] + [j[2] for j in jobs],
        scratch_shapes=list(scratch_shapes),
        compiler_params=_cparams(semantics),
        name=name,
    )(*operands, *[j[0] for j in jobs])
    return outs[0], list(outs[1:])


def _ffn_kernel(x_ref, g_ref, wg_ref, wu_ref, wo_ref, o_ref, xn_ref):
    j = pl.program_id(1)

    @pl.when(j == 0)
    def _():
        x = x_ref[...]
        xn_ref[...] = _rms_rows(x, g_ref[...]).astype(BF16)
        o_ref[...] = x

    xn = xn_ref[...]
    hg = _dot(xn, wg_ref[...])
    hu = _dot(xn, wu_ref[...])
    h = (0.5 * hg * jax.nn.sigmoid(hg) * hu).astype(BF16)
    o_ref[...] += _dot(h, wo_ref[...])


def _ffn(x, g, w_in, w_out, tm=1024, tf=512):
    n, d = x.shape
    d_ff = w_out.shape[0]
    tm = _pick_tile(n, tm)
    tf = _pick_tile(d_ff, tf)
    nj = d_ff // tf
    return pl.pallas_call(
        _ffn_kernel,
        out_shape=jax.ShapeDtypeStruct((n, d), F32),
        grid=(n // tm, nj),
        in_specs=[
            pl.BlockSpec((tm, d), lambda i, j: (i, 0)),
            pl.BlockSpec((1, d), lambda i, j: (0, 0)),
            pl.BlockSpec((d, tf), lambda i, j: (0, j)),
            pl.BlockSpec((d, tf), lambda i, j: (0, j + nj)),
            pl.BlockSpec((tf, d), lambda i, j: (j, 0)),
        ],
        out_specs=pl.BlockSpec((tm, d), lambda i, j: (i, 0)),
        scratch_shapes=[pltpu.VMEM((tm, d), BF16)],
        compiler_params=_cparams(("parallel", "arbitrary")),
        name="ffn",
    )(x, g.reshape(1, d), w_in, w_in, w_out)


def _norm_matmul_kernel(x_ref, g_ref, w_ref, o_ref, xn_ref):
    @pl.when(pl.program_id(1) == 0)
    def _():
        xn_ref[...] = _rms_rows(x_ref[...], g_ref[...]).astype(BF16)

    o_ref[...] = _dot(xn_ref[...], w_ref[...])


def _norm_matmul(x, g, w, tm=1024, tn=1024):
    n, d = x.shape
    wn = w.shape[1]
    tm = _pick_tile(n, tm)
    while wn % tn:
        tn -= 128
    return pl.pallas_call(
        _norm_matmul_kernel,
        out_shape=jax.ShapeDtypeStruct((n, wn), F32),
        grid=(n // tm, wn // tn),
        in_specs=[
            pl.BlockSpec((tm, d), lambda i, j: (i, 0)),
            pl.BlockSpec((1, d), lambda i, j: (0, 0)),
            pl.BlockSpec((d, tn), lambda i, j: (0, j)),
        ],
        out_specs=pl.BlockSpec((tm, tn), lambda i, j: (i, j)),
        scratch_shapes=[pltpu.VMEM((tm, d), BF16)],
        compiler_params=_cparams(("parallel", "arbitrary")),
        name="norm_matmul",
    )(x, g.reshape(1, d), w)


def _out_proj_kernel(x_ref, ya_ref, yb_ref, wa_ref, wb_ref, o_ref):
    o_ref[...] = x_ref[...] + _dot(ya_ref[...], wa_ref[...]) + _dot(yb_ref[...], wb_ref[...])


def _out_proj(x, ya, yb, w, tm=512):
    n, d = x.shape
    da = ya.shape[1]
    db = yb.shape[1]
    assert da == db and w.shape[0] == da + db
    tm = _pick_tile(n, tm)
    return pl.pallas_call(
        _out_proj_kernel,
        out_shape=jax.ShapeDtypeStruct((n, d), F32),
        grid=(n // tm,),
        in_specs=[
            pl.BlockSpec((tm, d), lambda i: (i, 0)),
            pl.BlockSpec((tm, da), lambda i: (i, 0)),
            pl.BlockSpec((tm, db), lambda i: (i, 0)),
            pl.BlockSpec((da, d), lambda i: (0, 0)),
            pl.BlockSpec((db, d), lambda i: (1, 0)),
        ],
        out_specs=pl.BlockSpec((tm, d), lambda i: (i, 0)),
        compiler_params=_cparams(("parallel",)),
        name="out_proj",
    )(x, ya, yb, w, w)


def _causal_conv(x, w, b):
    y = x * w[CONV_W - 1:CONV_W] + b
    for k in range(1, CONV_W):
        y = y + _shift_rows(x, k, 0.0) * w[CONV_W - 1 - k:CONV_W - k]
    return y


def _shift_rows(x, k, fill):
    assert 0 < k < SUBLANES
    rolled = pltpu.roll(x, k, axis=0)
    t = lax.broadcasted_iota(jnp.int32, (SUBLANES, x.shape[1]), 0)
    head = jnp.where(t >= k, rolled[:SUBLANES], fill)
    return jnp.concatenate([head, rolled[SUBLANES:]], axis=0)


def _mixer_a_kernel(xr_ref, gate_ref, cw_ref, cb_ref, wr_ref, br_ref, wi_ref, bi_ref,
                    lam_ref, o_ref):
    s = xr_ref.shape[0]
    xc = _causal_conv(xr_ref[...], cw_ref[...], cb_ref[...])
    xb = xc.astype(BF16)
    r = jax.nn.sigmoid(_dot(xb, wr_ref[...]) + br_ref[...])
    i = jax.nn.sigmoid(_dot(xb, wi_ref[...]) + bi_ref[...])
    neg_lam = -lam_ref[...]
    softplus = jnp.maximum(neg_lam, 0.0) + jnp.log1p(jnp.exp(-jnp.abs(neg_lam)))
    log_a = -RG_C * r * softplus
    a = jnp.exp(log_a)
    u = jnp.sqrt(-jnp.tanh(log_a) * (a * a + 1.0)) * (i * xc)
    k = 1
    while k < s:
        if k < SUBLANES:
            u = a * _shift_rows(u, k, 0.0) + u
            a = a * _shift_rows(a, k, 1.0)
        else:
            u = jnp.concatenate([u[:k], a[k:] * u[:s - k] + u[k:]], axis=0)
            a = jnp.concatenate([a[:k], a[k:] * a[:s - k]], axis=0)
        k *= 2
    o_ref[...] = (jax.nn.gelu(gate_ref[...]) * u).astype(o_ref.dtype)


def _mixer_a(proj, conv_w, conv_b, w_r, b_r, w_i, b_i, lam, d_rg, cast_jobs):
    bsz, s, _ = proj.shape
    nb = d_rg // RG_BLOCK
    grid = (bsz, nb)
    vec = lambda v: v.reshape(1, d_rg)
    col = lambda off: pl.BlockSpec((None, s, RG_BLOCK), lambda b, g: (b, 0, off + g))
    row = lambda r: pl.BlockSpec((r, RG_BLOCK), lambda b, g: (0, g))
    blk = pl.BlockSpec((None, RG_BLOCK, RG_BLOCK), lambda b, g: (g, 0, 0))
    return _call_with_casts(
        _mixer_a_kernel, cast_jobs(grid),
        out_shape=jax.ShapeDtypeStruct((bsz, s, d_rg), BF16),
        grid=grid,
        in_specs=[col(nb), col(0), row(CONV_W), row(1), blk, row(1), blk, row(1), row(1)],
        out_specs=pl.BlockSpec((None, s, RG_BLOCK), lambda b, g: (b, 0, g)),
        semantics=("arbitrary", "arbitrary"),
        name="mixer_a",
        operands=(proj, proj, conv_w, vec(conv_b), w_r.astype(BF16), vec(b_r), w_i.astype(BF16),
                  vec(b_i), vec(lam)))


def _t5_bucket_np(dist):
    max_exact = N_BUCKETS // 2
    d_f = np.maximum(dist, 1).astype(np.float32)
    large = max_exact + (np.log(d_f / np.float32(max_exact)) / np.float32(math.log(MAX_DIST / max_exact))
                         * np.float32(N_BUCKETS - max_exact)).astype(np.int32)
    large = np.minimum(large, N_BUCKETS - 1)
    return np.where(dist < max_exact, dist, large).astype(np.int32)


def _dsw_bias_vector(rel_bias, seq, qb):
    dist = np.arange(seq)
    mult = np.zeros(seq, np.float32)
    for window, dilation in DSW_PATTERNS:
        mult += ((dist % dilation == 0) & (dist <= window)).astype(np.float32)
    logm = np.where(mult > 0, np.log(np.maximum(mult, 1.0)), NEG).astype(np.float32)
    f = (rel_bias.astype(F32)[_t5_bucket_np(dist)] + logm[:, None]) * LOG2E
    n_heads = f.shape[1]
    hv = jnp.concatenate([f[::-1].T, jnp.full((n_heads, qb), NEG, F32)], axis=1)
    return hv[:, None, :]


def _mixer_b_kernel(q_ref, k_ref, v_ref, g_ref, hv_ref, o_ref, kn_ref, vb_ref, tbl_ref):
    s, hd = q_ref.shape
    qb = tbl_ref.shape[0]

    @pl.when(pl.program_id(1) == 0)
    def _():
        skew = pltpu.roll(jnp.broadcast_to(hv_ref[...], (qb, s + qb)), 1, axis=1, stride=1, stride_axis=0)
        tbl_ref[...] = skew[:, qb:]

    kn_ref[...] = _rms_rows(k_ref[...], g_ref[1:2]).astype(BF16)
    vb_ref[...] = v_ref[...].astype(BF16)
    for gi in range(s // qb):
        n_keys = (gi + 1) * qb
        rows = slice(gi * qb, n_keys)
        qn = (_rms_rows(q_ref[rows, :], g_ref[0:1]) * (LOG2E / math.sqrt(hd))).astype(BF16)
        sc = _dot_nt(qn, kn_ref[0:n_keys, :]) + tbl_ref[:, s - n_keys:]
        m = jnp.max(sc, axis=-1, keepdims=True)
        p = jnp.exp2(sc - m)
        l = jnp.sum(p, axis=-1, keepdims=True)
        o = _dot(p.astype(BF16), vb_ref[0:n_keys, :])
        o_ref[rows, :] = (o / l).astype(o_ref.dtype)


def _mixer_b(proj, qk_g, rel_bias, col0, n_heads, cast_jobs, qb=256):
    bsz, s, _ = proj.shape
    qb = _pick_tile(s, qb)
    hv = _dsw_bias_vector(rel_bias, s, qb)
    c0 = col0 // HEAD_B
    grid = (n_heads, bsz)
    seq_spec = lambda off: pl.BlockSpec((None, s, HEAD_B), lambda h, b: (b, 0, off + h))
    return _call_with_casts(
        _mixer_b_kernel, cast_jobs(grid),
        out_shape=jax.ShapeDtypeStruct((bsz, s, n_heads * HEAD_B), BF16),
        grid=grid,
        in_specs=[
            seq_spec(c0), seq_spec(c0 + n_heads), seq_spec(c0 + 2 * n_heads),
            pl.BlockSpec((2, HEAD_B), lambda h, b: (0, 0)),
            pl.BlockSpec((None, 1, s + qb), lambda h, b: (h, 0, 0)),
        ],
        out_specs=pl.BlockSpec((None, s, HEAD_B), lambda h, b: (b, 0, h)),
        scratch_shapes=[pltpu.VMEM((s, HEAD_B), BF16), pltpu.VMEM((s, HEAD_B), BF16),
                        pltpu.VMEM((qb, s), F32)],
        semantics=("arbitrary", "arbitrary"),
        name="mixer_b",
        operands=(proj, proj, proj, qk_g, hv))


def _mixer_c_kernel(gb_ref, q_ref, k_ref, v_ref, og_ref, cwq_ref, cwk_ref, cbq_ref, cbk_ref,
                    ig_ref, fg_ref, gain_ref, o_ref, qc_ref, kc_ref, c_ref, n_ref, m_ref):
    h = pl.program_id(1)
    L = SEQ_BLOCK
    s, d = q_ref.shape
    nc = s // L

    qc_ref[...] = jax.nn.silu(_causal_conv(q_ref[...], cwq_ref[...], cbq_ref[...]))
    kc_ref[...] = jax.nn.silu(_causal_conv(k_ref[...], cwk_ref[...], cbk_ref[...])) * (d ** -0.5)
    c_ref[...] = jnp.zeros_like(c_ref)
    n_ref[...] = jnp.zeros_like(n_ref)
    m_ref[...] = jnp.zeros_like(m_ref)

    ri = lax.broadcasted_iota(jnp.int32, (L, L), 0)
    ci = lax.broadcasted_iota(jnp.int32, (L, L), 1)
    eye = ri == ci
    causal = ci <= ri

    def to_col(row):
        return jnp.sum(jnp.where(eye, row, 0.0), axis=1, keepdims=True)

    ig_all = ig_ref[...] + gb_ref[0, h]
    f_all = fg_ref[...] + gb_ref[1, h]
    ls_all = jnp.minimum(f_all, 0.0) - jnp.log1p(jnp.exp(-jnp.abs(f_all)))

    for n in range(nc):
        rows = slice(n * L, (n + 1) * L)
        qc = qc_ref[rows, :]
        kc = kc_ref[rows, :]
        qb = qc.astype(BF16)
        kb = kc.astype(BF16)
        vb = v_ref[rows, :].astype(BF16)

        ig_row = ig_all[n:n + 1]
        ls_row = ls_all[n:n + 1]
        ls_col = to_col(ls_row)
        ig_col = to_col(ig_row)
        b_col = jnp.sum(jnp.where(causal, ls_row, 0.0), axis=1, keepdims=True)
        b_row = jnp.sum(jnp.where(ri <= ci, ls_col, 0.0), axis=0, keepdims=True)
        b_last = jnp.sum(ls_row, axis=1, keepdims=True)

        log_d = b_col - b_row + ig_row
        g_row = b_last - b_row + ig_row
        g_col = b_last - b_col + ig_col
        g_max = jnp.max(g_row, axis=1, keepdims=True)
        d_max = jnp.max(jnp.where(causal, log_d, -jnp.inf), axis=1, keepdims=True)
        qk = _dot_nt(qb, kb)

        c_st = c_ref[...]
        n_st = n_ref[...]
        m_st = m_ref[...]

        inter = b_col + m_st
        m_t = jnp.maximum(inter, d_max)
        s_qk = jnp.where(causal, qk * jnp.exp(log_d - m_t), 0.0)
        inter_w = jnp.exp(inter - m_t)
        num = _dot(s_qk.astype(BF16), vb) + inter_w * _dot(qb, c_st.astype(BF16))
        qn = jnp.sum(qc * n_st, axis=1, keepdims=True)
        den = jnp.sum(s_qk, axis=1, keepdims=True) + inter_w * qn
        hout = num / jnp.maximum(jnp.abs(den), jnp.exp(-m_t))

        hn = _rms_rows(hout, gain_ref[...])
        o_ref[rows, :] = (jax.nn.sigmoid(og_ref[rows, :]) * hn).astype(o_ref.dtype)

        wk = jnp.exp(g_col - g_max) * kc
        c_loc = _dot_tn(wk.astype(BF16), vb)
        n_loc = jnp.sum(wk, axis=0, keepdims=True)
        m_new = jnp.maximum(b_last + m_st, g_max)
        decay = jnp.exp(b_last + m_st - m_new)
        fresh = jnp.exp(g_max - m_new)
        c_ref[...] = decay * c_st + fresh * c_loc
        n_ref[...] = decay * n_st + fresh * n_loc
        m_ref[...] = m_new


def _mixer_c(proj, gates, conv_w, conv_b, gate_b, h_gain, n_heads, cast_jobs):
    bsz, s, _ = proj.shape
    d = HEAD_C
    L = SEQ_BLOCK
    nc = s // L
    grid = (bsz, n_heads)
    g_rows = jnp.transpose(gates, (0, 2, 1)).reshape(bsz, 2 * n_heads, nc, L)
    cb = conv_b.reshape(1, -1)
    full = lambda off: pl.BlockSpec((None, s, d), lambda b, h: (b, 0, off + h))
    row = lambda r, off: pl.BlockSpec((r, d), lambda b, h: (0, off + h))
    grow = lambda off: pl.BlockSpec((None, None, nc, L), lambda b, h: (b, off + h, 0, 0))
    return _call_with_casts(
        _mixer_c_kernel, cast_jobs(grid),
        out_shape=jax.ShapeDtypeStruct((bsz, s, n_heads * d), BF16),
        grid=grid,
        in_specs=[
            pl.BlockSpec(memory_space=pltpu.SMEM),
            full(0), full(n_heads), full(2 * n_heads), full(3 * n_heads),
            row(CONV_W, 0), row(CONV_W, n_heads), row(1, 0), row(1, n_heads),
            grow(0), grow(n_heads), row(1, 0),
        ],
        out_specs=pl.BlockSpec((None, s, d), lambda b, h: (b, 0, h)),
        scratch_shapes=[pltpu.VMEM((s, d), F32), pltpu.VMEM((s, d), F32),
                        pltpu.VMEM((d, d), F32), pltpu.VMEM((1, d), F32), pltpu.VMEM((1, 1), F32)],
        semantics=("arbitrary", "arbitrary"),
        name="mixer_c",
        operands=(gate_b, proj, proj, proj, proj, conv_w, conv_w, cb, cb, g_rows, g_rows,
                  h_gain.reshape(1, -1)))


def _mixer_d_kernel(q_ref, k_ref, v_ref, o_ref, kb_ref, vb_ref, acc_ref, suf_ref):
    s, hd = q_ref.shape
    B = SEQ_BLOCK
    G = min(SB_QUERY_ROWS, s)
    kb_ref[...] = k_ref[...].astype(BF16)
    vb_ref[...] = v_ref[...].astype(BF16)
    ri = lax.broadcasted_iota(jnp.int32, (B, B), 0)
    ci = lax.broadcasted_iota(jnp.int32, (B, B), 1)
    after = (ri > ci).astype(BF16)
    after2 = jnp.concatenate([after, after], axis=0)
    row_g = lax.broadcasted_iota(jnp.int32, (G, B), 0)
    col_g = lax.broadcasted_iota(jnp.int32, (G, B), 1)

    def span(q0, first, last, suffix):
        n = last - first + 1
        keys = slice(first * B, (last + 1) * B)
        qb = (q_ref[q0:q0 + G, :] * (hd ** -0.5)).astype(BF16)
        z = _dot_nt(qb, kb_ref[keys, :])
        sp = jnp.log(1.0 + jnp.exp2(jnp.abs(z) * (-LOG2E)))
        nm = jnp.minimum(z, 0.0)
        log_beta = nm - sp
        log_stay = (nm - z) - sp
        before = [col_g + ((first + j) * B - q0) < row_g if (first + j + 1) * B > q0 else None
                  for j in range(n)]
        later, tot = [], []
        for j in range(n):
            x = log_stay[:, j * B:(j + 1) * B]
            if before[j] is not None:
                x = jnp.where(before[j], x, 0.0)
            hi = lax.bitcast_convert_type(lax.bitcast_convert_type(x, jnp.uint32) & HI_MASK, F32)
            hl = jnp.concatenate([hi.astype(BF16), (x - hi).astype(BF16)], axis=1)
            later.append(_dot(hl, after2))
            tot.append(jnp.sum(x, axis=-1, keepdims=True))
        att = [None] * n
        for j in range(n - 1, -1, -1):
            a = jnp.exp(log_beta[:, j * B:(j + 1) * B] + later[j] + suffix)
            if before[j] is not None:
                a = jnp.where(before[j], a, 0.0)
            att[j] = a.astype(BF16)
            suffix = suffix + tot[j]
        att = att[0] if n == 1 else jnp.concatenate(att, axis=1)
        return _dot(att, vb_ref[keys, :]), suffix

    def first_near(q0):
        return max(q0 // B - SB_NEAR_BLOCKS + 1, 0)

    worst = None
    for q0 in range(0, s, G):
        rows = slice(q0, q0 + G)
        acc, suffix = span(q0, first_near(q0), (q0 + G) // B - 1, jnp.zeros((G, 1), F32))
        o_ref[rows, :] = acc.astype(o_ref.dtype)
        if first_near(q0) > 0:
            acc_ref[rows, :] = acc
            suf_ref[rows, :] = suffix
            worst = suffix if worst is None else jnp.maximum(worst, suffix)

    if worst is not None:
        @pl.when(jnp.max(worst) > SB_UNDERFLOW)
        def _():
            for q0 in range(0, s, G):
                if first_near(q0) == 0:
                    continue
                rows = slice(q0, q0 + G)
                suffix = suf_ref[rows, :]

                @pl.when(jnp.max(suffix) > SB_UNDERFLOW)
                def _():
                    far, _ = span(q0, 0, first_near(q0) - 1, suffix)
                    o_ref[rows, :] = (acc_ref[rows, :] + far).astype(o_ref.dtype)


def _mixer_d(proj, col0, n_heads, cast_jobs):
    bsz, s, _ = proj.shape
    c0 = col0 // HEAD_D
    grid = (bsz, n_heads)
    seq_spec = lambda off: pl.BlockSpec((None, s, HEAD_D), lambda b, h: (b, 0, off + h))
    return _call_with_casts(
        _mixer_d_kernel, cast_jobs(grid),
        out_shape=jax.ShapeDtypeStruct((bsz, s, n_heads * HEAD_D), BF16),
        grid=grid,
        in_specs=[seq_spec(c0), seq_spec(c0 + n_heads), seq_spec(c0 + 2 * n_heads)],
        out_specs=pl.BlockSpec((None, s, HEAD_D), lambda b, h: (b, 0, h)),
        scratch_shapes=[pltpu.VMEM((s, HEAD_D), BF16), pltpu.VMEM((s, HEAD_D), BF16),
                        pltpu.VMEM((s, HEAD_D), F32), pltpu.VMEM((s, 1), F32)],
        semantics=("arbitrary", "arbitrary"),
        name="mixer_d",
        operands=(proj, proj, proj))


def _layer_ab(x, bsz, s, g, w_in, conv_w, conv_b, w_r, b_r, w_i, b_i, lam, qk_g, rel_bias, w_out,
              casts):
    d_rg = conv_w.shape[-1]
    n_heads = rel_bias.shape[-1]
    proj = _norm_matmul(x, g, w_in.astype(BF16), tn=1280).reshape(bsz, s, -1)
    ya, cast_a = _mixer_a(proj, conv_w, conv_b, w_r, b_r, w_i, b_i, lam, d_rg, casts[0])
    yb, cast_b = _mixer_b(proj, qk_g, rel_bias, 2 * d_rg, n_heads, casts[1])
    n = bsz * s
    return _out_proj(x, ya.reshape(n, -1), yb.reshape(n, -1), w_out.astype(BF16)), cast_a + cast_b


def _cd_weight_kernel(a_ref, b_ref, o_ref, *, n_aligned, n_shifted, n_gate):
    c = pl.program_id(0)
    w = o_ref.shape[1]
    lane = lax.broadcasted_iota(jnp.int32, o_ref.shape, 1)

    @pl.when(c < n_aligned)
    def _():
        o_ref[...] = a_ref[...].astype(o_ref.dtype)

    @pl.when((c >= n_aligned) & (c < n_aligned + n_shifted))
    def _():
        a = pltpu.roll(a_ref[...], w - n_gate, axis=1)
        b = pltpu.roll(b_ref[...], w - n_gate, axis=1)
        o_ref[...] = jnp.where(lane < w - n_gate, a, b).astype(o_ref.dtype)

    @pl.when(c >= n_aligned + n_shifted)
    def _():
        o_ref[...] = jnp.where(lane < n_gate, a_ref[...], 0.0).astype(o_ref.dtype)


def _cd_weight(w_in, j, gate_col, n_gate, width, tw=512):
    _, d, cols = w_in.shape
    main = cols - n_gate
    assert gate_col % tw == 0 and (main - gate_col) % tw == 0 and width == main + tw
    n_aligned = gate_col // tw
    n_shifted = (main - gate_col) // tw
    last = (cols - 1) // tw
    return pl.pallas_call(
        functools.partial(_cd_weight_kernel, n_aligned=n_aligned, n_shifted=n_shifted, n_gate=n_gate),
        out_shape=jax.ShapeDtypeStruct((d, width), BF16),
        grid=(width // tw,),
        in_specs=[
            pl.BlockSpec((None, d, tw),
                         lambda c: (j, 0, jnp.where(c < n_aligned + n_shifted, c, n_aligned))),
            pl.BlockSpec((None, d, tw), lambda c: (j, 0, jnp.minimum(c + 1, last))),
        ],
        out_specs=pl.BlockSpec((d, tw), lambda c: (0, c)),
        compiler_params=_cparams(("arbitrary",)),
        name="cd_weight",
    )(w_in, w_in)


def _layer_cd(x, bsz, s, g, w_in_all, j, conv_w, conv_b, gate_b, h_gain, w_out, casts):
    n_heads_c = gate_b.shape[-1]
    dc = n_heads_c * HEAD_C
    d_mix = w_out.shape[0]
    dd = d_mix - dc
    n_heads_d = dd // HEAD_D
    n_gate = 2 * n_heads_c
    main = 4 * dc + 3 * dd
    width = -(-(main + n_gate) // 512) * 512
    w = _cd_weight(w_in_all, j, 4 * dc, n_gate, width)
    proj = _norm_matmul(x, g, w, tn=1536).reshape(bsz, s, width)
    gates = proj[:, :, main:main + n_gate]
    yd, cast_d = _mixer_d(proj, 4 * dc, n_heads_d, casts[0])
    yc, cast_c = _mixer_c(proj, gates, conv_w, conv_b, gate_b, h_gain, n_heads_c, casts[1])
    n = bsz * s
    return _out_proj(x, yc.reshape(n, -1), yd.reshape(n, -1), w_out.astype(BF16)), cast_d + cast_c


def kernel(x, norm_g, ffn_w_in, ffn_w_out, ab_w_in, ab_conv_w, ab_conv_b, rg_w_r, rg_b_r, rg_w_i,
           rg_b_i, rg_lambda, qk_gain, rel_bias, ab_w_out, cd_w_in, cd_conv_w, cd_conv_b,
           mlstm_gate_bias, mlstm_h_gain, cd_w_out):
    bsz, s, d = x.shape
    depth = norm_g.shape[0]
    h = x.reshape(bsz * s, d)
    ffn_w = {(0, 0): (ffn_w_in[0, 0].astype(BF16), ffn_w_out[0, 0].astype(BF16))}
    for layer in range(depth):
        j = layer // 2
        h = _ffn(h, norm_g[layer, 0], *ffn_w[(layer, 0)])
        targets = [(layer, 1)] + ([(layer + 1, 0)] if layer + 1 < depth else [])
        casts = [functools.partial(_ffn_cast_jobs, ffn_w_in, ffn_w_out, *t) for t in targets]
        casts += [lambda grid: []] * (2 - len(casts))
        if layer % 2 == 0:
            h, cast = _layer_ab(h, bsz, s, norm_g[layer, 1], ab_w_in[j], ab_conv_w[j], ab_conv_b[j],
                                rg_w_r[j], rg_b_r[j], rg_w_i[j], rg_b_i[j], rg_lambda[j], qk_gain[j],
                                rel_bias, ab_w_out[j], casts)
        else:
            h, cast = _layer_cd(h, bsz, s, norm_g[layer, 1], cd_w_in, j, cd_conv_w[j], cd_conv_b[j],
                                mlstm_gate_bias[j], mlstm_h_gain[j], cd_w_out[j], casts)
        for i, t in enumerate(targets):
            ffn_w[t] = (cast[2 * i], cast[2 * i + 1])
        h = _ffn(h, norm_g[layer, 2], *ffn_w[(layer, 1)])
    return h.reshape(bsz, s, d)
```

```python
import functools
import math

import numpy as np
import jax
import jax.numpy as jnp
from jax import lax
from jax.experimental import pallas as pl
from jax.experimental.pallas import tpu as pltpu

F32 = jnp.float32
BF16 = jnp.bfloat16

RMS_EPS = 1e-6
RG_C = 8.0
CONV_W = 4
RG_BLOCK = 128
HEAD_B = 128
HEAD_C = 256
HEAD_D = 128
SEQ_BLOCK = 128
SUBLANES = 8
BF16_SUBLANES = 16
DSW_PATTERNS = ((128, 1), (512, 4), (2048, 16))
N_BUCKETS = 32
MAX_DIST = 2048
NEG = -1e30
LOG2E = 1.4426950408889634
HI_MASK = np.uint32(0xFFFF0000)
SB_NEAR_BLOCKS = 3
SB_QUERY_ROWS = 256
SB_UNDERFLOW = -104.0
VMEM_LIMIT = 56 * 1024 * 1024


def _cparams(sem):
    return pltpu.CompilerParams(dimension_semantics=sem, vmem_limit_bytes=VMEM_LIMIT)


def _rms_rows(x, g):
    return x * lax.rsqrt(jnp.mean(x * x, axis=-1, keepdims=True) + RMS_EPS) * g


def _dot(a, b):
    return jnp.dot(a, b, preferred_element_type=F32)


def _dot_nt(a, b):
    return lax.dot_general(a, b, (((1,), (1,)), ((), ())), preferred_element_type=F32)


def _dot_tn(a, b):
    return lax.dot_general(a, b, (((0,), (0,)), ((), ())), preferred_element_type=F32)


def _pick_tile(n, pref):
    t = min(pref, n)
    while n % t:
        t //= 2
    return t


def _ffn_cast_jobs(w_in, w_out, layer, half, grid):
    n_steps = math.prod(grid)

    def step(*idx):
        t = idx[0]
        for i, g in zip(idx[1:], grid[1:]):
            t = t * g + i
        return t

    jobs = []
    for w in (w_in, w_out):
        rows, cols = w.shape[2:]
        n = n_steps
        while rows % n or (rows // n) % BF16_SUBLANES:
            n -= 1
        slab = rows // n
        blk = lambda *idx, n=n: jnp.minimum(step(*idx), n - 1)
        jobs.append((
            w,
            pl.BlockSpec((None, None, slab, cols), lambda *idx, blk=blk: (layer, half, blk(*idx), 0)),
            pl.BlockSpec((slab, cols), lambda *idx, blk=blk: (blk(*idx), 0)),
            jax.ShapeDtypeStruct((rows, cols), BF16),
        ))
    return jobs


def _call_with_casts(body, jobs, *, out_shape, grid, in_specs, out_specs, scratch_shapes=(),
                     semantics, name, operands):
    n_in = len(in_specs)
    n_jobs = len(jobs)

    def kernel(*refs):
        for i in range(n_jobs):
            dst = refs[n_in + n_jobs + 1 + i]
            dst[...] = refs[n_in + i][...].astype(dst.dtype)
        body(*refs[:n_in], refs[n_in + n_jobs], *refs[n_in + 2 * n_jobs + 1:])

    outs = pl.pallas_call(
        kernel,
        out_shape=[out_shape] + [j[3] for j in jobs],
        grid=grid,
        in_specs=list(in_specs) + [j[1] for j in jobs],
        out_specs=[out_specs] + [j[2] for j in jobs],
        scratch_shapes=list(scratch_shapes),
        compiler_params=_cparams(semantics),
        name=name,
    )(*operands, *[j[0] for j in jobs])
    return outs[0], list(outs[1:])


def _ffn_kernel(x_ref, g_ref, wg_ref, wu_ref, wo_ref, o_ref, xn_ref):
    j = pl.program_id(1)

    @pl.when(j == 0)
    def _():
        x = x_ref[...]
        xn_ref[...] = _rms_rows(x, g_ref[...]).astype(BF16)
        o_ref[...] = x

    xn = xn_ref[...]
    hg = _dot(xn, wg_ref[...])
    hu = _dot(xn, wu_ref[...])
    h = (0.5 * hg * jax.nn.sigmoid(hg) * hu).astype(BF16)
    o_ref[...] += _dot(h, wo_ref[...])


def _ffn(x, g, w_in, w_out, tm=1024, tf=512):
    n, d = x.shape
    d_ff = w_out.shape[0]
    tm = _pick_tile(n, tm)
    tf = _pick_tile(d_ff, tf)
    nj = d_ff // tf
    return pl.pallas_call(
        _ffn_kernel,
        out_shape=jax.ShapeDtypeStruct((n, d), F32),
        grid=(n // tm, nj),
        in_specs=[
            pl.BlockSpec((tm, d), lambda i, j: (i, 0)),
            pl.BlockSpec((1, d), lambda i, j: (0, 0)),
            pl.BlockSpec((d, tf), lambda i, j: (0, j)),
            pl.BlockSpec((d, tf), lambda i, j: (0, j + nj)),
            pl.BlockSpec((tf, d), lambda i, j: (j, 0)),
        ],
        out_specs=pl.BlockSpec((tm, d), lambda i, j: (i, 0)),
        scratch_shapes=[pltpu.VMEM((tm, d), BF16)],
        compiler_params=_cparams(("parallel", "arbitrary")),
        name="ffn",
    )(x, g.reshape(1, d), w_in, w_in, w_out)


def _norm_matmul_kernel(x_ref, g_ref, w_ref, o_ref, xn_ref):
    @pl.when(pl.program_id(1) == 0)
    def _():
        xn_ref[...] = _rms_rows(x_ref[...], g_ref[...]).astype(BF16)

    o_ref[...] = _dot(xn_ref[...], w_ref[...])


def _norm_matmul(x, g, w, tm=1024, tn=1024):
    n, d = x.shape
    wn = w.shape[1]
    tm = _pick_tile(n, tm)
    while wn % tn:
        tn -= 128
    return pl.pallas_call(
        _norm_matmul_kernel,
        out_shape=jax.ShapeDtypeStruct((n, wn), F32),
        grid=(n // tm, wn // tn),
        in_specs=[
            pl.BlockSpec((tm, d), lambda i, j: (i, 0)),
            pl.BlockSpec((1, d), lambda i, j: (0, 0)),
            pl.BlockSpec((d, tn), lambda i, j: (0, j)),
        ],
        out_specs=pl.BlockSpec((tm, tn), lambda i, j: (i, j)),
        scratch_shapes=[pltpu.VMEM((tm, d), BF16)],
        compiler_params=_cparams(("parallel", "arbitrary")),
        name="norm_matmul",
    )(x, g.reshape(1, d), w)


def _out_proj_kernel(x_ref, ya_ref, yb_ref, wa_ref, wb_ref, o_ref):
    o_ref[...] = x_ref[...] + _dot(ya_ref[...], wa_ref[...]) + _dot(yb_ref[...], wb_ref[...])


def _out_proj(x, ya, yb, w, tm=512):
    n, d = x.shape
    da = ya.shape[1]
    db = yb.shape[1]
    assert da == db and w.shape[0] == da + db
    tm = _pick_tile(n, tm)
    return pl.pallas_call(
        _out_proj_kernel,
        out_shape=jax.ShapeDtypeStruct((n, d), F32),
        grid=(n // tm,),
        in_specs=[
            pl.BlockSpec((tm, d), lambda i: (i, 0)),
            pl.BlockSpec((tm, da), lambda i: (i, 0)),
            pl.BlockSpec((tm, db), lambda i: (i, 0)),
            pl.BlockSpec((da, d), lambda i: (0, 0)),
            pl.BlockSpec((db, d), lambda i: (1, 0)),
        ],
        out_specs=pl.BlockSpec((tm, d), lambda i: (i, 0)),
        compiler_params=_cparams(("parallel",)),
        name="out_proj",
    )(x, ya, yb, w, w)


def _causal_conv(x, w, b):
    y = x * w[CONV_W - 1:CONV_W] + b
    for k in range(1, CONV_W):
        y = y + _shift_rows(x, k, 0.0) * w[CONV_W - 1 - k:CONV_W - k]
    return y


def _shift_rows(x, k, fill):
    assert 0 < k < SUBLANES
    rolled = pltpu.roll(x, k, axis=0)
    t = lax.broadcasted_iota(jnp.int32, (SUBLANES, x.shape[1]), 0)
    head = jnp.where(t >= k, rolled[:SUBLANES], fill)
    return jnp.concatenate([head, rolled[SUBLANES:]], axis=0)


def _mixer_a_kernel(xr_ref, gate_ref, cw_ref, cb_ref, wr_ref, br_ref, wi_ref, bi_ref,
                    lam_ref, o_ref):
    s = xr_ref.shape[0]
    xc = _causal_conv(xr_ref[...], cw_ref[...], cb_ref[...])
    xb = xc.astype(BF16)
    r = jax.nn.sigmoid(_dot(xb, wr_ref[...]) + br_ref[...])
    i = jax.nn.sigmoid(_dot(xb, wi_ref[...]) + bi_ref[...])
    neg_lam = -lam_ref[...]
    softplus = jnp.maximum(neg_lam, 0.0) + jnp.log1p(jnp.exp(-jnp.abs(neg_lam)))
    log_a = -RG_C * r * softplus
    a = jnp.exp(log_a)
    u = jnp.sqrt(-jnp.tanh(log_a) * (a * a + 1.0)) * (i * xc)
    k = 1
    while k < s:
        if k < SUBLANES:
            u = a * _shift_rows(u, k, 0.0) + u
            a = a * _shift_rows(a, k, 1.0)
        else:
            u = jnp.concatenate([u[:k], a[k:] * u[:s - k] + u[k:]], axis=0)
            a = jnp.concatenate([a[:k], a[k:] * a[:s - k]], axis=0)
        k *= 2
    o_ref[...] = (jax.nn.gelu(gate_ref[...]) * u).astype(o_ref.dtype)


def _mixer_a(proj, conv_w, conv_b, w_r, b_r, w_i, b_i, lam, d_rg, cast_jobs):
    bsz, s, _ = proj.shape
    nb = d_rg // RG_BLOCK
    grid = (bsz, nb)
    vec = lambda v: v.reshape(1, d_rg)
    col = lambda off: pl.BlockSpec((None, s, RG_BLOCK), lambda b, g: (b, 0, off + g))
    row = lambda r: pl.BlockSpec((r, RG_BLOCK), lambda b, g: (0, g))
    blk = pl.BlockSpec((None, RG_BLOCK, RG_BLOCK), lambda b, g: (g, 0, 0))
    return _call_with_casts(
        _mixer_a_kernel, cast_jobs(grid),
        out_shape=jax.ShapeDtypeStruct((bsz, s, d_rg), BF16),
        grid=grid,
        in_specs=[col(nb), col(0), row(CONV_W), row(1), blk, row(1), blk, row(1), row(1)],
        out_specs=pl.BlockSpec((None, s, RG_BLOCK), lambda b, g: (b, 0, g)),
        semantics=("arbitrary", "arbitrary"),
        name="mixer_a",
        operands=(proj, proj, conv_w, vec(conv_b), w_r.astype(BF16), vec(b_r), w_i.astype(BF16),
                  vec(b_i), vec(lam)))


def _t5_bucket_np(dist):
    max_exact = N_BUCKETS // 2
    d_f = np.maximum(dist, 1).astype(np.float32)
    large = max_exact + (np.log(d_f / np.float32(max_exact)) / np.float32(math.log(MAX_DIST / max_exact))
                         * np.float32(N_BUCKETS - max_exact)).astype(np.int32)
    large = np.minimum(large, N_BUCKETS - 1)
    return np.where(dist < max_exact, dist, large).astype(np.int32)


def _dsw_bias_vector(rel_bias, seq, qb):
    dist = np.arange(seq)
    mult = np.zeros(seq, np.float32)
    for window, dilation in DSW_PATTERNS:
        mult += ((dist % dilation == 0) & (dist <= window)).astype(np.float32)
    logm = np.where(mult > 0, np.log(np.maximum(mult, 1.0)), NEG).astype(np.float32)
    f = (rel_bias.astype(F32)[_t5_bucket_np(dist)] + logm[:, None]) * LOG2E
    n_heads = f.shape[1]
    hv = jnp.concatenate([f[::-1].T, jnp.full((n_heads, qb), NEG, F32)], axis=1)
    return hv[:, None, :]


def _mixer_b_kernel(q_ref, k_ref, v_ref, g_ref, hv_ref, o_ref, kn_ref, vb_ref, tbl_ref):
    s, hd = q_ref.shape
    qb = tbl_ref.shape[0]

    @pl.when(pl.program_id(1) == 0)
    def _():
        skew = pltpu.roll(jnp.broadcast_to(hv_ref[...], (qb, s + qb)), 1, axis=1, stride=1, stride_axis=0)
        tbl_ref[...] = skew[:, qb:]

    kn_ref[...] = _rms_rows(k_ref[...], g_ref[1:2]).astype(BF16)
    vb_ref[...] = v_ref[...].astype(BF16)
    for gi in range(s // qb):
        n_keys = (gi + 1) * qb
        rows = slice(gi * qb, n_keys)
        qn = (_rms_rows(q_ref[rows, :], g_ref[0:1]) * (LOG2E / math.sqrt(hd))).astype(BF16)
        sc = _dot_nt(qn, kn_ref[0:n_keys, :]) + tbl_ref[:, s - n_keys:]
        m = jnp.max(sc, axis=-1, keepdims=True)
        p = jnp.exp2(sc - m)
        l = jnp.sum(p, axis=-1, keepdims=True)
        o = _dot(p.astype(BF16), vb_ref[0:n_keys, :])
        o_ref[rows, :] = (o / l).astype(o_ref.dtype)


def _mixer_b(proj, qk_g, rel_bias, col0, n_heads, cast_jobs, qb=256):
    bsz, s, _ = proj.shape
    qb = _pick_tile(s, qb)
    hv = _dsw_bias_vector(rel_bias, s, qb)
    c0 = col0 // HEAD_B
    grid = (n_heads, bsz)
    seq_spec = lambda off: pl.BlockSpec((None, s, HEAD_B), lambda h, b: (b, 0, off + h))
    return _call_with_casts(
        _mixer_b_kernel, cast_jobs(grid),
        out_shape=jax.ShapeDtypeStruct((bsz, s, n_heads * HEAD_B), BF16),
        grid=grid,
        in_specs=[
            seq_spec(c0), seq_spec(c0 + n_heads), seq_spec(c0 + 2 * n_heads),
            pl.BlockSpec((2, HEAD_B), lambda h, b: (0, 0)),
            pl.BlockSpec((None, 1, s + qb), lambda h, b: (h, 0, 0)),
        ],
        out_specs=pl.BlockSpec((None, s, HEAD_B), lambda h, b: (b, 0, h)),
        scratch_shapes=[pltpu.VMEM((s, HEAD_B), BF16), pltpu.VMEM((s, HEAD_B), BF16),
                        pltpu.VMEM((qb, s), F32)],
        semantics=("arbitrary", "arbitrary"),
        name="mixer_b",
        operands=(proj, proj, proj, qk_g, hv))


def _mixer_c_kernel(gb_ref, q_ref, k_ref, v_ref, og_ref, cwq_ref, cwk_ref, cbq_ref, cbk_ref,
                    ig_ref, fg_ref, gain_ref, o_ref, qc_ref, kc_ref, c_ref, n_ref, m_ref):
    h = pl.program_id(1)
    L = SEQ_BLOCK
    s, d = q_ref.shape
    nc = s // L

    qc_ref[...] = jax.nn.silu(_causal_conv(q_ref[...], cwq_ref[...], cbq_ref[...]))
    kc_ref[...] = jax.nn.silu(_causal_conv(k_ref[...], cwk_ref[...], cbk_ref[...])) * (d ** -0.5)
    c_ref[...] = jnp.zeros_like(c_ref)
    n_ref[...] = jnp.zeros_like(n_ref)
    m_ref[...] = jnp.zeros_like(m_ref)

    ri = lax.broadcasted_iota(jnp.int32, (L, L), 0)
    ci = lax.broadcasted_iota(jnp.int32, (L, L), 1)
    eye = ri == ci
    causal = ci <= ri

    def to_col(row):
        return jnp.sum(jnp.where(eye, row, 0.0), axis=1, keepdims=True)

    ig_all = ig_ref[...] + gb_ref[0, h]
    f_all = fg_ref[...] + gb_ref[1, h]
    ls_all = jnp.minimum(f_all, 0.0) - jnp.log1p(jnp.exp(-jnp.abs(f_all)))

    for n in range(nc):
        rows = slice(n * L, (n + 1) * L)
        qc = qc_ref[rows, :]
        kc = kc_ref[rows, :]
        qb = qc.astype(BF16)
        kb = kc.astype(BF16)
        vb = v_ref[rows, :].astype(BF16)

        ig_row = ig_all[n:n + 1]
        ls_row = ls_all[n:n + 1]
        ls_col = to_col(ls_row)
        ig_col = to_col(ig_row)
        b_col = jnp.sum(jnp.where(causal, ls_row, 0.0), axis=1, keepdims=True)
        b_row = jnp.sum(jnp.where(ri <= ci, ls_col, 0.0), axis=0, keepdims=True)
        b_last = jnp.sum(ls_row, axis=1, keepdims=True)

        log_d = b_col - b_row + ig_row
        g_row = b_last - b_row + ig_row
        g_col = b_last - b_col + ig_col
        g_max = jnp.max(g_row, axis=1, keepdims=True)
        d_max = jnp.max(jnp.where(causal, log_d, -jnp.inf), axis=1, keepdims=True)
        qk = _dot_nt(qb, kb)

        c_st = c_ref[...]
        n_st = n_ref[...]
        m_st = m_ref[...]

        inter = b_col + m_st
        m_t = jnp.maximum(inter, d_max)
        s_qk = jnp.where(causal, qk * jnp.exp(log_d - m_t), 0.0)
        inter_w = jnp.exp(inter - m_t)
        num = _dot(s_qk.astype(BF16), vb) + inter_w * _dot(qb, c_st.astype(BF16))
        qn = jnp.sum(qc * n_st, axis=1, keepdims=True)
        den = jnp.sum(s_qk, axis=1, keepdims=True) + inter_w * qn
        hout = num / jnp.maximum(jnp.abs(den), jnp.exp(-m_t))

        hn = _rms_rows(hout, gain_ref[...])
        o_ref[rows, :] = (jax.nn.sigmoid(og_ref[rows, :]) * hn).astype(o_ref.dtype)

        wk = jnp.exp(g_col - g_max) * kc
        c_loc = _dot_tn(wk.astype(BF16), vb)
        n_loc = jnp.sum(wk, axis=0, keepdims=True)
        m_new = jnp.maximum(b_last + m_st, g_max)
        decay = jnp.exp(b_last + m_st - m_new)
        fresh = jnp.exp(g_max - m_new)
        c_ref[...] = decay * c_st + fresh * c_loc
        n_ref[...] = decay * n_st + fresh * n_loc
        m_ref[...] = m_new


def _mixer_c(proj, gates, conv_w, conv_b, gate_b, h_gain, n_heads, cast_jobs):
    bsz, s, _ = proj.shape
    d = HEAD_C
    L = SEQ_BLOCK
    nc = s // L
    grid = (bsz, n_heads)
    g_rows = jnp.transpose(gates, (0, 2, 1)).reshape(bsz, 2 * n_heads, nc, L)
    cb = conv_b.reshape(1, -1)
    full = lambda off: pl.BlockSpec((None, s, d), lambda b, h: (b, 0, off + h))
    row = lambda r, off: pl.BlockSpec((r, d), lambda b, h: (0, off + h))
    grow = lambda off: pl.BlockSpec((None, None, nc, L), lambda b, h: (b, off + h, 0, 0))
    return _call_with_casts(
        _mixer_c_kernel, cast_jobs(grid),
        out_shape=jax.ShapeDtypeStruct((bsz, s, n_heads * d), BF16),
        grid=grid,
        in_specs=[
            pl.BlockSpec(memory_space=pltpu.SMEM),
            full(0), full(n_heads), full(2 * n_heads), full(3 * n_heads),
            row(CONV_W, 0), row(CONV_W, n_heads), row(1, 0), row(1, n_heads),
            grow(0), grow(n_heads), row(1, 0),
        ],
        out_specs=pl.BlockSpec((None, s, d), lambda b, h: (b, 0, h)),
        scratch_shapes=[pltpu.VMEM((s, d), F32), pltpu.VMEM((s, d), F32),
                        pltpu.VMEM((d, d), F32), pltpu.VMEM((1, d), F32), pltpu.VMEM((1, 1), F32)],
        semantics=("arbitrary", "arbitrary"),
        name="mixer_c",
        operands=(gate_b, proj, proj, proj, proj, conv_w, conv_w, cb, cb, g_rows, g_rows,
                  h_gain.reshape(1, -1)))


def _mixer_d_kernel(q_ref, k_ref, v_ref, o_ref, kb_ref, vb_ref, acc_ref, suf_ref):
    s, hd = q_ref.shape
    B = SEQ_BLOCK
    G = min(SB_QUERY_ROWS, s)
    kb_ref[...] = k_ref[...].astype(BF16)
    vb_ref[...] = v_ref[...].astype(BF16)
    ri = lax.broadcasted_iota(jnp.int32, (B, B), 0)
    ci = lax.broadcasted_iota(jnp.int32, (B, B), 1)
    after = (ri > ci).astype(BF16)
    after2 = jnp.concatenate([after, after], axis=0)
    row_g = lax.broadcasted_iota(jnp.int32, (G, B), 0)
    col_g = lax.broadcasted_iota(jnp.int32, (G, B), 1)

    def span(q0, first, last, suffix):
        n = last - first + 1
        keys = slice(first * B, (last + 1) * B)
        qb = (q_ref[q0:q0 + G, :] * (hd ** -0.5)).astype(BF16)
        z = _dot_nt(qb, kb_ref[keys, :])
        sp = jnp.log(1.0 + jnp.exp2(jnp.abs(z) * (-LOG2E)))
        nm = jnp.minimum(z, 0.0)
        log_beta = nm - sp
        log_stay = (nm - z) - sp
        before = [col_g + ((first + j) * B - q0) < row_g if (first + j + 1) * B > q0 else None
                  for j in range(n)]
        later, tot = [], []
        for j in range(n):
            x = log_stay[:, j * B:(j + 1) * B]
            if before[j] is not None:
                x = jnp.where(before[j], x, 0.0)
            hi = lax.bitcast_convert_type(lax.bitcast_convert_type(x, jnp.uint32) & HI_MASK, F32)
            hl = jnp.concatenate([hi.astype(BF16), (x - hi).astype(BF16)], axis=1)
            later.append(_dot(hl, after2))
            tot.append(jnp.sum(x, axis=-1, keepdims=True))
        att = [None] * n
        for j in range(n - 1, -1, -1):
            a = jnp.exp(log_beta[:, j * B:(j + 1) * B] + later[j] + suffix)
            if before[j] is not None:
                a = jnp.where(before[j], a, 0.0)
            att[j] = a.astype(BF16)
            suffix = suffix + tot[j]
        att = att[0] if n == 1 else jnp.concatenate(att, axis=1)
        return _dot(att, vb_ref[keys, :]), suffix

    def first_near(q0):
        return max(q0 // B - SB_NEAR_BLOCKS + 1, 0)

    worst = None
    for q0 in range(0, s, G):
        rows = slice(q0, q0 + G)
        acc, suffix = span(q0, first_near(q0), (q0 + G) // B - 1, jnp.zeros((G, 1), F32))
        o_ref[rows, :] = acc.astype(o_ref.dtype)
        if first_near(q0) > 0:
            acc_ref[rows, :] = acc
            suf_ref[rows, :] = suffix
            worst = suffix if worst is None else jnp.maximum(worst, suffix)

    if worst is not None:
        @pl.when(jnp.max(worst) > SB_UNDERFLOW)
        def _():
            for q0 in range(0, s, G):
                if first_near(q0) == 0:
                    continue
                rows = slice(q0, q0 + G)
                suffix = suf_ref[rows, :]

                @pl.when(jnp.max(suffix) > SB_UNDERFLOW)
                def _():
                    far, _ = span(q0, 0, first_near(q0) - 1, suffix)
                    o_ref[rows, :] = (acc_ref[rows, :] + far).astype(o_ref.dtype)


def _mixer_d(proj, col0, n_heads, cast_jobs):
    bsz, s, _ = proj.shape
    c0 = col0 // HEAD_D
    grid = (bsz, n_heads)
    seq_spec = lambda off: pl.BlockSpec((None, s, HEAD_D), lambda b, h: (b, 0, off + h))
    return _call_with_casts(
        _mixer_d_kernel, cast_jobs(grid),
        out_shape=jax.ShapeDtypeStruct((bsz, s, n_heads * HEAD_D), BF16),
        grid=grid,
        in_specs=[seq_spec(c0), seq_spec(c0 + n_heads), seq_spec(c0 + 2 * n_heads)],
        out_specs=pl.BlockSpec((None, s, HEAD_D), lambda b, h: (b, 0, h)),
        scratch_shapes=[pltpu.VMEM((s, HEAD_D), BF16), pltpu.VMEM((s, HEAD_D), BF16),
                        pltpu.VMEM((s, HEAD_D), F32), pltpu.VMEM((s, 1), F32)],
        semantics=("arbitrary", "arbitrary"),
        name="mixer_d",
        operands=(proj, proj, proj))


def _layer_ab(x, bsz, s, g, w_in, conv_w, conv_b, w_r, b_r, w_i, b_i, lam, qk_g, rel_bias, w_out,
              casts):
    d_rg = conv_w.shape[-1]
    n_heads = rel_bias.shape[-1]
    proj = _norm_matmul(x, g, w_in.astype(BF16), tn=1280).reshape(bsz, s, -1)
    ya, cast_a = _mixer_a(proj, conv_w, conv_b, w_r, b_r, w_i, b_i, lam, d_rg, casts[0])
    yb, cast_b = _mixer_b(proj, qk_g, rel_bias, 2 * d_rg, n_heads, casts[1])
    n = bsz * s
    return _out_proj(x, ya.reshape(n, -1), yb.reshape(n, -1), w_out.astype(BF16)), cast_a + cast_b


def _cd_weight_kernel(a_ref, b_ref, o_ref, *, n_aligned, n_shifted, n_gate):
    c = pl.program_id(0)
    w = o_ref.shape[1]
    lane = lax.broadcasted_iota(jnp.int32, o_ref.shape, 1)

    @pl.when(c < n_aligned)
    def _():
        o_ref[...] = a_ref[...].astype(o_ref.dtype)

    @pl.when((c >= n_aligned) & (c < n_aligned + n_shifted))
    def _():
        a = pltpu.roll(a_ref[...], w - n_gate, axis=1)
        b = pltpu.roll(b_ref[...], w - n_gate, axis=1)
        o_ref[...] = jnp.where(lane < w - n_gate, a, b).astype(o_ref.dtype)

    @pl.when(c >= n_aligned + n_shifted)
    def _():
        o_ref[...] = jnp.where(lane < n_gate, a_ref[...], 0.0).astype(o_ref.dtype)


def _cd_weight(w_in, j, gate_col, n_gate, width, tw=512):
    _, d, cols = w_in.shape
    main = cols - n_gate
    assert gate_col % tw == 0 and (main - gate_col) % tw == 0 and width == main + tw
    n_aligned = gate_col // tw
    n_shifted = (main - gate_col) // tw
    last = (cols - 1) // tw
    return pl.pallas_call(
        functools.partial(_cd_weight_kernel, n_aligned=n_aligned, n_shifted=n_shifted, n_gate=n_gate),
        out_shape=jax.ShapeDtypeStruct((d, width), BF16),
        grid=(width // tw,),
        in_specs=[
            pl.BlockSpec((None, d, tw),
                         lambda c: (j, 0, jnp.where(c < n_aligned + n_shifted, c, n_aligned))),
            pl.BlockSpec((None, d, tw), lambda c: (j, 0, jnp.minimum(c + 1, last))),
        ],
        out_specs=pl.BlockSpec((d, tw), lambda c: (0, c)),
        compiler_params=_cparams(("arbitrary",)),
        name="cd_weight",
    )(w_in, w_in)


def _layer_cd(x, bsz, s, g, w_in_all, j, conv_w, conv_b, gate_b, h_gain, w_out, casts):
    n_heads_c = gate_b.shape[-1]
    dc = n_heads_c * HEAD_C
    d_mix = w_out.shape[0]
    dd = d_mix - dc
    n_heads_d = dd // HEAD_D
    n_gate = 2 * n_heads_c
    main = 4 * dc + 3 * dd
    width = -(-(main + n_gate) // 512) * 512
    w = _cd_weight(w_in_all, j, 4 * dc, n_gate, width)
    proj = _norm_matmul(x, g, w, tn=1920).reshape(bsz, s, width)
    gates = proj[:, :, main:main + n_gate]
    yd, cast_d = _mixer_d(proj, 4 * dc, n_heads_d, casts[0])
    yc, cast_c = _mixer_c(proj, gates, conv_w, conv_b, gate_b, h_gain, n_heads_c, casts[1])
    n = bsz * s
    return _out_proj(x, yc.reshape(n, -1), yd.reshape(n, -1), w_out.astype(BF16)), cast_d + cast_c


def kernel(x, norm_g, ffn_w_in, ffn_w_out, ab_w_in, ab_conv_w, ab_conv_b, rg_w_r, rg_b_r, rg_w_i,
           rg_b_i, rg_lambda, qk_gain, rel_bias, ab_w_out, cd_w_in, cd_conv_w, cd_conv_b,
           mlstm_gate_bias, mlstm_h_gain, cd_w_out):
    bsz, s, d = x.shape
    depth = norm_g.shape[0]
    h = x.reshape(bsz * s, d)
    ffn_w = {(0, 0): (ffn_w_in[0, 0].astype(BF16), ffn_w_out[0, 0].astype(BF16))}
    for layer in range(depth):
        j = layer // 2
        h = _ffn(h, norm_g[layer, 0], *ffn_w[(layer, 0)])
        targets = [(layer, 1)] + ([(layer + 1, 0)] if layer + 1 < depth else [])
        casts = [functools.partial(_ffn_cast_jobs, ffn_w_in, ffn_w_out, *t) for t in targets]
        casts += [lambda grid: []] * (2 - len(casts))
        if layer % 2 == 0:
            h, cast = _layer_ab(h, bsz, s, norm_g[layer, 1], ab_w_in[j], ab_conv_w[j], ab_conv_b[j],
                                rg_w_r[j], rg_b_r[j], rg_w_i[j], rg_b_i[j], rg_lambda[j], qk_gain[j],
                                rel_bias, ab_w_out[j], casts)
        else:
            h, cast = _layer_cd(h, bsz, s, norm_g[layer, 1], cd_w_in, j, cd_conv_w[j], cd_conv_b[j],
                                mlstm_gate_bias[j], mlstm_h_gain[j], cd_w_out[j], casts)
        for i, t in enumerate(targets):
            ffn_w[t] = (cast[2 * i], cast[2 * i + 1])
        h = _ffn(h, norm_g[layer, 2], *ffn_w[(layer, 1)])
    return h.reshape(bsz, s, d)
```

```python
import functools
import math

import numpy as np
import jax
import jax.numpy as jnp
from jax import lax
from jax.experimental import pallas as pl
from jax.experimental.pallas import tpu as pltpu

F32 = jnp.float32
BF16 = jnp.bfloat16

RMS_EPS = 1e-6
RG_C = 8.0
CONV_W = 4
RG_BLOCK = 128
HEAD_B = 128
HEAD_C = 256
HEAD_D = 128
SEQ_BLOCK = 128
SUBLANES = 8
BF16_SUBLANES = 16
DSW_PATTERNS = ((128, 1), (512, 4), (2048, 16))
N_BUCKETS = 32
MAX_DIST = 2048
NEG = -1e30
LOG2E = 1.4426950408889634
HI_MASK = np.uint32(0xFFFF0000)
SB_NEAR_BLOCKS = 3
SB_QUERY_ROWS = 256
SB_UNDERFLOW = -104.0
DSW_QUERY_ROWS = 256

FFN_TOKEN_TILE = 1024
FFN_HIDDEN_TILE = 512
PROJ_TOKEN_TILE = 1024
PROJ_COL_TILE_AB = 1280
PROJ_COL_TILE_CD = 1536
OUT_PROJ_TOKEN_TILE = 512
CD_WIDTH_MULTIPLE = 512
CD_WEIGHT_COL_TILE = 256
V7X_VMEM_BYTES = 64 * 1024 * 1024
VMEM_LIMIT = V7X_VMEM_BYTES * 7 // 8


def _cparams(sem):
    return pltpu.CompilerParams(dimension_semantics=sem, vmem_limit_bytes=VMEM_LIMIT)


def _rms_rows(x, g):
    return x * lax.rsqrt(jnp.mean(x * x, axis=-1, keepdims=True) + RMS_EPS) * g


def _dot(a, b):
    return jnp.dot(a, b, preferred_element_type=F32)


def _dot_nt(a, b):
    return lax.dot_general(a, b, (((1,), (1,)), ((), ())), preferred_element_type=F32)


def _dot_tn(a, b):
    return lax.dot_general(a, b, (((0,), (0,)), ((), ())), preferred_element_type=F32)


def _pick_tile(n, pref):
    t = min(pref, n)
    while n % t:
        t //= 2
    return t


def _linear_step(grid, idx):
    t = idx[0]
    for i, g in zip(idx[1:], grid[1:]):
        t = t * g + i
    return t


def _ffn_cast_jobs(w_in, w_out, layer, half, grid):
    n_steps = math.prod(grid)
    jobs = []
    for w in (w_in, w_out):
        rows, cols = w.shape[2:]
        n = n_steps
        while rows % n or (rows // n) % BF16_SUBLANES:
            n -= 1
        slab = rows // n
        blk = lambda *idx, n=n: jnp.minimum(_linear_step(grid, idx), n - 1)

        def cast(t, src_ref, dst_ref, n=n):
            @pl.when(t < n)
            def _():
                dst_ref[...] = src_ref[...].astype(dst_ref.dtype)

        jobs.append((
            [w],
            [pl.BlockSpec((None, None, slab, cols), lambda *idx, blk=blk: (layer, half, blk(*idx), 0))],
            pl.BlockSpec((slab, cols), lambda *idx, blk=blk: (blk(*idx), 0)),
            jax.ShapeDtypeStruct((rows, cols), BF16),
            cast,
        ))
    return jobs


def _call_with_jobs(body, jobs, *, out_shape, grid, in_specs, out_specs, scratch_shapes=(),
                    semantics, name, operands):
    n_in = len(in_specs)
    n_job_in = [len(j[0]) for j in jobs]
    n_extra = sum(n_job_in)

    def kernel(*refs):
        t = _linear_step(grid, [pl.program_id(ax) for ax in range(len(grid))])
        at = n_in
        for i, j in enumerate(jobs):
            j[4](t, *refs[at:at + n_job_in[i]], refs[n_in + n_extra + 1 + i])
            at += n_job_in[i]
        body(*refs[:n_in], refs[n_in + n_extra], *refs[n_in + n_extra + 1 + len(jobs):])

    outs = pl.pallas_call(
        kernel,
        out_shape=[out_shape] + [j[3] for j in jobs],
        grid=grid,
        in_specs=list(in_specs) + [sp for j in jobs for sp in j[1]],
        out_specs=[out_specs] + [j[2] for j in jobs],
        scratch_shapes=list(scratch_shapes),
        compiler_params=_cparams(semantics),
        name=name,
    )(*operands, *[op for j in jobs for op in j[0]])
    return outs[0], list(outs[1:])


def _ffn_kernel(x_ref, g_ref, wg_ref, wu_ref, wo_ref, o_ref, xn_ref):
    j = pl.program_id(1)

    @pl.when(j == 0)
    def _():
        x = x_ref[...]
        xn_ref[...] = _rms_rows(x, g_ref[...]).astype(BF16)
        o_ref[...] = x

    xn = xn_ref[...]
    hg = _dot(xn, wg_ref[...])
    hu = _dot(xn, wu_ref[...])
    h = (0.5 * hg * jax.nn.sigmoid(hg) * hu).astype(BF16)
    o_ref[...] += _dot(h, wo_ref[...])


def _ffn(x, g, w_in, w_out, tm=FFN_TOKEN_TILE, tf=FFN_HIDDEN_TILE):
    n, d = x.shape
    d_ff = w_out.shape[0]
    tm = _pick_tile(n, tm)
    tf = _pick_tile(d_ff, tf)
    nj = d_ff // tf
    return pl.pallas_call(
        _ffn_kernel,
        out_shape=jax.ShapeDtypeStruct((n, d), F32),
        grid=(n // tm, nj),
        in_specs=[
            pl.BlockSpec((tm, d), lambda i, j: (i, 0)),
            pl.BlockSpec((1, d), lambda i, j: (0, 0)),
            pl.BlockSpec((d, tf), lambda i, j: (0, j)),
            pl.BlockSpec((d, tf), lambda i, j: (0, j + nj)),
            pl.BlockSpec((tf, d), lambda i, j: (j, 0)),
        ],
        out_specs=pl.BlockSpec((tm, d), lambda i, j: (i, 0)),
        scratch_shapes=[pltpu.VMEM((tm, d), BF16)],
        compiler_params=_cparams(("parallel", "arbitrary")),
        name="ffn",
    )(x, g.reshape(1, d), w_in, w_in, w_out)


def _norm_matmul_kernel(x_ref, g_ref, w_ref, o_ref, xn_ref):
    @pl.when(pl.program_id(1) == 0)
    def _():
        xn_ref[...] = _rms_rows(x_ref[...], g_ref[...]).astype(BF16)

    o_ref[...] = _dot(xn_ref[...], w_ref[...])


def _norm_matmul(x, g, w, tn, tm=PROJ_TOKEN_TILE):
    n, d = x.shape
    wn = w.shape[1]
    tm = _pick_tile(n, tm)
    while wn % tn:
        tn -= 128
    return pl.pallas_call(
        _norm_matmul_kernel,
        out_shape=jax.ShapeDtypeStruct((n, wn), F32),
        grid=(n // tm, wn // tn),
        in_specs=[
            pl.BlockSpec((tm, d), lambda i, j: (i, 0)),
            pl.BlockSpec((1, d), lambda i, j: (0, 0)),
            pl.BlockSpec((d, tn), lambda i, j: (0, j)),
        ],
        out_specs=pl.BlockSpec((tm, tn), lambda i, j: (i, j)),
        scratch_shapes=[pltpu.VMEM((tm, d), BF16)],
        compiler_params=_cparams(("parallel", "arbitrary")),
        name="norm_matmul",
    )(x, g.reshape(1, d), w)


def _out_proj_kernel(x_ref, ya_ref, yb_ref, wa_ref, wb_ref, o_ref):
    o_ref[...] = x_ref[...] + _dot(ya_ref[...], wa_ref[...]) + _dot(yb_ref[...], wb_ref[...])


def _out_proj(x, ya, yb, w, tm=OUT_PROJ_TOKEN_TILE):
    n, d = x.shape
    da = ya.shape[1]
    db = yb.shape[1]
    assert da == db and w.shape[0] == da + db
    tm = _pick_tile(n, tm)
    return pl.pallas_call(
        _out_proj_kernel,
        out_shape=jax.ShapeDtypeStruct((n, d), F32),
        grid=(n // tm,),
        in_specs=[
            pl.BlockSpec((tm, d), lambda i: (i, 0)),
            pl.BlockSpec((tm, da), lambda i: (i, 0)),
            pl.BlockSpec((tm, db), lambda i: (i, 0)),
            pl.BlockSpec((da, d), lambda i: (0, 0)),
            pl.BlockSpec((db, d), lambda i: (1, 0)),
        ],
        out_specs=pl.BlockSpec((tm, d), lambda i: (i, 0)),
        compiler_params=_cparams(("parallel",)),
        name="out_proj",
    )(x, ya, yb, w, w)


def _causal_conv(x, w, b):
    y = x * w[CONV_W - 1:CONV_W] + b
    for k in range(1, CONV_W):
        y = y + _shift_rows(x, k, 0.0) * w[CONV_W - 1 - k:CONV_W - k]
    return y


def _shift_rows(x, k, fill):
    assert 0 < k < SUBLANES
    rolled = pltpu.roll(x, k, axis=0)
    t = lax.broadcasted_iota(jnp.int32, (SUBLANES, x.shape[1]), 0)
    head = jnp.where(t >= k, rolled[:SUBLANES], fill)
    return jnp.concatenate([head, rolled[SUBLANES:]], axis=0)


def _mixer_a_kernel(xr_ref, gate_ref, cw_ref, cb_ref, wr_ref, br_ref, wi_ref, bi_ref,
                    lam_ref, o_ref):
    s = xr_ref.shape[0]
    xc = _causal_conv(xr_ref[...], cw_ref[...], cb_ref[...])
    xb = xc.astype(BF16)
    r = jax.nn.sigmoid(_dot(xb, wr_ref[...]) + br_ref[...])
    i = jax.nn.sigmoid(_dot(xb, wi_ref[...]) + bi_ref[...])
    neg_lam = -lam_ref[...]
    softplus = jnp.maximum(neg_lam, 0.0) + jnp.log1p(jnp.exp(-jnp.abs(neg_lam)))
    log_a = -RG_C * r * softplus
    a = jnp.exp(log_a)
    u = jnp.sqrt(-jnp.tanh(log_a) * (a * a + 1.0)) * (i * xc)
    k = 1
    while k < s:
        if k < SUBLANES:
            u = a * _shift_rows(u, k, 0.0) + u
            a = a * _shift_rows(a, k, 1.0)
        else:
            u = jnp.concatenate([u[:k], a[k:] * u[:s - k] + u[k:]], axis=0)
            a = jnp.concatenate([a[:k], a[k:] * a[:s - k]], axis=0)
        k *= 2
    o_ref[...] = (jax.nn.gelu(gate_ref[...]) * u).astype(o_ref.dtype)


def _mixer_a(proj, conv_w, conv_b, w_r, b_r, w_i, b_i, lam, d_rg, cast_jobs):
    bsz, s, _ = proj.shape
    nb = d_rg // RG_BLOCK
    grid = (bsz, nb)
    vec = lambda v: v.reshape(1, d_rg)
    col = lambda off: pl.BlockSpec((None, s, RG_BLOCK), lambda b, g: (b, 0, off + g))
    row = lambda r: pl.BlockSpec((r, RG_BLOCK), lambda b, g: (0, g))
    blk = pl.BlockSpec((None, RG_BLOCK, RG_BLOCK), lambda b, g: (g, 0, 0))
    return _call_with_jobs(
        _mixer_a_kernel, cast_jobs(grid),
        out_shape=jax.ShapeDtypeStruct((bsz, s, d_rg), BF16),
        grid=grid,
        in_specs=[col(nb), col(0), row(CONV_W), row(1), blk, row(1), blk, row(1), row(1)],
        out_specs=pl.BlockSpec((None, s, RG_BLOCK), lambda b, g: (b, 0, g)),
        semantics=("arbitrary", "arbitrary"),
        name="mixer_a",
        operands=(proj, proj, conv_w, vec(conv_b), w_r.astype(BF16), vec(b_r), w_i.astype(BF16),
                  vec(b_i), vec(lam)))


def _t5_bucket_np(dist):
    max_exact = N_BUCKETS // 2
    d_f = np.maximum(dist, 1).astype(np.float32)
    large = max_exact + (np.log(d_f / np.float32(max_exact)) / np.float32(math.log(MAX_DIST / max_exact))
                         * np.float32(N_BUCKETS - max_exact)).astype(np.int32)
    large = np.minimum(large, N_BUCKETS - 1)
    return np.where(dist < max_exact, dist, large).astype(np.int32)


def _dsw_bias_vector(rel_bias, seq, qb):
    dist = np.arange(seq)
    mult = np.zeros(seq, np.float32)
    for window, dilation in DSW_PATTERNS:
        mult += ((dist % dilation == 0) & (dist <= window)).astype(np.float32)
    logm = np.where(mult > 0, np.log(np.maximum(mult, 1.0)), NEG).astype(np.float32)
    f = (rel_bias.astype(F32)[_t5_bucket_np(dist)] + logm[:, None]) * LOG2E
    n_heads = f.shape[1]
    hv = jnp.concatenate([f[::-1].T, jnp.full((n_heads, qb), NEG, F32)], axis=1)
    return hv[:, None, :]


def _mixer_b_kernel(q_ref, k_ref, v_ref, g_ref, hv_ref, o_ref, kn_ref, vb_ref, tbl_ref):
    s, hd = q_ref.shape
    qb = tbl_ref.shape[0]

    @pl.when(pl.program_id(1) == 0)
    def _():
        skew = pltpu.roll(jnp.broadcast_to(hv_ref[...], (qb, s + qb)), 1, axis=1, stride=1, stride_axis=0)
        tbl_ref[...] = skew[:, qb:]

    kn_ref[...] = _rms_rows(k_ref[...], g_ref[1:2]).astype(BF16)
    vb_ref[...] = v_ref[...].astype(BF16)
    for gi in range(s // qb):
        n_keys = (gi + 1) * qb
        rows = slice(gi * qb, n_keys)
        qn = (_rms_rows(q_ref[rows, :], g_ref[0:1]) * (LOG2E / math.sqrt(hd))).astype(BF16)
        sc = _dot_nt(qn, kn_ref[0:n_keys, :]) + tbl_ref[:, s - n_keys:]
        m = jnp.max(sc, axis=-1, keepdims=True)
        p = jnp.exp2(sc - m)
        l = jnp.sum(p, axis=-1, keepdims=True)
        o = _dot(p.astype(BF16), vb_ref[0:n_keys, :])
        o_ref[rows, :] = (o / l).astype(o_ref.dtype)


def _mixer_b(proj, qk_g, rel_bias, col0, n_heads, cast_jobs, qb=DSW_QUERY_ROWS):
    bsz, s, _ = proj.shape
    qb = _pick_tile(s, qb)
    hv = _dsw_bias_vector(rel_bias, s, qb)
    c0 = col0 // HEAD_B
    grid = (n_heads, bsz)
    seq_spec = lambda off: pl.BlockSpec((None, s, HEAD_B), lambda h, b: (b, 0, off + h))
    return _call_with_jobs(
        _mixer_b_kernel, cast_jobs(grid),
        out_shape=jax.ShapeDtypeStruct((bsz, s, n_heads * HEAD_B), BF16),
        grid=grid,
        in_specs=[
            seq_spec(c0), seq_spec(c0 + n_heads), seq_spec(c0 + 2 * n_heads),
            pl.BlockSpec((2, HEAD_B), lambda h, b: (0, 0)),
            pl.BlockSpec((None, 1, s + qb), lambda h, b: (h, 0, 0)),
        ],
        out_specs=pl.BlockSpec((None, s, HEAD_B), lambda h, b: (b, 0, h)),
        scratch_shapes=[pltpu.VMEM((s, HEAD_B), BF16), pltpu.VMEM((s, HEAD_B), BF16),
                        pltpu.VMEM((qb, s), F32)],
        semantics=("arbitrary", "arbitrary"),
        name="mixer_b",
        operands=(proj, proj, proj, qk_g, hv))


def _mixer_c_kernel(gb_ref, q_ref, k_ref, v_ref, og_ref, cwq_ref, cwk_ref, cbq_ref, cbk_ref,
                    ig_ref, fg_ref, gain_ref, o_ref, qc_ref, kc_ref, c_ref, n_ref, m_ref):
    h = pl.program_id(1)
    L = SEQ_BLOCK
    s, d = q_ref.shape
    nc = s // L

    qc_ref[...] = jax.nn.silu(_causal_conv(q_ref[...], cwq_ref[...], cbq_ref[...]))
    kc_ref[...] = jax.nn.silu(_causal_conv(k_ref[...], cwk_ref[...], cbk_ref[...])) * (d ** -0.5)
    c_ref[...] = jnp.zeros_like(c_ref)
    n_ref[...] = jnp.zeros_like(n_ref)
    m_ref[...] = jnp.zeros_like(m_ref)

    ri = lax.broadcasted_iota(jnp.int32, (L, L), 0)
    ci = lax.broadcasted_iota(jnp.int32, (L, L), 1)
    eye = ri == ci
    causal = ci <= ri

    def to_col(row):
        return jnp.sum(jnp.where(eye, row, 0.0), axis=1, keepdims=True)

    ig_all = ig_ref[...] + gb_ref[0, h]
    f_all = fg_ref[...] + gb_ref[1, h]
    ls_all = jnp.minimum(f_all, 0.0) - jnp.log1p(jnp.exp(-jnp.abs(f_all)))

    for n in range(nc):
        rows = slice(n * L, (n + 1) * L)
        qc = qc_ref[rows, :]
        kc = kc_ref[rows, :]
        qb = qc.astype(BF16)
        kb = kc.astype(BF16)
        vb = v_ref[rows, :].astype(BF16)

        ig_row = ig_all[n:n + 1]
        ls_row = ls_all[n:n + 1]
        ls_col = to_col(ls_row)
        ig_col = to_col(ig_row)
        b_col = jnp.sum(jnp.where(causal, ls_row, 0.0), axis=1, keepdims=True)
        b_row = jnp.sum(jnp.where(ri <= ci, ls_col, 0.0), axis=0, keepdims=True)
        b_last = jnp.sum(ls_row, axis=1, keepdims=True)

        log_d = b_col - b_row + ig_row
        g_row = b_last - b_row + ig_row
        g_col = b_last - b_col + ig_col
        g_max = jnp.max(g_row, axis=1, keepdims=True)
        d_max = jnp.max(jnp.where(causal, log_d, -jnp.inf), axis=1, keepdims=True)
        qk = _dot_nt(qb, kb)

        c_st = c_ref[...]
        n_st = n_ref[...]
        m_st = m_ref[...]

        inter = b_col + m_st
        m_t = jnp.maximum(inter, d_max)
        s_qk = jnp.where(causal, qk * jnp.exp(log_d - m_t), 0.0)
        inter_w = jnp.exp(inter - m_t)
        num = _dot(s_qk.astype(BF16), vb) + inter_w * _dot(qb, c_st.astype(BF16))
        qn = jnp.sum(qc * n_st, axis=1, keepdims=True)
        den = jnp.sum(s_qk, axis=1, keepdims=True) + inter_w * qn
        hout = num / jnp.maximum(jnp.abs(den), jnp.exp(-m_t))

        hn = _rms_rows(hout, gain_ref[...])
        o_ref[rows, :] = (jax.nn.sigmoid(og_ref[rows, :]) * hn).astype(o_ref.dtype)

        wk = jnp.exp(g_col - g_max) * kc
        c_loc = _dot_tn(wk.astype(BF16), vb)
        n_loc = jnp.sum(wk, axis=0, keepdims=True)
        m_new = jnp.maximum(b_last + m_st, g_max)
        decay = jnp.exp(b_last + m_st - m_new)
        fresh = jnp.exp(g_max - m_new)
        c_ref[...] = decay * c_st + fresh * c_loc
        n_ref[...] = decay * n_st + fresh * n_loc
        m_ref[...] = m_new


def _mixer_c(proj, gates, conv_w, conv_b, gate_b, h_gain, n_heads, cast_jobs):
    bsz, s, _ = proj.shape
    d = HEAD_C
    L = SEQ_BLOCK
    nc = s // L
    grid = (bsz, n_heads)
    g_rows = jnp.transpose(gates, (0, 2, 1)).reshape(bsz, 2 * n_heads, nc, L)
    cb = conv_b.reshape(1, -1)
    full = lambda off: pl.BlockSpec((None, s, d), lambda b, h: (b, 0, off + h))
    row = lambda r, off: pl.BlockSpec((r, d), lambda b, h: (0, off + h))
    grow = lambda off: pl.BlockSpec((None, None, nc, L), lambda b, h: (b, off + h, 0, 0))
    return _call_with_jobs(
        _mixer_c_kernel, cast_jobs(grid),
        out_shape=jax.ShapeDtypeStruct((bsz, s, n_heads * d), BF16),
        grid=grid,
        in_specs=[
            pl.BlockSpec(memory_space=pltpu.SMEM),
            full(0), full(n_heads), full(2 * n_heads), full(3 * n_heads),
            row(CONV_W, 0), row(CONV_W, n_heads), row(1, 0), row(1, n_heads),
            grow(0), grow(n_heads), row(1, 0),
        ],
        out_specs=pl.BlockSpec((None, s, d), lambda b, h: (b, 0, h)),
        scratch_shapes=[pltpu.VMEM((s, d), F32), pltpu.VMEM((s, d), F32),
                        pltpu.VMEM((d, d), F32), pltpu.VMEM((1, d), F32), pltpu.VMEM((1, 1), F32)],
        semantics=("arbitrary", "arbitrary"),
        name="mixer_c",
        operands=(gate_b, proj, proj, proj, proj, conv_w, conv_w, cb, cb, g_rows, g_rows,
                  h_gain.reshape(1, -1)))


def _mixer_d_kernel(q_ref, k_ref, v_ref, o_ref, kb_ref, vb_ref, acc_ref, suf_ref):
    s, hd = q_ref.shape
    B = SEQ_BLOCK
    G = min(SB_QUERY_ROWS, s)
    kb_ref[...] = k_ref[...].astype(BF16)
    vb_ref[...] = v_ref[...].astype(BF16)
    ri = lax.broadcasted_iota(jnp.int32, (B, B), 0)
    ci = lax.broadcasted_iota(jnp.int32, (B, B), 1)
    after = (ri > ci).astype(BF16)
    after2 = jnp.concatenate([after, after], axis=0)
    row_g = lax.broadcasted_iota(jnp.int32, (G, B), 0)
    col_g = lax.broadcasted_iota(jnp.int32, (G, B), 1)

    def span(q0, first, last, suffix):
        n = last - first + 1
        keys = slice(first * B, (last + 1) * B)
        qb = (q_ref[q0:q0 + G, :] * (hd ** -0.5)).astype(BF16)
        z = _dot_nt(qb, kb_ref[keys, :])
        sp = jnp.log(1.0 + jnp.exp2(jnp.abs(z) * (-LOG2E)))
        nm = jnp.minimum(z, 0.0)
        log_beta = nm - sp
        log_stay = (nm - z) - sp
        before = [col_g + ((first + j) * B - q0) < row_g if (first + j + 1) * B > q0 else None
                  for j in range(n)]
        later, tot = [], []
        for j in range(n):
            x = log_stay[:, j * B:(j + 1) * B]
            if before[j] is not None:
                x = jnp.where(before[j], x, 0.0)
            hi = lax.bitcast_convert_type(lax.bitcast_convert_type(x, jnp.uint32) & HI_MASK, F32)
            hl = jnp.concatenate([hi.astype(BF16), (x - hi).astype(BF16)], axis=1)
            later.append(_dot(hl, after2))
            tot.append(jnp.sum(x, axis=-1, keepdims=True))
        att = [None] * n
        for j in range(n - 1, -1, -1):
            a = jnp.exp(log_beta[:, j * B:(j + 1) * B] + later[j] + suffix)
            if before[j] is not None:
                a = jnp.where(before[j], a, 0.0)
            att[j] = a.astype(BF16)
            suffix = suffix + tot[j]
        att = att[0] if n == 1 else jnp.concatenate(att, axis=1)
        return _dot(att, vb_ref[keys, :]), suffix

    def first_near(q0):
        return max(q0 // B - SB_NEAR_BLOCKS + 1, 0)

    worst = None
    for q0 in range(0, s, G):
        rows = slice(q0, q0 + G)
        acc, suffix = span(q0, first_near(q0), (q0 + G) // B - 1, jnp.zeros((G, 1), F32))
        o_ref[rows, :] = acc.astype(o_ref.dtype)
        if first_near(q0) > 0:
            acc_ref[rows, :] = acc
            suf_ref[rows, :] = suffix
            worst = suffix if worst is None else jnp.maximum(worst, suffix)

    if worst is not None:
        @pl.when(jnp.max(worst) > SB_UNDERFLOW)
        def _():
            for q0 in range(0, s, G):
                if first_near(q0) == 0:
                    continue
                rows = slice(q0, q0 + G)
                suffix = suf_ref[rows, :]

                @pl.when(jnp.max(suffix) > SB_UNDERFLOW)
                def _():
                    far, _ = span(q0, 0, first_near(q0) - 1, suffix)
                    o_ref[rows, :] = (acc_ref[rows, :] + far).astype(o_ref.dtype)


def _mixer_d(proj, col0, n_heads, cast_jobs):
    bsz, s, _ = proj.shape
    c0 = col0 // HEAD_D
    grid = (bsz, n_heads)
    seq_spec = lambda off: pl.BlockSpec((None, s, HEAD_D), lambda b, h: (b, 0, off + h))
    return _call_with_jobs(
        _mixer_d_kernel, cast_jobs(grid),
        out_shape=jax.ShapeDtypeStruct((bsz, s, n_heads * HEAD_D), BF16),
        grid=grid,
        in_specs=[seq_spec(c0), seq_spec(c0 + n_heads), seq_spec(c0 + 2 * n_heads)],
        out_specs=pl.BlockSpec((None, s, HEAD_D), lambda b, h: (b, 0, h)),
        scratch_shapes=[pltpu.VMEM((s, HEAD_D), BF16), pltpu.VMEM((s, HEAD_D), BF16),
                        pltpu.VMEM((s, HEAD_D), F32), pltpu.VMEM((s, 1), F32)],
        semantics=("arbitrary", "arbitrary"),
        name="mixer_d",
        operands=(proj, proj, proj))


def _layer_ab(x, bsz, s, g, w_in, conv_w, conv_b, w_r, b_r, w_i, b_i, lam, qk_g, rel_bias, w_out,
              jobs):
    d_rg = conv_w.shape[-1]
    n_heads = rel_bias.shape[-1]
    proj = _norm_matmul(x, g, w_in.astype(BF16), tn=PROJ_COL_TILE_AB).reshape(bsz, s, -1)
    ya, out_a = _mixer_a(proj, conv_w, conv_b, w_r, b_r, w_i, b_i, lam, d_rg, jobs[0])
    yb, out_b = _mixer_b(proj, qk_g, rel_bias, 2 * d_rg, n_heads, jobs[1])
    n = bsz * s
    return _out_proj(x, ya.reshape(n, -1), yb.reshape(n, -1), w_out.astype(BF16)), out_a + out_b


def _cd_layout(gate_b, w_out):
    n_heads_c = gate_b.shape[-1]
    dc = n_heads_c * HEAD_C
    dd = w_out.shape[0] - dc
    n_gate = 2 * n_heads_c
    main = 4 * dc + 3 * dd
    width = -(-(main + n_gate) // CD_WIDTH_MULTIPLE) * CD_WIDTH_MULTIPLE
    return n_heads_c, dc, dd, n_gate, main, width


def _cd_weight_job(w_in, j, gate_col, n_gate, width, tw=CD_WEIGHT_COL_TILE):
    _, d, cols = w_in.shape
    main = cols - n_gate
    assert gate_col % tw == 0 and (main - gate_col) % tw == 0 and width % tw == 0 and n_gate <= tw
    n_aligned = gate_col // tw
    n_shifted = (main - gate_col) // tw
    n_blk = width // tw
    last = (cols - 1) // tw

    def job(grid):
        assert n_blk <= math.prod(grid)
        blk = lambda *idx: jnp.minimum(_linear_step(grid, idx), n_blk - 1)

        def reorder(t, a_ref, b_ref, o_ref):
            lane = lax.broadcasted_iota(jnp.int32, o_ref.shape, 1)

            @pl.when(t < n_aligned)
            def _():
                o_ref[...] = a_ref[...].astype(o_ref.dtype)

            @pl.when((t >= n_aligned) & (t < n_aligned + n_shifted))
            def _():
                a = pltpu.roll(a_ref[...], tw - n_gate, axis=1)
                b = pltpu.roll(b_ref[...], tw - n_gate, axis=1)
                o_ref[...] = jnp.where(lane < tw - n_gate, a, b).astype(o_ref.dtype)

            @pl.when(t == n_aligned + n_shifted)
            def _():
                o_ref[...] = jnp.where(lane < n_gate, a_ref[...], 0.0).astype(o_ref.dtype)

            @pl.when((t > n_aligned + n_shifted) & (t < n_blk))
            def _():
                o_ref[...] = jnp.zeros_like(o_ref)

        a_blk = lambda *idx: jnp.where(blk(*idx) < n_aligned + n_shifted, blk(*idx), n_aligned)
        return [([w_in, w_in],
                 [pl.BlockSpec((None, d, tw), lambda *idx: (j, 0, a_blk(*idx))),
                  pl.BlockSpec((None, d, tw), lambda *idx: (j, 0, jnp.minimum(blk(*idx) + 1, last)))],
                 pl.BlockSpec((d, tw), lambda *idx: (0, blk(*idx))),
                 jax.ShapeDtypeStruct((d, width), BF16),
                 reorder)]
    return job


def _layer_cd(x, bsz, s, g, w, conv_w, conv_b, gate_b, h_gain, w_out, jobs):
    n_heads_c, dc, dd, n_gate, main, width = _cd_layout(gate_b, w_out)
    n_heads_d = dd // HEAD_D
    proj = _norm_matmul(x, g, w, tn=PROJ_COL_TILE_CD).reshape(bsz, s, width)
    gates = proj[:, :, main:main + n_gate]
    yd, out_d = _mixer_d(proj, 4 * dc, n_heads_d, jobs[0])
    yc, out_c = _mixer_c(proj, gates, conv_w, conv_b, gate_b, h_gain, n_heads_c, jobs[1])
    n = bsz * s
    return _out_proj(x, yc.reshape(n, -1), yd.reshape(n, -1), w_out.astype(BF16)), out_d + out_c


def kernel(x, norm_g, ffn_w_in, ffn_w_out, ab_w_in, ab_conv_w, ab_conv_b, rg_w_r, rg_b_r, rg_w_i,
           rg_b_i, rg_lambda, qk_gain, rel_bias, ab_w_out, cd_w_in, cd_conv_w, cd_conv_b,
           mlstm_gate_bias, mlstm_h_gain, cd_w_out):
    bsz, s, d = x.shape
    depth = norm_g.shape[0]
    h = x.reshape(bsz * s, d)
    ffn_w = {(0, 0): (ffn_w_in[0, 0].astype(BF16), ffn_w_out[0, 0].astype(BF16))}
    cd_w = None
    for layer in range(depth):
        j = layer // 2
        h = _ffn(h, norm_g[layer, 0], *ffn_w[(layer, 0)])
        targets = [(layer, 1)] + ([(layer + 1, 0)] if layer + 1 < depth else [])
        jobs = [functools.partial(_ffn_cast_jobs, ffn_w_in, ffn_w_out, *t) for t in targets]
        jobs += [lambda grid: []] * (2 - len(jobs))
        if layer % 2 == 0:
            if layer + 1 < depth:
                _, dc, _, n_gate, _, width = _cd_layout(mlstm_gate_bias[j], cd_w_out[j])
                ffn_jobs, cd_job = jobs[0], _cd_weight_job(cd_w_in, j, 4 * dc, n_gate, width)
                jobs[0] = lambda grid: ffn_jobs(grid) + cd_job(grid)
            h, out = _layer_ab(h, bsz, s, norm_g[layer, 1], ab_w_in[j], ab_conv_w[j], ab_conv_b[j],
                               rg_w_r[j], rg_b_r[j], rg_w_i[j], rg_b_i[j], rg_lambda[j], qk_gain[j],
                               rel_bias, ab_w_out[j], jobs)
            if layer + 1 < depth:
                cd_w = out.pop(2)
        else:
            h, out = _layer_cd(h, bsz, s, norm_g[layer, 1], cd_w, cd_conv_w[j], cd_conv_b[j],
                               mlstm_gate_bias[j], mlstm_h_gain[j], cd_w_out[j], jobs)
        for i, t in enumerate(targets):
            ffn_w[t] = (out[2 * i], out[2 * i + 1])
        h = _ffn(h, norm_g[layer, 2], *ffn_w[(layer, 1)])
    return h.reshape(bsz, s, d)
```

```python
import functools
import math

import numpy as np
import jax
import jax.numpy as jnp
from jax import lax
from jax.experimental import pallas as pl
from jax.experimental.pallas import tpu as pltpu

F32 = jnp.float32
BF16 = jnp.bfloat16

RMS_EPS = 1e-6
RG_C = 8.0
CONV_W = 4
RG_BLOCK = 128
HEAD_B = 128
HEAD_C = 256
HEAD_D = 128
SEQ_BLOCK = 128
SUBLANES = 8
BF16_SUBLANES = 16
DSW_PATTERNS = ((128, 1), (512, 4), (2048, 16))
N_BUCKETS = 32
MAX_DIST = 2048
NEG = -1e30
LOG2E = 1.4426950408889634
HI_MASK = np.uint32(0xFFFF0000)
SB_NEAR_BLOCKS = 3
SB_QUERY_ROWS = 256
SB_UNDERFLOW = -104.0
DSW_QUERY_ROWS = 256

FFN_TOKEN_TILE = 1024
FFN_HIDDEN_TILE = 512
PROJ_TOKEN_TILE = 1024
PROJ_COL_TILE_AB = 1280
PROJ_COL_TILE_CD = 1536
OUT_PROJ_TOKEN_TILE = 512
CD_WIDTH_MULTIPLE = 512
CD_WEIGHT_COL_TILE = 256
V7X_VMEM_BYTES = 64 * 1024 * 1024
VMEM_LIMIT = V7X_VMEM_BYTES * 7 // 8


def _cparams(sem):
    return pltpu.CompilerParams(dimension_semantics=sem, vmem_limit_bytes=VMEM_LIMIT)


def _rms_rows(x, g):
    return x * lax.rsqrt(jnp.mean(x * x, axis=-1, keepdims=True) + RMS_EPS) * g


def _dot(a, b):
    return jnp.dot(a, b, preferred_element_type=F32)


def _dot_nt(a, b):
    return lax.dot_general(a, b, (((1,), (1,)), ((), ())), preferred_element_type=F32)


def _dot_tn(a, b):
    return lax.dot_general(a, b, (((0,), (0,)), ((), ())), preferred_element_type=F32)


def _pick_tile(n, pref):
    t = min(pref, n)
    while n % t:
        t //= 2
    return t


def _linear_step(grid, idx):
    t = idx[0]
    for i, g in zip(idx[1:], grid[1:]):
        t = t * g + i
    return t


def _ffn_cast_jobs(w_in, w_out, layer, half, grid):
    n_steps = math.prod(grid)
    jobs = []
    for w in (w_in, w_out):
        rows, cols = w.shape[2:]
        n = n_steps
        while rows % n or (rows // n) % BF16_SUBLANES:
            n -= 1
        slab = rows // n
        blk = lambda *idx, n=n: jnp.minimum(_linear_step(grid, idx), n - 1)

        def cast(t, src_ref, dst_ref, n=n):
            @pl.when(t < n)
            def _():
                dst_ref[...] = src_ref[...].astype(dst_ref.dtype)

        jobs.append((
            [w],
            [pl.BlockSpec((None, None, slab, cols), lambda *idx, blk=blk: (layer, half, blk(*idx), 0))],
            pl.BlockSpec((slab, cols), lambda *idx, blk=blk: (blk(*idx), 0)),
            jax.ShapeDtypeStruct((rows, cols), BF16),
            cast,
        ))
    return jobs


def _call_with_jobs(body, jobs, *, out_shape, grid, in_specs, out_specs, scratch_shapes=(),
                    semantics, name, operands):
    n_in = len(in_specs)
    n_job_in = [len(j[0]) for j in jobs]
    n_extra = sum(n_job_in)

    def kernel(*refs):
        t = _linear_step(grid, [pl.program_id(ax) for ax in range(len(grid))])
        at = n_in
        for i, j in enumerate(jobs):
            j[4](t, *refs[at:at + n_job_in[i]], refs[n_in + n_extra + 1 + i])
            at += n_job_in[i]
        body(*refs[:n_in], refs[n_in + n_extra], *refs[n_in + n_extra + 1 + len(jobs):])

    outs = pl.pallas_call(
        kernel,
        out_shape=[out_shape] + [j[3] for j in jobs],
        grid=grid,
        in_specs=list(in_specs) + [sp for j in jobs for sp in j[1]],
        out_specs=[out_specs] + [j[2] for j in jobs],
        scratch_shapes=list(scratch_shapes),
        compiler_params=_cparams(semantics),
        name=name,
    )(*operands, *[op for j in jobs for op in j[0]])
    return outs[0], list(outs[1:])


def _ffn_kernel(x_ref, g_ref, w_in_ref, w_out_ref, o_ref, xn_ref, wg_buf, wu_buf, wo_buf, sem,
                *, tf):
    d_ff = w_out_ref.shape[0]
    nj = d_ff // tf

    def copies(j, slot):
        return (
            pltpu.make_async_copy(w_in_ref.at[:, pl.ds(j * tf, tf)], wg_buf.at[slot], sem.at[slot, 0]),
            pltpu.make_async_copy(w_in_ref.at[:, pl.ds(d_ff + j * tf, tf)], wu_buf.at[slot],
                                  sem.at[slot, 1]),
            pltpu.make_async_copy(w_out_ref.at[pl.ds(j * tf, tf), :], wo_buf.at[slot], sem.at[slot, 2]),
        )

    for c in copies(0, 0):
        c.start()
    x = x_ref[...]
    xn_ref[...] = _rms_rows(x, g_ref[...]).astype(BF16)
    o_ref[...] = x
    for j in range(nj):
        slot = j % 2
        if j + 1 < nj:
            for c in copies(j + 1, 1 - slot):
                c.start()
        for c in copies(j, slot):
            c.wait()
        xn = xn_ref[...]
        hg = _dot(xn, wg_buf[slot])
        hu = _dot(xn, wu_buf[slot])
        h = (0.5 * hg * jax.nn.sigmoid(hg) * hu).astype(BF16)
        o_ref[...] += _dot(h, wo_buf[slot])


def _ffn(x, g, w_in, w_out, tm=FFN_TOKEN_TILE, tf=FFN_HIDDEN_TILE):
    n, d = x.shape
    d_ff = w_out.shape[0]
    tm = _pick_tile(n, tm)
    tf = _pick_tile(d_ff, tf)
    return pl.pallas_call(
        functools.partial(_ffn_kernel, tf=tf),
        out_shape=jax.ShapeDtypeStruct((n, d), F32),
        grid=(n // tm,),
        in_specs=[
            pl.BlockSpec((tm, d), lambda i: (i, 0)),
            pl.BlockSpec((1, d), lambda i: (0, 0)),
            pl.BlockSpec(memory_space=pl.ANY),
            pl.BlockSpec(memory_space=pl.ANY),
        ],
        out_specs=pl.BlockSpec((tm, d), lambda i: (i, 0)),
        scratch_shapes=[pltpu.VMEM((tm, d), BF16),
                        pltpu.VMEM((2, d, tf), BF16), pltpu.VMEM((2, d, tf), BF16),
                        pltpu.VMEM((2, tf, d), BF16), pltpu.SemaphoreType.DMA((2, 3))],
        compiler_params=_cparams(("arbitrary",)),
        name="ffn",
    )(x, g.reshape(1, d), w_in, w_out)


def _norm_matmul_kernel(x_ref, g_ref, w_ref, o_ref, xn_ref):
    @pl.when(pl.program_id(1) == 0)
    def _():
        xn_ref[...] = _rms_rows(x_ref[...], g_ref[...]).astype(BF16)

    o_ref[...] = _dot(xn_ref[...], w_ref[...])


def _norm_matmul(x, g, w, tn, tm=PROJ_TOKEN_TILE):
    n, d = x.shape
    wn = w.shape[1]
    tm = _pick_tile(n, tm)
    while wn % tn:
        tn -= 128
    return pl.pallas_call(
        _norm_matmul_kernel,
        out_shape=jax.ShapeDtypeStruct((n, wn), F32),
        grid=(n // tm, wn // tn),
        in_specs=[
            pl.BlockSpec((tm, d), lambda i, j: (i, 0)),
            pl.BlockSpec((1, d), lambda i, j: (0, 0)),
            pl.BlockSpec((d, tn), lambda i, j: (0, j)),
        ],
        out_specs=pl.BlockSpec((tm, tn), lambda i, j: (i, j)),
        scratch_shapes=[pltpu.VMEM((tm, d), BF16)],
        compiler_params=_cparams(("parallel", "arbitrary")),
        name="norm_matmul",
    )(x, g.reshape(1, d), w)


def _out_proj_kernel(x_ref, ya_ref, yb_ref, wa_ref, wb_ref, o_ref):
    o_ref[...] = x_ref[...] + _dot(ya_ref[...], wa_ref[...]) + _dot(yb_ref[...], wb_ref[...])


def _out_proj(x, ya, yb, w, tm=OUT_PROJ_TOKEN_TILE):
    n, d = x.shape
    da = ya.shape[1]
    db = yb.shape[1]
    assert da == db and w.shape[0] == da + db
    tm = _pick_tile(n, tm)
    return pl.pallas_call(
        _out_proj_kernel,
        out_shape=jax.ShapeDtypeStruct((n, d), F32),
        grid=(n // tm,),
        in_specs=[
            pl.BlockSpec((tm, d), lambda i: (i, 0)),
            pl.BlockSpec((tm, da), lambda i: (i, 0)),
            pl.BlockSpec((tm, db), lambda i: (i, 0)),
            pl.BlockSpec((da, d), lambda i: (0, 0)),
            pl.BlockSpec((db, d), lambda i: (1, 0)),
        ],
        out_specs=pl.BlockSpec((tm, d), lambda i: (i, 0)),
        compiler_params=_cparams(("parallel",)),
        name="out_proj",
    )(x, ya, yb, w, w)


def _causal_conv(x, w, b):
    y = x * w[CONV_W - 1:CONV_W] + b
    for k in range(1, CONV_W):
        y = y + _shift_rows(x, k, 0.0) * w[CONV_W - 1 - k:CONV_W - k]
    return y


def _shift_rows(x, k, fill):
    assert 0 < k < SUBLANES
    rolled = pltpu.roll(x, k, axis=0)
    t = lax.broadcasted_iota(jnp.int32, (SUBLANES, x.shape[1]), 0)
    head = jnp.where(t >= k, rolled[:SUBLANES], fill)
    return jnp.concatenate([head, rolled[SUBLANES:]], axis=0)


def _mixer_a_kernel(xr_ref, gate_ref, cw_ref, cb_ref, wr_ref, br_ref, wi_ref, bi_ref,
                    lam_ref, o_ref):
    s = xr_ref.shape[0]
    xc = _causal_conv(xr_ref[...], cw_ref[...], cb_ref[...])
    xb = xc.astype(BF16)
    r = jax.nn.sigmoid(_dot(xb, wr_ref[...]) + br_ref[...])
    i = jax.nn.sigmoid(_dot(xb, wi_ref[...]) + bi_ref[...])
    neg_lam = -lam_ref[...]
    softplus = jnp.maximum(neg_lam, 0.0) + jnp.log1p(jnp.exp(-jnp.abs(neg_lam)))
    log_a = -RG_C * r * softplus
    a = jnp.exp(log_a)
    u = jnp.sqrt(-jnp.tanh(log_a) * (a * a + 1.0)) * (i * xc)
    k = 1
    while k < s:
        if k < SUBLANES:
            u = a * _shift_rows(u, k, 0.0) + u
            a = a * _shift_rows(a, k, 1.0)
        else:
            u = jnp.concatenate([u[:k], a[k:] * u[:s - k] + u[k:]], axis=0)
            a = jnp.concatenate([a[:k], a[k:] * a[:s - k]], axis=0)
        k *= 2
    o_ref[...] = (jax.nn.gelu(gate_ref[...]) * u).astype(o_ref.dtype)


def _mixer_a(proj, conv_w, conv_b, w_r, b_r, w_i, b_i, lam, d_rg, cast_jobs):
    bsz, s, _ = proj.shape
    nb = d_rg // RG_BLOCK
    grid = (bsz, nb)
    vec = lambda v: v.reshape(1, d_rg)
    col = lambda off: pl.BlockSpec((None, s, RG_BLOCK), lambda b, g: (b, 0, off + g))
    row = lambda r: pl.BlockSpec((r, RG_BLOCK), lambda b, g: (0, g))
    blk = pl.BlockSpec((None, RG_BLOCK, RG_BLOCK), lambda b, g: (g, 0, 0))
    return _call_with_jobs(
        _mixer_a_kernel, cast_jobs(grid),
        out_shape=jax.ShapeDtypeStruct((bsz, s, d_rg), BF16),
        grid=grid,
        in_specs=[col(nb), col(0), row(CONV_W), row(1), blk, row(1), blk, row(1), row(1)],
        out_specs=pl.BlockSpec((None, s, RG_BLOCK), lambda b, g: (b, 0, g)),
        semantics=("arbitrary", "arbitrary"),
        name="mixer_a",
        operands=(proj, proj, conv_w, vec(conv_b), w_r.astype(BF16), vec(b_r), w_i.astype(BF16),
                  vec(b_i), vec(lam)))


def _t5_bucket_np(dist):
    max_exact = N_BUCKETS // 2
    d_f = np.maximum(dist, 1).astype(np.float32)
    large = max_exact + (np.log(d_f / np.float32(max_exact)) / np.float32(math.log(MAX_DIST / max_exact))
                         * np.float32(N_BUCKETS - max_exact)).astype(np.int32)
    large = np.minimum(large, N_BUCKETS - 1)
    return np.where(dist < max_exact, dist, large).astype(np.int32)


def _dsw_bias_vector(rel_bias, seq, qb):
    dist = np.arange(seq)
    mult = np.zeros(seq, np.float32)
    for window, dilation in DSW_PATTERNS:
        mult += ((dist % dilation == 0) & (dist <= window)).astype(np.float32)
    logm = np.where(mult > 0, np.log(np.maximum(mult, 1.0)), NEG).astype(np.float32)
    f = (rel_bias.astype(F32)[_t5_bucket_np(dist)] + logm[:, None]) * LOG2E
    n_heads = f.shape[1]
    hv = jnp.concatenate([f[::-1].T, jnp.full((n_heads, qb), NEG, F32)], axis=1)
    return hv[:, None, :]


def _mixer_b_kernel(q_ref, k_ref, v_ref, g_ref, hv_ref, o_ref, kn_ref, vb_ref, tbl_ref):
    s, hd = q_ref.shape
    qb = tbl_ref.shape[0]

    @pl.when(pl.program_id(1) == 0)
    def _():
        skew = pltpu.roll(jnp.broadcast_to(hv_ref[...], (qb, s + qb)), 1, axis=1, stride=1, stride_axis=0)
        tbl_ref[...] = skew[:, qb:]

    kn_ref[...] = _rms_rows(k_ref[...], g_ref[1:2]).astype(BF16)
    vb_ref[...] = v_ref[...].astype(BF16)
    for gi in range(s // qb):
        n_keys = (gi + 1) * qb
        rows = slice(gi * qb, n_keys)
        qn = (_rms_rows(q_ref[rows, :], g_ref[0:1]) * (LOG2E / math.sqrt(hd))).astype(BF16)
        sc = _dot_nt(qn, kn_ref[0:n_keys, :]) + tbl_ref[:, s - n_keys:]
        m = jnp.max(sc, axis=-1, keepdims=True)
        p = jnp.exp2(sc - m)
        l = jnp.sum(p, axis=-1, keepdims=True)
        o = _dot(p.astype(BF16), vb_ref[0:n_keys, :])
        o_ref[rows, :] = (o / l).astype(o_ref.dtype)


def _mixer_b(proj, qk_g, rel_bias, col0, n_heads, cast_jobs, qb=DSW_QUERY_ROWS):
    bsz, s, _ = proj.shape
    qb = _pick_tile(s, qb)
    hv = _dsw_bias_vector(rel_bias, s, qb)
    c0 = col0 // HEAD_B
    grid = (n_heads, bsz)
    seq_spec = lambda off: pl.BlockSpec((None, s, HEAD_B), lambda h, b: (b, 0, off + h))
    return _call_with_jobs(
        _mixer_b_kernel, cast_jobs(grid),
        out_shape=jax.ShapeDtypeStruct((bsz, s, n_heads * HEAD_B), BF16),
        grid=grid,
        in_specs=[
            seq_spec(c0), seq_spec(c0 + n_heads), seq_spec(c0 + 2 * n_heads),
            pl.BlockSpec((2, HEAD_B), lambda h, b: (0, 0)),
            pl.BlockSpec((None, 1, s + qb), lambda h, b: (h, 0, 0)),
        ],
        out_specs=pl.BlockSpec((None, s, HEAD_B), lambda h, b: (b, 0, h)),
        scratch_shapes=[pltpu.VMEM((s, HEAD_B), BF16), pltpu.VMEM((s, HEAD_B), BF16),
                        pltpu.VMEM((qb, s), F32)],
        semantics=("arbitrary", "arbitrary"),
        name="mixer_b",
        operands=(proj, proj, proj, qk_g, hv))


def _mixer_c_kernel(gb_ref, q_ref, k_ref, v_ref, og_ref, cwq_ref, cwk_ref, cbq_ref, cbk_ref,
                    ig_ref, fg_ref, gain_ref, o_ref, qc_ref, kc_ref, c_ref, n_ref, m_ref):
    h = pl.program_id(1)
    L = SEQ_BLOCK
    s, d = q_ref.shape
    nc = s // L

    qc_ref[...] = jax.nn.silu(_causal_conv(q_ref[...], cwq_ref[...], cbq_ref[...]))
    kc_ref[...] = jax.nn.silu(_causal_conv(k_ref[...], cwk_ref[...], cbk_ref[...])) * (d ** -0.5)
    c_ref[...] = jnp.zeros_like(c_ref)
    n_ref[...] = jnp.zeros_like(n_ref)
    m_ref[...] = jnp.zeros_like(m_ref)

    ri = lax.broadcasted_iota(jnp.int32, (L, L), 0)
    ci = lax.broadcasted_iota(jnp.int32, (L, L), 1)
    eye = ri == ci
    causal = ci <= ri

    def to_col(row):
        return jnp.sum(jnp.where(eye, row, 0.0), axis=1, keepdims=True)

    ig_all = ig_ref[...] + gb_ref[0, h]
    f_all = fg_ref[...] + gb_ref[1, h]
    ls_all = jnp.minimum(f_all, 0.0) - jnp.log1p(jnp.exp(-jnp.abs(f_all)))

    for n in range(nc):
        rows = slice(n * L, (n + 1) * L)
        qc = qc_ref[rows, :]
        kc = kc_ref[rows, :]
        qb = qc.astype(BF16)
        kb = kc.astype(BF16)
        vb = v_ref[rows, :].astype(BF16)

        ig_row = ig_all[n:n + 1]
        ls_row = ls_all[n:n + 1]
        ls_col = to_col(ls_row)
        ig_col = to_col(ig_row)
        b_col = jnp.sum(jnp.where(causal, ls_row, 0.0), axis=1, keepdims=True)
        b_row = jnp.sum(jnp.where(ri <= ci, ls_col, 0.0), axis=0, keepdims=True)
        b_last = jnp.sum(ls_row, axis=1, keepdims=True)

        log_d = b_col - b_row + ig_row
        g_row = b_last - b_row + ig_row
        g_col = b_last - b_col + ig_col
        g_max = jnp.max(g_row, axis=1, keepdims=True)
        d_max = jnp.max(jnp.where(causal, log_d, -jnp.inf), axis=1, keepdims=True)
        qk = _dot_nt(qb, kb)

        c_st = c_ref[...]
        n_st = n_ref[...]
        m_st = m_ref[...]

        inter = b_col + m_st
        m_t = jnp.maximum(inter, d_max)
        s_qk = jnp.where(causal, qk * jnp.exp(log_d - m_t), 0.0)
        inter_w = jnp.exp(inter - m_t)
        num = _dot(s_qk.astype(BF16), vb) + inter_w * _dot(qb, c_st.astype(BF16))
        qn = jnp.sum(qc * n_st, axis=1, keepdims=True)
        den = jnp.sum(s_qk, axis=1, keepdims=True) + inter_w * qn
        hout = num / jnp.maximum(jnp.abs(den), jnp.exp(-m_t))

        hn = _rms_rows(hout, gain_ref[...])
        o_ref[rows, :] = (jax.nn.sigmoid(og_ref[rows, :]) * hn).astype(o_ref.dtype)

        wk = jnp.exp(g_col - g_max) * kc
        c_loc = _dot_tn(wk.astype(BF16), vb)
        n_loc = jnp.sum(wk, axis=0, keepdims=True)
        m_new = jnp.maximum(b_last + m_st, g_max)
        decay = jnp.exp(b_last + m_st - m_new)
        fresh = jnp.exp(g_max - m_new)
        c_ref[...] = decay * c_st + fresh * c_loc
        n_ref[...] = decay * n_st + fresh * n_loc
        m_ref[...] = m_new


def _mixer_c(proj, gates, conv_w, conv_b, gate_b, h_gain, n_heads, cast_jobs):
    bsz, s, _ = proj.shape
    d = HEAD_C
    L = SEQ_BLOCK
    nc = s // L
    grid = (bsz, n_heads)
    g_rows = jnp.transpose(gates, (0, 2, 1)).reshape(bsz, 2 * n_heads, nc, L)
    cb = conv_b.reshape(1, -1)
    full = lambda off: pl.BlockSpec((None, s, d), lambda b, h: (b, 0, off + h))
    row = lambda r, off: pl.BlockSpec((r, d), lambda b, h: (0, off + h))
    grow = lambda off: pl.BlockSpec((None, None, nc, L), lambda b, h: (b, off + h, 0, 0))
    return _call_with_jobs(
        _mixer_c_kernel, cast_jobs(grid),
        out_shape=jax.ShapeDtypeStruct((bsz, s, n_heads * d), BF16),
        grid=grid,
        in_specs=[
            pl.BlockSpec(memory_space=pltpu.SMEM),
            full(0), full(n_heads), full(2 * n_heads), full(3 * n_heads),
            row(CONV_W, 0), row(CONV_W, n_heads), row(1, 0), row(1, n_heads),
            grow(0), grow(n_heads), row(1, 0),
        ],
        out_specs=pl.BlockSpec((None, s, d), lambda b, h: (b, 0, h)),
        scratch_shapes=[pltpu.VMEM((s, d), F32), pltpu.VMEM((s, d), F32),
                        pltpu.VMEM((d, d), F32), pltpu.VMEM((1, d), F32), pltpu.VMEM((1, 1), F32)],
        semantics=("arbitrary", "arbitrary"),
        name="mixer_c",
        operands=(gate_b, proj, proj, proj, proj, conv_w, conv_w, cb, cb, g_rows, g_rows,
                  h_gain.reshape(1, -1)))


def _mixer_d_kernel(q_ref, k_ref, v_ref, o_ref, kb_ref, vb_ref, acc_ref, suf_ref):
    s, hd = q_ref.shape
    B = SEQ_BLOCK
    G = min(SB_QUERY_ROWS, s)
    kb_ref[...] = k_ref[...].astype(BF16)
    vb_ref[...] = v_ref[...].astype(BF16)
    ri = lax.broadcasted_iota(jnp.int32, (B, B), 0)
    ci = lax.broadcasted_iota(jnp.int32, (B, B), 1)
    after = (ri > ci).astype(BF16)
    after2 = jnp.concatenate([after, after], axis=0)
    row_g = lax.broadcasted_iota(jnp.int32, (G, B), 0)
    col_g = lax.broadcasted_iota(jnp.int32, (G, B), 1)

    def span(q0, first, last, suffix):
        n = last - first + 1
        keys = slice(first * B, (last + 1) * B)
        qb = (q_ref[q0:q0 + G, :] * (hd ** -0.5)).astype(BF16)
        z = _dot_nt(qb, kb_ref[keys, :])
        sp = jnp.log(1.0 + jnp.exp2(jnp.abs(z) * (-LOG2E)))
        nm = jnp.minimum(z, 0.0)
        log_beta = nm - sp
        log_stay = (nm - z) - sp
        before = [col_g + ((first + j) * B - q0) < row_g if (first + j + 1) * B > q0 else None
                  for j in range(n)]
        later, tot = [], []
        for j in range(n):
            x = log_stay[:, j * B:(j + 1) * B]
            if before[j] is not None:
                x = jnp.where(before[j], x, 0.0)
            hi = lax.bitcast_convert_type(lax.bitcast_convert_type(x, jnp.uint32) & HI_MASK, F32)
            hl = jnp.concatenate([hi.astype(BF16), (x - hi).astype(BF16)], axis=1)
            later.append(_dot(hl, after2))
            tot.append(jnp.sum(x, axis=-1, keepdims=True))
        att = [None] * n
        for j in range(n - 1, -1, -1):
            a = jnp.exp(log_beta[:, j * B:(j + 1) * B] + later[j] + suffix)
            if before[j] is not None:
                a = jnp.where(before[j], a, 0.0)
            att[j] = a.astype(BF16)
            suffix = suffix + tot[j]
        att = att[0] if n == 1 else jnp.concatenate(att, axis=1)
        return _dot(att, vb_ref[keys, :]), suffix

    def first_near(q0):
        return max(q0 // B - SB_NEAR_BLOCKS + 1, 0)

    worst = None
    for q0 in range(0, s, G):
        rows = slice(q0, q0 + G)
        acc, suffix = span(q0, first_near(q0), (q0 + G) // B - 1, jnp.zeros((G, 1), F32))
        o_ref[rows, :] = acc.astype(o_ref.dtype)
        if first_near(q0) > 0:
            acc_ref[rows, :] = acc
            suf_ref[rows, :] = suffix
            worst = suffix if worst is None else jnp.maximum(worst, suffix)

    if worst is not None:
        @pl.when(jnp.max(worst) > SB_UNDERFLOW)
        def _():
            for q0 in range(0, s, G):
                if first_near(q0) == 0:
                    continue
                rows = slice(q0, q0 + G)
                suffix = suf_ref[rows, :]

                @pl.when(jnp.max(suffix) > SB_UNDERFLOW)
                def _():
                    far, _ = span(q0, 0, first_near(q0) - 1, suffix)
                    o_ref[rows, :] = (acc_ref[rows, :] + far).astype(o_ref.dtype)


def _mixer_d(proj, col0, n_heads, cast_jobs):
    bsz, s, _ = proj.shape
    c0 = col0 // HEAD_D
    grid = (bsz, n_heads)
    seq_spec = lambda off: pl.BlockSpec((None, s, HEAD_D), lambda b, h: (b, 0, off + h))
    return _call_with_jobs(
        _mixer_d_kernel, cast_jobs(grid),
        out_shape=jax.ShapeDtypeStruct((bsz, s, n_heads * HEAD_D), BF16),
        grid=grid,
        in_specs=[seq_spec(c0), seq_spec(c0 + n_heads), seq_spec(c0 + 2 * n_heads)],
        out_specs=pl.BlockSpec((None, s, HEAD_D), lambda b, h: (b, 0, h)),
        scratch_shapes=[pltpu.VMEM((s, HEAD_D), BF16), pltpu.VMEM((s, HEAD_D), BF16),
                        pltpu.VMEM((s, HEAD_D), F32), pltpu.VMEM((s, 1), F32)],
        semantics=("arbitrary", "arbitrary"),
        name="mixer_d",
        operands=(proj, proj, proj))


def _layer_ab(x, bsz, s, g, w_in, conv_w, conv_b, w_r, b_r, w_i, b_i, lam, qk_g, rel_bias, w_out,
              jobs):
    d_rg = conv_w.shape[-1]
    n_heads = rel_bias.shape[-1]
    proj = _norm_matmul(x, g, w_in.astype(BF16), tn=PROJ_COL_TILE_AB).reshape(bsz, s, -1)
    ya, out_a = _mixer_a(proj, conv_w, conv_b, w_r, b_r, w_i, b_i, lam, d_rg, jobs[0])
    yb, out_b = _mixer_b(proj, qk_g, rel_bias, 2 * d_rg, n_heads, jobs[1])
    n = bsz * s
    return _out_proj(x, ya.reshape(n, -1), yb.reshape(n, -1), w_out.astype(BF16)), out_a + out_b


def _cd_layout(gate_b, w_out):
    n_heads_c = gate_b.shape[-1]
    dc = n_heads_c * HEAD_C
    dd = w_out.shape[0] - dc
    n_gate = 2 * n_heads_c
    main = 4 * dc + 3 * dd
    width = -(-(main + n_gate) // CD_WIDTH_MULTIPLE) * CD_WIDTH_MULTIPLE
    return n_heads_c, dc, dd, n_gate, main, width


def _cd_weight_job(w_in, j, gate_col, n_gate, width, tw=CD_WEIGHT_COL_TILE):
    _, d, cols = w_in.shape
    main = cols - n_gate
    assert gate_col % tw == 0 and (main - gate_col) % tw == 0 and width % tw == 0 and n_gate <= tw
    n_aligned = gate_col // tw
    n_shifted = (main - gate_col) // tw
    n_blk = width // tw
    last = (cols - 1) // tw

    def job(grid):
        assert n_blk <= math.prod(grid)
        blk = lambda *idx: jnp.minimum(_linear_step(grid, idx), n_blk - 1)

        def reorder(t, a_ref, b_ref, o_ref):
            lane = lax.broadcasted_iota(jnp.int32, o_ref.shape, 1)

            @pl.when(t < n_aligned)
            def _():
                o_ref[...] = a_ref[...].astype(o_ref.dtype)

            @pl.when((t >= n_aligned) & (t < n_aligned + n_shifted))
            def _():
                a = pltpu.roll(a_ref[...], tw - n_gate, axis=1)
                b = pltpu.roll(b_ref[...], tw - n_gate, axis=1)
                o_ref[...] = jnp.where(lane < tw - n_gate, a, b).astype(o_ref.dtype)

            @pl.when(t == n_aligned + n_shifted)
            def _():
                o_ref[...] = jnp.where(lane < n_gate, a_ref[...], 0.0).astype(o_ref.dtype)

            @pl.when((t > n_aligned + n_shifted) & (t < n_blk))
            def _():
                o_ref[...] = jnp.zeros_like(o_ref)

        a_blk = lambda *idx: jnp.where(blk(*idx) < n_aligned + n_shifted, blk(*idx), n_aligned)
        return [([w_in, w_in],
                 [pl.BlockSpec((None, d, tw), lambda *idx: (j, 0, a_blk(*idx))),
                  pl.BlockSpec((None, d, tw), lambda *idx: (j, 0, jnp.minimum(blk(*idx) + 1, last)))],
                 pl.BlockSpec((d, tw), lambda *idx: (0, blk(*idx))),
                 jax.ShapeDtypeStruct((d, width), BF16),
                 reorder)]
    return job


def _layer_cd(x, bsz, s, g, w, conv_w, conv_b, gate_b, h_gain, w_out, jobs):
    n_heads_c, dc, dd, n_gate, main, width = _cd_layout(gate_b, w_out)
    n_heads_d = dd // HEAD_D
    proj = _norm_matmul(x, g, w, tn=PROJ_COL_TILE_CD).reshape(bsz, s, width)
    gates = proj[:, :, main:main + n_gate]
    yd, out_d = _mixer_d(proj, 4 * dc, n_heads_d, jobs[0])
    yc, out_c = _mixer_c(proj, gates, conv_w, conv_b, gate_b, h_gain, n_heads_c, jobs[1])
    n = bsz * s
    return _out_proj(x, yc.reshape(n, -1), yd.reshape(n, -1), w_out.astype(BF16)), out_d + out_c


def kernel(x, norm_g, ffn_w_in, ffn_w_out, ab_w_in, ab_conv_w, ab_conv_b, rg_w_r, rg_b_r, rg_w_i,
           rg_b_i, rg_lambda, qk_gain, rel_bias, ab_w_out, cd_w_in, cd_conv_w, cd_conv_b,
           mlstm_gate_bias, mlstm_h_gain, cd_w_out):
    bsz, s, d = x.shape
    depth = norm_g.shape[0]
    h = x.reshape(bsz * s, d)
    ffn_w = {(0, 0): (ffn_w_in[0, 0].astype(BF16), ffn_w_out[0, 0].astype(BF16))}
    cd_w = None
    for layer in range(depth):
        j = layer // 2
        h = _ffn(h, norm_g[layer, 0], *ffn_w[(layer, 0)])
        targets = [(layer, 1)] + ([(layer + 1, 0)] if layer + 1 < depth else [])
        jobs = [functools.partial(_ffn_cast_jobs, ffn_w_in, ffn_w_out, *t) for t in targets]
        jobs += [lambda grid: []] * (2 - len(jobs))
        if layer % 2 == 0:
            if layer + 1 < depth:
                _, dc, _, n_gate, _, width = _cd_layout(mlstm_gate_bias[j], cd_w_out[j])
                ffn_jobs, cd_job = jobs[0], _cd_weight_job(cd_w_in, j, 4 * dc, n_gate, width)
                jobs[0] = lambda grid: ffn_jobs(grid) + cd_job(grid)
            h, out = _layer_ab(h, bsz, s, norm_g[layer, 1], ab_w_in[j], ab_conv_w[j], ab_conv_b[j],
                               rg_w_r[j], rg_b_r[j], rg_w_i[j], rg_b_i[j], rg_lambda[j], qk_gain[j],
                               rel_bias, ab_w_out[j], jobs)
            if layer + 1 < depth:
                cd_w = out.pop(2)
        else:
            h, out = _layer_cd(h, bsz, s, norm_g[layer, 1], cd_w, cd_conv_w[j], cd_conv_b[j],
                               mlstm_gate_bias[j], mlstm_h_gain[j], cd_w_out[j], jobs)
        for i, t in enumerate(targets):
            ffn_w[t] = (out[2 * i], out[2 * i + 1])
        h = _ffn(h, norm_g[layer, 2], *ffn_w[(layer, 1)])
    return h.reshape(bsz, s, d)
```

```python
import functools
import math

import numpy as np
import jax
import jax.numpy as jnp
from jax import lax
from jax.experimental import pallas as pl
from jax.experimental.pallas import tpu as pltpu

F32 = jnp.float32
BF16 = jnp.bfloat16

RMS_EPS = 1e-6
RG_C = 8.0
CONV_W = 4
RG_BLOCK = 128
HEAD_B = 128
HEAD_C = 256
HEAD_D = 128
SEQ_BLOCK = 128
SUBLANES = 8
BF16_SUBLANES = 16
DSW_PATTERNS = ((128, 1), (512, 4), (2048, 16))
N_BUCKETS = 32
MAX_DIST = 2048
NEG = -1e30
LOG2E = 1.4426950408889634
HI_MASK = np.uint32(0xFFFF0000)
SB_NEAR_BLOCKS = 3
SB_QUERY_ROWS = 256
SB_UNDERFLOW = -104.0
DSW_QUERY_ROWS = 256

FFN_TOKEN_TILE = 1024
FFN_HIDDEN_TILE = 512
PROJ_TOKEN_TILE = 1024
PROJ_COL_TILE_AB = 1280
PROJ_COL_TILE_CD = 1536
OUT_PROJ_TOKEN_TILE = 512
CD_WIDTH_MULTIPLE = 512
CD_WEIGHT_COL_TILE = 256
V7X_VMEM_BYTES = 64 * 1024 * 1024
VMEM_LIMIT = V7X_VMEM_BYTES * 7 // 8


def _cparams(sem):
    return pltpu.CompilerParams(dimension_semantics=sem, vmem_limit_bytes=VMEM_LIMIT)


def _rms_rows(x, g):
    return x * lax.rsqrt(jnp.mean(x * x, axis=-1, keepdims=True) + RMS_EPS) * g


def _dot(a, b):
    return jnp.dot(a, b, preferred_element_type=F32)


def _dot_nt(a, b):
    return lax.dot_general(a, b, (((1,), (1,)), ((), ())), preferred_element_type=F32)


def _dot_tn(a, b):
    return lax.dot_general(a, b, (((0,), (0,)), ((), ())), preferred_element_type=F32)


def _pick_tile(n, pref):
    t = min(pref, n)
    while n % t:
        t //= 2
    return t


def _linear_step(grid, idx):
    t = idx[0]
    for i, g in zip(idx[1:], grid[1:]):
        t = t * g + i
    return t


def _ffn_cast_jobs(w_in, w_out, layer, half, grid):
    n_steps = math.prod(grid)
    d_ff = w_out.shape[2]
    tf = _pick_tile(d_ff, FFN_HIDDEN_TILE)
    nj = d_ff // tf
    jobs = []
    for w, packed in ((w_in, True), (w_out, False)):
        rows, cols = w.shape[2:]
        n = n_steps
        while rows % n or (rows // n) % BF16_SUBLANES:
            n -= 1
        slab = rows // n
        blk = lambda *idx, n=n: jnp.minimum(_linear_step(grid, idx), n - 1)

        def cast(t, src_ref, dst_ref, n=n, packed=packed):
            @pl.when(t < n)
            def _():
                if packed:
                    for j in range(nj):
                        dst_ref[j, :, :tf] = src_ref[:, j * tf:(j + 1) * tf].astype(dst_ref.dtype)
                        dst_ref[j, :, tf:] = src_ref[:, d_ff + j * tf:d_ff + (j + 1) * tf].astype(
                            dst_ref.dtype)
                else:
                    dst_ref[...] = src_ref[...].astype(dst_ref.dtype)

        if packed:
            out_spec = pl.BlockSpec((nj, slab, 2 * tf), lambda *idx, blk=blk: (0, blk(*idx), 0))
            out_shape = jax.ShapeDtypeStruct((nj, rows, 2 * tf), BF16)
        else:
            out_spec = pl.BlockSpec((slab, cols), lambda *idx, blk=blk: (blk(*idx), 0))
            out_shape = jax.ShapeDtypeStruct((rows, cols), BF16)
        jobs.append((
            [w],
            [pl.BlockSpec((None, None, slab, cols), lambda *idx, blk=blk: (layer, half, blk(*idx), 0))],
            out_spec, out_shape, cast,
        ))
    return jobs


def _pack_gate_up(w_in, tf):
    d, two_ff = w_in.shape
    nj = two_ff // (2 * tf)
    return jnp.transpose(w_in.reshape(d, 2, nj, tf), (2, 0, 1, 3)).reshape(nj, d, 2 * tf)


def _call_with_jobs(body, jobs, *, out_shape, grid, in_specs, out_specs, scratch_shapes=(),
                    semantics, name, operands):
    n_in = len(in_specs)
    n_job_in = [len(j[0]) for j in jobs]
    n_extra = sum(n_job_in)

    def kernel(*refs):
        t = _linear_step(grid, [pl.program_id(ax) for ax in range(len(grid))])
        at = n_in
        for i, j in enumerate(jobs):
            j[4](t, *refs[at:at + n_job_in[i]], refs[n_in + n_extra + 1 + i])
            at += n_job_in[i]
        body(*refs[:n_in], refs[n_in + n_extra], *refs[n_in + n_extra + 1 + len(jobs):])

    outs = pl.pallas_call(
        kernel,
        out_shape=[out_shape] + [j[3] for j in jobs],
        grid=grid,
        in_specs=list(in_specs) + [sp for j in jobs for sp in j[1]],
        out_specs=[out_specs] + [j[2] for j in jobs],
        scratch_shapes=list(scratch_shapes),
        compiler_params=_cparams(semantics),
        name=name,
    )(*operands, *[op for j in jobs for op in j[0]])
    return outs[0], list(outs[1:])


def _ffn_kernel(x_ref, g_ref, wgu_ref, wo_ref, o_ref, xn_ref):
    j = pl.program_id(1)
    tf = wo_ref.shape[0]

    @pl.when(j == 0)
    def _():
        x = x_ref[...]
        xn_ref[...] = _rms_rows(x, g_ref[...]).astype(BF16)
        o_ref[...] = x

    hgu = _dot(xn_ref[...], wgu_ref[...])
    hg = hgu[:, :tf]
    hu = hgu[:, tf:]
    h = (0.5 * hg * jax.nn.sigmoid(hg) * hu).astype(BF16)
    o_ref[...] += _dot(h, wo_ref[...])


def _ffn(x, g, w_gu, w_out, tm=FFN_TOKEN_TILE):
    n, d = x.shape
    nj, _, two_tf = w_gu.shape
    tf = two_tf // 2
    tm = _pick_tile(n, tm)
    return pl.pallas_call(
        _ffn_kernel,
        out_shape=jax.ShapeDtypeStruct((n, d), F32),
        grid=(n // tm, nj),
        in_specs=[
            pl.BlockSpec((tm, d), lambda i, j: (i, 0)),
            pl.BlockSpec((1, d), lambda i, j: (0, 0)),
            pl.BlockSpec((None, d, two_tf), lambda i, j: (j, 0, 0)),
            pl.BlockSpec((tf, d), lambda i, j: (j, 0)),
        ],
        out_specs=pl.BlockSpec((tm, d), lambda i, j: (i, 0)),
        scratch_shapes=[pltpu.VMEM((tm, d), BF16)],
        compiler_params=_cparams(("parallel", "arbitrary")),
        name="ffn",
    )(x, g.reshape(1, d), w_gu, w_out)


def _norm_matmul_kernel(x_ref, g_ref, w_ref, o_ref, xn_ref):
    @pl.when(pl.program_id(1) == 0)
    def _():
        xn_ref[...] = _rms_rows(x_ref[...], g_ref[...]).astype(BF16)

    o_ref[...] = _dot(xn_ref[...], w_ref[...])


def _norm_matmul(x, g, w, tn, tm=PROJ_TOKEN_TILE):
    n, d = x.shape
    wn = w.shape[1]
    tm = _pick_tile(n, tm)
    while wn % tn:
        tn -= 128
    return pl.pallas_call(
        _norm_matmul_kernel,
        out_shape=jax.ShapeDtypeStruct((n, wn), F32),
        grid=(n // tm, wn // tn),
        in_specs=[
            pl.BlockSpec((tm, d), lambda i, j: (i, 0)),
            pl.BlockSpec((1, d), lambda i, j: (0, 0)),
            pl.BlockSpec((d, tn), lambda i, j: (0, j)),
        ],
        out_specs=pl.BlockSpec((tm, tn), lambda i, j: (i, j)),
        scratch_shapes=[pltpu.VMEM((tm, d), BF16)],
        compiler_params=_cparams(("parallel", "arbitrary")),
        name="norm_matmul",
    )(x, g.reshape(1, d), w)


def _out_proj_kernel(x_ref, ya_ref, yb_ref, wa_ref, wb_ref, o_ref):
    o_ref[...] = x_ref[...] + _dot(ya_ref[...], wa_ref[...]) + _dot(yb_ref[...], wb_ref[...])


def _out_proj(x, ya, yb, w, tm=OUT_PROJ_TOKEN_TILE):
    n, d = x.shape
    da = ya.shape[1]
    db = yb.shape[1]
    assert da == db and w.shape[0] == da + db
    tm = _pick_tile(n, tm)
    return pl.pallas_call(
        _out_proj_kernel,
        out_shape=jax.ShapeDtypeStruct((n, d), F32),
        grid=(n // tm,),
        in_specs=[
            pl.BlockSpec((tm, d), lambda i: (i, 0)),
            pl.BlockSpec((tm, da), lambda i: (i, 0)),
            pl.BlockSpec((tm, db), lambda i: (i, 0)),
            pl.BlockSpec((da, d), lambda i: (0, 0)),
            pl.BlockSpec((db, d), lambda i: (1, 0)),
        ],
        out_specs=pl.BlockSpec((tm, d), lambda i: (i, 0)),
        compiler_params=_cparams(("parallel",)),
        name="out_proj",
    )(x, ya, yb, w, w)


def _causal_conv(x, w, b):
    y = x * w[CONV_W - 1:CONV_W] + b
    for k in range(1, CONV_W):
        y = y + _shift_rows(x, k, 0.0) * w[CONV_W - 1 - k:CONV_W - k]
    return y


def _shift_rows(x, k, fill):
    assert 0 < k < SUBLANES
    rolled = pltpu.roll(x, k, axis=0)
    t = lax.broadcasted_iota(jnp.int32, (SUBLANES, x.shape[1]), 0)
    head = jnp.where(t >= k, rolled[:SUBLANES], fill)
    return jnp.concatenate([head, rolled[SUBLANES:]], axis=0)


def _mixer_a_kernel(xr_ref, gate_ref, cw_ref, cb_ref, wr_ref, br_ref, wi_ref, bi_ref,
                    lam_ref, o_ref):
    s = xr_ref.shape[0]
    xc = _causal_conv(xr_ref[...], cw_ref[...], cb_ref[...])
    xb = xc.astype(BF16)
    r = jax.nn.sigmoid(_dot(xb, wr_ref[...]) + br_ref[...])
    i = jax.nn.sigmoid(_dot(xb, wi_ref[...]) + bi_ref[...])
    neg_lam = -lam_ref[...]
    softplus = jnp.maximum(neg_lam, 0.0) + jnp.log1p(jnp.exp(-jnp.abs(neg_lam)))
    log_a = -RG_C * r * softplus
    a = jnp.exp(log_a)
    u = jnp.sqrt(-jnp.tanh(log_a) * (a * a + 1.0)) * (i * xc)
    k = 1
    while k < s:
        if k < SUBLANES:
            u = a * _shift_rows(u, k, 0.0) + u
            a = a * _shift_rows(a, k, 1.0)
        else:
            u = jnp.concatenate([u[:k], a[k:] * u[:s - k] + u[k:]], axis=0)
            a = jnp.concatenate([a[:k], a[k:] * a[:s - k]], axis=0)
        k *= 2
    o_ref[...] = (jax.nn.gelu(gate_ref[...]) * u).astype(o_ref.dtype)


def _mixer_a(proj, conv_w, conv_b, w_r, b_r, w_i, b_i, lam, d_rg, cast_jobs):
    bsz, s, _ = proj.shape
    nb = d_rg // RG_BLOCK
    grid = (bsz, nb)
    vec = lambda v: v.reshape(1, d_rg)
    col = lambda off: pl.BlockSpec((None, s, RG_BLOCK), lambda b, g: (b, 0, off + g))
    row = lambda r: pl.BlockSpec((r, RG_BLOCK), lambda b, g: (0, g))
    blk = pl.BlockSpec((None, RG_BLOCK, RG_BLOCK), lambda b, g: (g, 0, 0))
    return _call_with_jobs(
        _mixer_a_kernel, cast_jobs(grid),
        out_shape=jax.ShapeDtypeStruct((bsz, s, d_rg), BF16),
        grid=grid,
        in_specs=[col(nb), col(0), row(CONV_W), row(1), blk, row(1), blk, row(1), row(1)],
        out_specs=pl.BlockSpec((None, s, RG_BLOCK), lambda b, g: (b, 0, g)),
        semantics=("arbitrary", "arbitrary"),
        name="mixer_a",
        operands=(proj, proj, conv_w, vec(conv_b), w_r.astype(BF16), vec(b_r), w_i.astype(BF16),
                  vec(b_i), vec(lam)))


def _t5_bucket_np(dist):
    max_exact = N_BUCKETS // 2
    d_f = np.maximum(dist, 1).astype(np.float32)
    large = max_exact + (np.log(d_f / np.float32(max_exact)) / np.float32(math.log(MAX_DIST / max_exact))
                         * np.float32(N_BUCKETS - max_exact)).astype(np.int32)
    large = np.minimum(large, N_BUCKETS - 1)
    return np.where(dist < max_exact, dist, large).astype(np.int32)


def _dsw_bias_vector(rel_bias, seq, qb):
    dist = np.arange(seq)
    mult = np.zeros(seq, np.float32)
    for window, dilation in DSW_PATTERNS:
        mult += ((dist % dilation == 0) & (dist <= window)).astype(np.float32)
    logm = np.where(mult > 0, np.log(np.maximum(mult, 1.0)), NEG).astype(np.float32)
    f = (rel_bias.astype(F32)[_t5_bucket_np(dist)] + logm[:, None]) * LOG2E
    n_heads = f.shape[1]
    hv = jnp.concatenate([f[::-1].T, jnp.full((n_heads, qb), NEG, F32)], axis=1)
    return hv[:, None, :]


def _mixer_b_kernel(q_ref, k_ref, v_ref, g_ref, hv_ref, o_ref, kn_ref, vb_ref, tbl_ref):
    s, hd = q_ref.shape
    qb = tbl_ref.shape[0]

    @pl.when(pl.program_id(1) == 0)
    def _():
        skew = pltpu.roll(jnp.broadcast_to(hv_ref[...], (qb, s + qb)), 1, axis=1, stride=1, stride_axis=0)
        tbl_ref[...] = skew[:, qb:]

    kn_ref[...] = _rms_rows(k_ref[...], g_ref[1:2]).astype(BF16)
    vb_ref[...] = v_ref[...].astype(BF16)
    for gi in range(s // qb):
        n_keys = (gi + 1) * qb
        rows = slice(gi * qb, n_keys)
        qn = (_rms_rows(q_ref[rows, :], g_ref[0:1]) * (LOG2E / math.sqrt(hd))).astype(BF16)
        sc = _dot_nt(qn, kn_ref[0:n_keys, :]) + tbl_ref[:, s - n_keys:]
        m = jnp.max(sc, axis=-1, keepdims=True)
        p = jnp.exp2(sc - m)
        l = jnp.sum(p, axis=-1, keepdims=True)
        o = _dot(p.astype(BF16), vb_ref[0:n_keys, :])
        o_ref[rows, :] = (o / l).astype(o_ref.dtype)


def _mixer_b(proj, qk_g, rel_bias, col0, n_heads, cast_jobs, qb=DSW_QUERY_ROWS):
    bsz, s, _ = proj.shape
    qb = _pick_tile(s, qb)
    hv = _dsw_bias_vector(rel_bias, s, qb)
    c0 = col0 // HEAD_B
    grid = (n_heads, bsz)
    seq_spec = lambda off: pl.BlockSpec((None, s, HEAD_B), lambda h, b: (b, 0, off + h))
    return _call_with_jobs(
        _mixer_b_kernel, cast_jobs(grid),
        out_shape=jax.ShapeDtypeStruct((bsz, s, n_heads * HEAD_B), BF16),
        grid=grid,
        in_specs=[
            seq_spec(c0), seq_spec(c0 + n_heads), seq_spec(c0 + 2 * n_heads),
            pl.BlockSpec((2, HEAD_B), lambda h, b: (0, 0)),
            pl.BlockSpec((None, 1, s + qb), lambda h, b: (h, 0, 0)),
        ],
        out_specs=pl.BlockSpec((None, s, HEAD_B), lambda h, b: (b, 0, h)),
        scratch_shapes=[pltpu.VMEM((s, HEAD_B), BF16), pltpu.VMEM((s, HEAD_B), BF16),
                        pltpu.VMEM((qb, s), F32)],
        semantics=("arbitrary", "arbitrary"),
        name="mixer_b",
        operands=(proj, proj, proj, qk_g, hv))


def _mixer_c_kernel(gb_ref, q_ref, k_ref, v_ref, og_ref, cwq_ref, cwk_ref, cbq_ref, cbk_ref,
                    ig_ref, fg_ref, gain_ref, o_ref, qc_ref, kc_ref, c_ref, n_ref, m_ref):
    h = pl.program_id(1)
    L = SEQ_BLOCK
    s, d = q_ref.shape
    nc = s // L

    qc_ref[...] = jax.nn.silu(_causal_conv(q_ref[...], cwq_ref[...], cbq_ref[...]))
    kc_ref[...] = jax.nn.silu(_causal_conv(k_ref[...], cwk_ref[...], cbk_ref[...])) * (d ** -0.5)
    c_ref[...] = jnp.zeros_like(c_ref)
    n_ref[...] = jnp.zeros_like(n_ref)
    m_ref[...] = jnp.zeros_like(m_ref)

    ri = lax.broadcasted_iota(jnp.int32, (L, L), 0)
    ci = lax.broadcasted_iota(jnp.int32, (L, L), 1)
    eye = ri == ci
    causal = ci <= ri

    def to_col(row):
        return jnp.sum(jnp.where(eye, row, 0.0), axis=1, keepdims=True)

    ig_all = ig_ref[...] + gb_ref[0, h]
    f_all = fg_ref[...] + gb_ref[1, h]
    ls_all = jnp.minimum(f_all, 0.0) - jnp.log1p(jnp.exp(-jnp.abs(f_all)))

    for n in range(nc):
        rows = slice(n * L, (n + 1) * L)
        qc = qc_ref[rows, :]
        kc = kc_ref[rows, :]
        qb = qc.astype(BF16)
        kb = kc.astype(BF16)
        vb = v_ref[rows, :].astype(BF16)

        ig_row = ig_all[n:n + 1]
        ls_row = ls_all[n:n + 1]
        ls_col = to_col(ls_row)
        ig_col = to_col(ig_row)
        b_col = jnp.sum(jnp.where(causal, ls_row, 0.0), axis=1, keepdims=True)
        b_row = jnp.sum(jnp.where(ri <= ci, ls_col, 0.0), axis=0, keepdims=True)
        b_last = jnp.sum(ls_row, axis=1, keepdims=True)

        log_d = b_col - b_row + ig_row
        g_row = b_last - b_row + ig_row
        g_col = b_last - b_col + ig_col
        g_max = jnp.max(g_row, axis=1, keepdims=True)
        d_max = jnp.max(jnp.where(causal, log_d, -jnp.inf), axis=1, keepdims=True)
        qk = _dot_nt(qb, kb)

        c_st = c_ref[...]
        n_st = n_ref[...]
        m_st = m_ref[...]

        inter = b_col + m_st
        m_t = jnp.maximum(inter, d_max)
        s_qk = jnp.where(causal, qk * jnp.exp(log_d - m_t), 0.0)
        inter_w = jnp.exp(inter - m_t)
        num = _dot(s_qk.astype(BF16), vb) + inter_w * _dot(qb, c_st.astype(BF16))
        qn = jnp.sum(qc * n_st, axis=1, keepdims=True)
        den = jnp.sum(s_qk, axis=1, keepdims=True) + inter_w * qn
        hout = num / jnp.maximum(jnp.abs(den), jnp.exp(-m_t))

        hn = _rms_rows(hout, gain_ref[...])
        o_ref[rows, :] = (jax.nn.sigmoid(og_ref[rows, :]) * hn).astype(o_ref.dtype)

        wk = jnp.exp(g_col - g_max) * kc
        c_loc = _dot_tn(wk.astype(BF16), vb)
        n_loc = jnp.sum(wk, axis=0, keepdims=True)
        m_new = jnp.maximum(b_last + m_st, g_max)
        decay = jnp.exp(b_last + m_st - m_new)
        fresh = jnp.exp(g_max - m_new)
        c_ref[...] = decay * c_st + fresh * c_loc
        n_ref[...] = decay * n_st + fresh * n_loc
        m_ref[...] = m_new


def _mixer_c(proj, gates, conv_w, conv_b, gate_b, h_gain, n_heads, cast_jobs):
    bsz, s, _ = proj.shape
    d = HEAD_C
    L = SEQ_BLOCK
    nc = s // L
    grid = (bsz, n_heads)
    g_rows = jnp.transpose(gates, (0, 2, 1)).reshape(bsz, 2 * n_heads, nc, L)
    cb = conv_b.reshape(1, -1)
    full = lambda off: pl.BlockSpec((None, s, d), lambda b, h: (b, 0, off + h))
    row = lambda r, off: pl.BlockSpec((r, d), lambda b, h: (0, off + h))
    grow = lambda off: pl.BlockSpec((None, None, nc, L), lambda b, h: (b, off + h, 0, 0))
    return _call_with_jobs(
        _mixer_c_kernel, cast_jobs(grid),
        out_shape=jax.ShapeDtypeStruct((bsz, s, n_heads * d), BF16),
        grid=grid,
        in_specs=[
            pl.BlockSpec(memory_space=pltpu.SMEM),
            full(0), full(n_heads), full(2 * n_heads), full(3 * n_heads),
            row(CONV_W, 0), row(CONV_W, n_heads), row(1, 0), row(1, n_heads),
            grow(0), grow(n_heads), row(1, 0),
        ],
        out_specs=pl.BlockSpec((None, s, d), lambda b, h: (b, 0, h)),
        scratch_shapes=[pltpu.VMEM((s, d), F32), pltpu.VMEM((s, d), F32),
                        pltpu.VMEM((d, d), F32), pltpu.VMEM((1, d), F32), pltpu.VMEM((1, 1), F32)],
        semantics=("arbitrary", "arbitrary"),
        name="mixer_c",
        operands=(gate_b, proj, proj, proj, proj, conv_w, conv_w, cb, cb, g_rows, g_rows,
                  h_gain.reshape(1, -1)))


def _mixer_d_kernel(q_ref, k_ref, v_ref, o_ref, kb_ref, vb_ref, acc_ref, suf_ref):
    s, hd = q_ref.shape
    B = SEQ_BLOCK
    G = min(SB_QUERY_ROWS, s)
    kb_ref[...] = k_ref[...].astype(BF16)
    vb_ref[...] = v_ref[...].astype(BF16)
    ri = lax.broadcasted_iota(jnp.int32, (B, B), 0)
    ci = lax.broadcasted_iota(jnp.int32, (B, B), 1)
    after = (ri > ci).astype(BF16)
    after2 = jnp.concatenate([after, after], axis=0)
    row_g = lax.broadcasted_iota(jnp.int32, (G, B), 0)
    col_g = lax.broadcasted_iota(jnp.int32, (G, B), 1)

    def span(q0, first, last, suffix):
        n = last - first + 1
        keys = slice(first * B, (last + 1) * B)
        qb = (q_ref[q0:q0 + G, :] * (hd ** -0.5)).astype(BF16)
        z = _dot_nt(qb, kb_ref[keys, :])
        sp = jnp.log(1.0 + jnp.exp2(jnp.abs(z) * (-LOG2E)))
        nm = jnp.minimum(z, 0.0)
        log_beta = nm - sp
        log_stay = (nm - z) - sp
        before = [col_g + ((first + j) * B - q0) < row_g if (first + j + 1) * B > q0 else None
                  for j in range(n)]
        later, tot = [], []
        for j in range(n):
            x = log_stay[:, j * B:(j + 1) * B]
            if before[j] is not None:
                x = jnp.where(before[j], x, 0.0)
            hi = lax.bitcast_convert_type(lax.bitcast_convert_type(x, jnp.uint32) & HI_MASK, F32)
            hl = jnp.concatenate([hi.astype(BF16), (x - hi).astype(BF16)], axis=1)
            later.append(_dot(hl, after2))
            tot.append(jnp.sum(x, axis=-1, keepdims=True))
        att = [None] * n
        for j in range(n - 1, -1, -1):
            a = jnp.exp(log_beta[:, j * B:(j + 1) * B] + later[j] + suffix)
            if before[j] is not None:
                a = jnp.where(before[j], a, 0.0)
            att[j] = a.astype(BF16)
            suffix = suffix + tot[j]
        att = att[0] if n == 1 else jnp.concatenate(att, axis=1)
        return _dot(att, vb_ref[keys, :]), suffix

    def first_near(q0):
        return max(q0 // B - SB_NEAR_BLOCKS + 1, 0)

    worst = None
    for q0 in range(0, s, G):
        rows = slice(q0, q0 + G)
        acc, suffix = span(q0, first_near(q0), (q0 + G) // B - 1, jnp.zeros((G, 1), F32))
        o_ref[rows, :] = acc.astype(o_ref.dtype)
        if first_near(q0) > 0:
            acc_ref[rows, :] = acc
            suf_ref[rows, :] = suffix
            worst = suffix if worst is None else jnp.maximum(worst, suffix)

    if worst is not None:
        @pl.when(jnp.max(worst) > SB_UNDERFLOW)
        def _():
            for q0 in range(0, s, G):
                if first_near(q0) == 0:
                    continue
                rows = slice(q0, q0 + G)
                suffix = suf_ref[rows, :]

                @pl.when(jnp.max(suffix) > SB_UNDERFLOW)
                def _():
                    far, _ = span(q0, 0, first_near(q0) - 1, suffix)
                    o_ref[rows, :] = (acc_ref[rows, :] + far).astype(o_ref.dtype)


def _mixer_d(proj, col0, n_heads, cast_jobs):
    bsz, s, _ = proj.shape
    c0 = col0 // HEAD_D
    grid = (bsz, n_heads)
    seq_spec = lambda off: pl.BlockSpec((None, s, HEAD_D), lambda b, h: (b, 0, off + h))
    return _call_with_jobs(
        _mixer_d_kernel, cast_jobs(grid),
        out_shape=jax.ShapeDtypeStruct((bsz, s, n_heads * HEAD_D), BF16),
        grid=grid,
        in_specs=[seq_spec(c0), seq_spec(c0 + n_heads), seq_spec(c0 + 2 * n_heads)],
        out_specs=pl.BlockSpec((None, s, HEAD_D), lambda b, h: (b, 0, h)),
        scratch_shapes=[pltpu.VMEM((s, HEAD_D), BF16), pltpu.VMEM((s, HEAD_D), BF16),
                        pltpu.VMEM((s, HEAD_D), F32), pltpu.VMEM((s, 1), F32)],
        semantics=("arbitrary", "arbitrary"),
        name="mixer_d",
        operands=(proj, proj, proj))


def _layer_ab(x, bsz, s, g, w_in, conv_w, conv_b, w_r, b_r, w_i, b_i, lam, qk_g, rel_bias, w_out,
              jobs):
    d_rg = conv_w.shape[-1]
    n_heads = rel_bias.shape[-1]
    proj = _norm_matmul(x, g, w_in.astype(BF16), tn=PROJ_COL_TILE_AB).reshape(bsz, s, -1)
    ya, out_a = _mixer_a(proj, conv_w, conv_b, w_r, b_r, w_i, b_i, lam, d_rg, jobs[0])
    yb, out_b = _mixer_b(proj, qk_g, rel_bias, 2 * d_rg, n_heads, jobs[1])
    n = bsz * s
    return _out_proj(x, ya.reshape(n, -1), yb.reshape(n, -1), w_out.astype(BF16)), out_a + out_b


def _cd_layout(gate_b, w_out):
    n_heads_c = gate_b.shape[-1]
    dc = n_heads_c * HEAD_C
    dd = w_out.shape[0] - dc
    n_gate = 2 * n_heads_c
    main = 4 * dc + 3 * dd
    width = -(-(main + n_gate) // CD_WIDTH_MULTIPLE) * CD_WIDTH_MULTIPLE
    return n_heads_c, dc, dd, n_gate, main, width


def _cd_weight_job(w_in, j, gate_col, n_gate, width, tw=CD_WEIGHT_COL_TILE):
    _, d, cols = w_in.shape
    main = cols - n_gate
    assert gate_col % tw == 0 and (main - gate_col) % tw == 0 and width % tw == 0 and n_gate <= tw
    n_aligned = gate_col // tw
    n_shifted = (main - gate_col) // tw
    n_blk = width // tw
    last = (cols - 1) // tw

    def job(grid):
        assert n_blk <= math.prod(grid)
        blk = lambda *idx: jnp.minimum(_linear_step(grid, idx), n_blk - 1)

        def reorder(t, a_ref, b_ref, o_ref):
            lane = lax.broadcasted_iota(jnp.int32, o_ref.shape, 1)

            @pl.when(t < n_aligned)
            def _():
                o_ref[...] = a_ref[...].astype(o_ref.dtype)

            @pl.when((t >= n_aligned) & (t < n_aligned + n_shifted))
            def _():
                a = pltpu.roll(a_ref[...], tw - n_gate, axis=1)
                b = pltpu.roll(b_ref[...], tw - n_gate, axis=1)
                o_ref[...] = jnp.where(lane < tw - n_gate, a, b).astype(o_ref.dtype)

            @pl.when(t == n_aligned + n_shifted)
            def _():
                o_ref[...] = jnp.where(lane < n_gate, a_ref[...], 0.0).astype(o_ref.dtype)

            @pl.when((t > n_aligned + n_shifted) & (t < n_blk))
            def _():
                o_ref[...] = jnp.zeros_like(o_ref)

        a_blk = lambda *idx: jnp.where(blk(*idx) < n_aligned + n_shifted, blk(*idx), n_aligned)
        return [([w_in, w_in],
                 [pl.BlockSpec((None, d, tw), lambda *idx: (j, 0, a_blk(*idx))),
                  pl.BlockSpec((None, d, tw), lambda *idx: (j, 0, jnp.minimum(blk(*idx) + 1, last)))],
                 pl.BlockSpec((d, tw), lambda *idx: (0, blk(*idx))),
                 jax.ShapeDtypeStruct((d, width), BF16),
                 reorder)]
    return job


def _layer_cd(x, bsz, s, g, w, conv_w, conv_b, gate_b, h_gain, w_out, jobs):
    n_heads_c, dc, dd, n_gate, main, width = _cd_layout(gate_b, w_out)
    n_heads_d = dd // HEAD_D
    proj = _norm_matmul(x, g, w, tn=PROJ_COL_TILE_CD).reshape(bsz, s, width)
    gates = proj[:, :, main:main + n_gate]
    yd, out_d = _mixer_d(proj, 4 * dc, n_heads_d, jobs[0])
    yc, out_c = _mixer_c(proj, gates, conv_w, conv_b, gate_b, h_gain, n_heads_c, jobs[1])
    n = bsz * s
    return _out_proj(x, yc.reshape(n, -1), yd.reshape(n, -1), w_out.astype(BF16)), out_d + out_c


def kernel(x, norm_g, ffn_w_in, ffn_w_out, ab_w_in, ab_conv_w, ab_conv_b, rg_w_r, rg_b_r, rg_w_i,
           rg_b_i, rg_lambda, qk_gain, rel_bias, ab_w_out, cd_w_in, cd_conv_w, cd_conv_b,
           mlstm_gate_bias, mlstm_h_gain, cd_w_out):
    bsz, s, d = x.shape
    depth = norm_g.shape[0]
    h = x.reshape(bsz * s, d)
    tf = _pick_tile(ffn_w_out.shape[2], FFN_HIDDEN_TILE)
    ffn_w = {(0, 0): (_pack_gate_up(ffn_w_in[0, 0].astype(BF16), tf), ffn_w_out[0, 0].astype(BF16))}
    cd_w = None
    for layer in range(depth):
        j = layer // 2
        h = _ffn(h, norm_g[layer, 0], *ffn_w[(layer, 0)])
        targets = [(layer, 1)] + ([(layer + 1, 0)] if layer + 1 < depth else [])
        jobs = [functools.partial(_ffn_cast_jobs, ffn_w_in, ffn_w_out, *t) for t in targets]
        jobs += [lambda grid: []] * (2 - len(jobs))
        if layer % 2 == 0:
            if layer + 1 < depth:
                _, dc, _, n_gate, _, width = _cd_layout(mlstm_gate_bias[j], cd_w_out[j])
                ffn_jobs, cd_job = jobs[0], _cd_weight_job(cd_w_in, j, 4 * dc, n_gate, width)
                jobs[0] = lambda grid: ffn_jobs(grid) + cd_job(grid)
            h, out = _layer_ab(h, bsz, s, norm_g[layer, 1], ab_w_in[j], ab_conv_w[j], ab_conv_b[j],
                               rg_w_r[j], rg_b_r[j], rg_w_i[j], rg_b_i[j], rg_lambda[j], qk_gain[j],
                               rel_bias, ab_w_out[j], jobs)
            if layer + 1 < depth:
                cd_w = out.pop(2)
        else:
            h, out = _layer_cd(h, bsz, s, norm_g[layer, 1], cd_w, cd_conv_w[j], cd_conv_b[j],
                               mlstm_gate_bias[j], mlstm_h_gain[j], cd_w_out[j], jobs)
        for i, t in enumerate(targets):
            ffn_w[t] = (out[2 * i], out[2 * i + 1])
        h = _ffn(h, norm_g[layer, 2], *ffn_w[(layer, 1)])
    return h.reshape(bsz, s, d)
```

```python
import functools
import math

import numpy as np
import jax
import jax.numpy as jnp
from jax import lax
from jax.experimental import pallas as pl
from jax.experimental.pallas import tpu as pltpu

F32 = jnp.float32
BF16 = jnp.bfloat16

RMS_EPS = 1e-6
RG_C = 8.0
CONV_W = 4
RG_BLOCK = 128
HEAD_B = 128
HEAD_C = 256
HEAD_D = 128
SEQ_BLOCK = 128
SUBLANES = 8
BF16_SUBLANES = 16
DSW_PATTERNS = ((128, 1), (512, 4), (2048, 16))
N_BUCKETS = 32
MAX_DIST = 2048
NEG = -1e30
LOG2E = 1.4426950408889634
HI_MASK = np.uint32(0xFFFF0000)
SB_NEAR_BLOCKS = 3
SB_QUERY_ROWS = 256
SB_UNDERFLOW = -104.0
DSW_QUERY_ROWS = 256

FFN_TOKEN_TILE = 1024
FFN_HIDDEN_TILE = 512
PROJ_TOKEN_TILE = 1024
PROJ_COL_TILE_AB = 1280
PROJ_COL_TILE_CD = 1536
OUT_PROJ_TOKEN_TILE = 512
CD_WIDTH_MULTIPLE = 512
CD_WEIGHT_COL_TILE = 256
V7X_VMEM_BYTES = 64 * 1024 * 1024
VMEM_LIMIT = V7X_VMEM_BYTES * 7 // 8


def _cparams(sem):
    return pltpu.CompilerParams(dimension_semantics=sem, vmem_limit_bytes=VMEM_LIMIT)


def _rms_rows(x, g):
    return x * lax.rsqrt(jnp.mean(x * x, axis=-1, keepdims=True) + RMS_EPS) * g


def _dot(a, b):
    return jnp.dot(a, b, preferred_element_type=F32)


def _dot_nt(a, b):
    return lax.dot_general(a, b, (((1,), (1,)), ((), ())), preferred_element_type=F32)


def _dot_tn(a, b):
    return lax.dot_general(a, b, (((0,), (0,)), ((), ())), preferred_element_type=F32)


def _pick_tile(n, pref):
    t = min(pref, n)
    while n % t:
        t //= 2
    return t


def _linear_step(grid, idx):
    t = idx[0]
    for i, g in zip(idx[1:], grid[1:]):
        t = t * g + i
    return t


def _ffn_cast_jobs(w_in, w_out, layer, half, grid):
    n_steps = math.prod(grid)
    jobs = []
    for w in (w_in, w_out):
        rows, cols = w.shape[2:]
        n = n_steps
        while rows % n or (rows // n) % BF16_SUBLANES:
            n -= 1
        slab = rows // n
        blk = lambda *idx, n=n: jnp.minimum(_linear_step(grid, idx), n - 1)

        def cast(t, src_ref, dst_ref, n=n):
            @pl.when(t < n)
            def _():
                dst_ref[...] = src_ref[...].astype(dst_ref.dtype)

        jobs.append((
            [w],
            [pl.BlockSpec((None, None, slab, cols), lambda *idx, blk=blk: (layer, half, blk(*idx), 0))],
            pl.BlockSpec((slab, cols), lambda *idx, blk=blk: (blk(*idx), 0)),
            jax.ShapeDtypeStruct((rows, cols), BF16),
            cast,
        ))
    return jobs


def _call_with_jobs(body, jobs, *, out_shape, grid, in_specs, out_specs, scratch_shapes=(),
                    semantics, name, operands):
    n_in = len(in_specs)
    n_job_in = [len(j[0]) for j in jobs]
    n_extra = sum(n_job_in)

    def kernel(*refs):
        t = _linear_step(grid, [pl.program_id(ax) for ax in range(len(grid))])
        at = n_in
        for i, j in enumerate(jobs):
            j[4](t, *refs[at:at + n_job_in[i]], refs[n_in + n_extra + 1 + i])
            at += n_job_in[i]
        body(*refs[:n_in], refs[n_in + n_extra], *refs[n_in + n_extra + 1 + len(jobs):])

    outs = pl.pallas_call(
        kernel,
        out_shape=[out_shape] + [j[3] for j in jobs],
        grid=grid,
        in_specs=list(in_specs) + [sp for j in jobs for sp in j[1]],
        out_specs=[out_specs] + [j[2] for j in jobs],
        scratch_shapes=list(scratch_shapes),
        compiler_params=_cparams(semantics),
        name=name,
    )(*operands, *[op for j in jobs for op in j[0]])
    return outs[0], list(outs[1:])


def _ffn_kernel(x_ref, g_ref, wg_ref, wu_ref, wo_ref, o_ref, xn_ref):
    j = pl.program_id(1)

    @pl.when(j == 0)
    def _():
        x = x_ref[...]
        xn_ref[...] = _rms_rows(x, g_ref[...]).astype(BF16)
        o_ref[...] = x

    xn = xn_ref[...]
    hg = _dot(xn, wg_ref[...])
    hu = _dot(xn, wu_ref[...])
    h = (0.5 * hg * jax.nn.sigmoid(hg) * hu).astype(BF16)
    o_ref[...] += _dot(h, wo_ref[...])


def _ffn(x, g, w_in, w_out, tm=FFN_TOKEN_TILE, tf=FFN_HIDDEN_TILE):
    n, d = x.shape
    d_ff = w_out.shape[0]
    tm = _pick_tile(n, tm)
    tf = _pick_tile(d_ff, tf)
    nj = d_ff // tf
    return pl.pallas_call(
        _ffn_kernel,
        out_shape=jax.ShapeDtypeStruct((n, d), F32),
        grid=(n // tm, nj),
        in_specs=[
            pl.BlockSpec((tm, d), lambda i, j: (i, 0)),
            pl.BlockSpec((1, d), lambda i, j: (0, 0)),
            pl.BlockSpec((d, tf), lambda i, j: (0, j)),
            pl.BlockSpec((d, tf), lambda i, j: (0, j + nj)),
            pl.BlockSpec((tf, d), lambda i, j: (j, 0)),
        ],
        out_specs=pl.BlockSpec((tm, d), lambda i, j: (i, 0)),
        scratch_shapes=[pltpu.VMEM((tm, d), BF16)],
        compiler_params=_cparams(("parallel", "arbitrary")),
        name="ffn",
    )(x, g.reshape(1, d), w_in, w_in, w_out)


def _norm_matmul_kernel(x_ref, g_ref, w_ref, o_ref, xn_ref):
    @pl.when(pl.program_id(1) == 0)
    def _():
        xn_ref[...] = _rms_rows(x_ref[...], g_ref[...]).astype(BF16)

    o_ref[...] = _dot(xn_ref[...], w_ref[...])


def _norm_matmul(x, g, w, tn, tm=PROJ_TOKEN_TILE):
    n, d = x.shape
    wn = w.shape[1]
    tm = _pick_tile(n, tm)
    while wn % tn:
        tn -= 128
    return pl.pallas_call(
        _norm_matmul_kernel,
        out_shape=jax.ShapeDtypeStruct((n, wn), F32),
        grid=(n // tm, wn // tn),
        in_specs=[
            pl.BlockSpec((tm, d), lambda i, j: (i, 0)),
            pl.BlockSpec((1, d), lambda i, j: (0, 0)),
            pl.BlockSpec((d, tn), lambda i, j: (0, j)),
        ],
        out_specs=pl.BlockSpec((tm, tn), lambda i, j: (i, j)),
        scratch_shapes=[pltpu.VMEM((tm, d), BF16)],
        compiler_params=_cparams(("parallel", "arbitrary")),
        name="norm_matmul",
    )(x, g.reshape(1, d), w)


def _out_proj_kernel(x_ref, ya_ref, yb_ref, wa_ref, wb_ref, o_ref):
    o_ref[...] = x_ref[...] + _dot(ya_ref[...], wa_ref[...]) + _dot(yb_ref[...], wb_ref[...])


def _out_proj(x, ya, yb, w, tm=OUT_PROJ_TOKEN_TILE):
    n, d = x.shape
    da = ya.shape[1]
    db = yb.shape[1]
    assert da == db and w.shape[0] == da + db
    tm = _pick_tile(n, tm)
    return pl.pallas_call(
        _out_proj_kernel,
        out_shape=jax.ShapeDtypeStruct((n, d), F32),
        grid=(n // tm,),
        in_specs=[
            pl.BlockSpec((tm, d), lambda i: (i, 0)),
            pl.BlockSpec((tm, da), lambda i: (i, 0)),
            pl.BlockSpec((tm, db), lambda i: (i, 0)),
            pl.BlockSpec((da, d), lambda i: (0, 0)),
            pl.BlockSpec((db, d), lambda i: (1, 0)),
        ],
        out_specs=pl.BlockSpec((tm, d), lambda i: (i, 0)),
        compiler_params=_cparams(("parallel",)),
        name="out_proj",
    )(x, ya, yb, w, w)


def _causal_conv(x, w, b):
    y = x * w[CONV_W - 1:CONV_W] + b
    for k in range(1, CONV_W):
        y = y + _shift_rows(x, k, 0.0) * w[CONV_W - 1 - k:CONV_W - k]
    return y


def _shift_rows(x, k, fill):
    assert 0 < k < SUBLANES
    rolled = pltpu.roll(x, k, axis=0)
    t = lax.broadcasted_iota(jnp.int32, (SUBLANES, x.shape[1]), 0)
    head = jnp.where(t >= k, rolled[:SUBLANES], fill)
    return jnp.concatenate([head, rolled[SUBLANES:]], axis=0)


def _mixer_a_kernel(xr_ref, gate_ref, cw_ref, cb_ref, wr_ref, br_ref, wi_ref, bi_ref,
                    lam_ref, o_ref):
    s = xr_ref.shape[0]
    xc = _causal_conv(xr_ref[...], cw_ref[...], cb_ref[...])
    xb = xc.astype(BF16)
    r = jax.nn.sigmoid(_dot(xb, wr_ref[...]) + br_ref[...])
    i = jax.nn.sigmoid(_dot(xb, wi_ref[...]) + bi_ref[...])
    neg_lam = -lam_ref[...]
    softplus = jnp.maximum(neg_lam, 0.0) + jnp.log1p(jnp.exp(-jnp.abs(neg_lam)))
    log_a = -RG_C * r * softplus
    a = jnp.exp(log_a)
    u = jnp.sqrt(-jnp.tanh(log_a) * (a * a + 1.0)) * (i * xc)
    k = 1
    while k < s:
        if k < SUBLANES:
            u = a * _shift_rows(u, k, 0.0) + u
            a = a * _shift_rows(a, k, 1.0)
        else:
            u = jnp.concatenate([u[:k], a[k:] * u[:s - k] + u[k:]], axis=0)
            a = jnp.concatenate([a[:k], a[k:] * a[:s - k]], axis=0)
        k *= 2
    o_ref[...] = (jax.nn.gelu(gate_ref[...]) * u).astype(o_ref.dtype)


def _mixer_a(proj, conv_w, conv_b, w_r, b_r, w_i, b_i, lam, d_rg, cast_jobs):
    bsz, s, _ = proj.shape
    nb = d_rg // RG_BLOCK
    grid = (bsz, nb)
    vec = lambda v: v.reshape(1, d_rg)
    col = lambda off: pl.BlockSpec((None, s, RG_BLOCK), lambda b, g: (b, 0, off + g))
    row = lambda r: pl.BlockSpec((r, RG_BLOCK), lambda b, g: (0, g))
    blk = pl.BlockSpec((None, RG_BLOCK, RG_BLOCK), lambda b, g: (g, 0, 0))
    return _call_with_jobs(
        _mixer_a_kernel, cast_jobs(grid),
        out_shape=jax.ShapeDtypeStruct((bsz, s, d_rg), BF16),
        grid=grid,
        in_specs=[col(nb), col(0), row(CONV_W), row(1), blk, row(1), blk, row(1), row(1)],
        out_specs=pl.BlockSpec((None, s, RG_BLOCK), lambda b, g: (b, 0, g)),
        semantics=("arbitrary", "arbitrary"),
        name="mixer_a",
        operands=(proj, proj, conv_w, vec(conv_b), w_r.astype(BF16), vec(b_r), w_i.astype(BF16),
                  vec(b_i), vec(lam)))


def _t5_bucket_np(dist):
    max_exact = N_BUCKETS // 2
    d_f = np.maximum(dist, 1).astype(np.float32)
    large = max_exact + (np.log(d_f / np.float32(max_exact)) / np.float32(math.log(MAX_DIST / max_exact))
                         * np.float32(N_BUCKETS - max_exact)).astype(np.int32)
    large = np.minimum(large, N_BUCKETS - 1)
    return np.where(dist < max_exact, dist, large).astype(np.int32)


def _dsw_bias_vector(rel_bias, seq, qb):
    dist = np.arange(seq)
    mult = np.zeros(seq, np.float32)
    for window, dilation in DSW_PATTERNS:
        mult += ((dist % dilation == 0) & (dist <= window)).astype(np.float32)
    logm = np.where(mult > 0, np.log(np.maximum(mult, 1.0)), NEG).astype(np.float32)
    f = (rel_bias.astype(F32)[_t5_bucket_np(dist)] + logm[:, None]) * LOG2E
    n_heads = f.shape[1]
    hv = jnp.concatenate([f[::-1].T, jnp.full((n_heads, qb), NEG, F32)], axis=1)
    return hv[:, None, :]


def _mixer_b_kernel(q_ref, k_ref, v_ref, g_ref, hv_ref, o_ref, kn_ref, vb_ref, tbl_ref):
    s, hd = q_ref.shape
    qb = tbl_ref.shape[0]

    @pl.when(pl.program_id(1) == 0)
    def _():
        skew = pltpu.roll(jnp.broadcast_to(hv_ref[...], (qb, s + qb)), 1, axis=1, stride=1, stride_axis=0)
        tbl_ref[...] = skew[:, qb:]

    kn_ref[...] = _rms_rows(k_ref[...], g_ref[1:2]).astype(BF16)
    vb_ref[...] = v_ref[...].astype(BF16)
    def logits(gi):
        n_keys = (gi + 1) * qb
        qn = (_rms_rows(q_ref[gi * qb:n_keys, :], g_ref[0:1]) * (LOG2E / math.sqrt(hd))).astype(BF16)
        return _dot_nt(qn, kn_ref[0:n_keys, :]) + tbl_ref[:, s - n_keys:]

    def weights(sc):
        m = jnp.max(sc, axis=-1, keepdims=True)
        p = jnp.exp2(sc - m)
        return p.astype(BF16), jnp.sum(p, axis=-1, keepdims=True)

    def output(gi, p, l):
        n_keys = (gi + 1) * qb
        o = _dot(p, vb_ref[0:n_keys, :])
        o_ref[gi * qb:n_keys, :] = (o / l).astype(o_ref.dtype)

    n_groups = s // qb
    held = None
    for gi in range(n_groups):
        cur = weights(logits(gi))
        if held is not None:
            output(gi - 1, *held)
        held = cur
    output(n_groups - 1, *held)


def _mixer_b(proj, qk_g, rel_bias, col0, n_heads, cast_jobs, qb=DSW_QUERY_ROWS):
    bsz, s, _ = proj.shape
    qb = _pick_tile(s, qb)
    hv = _dsw_bias_vector(rel_bias, s, qb)
    c0 = col0 // HEAD_B
    grid = (n_heads, bsz)
    seq_spec = lambda off: pl.BlockSpec((None, s, HEAD_B), lambda h, b: (b, 0, off + h))
    return _call_with_jobs(
        _mixer_b_kernel, cast_jobs(grid),
        out_shape=jax.ShapeDtypeStruct((bsz, s, n_heads * HEAD_B), BF16),
        grid=grid,
        in_specs=[
            seq_spec(c0), seq_spec(c0 + n_heads), seq_spec(c0 + 2 * n_heads),
            pl.BlockSpec((2, HEAD_B), lambda h, b: (0, 0)),
            pl.BlockSpec((None, 1, s + qb), lambda h, b: (h, 0, 0)),
        ],
        out_specs=pl.BlockSpec((None, s, HEAD_B), lambda h, b: (b, 0, h)),
        scratch_shapes=[pltpu.VMEM((s, HEAD_B), BF16), pltpu.VMEM((s, HEAD_B), BF16),
                        pltpu.VMEM((qb, s), F32)],
        semantics=("arbitrary", "arbitrary"),
        name="mixer_b",
        operands=(proj, proj, proj, qk_g, hv))


def _mixer_c_kernel(gb_ref, q_ref, k_ref, v_ref, og_ref, cwq_ref, cwk_ref, cbq_ref, cbk_ref,
                    ig_ref, fg_ref, gain_ref, o_ref, qc_ref, kc_ref, c_ref, n_ref, m_ref):
    h = pl.program_id(1)
    L = SEQ_BLOCK
    s, d = q_ref.shape
    nc = s // L

    qc_ref[...] = jax.nn.silu(_causal_conv(q_ref[...], cwq_ref[...], cbq_ref[...]))
    kc_ref[...] = jax.nn.silu(_causal_conv(k_ref[...], cwk_ref[...], cbk_ref[...])) * (d ** -0.5)
    c_ref[...] = jnp.zeros_like(c_ref)
    n_ref[...] = jnp.zeros_like(n_ref)
    m_ref[...] = jnp.zeros_like(m_ref)

    ri = lax.broadcasted_iota(jnp.int32, (L, L), 0)
    ci = lax.broadcasted_iota(jnp.int32, (L, L), 1)
    eye = ri == ci
    causal = ci <= ri

    def to_col(row):
        return jnp.sum(jnp.where(eye, row, 0.0), axis=1, keepdims=True)

    ig_all = ig_ref[...] + gb_ref[0, h]
    f_all = fg_ref[...] + gb_ref[1, h]
    ls_all = jnp.minimum(f_all, 0.0) - jnp.log1p(jnp.exp(-jnp.abs(f_all)))

    def chunk_local(n):
        rows = slice(n * L, (n + 1) * L)
        qc = qc_ref[rows, :]
        kc = kc_ref[rows, :]
        qb = qc.astype(BF16)
        kb = kc.astype(BF16)
        vb = v_ref[rows, :].astype(BF16)

        ig_row = ig_all[n:n + 1]
        ls_row = ls_all[n:n + 1]
        ls_col = to_col(ls_row)
        ig_col = to_col(ig_row)
        b_col = jnp.sum(jnp.where(causal, ls_row, 0.0), axis=1, keepdims=True)
        b_row = jnp.sum(jnp.where(ri <= ci, ls_col, 0.0), axis=0, keepdims=True)
        b_last = jnp.sum(ls_row, axis=1, keepdims=True)

        log_d = b_col - b_row + ig_row
        g_row = b_last - b_row + ig_row
        g_col = b_last - b_col + ig_col
        g_max = jnp.max(g_row, axis=1, keepdims=True)
        d_max = jnp.max(jnp.where(causal, log_d, -jnp.inf), axis=1, keepdims=True)
        qk = _dot_nt(qb, kb)
        wk = jnp.exp(g_col - g_max) * kc
        c_loc = _dot_tn(wk.astype(BF16), vb)
        n_loc = jnp.sum(wk, axis=0, keepdims=True)
        return rows, qc, qb, vb, b_col, b_last, log_d, g_max, d_max, qk, c_loc, n_loc

    local = chunk_local(0)
    for n in range(nc):
        rows, qc, qb, vb, b_col, b_last, log_d, g_max, d_max, qk, c_loc, n_loc = local
        if n + 1 < nc:
            local = chunk_local(n + 1)

        c_st = c_ref[...]
        n_st = n_ref[...]
        m_st = m_ref[...]

        inter = b_col + m_st
        m_t = jnp.maximum(inter, d_max)
        s_qk = jnp.where(causal, qk * jnp.exp(log_d - m_t), 0.0)
        inter_w = jnp.exp(inter - m_t)
        num = _dot(s_qk.astype(BF16), vb) + inter_w * _dot(qb, c_st.astype(BF16))
        qn = jnp.sum(qc * n_st, axis=1, keepdims=True)
        den = jnp.sum(s_qk, axis=1, keepdims=True) + inter_w * qn
        hout = num / jnp.maximum(jnp.abs(den), jnp.exp(-m_t))

        hn = _rms_rows(hout, gain_ref[...])
        o_ref[rows, :] = (jax.nn.sigmoid(og_ref[rows, :]) * hn).astype(o_ref.dtype)

        m_new = jnp.maximum(b_last + m_st, g_max)
        decay = jnp.exp(b_last + m_st - m_new)
        fresh = jnp.exp(g_max - m_new)
        c_ref[...] = decay * c_st + fresh * c_loc
        n_ref[...] = decay * n_st + fresh * n_loc
        m_ref[...] = m_new


def _mixer_c(proj, gates, conv_w, conv_b, gate_b, h_gain, n_heads, cast_jobs):
    bsz, s, _ = proj.shape
    d = HEAD_C
    L = SEQ_BLOCK
    nc = s // L
    grid = (bsz, n_heads)
    g_rows = jnp.transpose(gates, (0, 2, 1)).reshape(bsz, 2 * n_heads, nc, L)
    cb = conv_b.reshape(1, -1)
    full = lambda off: pl.BlockSpec((None, s, d), lambda b, h: (b, 0, off + h))
    row = lambda r, off: pl.BlockSpec((r, d), lambda b, h: (0, off + h))
    grow = lambda off: pl.BlockSpec((None, None, nc, L), lambda b, h: (b, off + h, 0, 0))
    return _call_with_jobs(
        _mixer_c_kernel, cast_jobs(grid),
        out_shape=jax.ShapeDtypeStruct((bsz, s, n_heads * d), BF16),
        grid=grid,
        in_specs=[
            pl.BlockSpec(memory_space=pltpu.SMEM),
            full(0), full(n_heads), full(2 * n_heads), full(3 * n_heads),
            row(CONV_W, 0), row(CONV_W, n_heads), row(1, 0), row(1, n_heads),
            grow(0), grow(n_heads), row(1, 0),
        ],
        out_specs=pl.BlockSpec((None, s, d), lambda b, h: (b, 0, h)),
        scratch_shapes=[pltpu.VMEM((s, d), F32), pltpu.VMEM((s, d), F32),
                        pltpu.VMEM((d, d), F32), pltpu.VMEM((1, d), F32), pltpu.VMEM((1, 1), F32)],
        semantics=("arbitrary", "arbitrary"),
        name="mixer_c",
        operands=(gate_b, proj, proj, proj, proj, conv_w, conv_w, cb, cb, g_rows, g_rows,
                  h_gain.reshape(1, -1)))


def _mixer_d_kernel(q_ref, k_ref, v_ref, o_ref, kb_ref, vb_ref, acc_ref, suf_ref):
    s, hd = q_ref.shape
    B = SEQ_BLOCK
    G = min(SB_QUERY_ROWS, s)
    kb_ref[...] = k_ref[...].astype(BF16)
    vb_ref[...] = v_ref[...].astype(BF16)
    ri = lax.broadcasted_iota(jnp.int32, (B, B), 0)
    ci = lax.broadcasted_iota(jnp.int32, (B, B), 1)
    after = (ri > ci).astype(BF16)
    after2 = jnp.concatenate([after, after], axis=0)
    row_g = lax.broadcasted_iota(jnp.int32, (G, B), 0)
    col_g = lax.broadcasted_iota(jnp.int32, (G, B), 1)

    def span(q0, first, last, suffix):
        n = last - first + 1
        keys = slice(first * B, (last + 1) * B)
        qb = (q_ref[q0:q0 + G, :] * (hd ** -0.5)).astype(BF16)
        z = _dot_nt(qb, kb_ref[keys, :])
        sp = jnp.log(1.0 + jnp.exp2(jnp.abs(z) * (-LOG2E)))
        nm = jnp.minimum(z, 0.0)
        log_beta = nm - sp
        log_stay = (nm - z) - sp
        before = [col_g + ((first + j) * B - q0) < row_g if (first + j + 1) * B > q0 else None
                  for j in range(n)]
        later, tot = [], []
        for j in range(n):
            x = log_stay[:, j * B:(j + 1) * B]
            if before[j] is not None:
                x = jnp.where(before[j], x, 0.0)
            hi = lax.bitcast_convert_type(lax.bitcast_convert_type(x, jnp.uint32) & HI_MASK, F32)
            hl = jnp.concatenate([hi.astype(BF16), (x - hi).astype(BF16)], axis=1)
            later.append(_dot(hl, after2))
            tot.append(jnp.sum(x, axis=-1, keepdims=True))
        att = [None] * n
        for j in range(n - 1, -1, -1):
            a = jnp.exp(log_beta[:, j * B:(j + 1) * B] + later[j] + suffix)
            if before[j] is not None:
                a = jnp.where(before[j], a, 0.0)
            att[j] = a.astype(BF16)
            suffix = suffix + tot[j]
        att = att[0] if n == 1 else jnp.concatenate(att, axis=1)
        return _dot(att, vb_ref[keys, :]), suffix

    def first_near(q0):
        return max(q0 // B - SB_NEAR_BLOCKS + 1, 0)

    worst = None
    for q0 in range(0, s, G):
        rows = slice(q0, q0 + G)
        acc, suffix = span(q0, first_near(q0), (q0 + G) // B - 1, jnp.zeros((G, 1), F32))
        o_ref[rows, :] = acc.astype(o_ref.dtype)
        if first_near(q0) > 0:
            acc_ref[rows, :] = acc
            suf_ref[rows, :] = suffix
            worst = suffix if worst is None else jnp.maximum(worst, suffix)

    if worst is not None:
        @pl.when(jnp.max(worst) > SB_UNDERFLOW)
        def _():
            for q0 in range(0, s, G):
                if first_near(q0) == 0:
                    continue
                rows = slice(q0, q0 + G)
                suffix = suf_ref[rows, :]

                @pl.when(jnp.max(suffix) > SB_UNDERFLOW)
                def _():
                    far, _ = span(q0, 0, first_near(q0) - 1, suffix)
                    o_ref[rows, :] = (acc_ref[rows, :] + far).astype(o_ref.dtype)


def _mixer_d(proj, col0, n_heads, cast_jobs):
    bsz, s, _ = proj.shape
    c0 = col0 // HEAD_D
    grid = (bsz, n_heads)
    seq_spec = lambda off: pl.BlockSpec((None, s, HEAD_D), lambda b, h: (b, 0, off + h))
    return _call_with_jobs(
        _mixer_d_kernel, cast_jobs(grid),
        out_shape=jax.ShapeDtypeStruct((bsz, s, n_heads * HEAD_D), BF16),
        grid=grid,
        in_specs=[seq_spec(c0), seq_spec(c0 + n_heads), seq_spec(c0 + 2 * n_heads)],
        out_specs=pl.BlockSpec((None, s, HEAD_D), lambda b, h: (b, 0, h)),
        scratch_shapes=[pltpu.VMEM((s, HEAD_D), BF16), pltpu.VMEM((s, HEAD_D), BF16),
                        pltpu.VMEM((s, HEAD_D), F32), pltpu.VMEM((s, 1), F32)],
        semantics=("arbitrary", "arbitrary"),
        name="mixer_d",
        operands=(proj, proj, proj))


def _layer_ab(x, bsz, s, g, w_in, conv_w, conv_b, w_r, b_r, w_i, b_i, lam, qk_g, rel_bias, w_out,
              jobs):
    d_rg = conv_w.shape[-1]
    n_heads = rel_bias.shape[-1]
    proj = _norm_matmul(x, g, w_in.astype(BF16), tn=PROJ_COL_TILE_AB).reshape(bsz, s, -1)
    ya, out_a = _mixer_a(proj, conv_w, conv_b, w_r, b_r, w_i, b_i, lam, d_rg, jobs[0])
    yb, out_b = _mixer_b(proj, qk_g, rel_bias, 2 * d_rg, n_heads, jobs[1])
    n = bsz * s
    return _out_proj(x, ya.reshape(n, -1), yb.reshape(n, -1), w_out.astype(BF16)), out_a + out_b


def _cd_layout(gate_b, w_out):
    n_heads_c = gate_b.shape[-1]
    dc = n_heads_c * HEAD_C
    dd = w_out.shape[0] - dc
    n_gate = 2 * n_heads_c
    main = 4 * dc + 3 * dd
    width = -(-(main + n_gate) // CD_WIDTH_MULTIPLE) * CD_WIDTH_MULTIPLE
    return n_heads_c, dc, dd, n_gate, main, width


def _cd_weight_job(w_in, j, gate_col, n_gate, width, tw=CD_WEIGHT_COL_TILE):
    _, d, cols = w_in.shape
    main = cols - n_gate
    assert gate_col % tw == 0 and (main - gate_col) % tw == 0 and width % tw == 0 and n_gate <= tw
    n_aligned = gate_col // tw
    n_shifted = (main - gate_col) // tw
    n_blk = width // tw
    last = (cols - 1) // tw

    def job(grid):
        assert n_blk <= math.prod(grid)
        blk = lambda *idx: jnp.minimum(_linear_step(grid, idx), n_blk - 1)

        def reorder(t, a_ref, b_ref, o_ref):
            lane = lax.broadcasted_iota(jnp.int32, o_ref.shape, 1)

            @pl.when(t < n_aligned)
            def _():
                o_ref[...] = a_ref[...].astype(o_ref.dtype)

            @pl.when((t >= n_aligned) & (t < n_aligned + n_shifted))
            def _():
                a = pltpu.roll(a_ref[...], tw - n_gate, axis=1)
                b = pltpu.roll(b_ref[...], tw - n_gate, axis=1)
                o_ref[...] = jnp.where(lane < tw - n_gate, a, b).astype(o_ref.dtype)

            @pl.when(t == n_aligned + n_shifted)
            def _():
                o_ref[...] = jnp.where(lane < n_gate, a_ref[...], 0.0).astype(o_ref.dtype)

            @pl.when((t > n_aligned + n_shifted) & (t < n_blk))
            def _():
                o_ref[...] = jnp.zeros_like(o_ref)

        a_blk = lambda *idx: jnp.where(blk(*idx) < n_aligned + n_shifted, blk(*idx), n_aligned)
        return [([w_in, w_in],
                 [pl.BlockSpec((None, d, tw), lambda *idx: (j, 0, a_blk(*idx))),
                  pl.BlockSpec((None, d, tw), lambda *idx: (j, 0, jnp.minimum(blk(*idx) + 1, last)))],
                 pl.BlockSpec((d, tw), lambda *idx: (0, blk(*idx))),
                 jax.ShapeDtypeStruct((d, width), BF16),
                 reorder)]
    return job


def _layer_cd(x, bsz, s, g, w, conv_w, conv_b, gate_b, h_gain, w_out, jobs):
    n_heads_c, dc, dd, n_gate, main, width = _cd_layout(gate_b, w_out)
    n_heads_d = dd // HEAD_D
    proj = _norm_matmul(x, g, w, tn=PROJ_COL_TILE_CD).reshape(bsz, s, width)
    gates = proj[:, :, main:main + n_gate]
    yd, out_d = _mixer_d(proj, 4 * dc, n_heads_d, jobs[0])
    yc, out_c = _mixer_c(proj, gates, conv_w, conv_b, gate_b, h_gain, n_heads_c, jobs[1])
    n = bsz * s
    return _out_proj(x, yc.reshape(n, -1), yd.reshape(n, -1), w_out.astype(BF16)), out_d + out_c


def kernel(x, norm_g, ffn_w_in, ffn_w_out, ab_w_in, ab_conv_w, ab_conv_b, rg_w_r, rg_b_r, rg_w_i,
           rg_b_i, rg_lambda, qk_gain, rel_bias, ab_w_out, cd_w_in, cd_conv_w, cd_conv_b,
           mlstm_gate_bias, mlstm_h_gain, cd_w_out):
    bsz, s, d = x.shape
    depth = norm_g.shape[0]
    h = x.reshape(bsz * s, d)
    ffn_w = {(0, 0): (ffn_w_in[0, 0].astype(BF16), ffn_w_out[0, 0].astype(BF16))}
    cd_w = None
    for layer in range(depth):
        j = layer // 2
        h = _ffn(h, norm_g[layer, 0], *ffn_w[(layer, 0)])
        targets = [(layer, 1)] + ([(layer + 1, 0)] if layer + 1 < depth else [])
        jobs = [functools.partial(_ffn_cast_jobs, ffn_w_in, ffn_w_out, *t) for t in targets]
        jobs += [lambda grid: []] * (2 - len(jobs))
        if layer % 2 == 0:
            if layer + 1 < depth:
                _, dc, _, n_gate, _, width = _cd_layout(mlstm_gate_bias[j], cd_w_out[j])
                ffn_jobs, cd_job = jobs[0], _cd_weight_job(cd_w_in, j, 4 * dc, n_gate, width)
                jobs[0] = lambda grid: ffn_jobs(grid) + cd_job(grid)
            h, out = _layer_ab(h, bsz, s, norm_g[layer, 1], ab_w_in[j], ab_conv_w[j], ab_conv_b[j],
                               rg_w_r[j], rg_b_r[j], rg_w_i[j], rg_b_i[j], rg_lambda[j], qk_gain[j],
                               rel_bias, ab_w_out[j], jobs)
            if layer + 1 < depth:
                cd_w = out.pop(2)
        else:
            h, out = _layer_cd(h, bsz, s, norm_g[layer, 1], cd_w, cd_conv_w[j], cd_conv_b[j],
                               mlstm_gate_bias[j], mlstm_h_gain[j], cd_w_out[j], jobs)
        for i, t in enumerate(targets):
            ffn_w[t] = (out[2 * i], out[2 * i + 1])
        h = _ffn(h, norm_g[layer, 2], *ffn_w[(layer, 1)])
    return h.reshape(bsz, s, d)
```

```python
import functools
import math

import numpy as np
import jax
import jax.numpy as jnp
from jax import lax
from jax.experimental import pallas as pl
from jax.experimental.pallas import tpu as pltpu

F32 = jnp.float32
BF16 = jnp.bfloat16

RMS_EPS = 1e-6
RG_C = 8.0
CONV_W = 4
RG_BLOCK = 128
HEAD_B = 128
HEAD_C = 256
HEAD_D = 128
SEQ_BLOCK = 128
SUBLANES = 8
BF16_SUBLANES = 16
DSW_PATTERNS = ((128, 1), (512, 4), (2048, 16))
N_BUCKETS = 32
MAX_DIST = 2048
NEG = -1e30
LOG2E = 1.4426950408889634
HI_MASK = np.uint32(0xFFFF0000)
SB_NEAR_BLOCKS = 3
SB_QUERY_ROWS = 256
SB_UNDERFLOW = -104.0
DSW_QUERY_ROWS = 256

FFN_TOKEN_TILE = 1024
FFN_HIDDEN_TILE = 512
PROJ_TOKEN_TILE = 1024
PROJ_COL_TILE_AB = 1280
PROJ_COL_TILE_CD = 1536
OUT_PROJ_TOKEN_TILE = 512
CD_WIDTH_MULTIPLE = 512
CD_WEIGHT_COL_TILE = 256
V7X_VMEM_BYTES = 64 * 1024 * 1024
VMEM_LIMIT = V7X_VMEM_BYTES * 7 // 8


def _cparams(sem):
    return pltpu.CompilerParams(dimension_semantics=sem, vmem_limit_bytes=VMEM_LIMIT)


def _rms_rows(x, g):
    return x * lax.rsqrt(jnp.mean(x * x, axis=-1, keepdims=True) + RMS_EPS) * g


def _dot(a, b):
    return jnp.dot(a, b, preferred_element_type=F32)


def _dot_nt(a, b):
    return lax.dot_general(a, b, (((1,), (1,)), ((), ())), preferred_element_type=F32)


def _dot_tn(a, b):
    return lax.dot_general(a, b, (((0,), (0,)), ((), ())), preferred_element_type=F32)


def _pick_tile(n, pref):
    t = min(pref, n)
    while n % t:
        t //= 2
    return t


def _linear_step(grid, idx):
    t = idx[0]
    for i, g in zip(idx[1:], grid[1:]):
        t = t * g + i
    return t


def _ffn_cast_jobs(w_in, w_out, layer, half, grid):
    n_steps = math.prod(grid)
    jobs = []
    for w in (w_in, w_out):
        rows, cols = w.shape[2:]
        n = n_steps
        while rows % n or (rows // n) % BF16_SUBLANES:
            n -= 1
        slab = rows // n
        blk = lambda *idx, n=n: jnp.minimum(_linear_step(grid, idx), n - 1)

        def cast(t, src_ref, dst_ref, n=n):
            @pl.when(t < n)
            def _():
                dst_ref[...] = src_ref[...].astype(dst_ref.dtype)

        jobs.append((
            [w],
            [pl.BlockSpec((None, None, slab, cols), lambda *idx, blk=blk: (layer, half, blk(*idx), 0))],
            pl.BlockSpec((slab, cols), lambda *idx, blk=blk: (blk(*idx), 0)),
            jax.ShapeDtypeStruct((rows, cols), BF16),
            cast,
        ))
    return jobs


def _call_with_jobs(body, jobs, *, out_shape, grid, in_specs, out_specs, scratch_shapes=(),
                    semantics, name, operands):
    n_in = len(in_specs)
    n_job_in = [len(j[0]) for j in jobs]
    n_extra = sum(n_job_in)

    def kernel(*refs):
        t = _linear_step(grid, [pl.program_id(ax) for ax in range(len(grid))])
        at = n_in
        for i, j in enumerate(jobs):
            j[4](t, *refs[at:at + n_job_in[i]], refs[n_in + n_extra + 1 + i])
            at += n_job_in[i]
        body(*refs[:n_in], refs[n_in + n_extra], *refs[n_in + n_extra + 1 + len(jobs):])

    outs = pl.pallas_call(
        kernel,
        out_shape=[out_shape] + [j[3] for j in jobs],
        grid=grid,
        in_specs=list(in_specs) + [sp for j in jobs for sp in j[1]],
        out_specs=[out_specs] + [j[2] for j in jobs],
        scratch_shapes=list(scratch_shapes),
        compiler_params=_cparams(semantics),
        name=name,
    )(*operands, *[op for j in jobs for op in j[0]])
    return outs[0], list(outs[1:])


def _ffn_kernel(x_ref, g_ref, wg_ref, wu_ref, wo_ref, o_ref, xn_ref):
    j = pl.program_id(1)

    @pl.when(j == 0)
    def _():
        x = x_ref[...]
        xn_ref[...] = _rms_rows(x, g_ref[...]).astype(BF16)
        o_ref[...] = x

    xn = xn_ref[...]
    hg = _dot(xn, wg_ref[...])
    hu = _dot(xn, wu_ref[...])
    h = (0.5 * hg * jax.nn.sigmoid(hg) * hu).astype(BF16)
    o_ref[...] += _dot(h, wo_ref[...])


def _ffn(x, g, w_in, w_out, tm=FFN_TOKEN_TILE, tf=FFN_HIDDEN_TILE):
    n, d = x.shape
    d_ff = w_out.shape[0]
    tm = _pick_tile(n, tm)
    tf = _pick_tile(d_ff, tf)
    nj = d_ff // tf
    return pl.pallas_call(
        _ffn_kernel,
        out_shape=jax.ShapeDtypeStruct((n, d), F32),
        grid=(n // tm, nj),
        in_specs=[
            pl.BlockSpec((tm, d), lambda i, j: (i, 0)),
            pl.BlockSpec((1, d), lambda i, j: (0, 0)),
            pl.BlockSpec((d, tf), lambda i, j: (0, j)),
            pl.BlockSpec((d, tf), lambda i, j: (0, j + nj)),
            pl.BlockSpec((tf, d), lambda i, j: (j, 0)),
        ],
        out_specs=pl.BlockSpec((tm, d), lambda i, j: (i, 0)),
        scratch_shapes=[pltpu.VMEM((tm, d), BF16)],
        compiler_params=_cparams(("parallel", "arbitrary")),
        name="ffn",
    )(x, g.reshape(1, d), w_in, w_in, w_out)


def _norm_matmul_kernel(x_ref, g_ref, w_ref, o_ref, xn_ref):
    @pl.when(pl.program_id(1) == 0)
    def _():
        xn_ref[...] = _rms_rows(x_ref[...], g_ref[...]).astype(BF16)

    o_ref[...] = _dot(xn_ref[...], w_ref[...])


def _norm_matmul(x, g, w, tn, tm=PROJ_TOKEN_TILE):
    n, d = x.shape
    wn = w.shape[1]
    tm = _pick_tile(n, tm)
    while wn % tn:
        tn -= 128
    return pl.pallas_call(
        _norm_matmul_kernel,
        out_shape=jax.ShapeDtypeStruct((n, wn), F32),
        grid=(n // tm, wn // tn),
        in_specs=[
            pl.BlockSpec((tm, d), lambda i, j: (i, 0)),
            pl.BlockSpec((1, d), lambda i, j: (0, 0)),
            pl.BlockSpec((d, tn), lambda i, j: (0, j)),
        ],
        out_specs=pl.BlockSpec((tm, tn), lambda i, j: (i, j)),
        scratch_shapes=[pltpu.VMEM((tm, d), BF16)],
        compiler_params=_cparams(("parallel", "arbitrary")),
        name="norm_matmul",
    )(x, g.reshape(1, d), w)


def _out_proj_kernel(x_ref, ya_ref, yb_ref, wa_ref, wb_ref, o_ref):
    o_ref[...] = x_ref[...] + _dot(ya_ref[...], wa_ref[...]) + _dot(yb_ref[...], wb_ref[...])


def _out_proj(x, ya, yb, w, tm=OUT_PROJ_TOKEN_TILE):
    n, d = x.shape
    da = ya.shape[1]
    db = yb.shape[1]
    assert da == db and w.shape[0] == da + db
    tm = _pick_tile(n, tm)
    return pl.pallas_call(
        _out_proj_kernel,
        out_shape=jax.ShapeDtypeStruct((n, d), F32),
        grid=(n // tm,),
        in_specs=[
            pl.BlockSpec((tm, d), lambda i: (i, 0)),
            pl.BlockSpec((tm, da), lambda i: (i, 0)),
            pl.BlockSpec((tm, db), lambda i: (i, 0)),
            pl.BlockSpec((da, d), lambda i: (0, 0)),
            pl.BlockSpec((db, d), lambda i: (1, 0)),
        ],
        out_specs=pl.BlockSpec((tm, d), lambda i: (i, 0)),
        compiler_params=_cparams(("parallel",)),
        name="out_proj",
    )(x, ya, yb, w, w)


def _causal_conv(x, w, b):
    y = x * w[CONV_W - 1:CONV_W] + b
    for k in range(1, CONV_W):
        y = y + _shift_rows(x, k, 0.0) * w[CONV_W - 1 - k:CONV_W - k]
    return y


def _shift_rows(x, k, fill):
    assert 0 < k < SUBLANES
    rolled = pltpu.roll(x, k, axis=0)
    t = lax.broadcasted_iota(jnp.int32, (SUBLANES, x.shape[1]), 0)
    head = jnp.where(t >= k, rolled[:SUBLANES], fill)
    return jnp.concatenate([head, rolled[SUBLANES:]], axis=0)


def _mixer_a_kernel(xr_ref, gate_ref, cw_ref, cb_ref, wr_ref, br_ref, wi_ref, bi_ref,
                    lam_ref, o_ref):
    s = xr_ref.shape[0]
    xc = _causal_conv(xr_ref[...], cw_ref[...], cb_ref[...])
    xb = xc.astype(BF16)
    r = jax.nn.sigmoid(_dot(xb, wr_ref[...]) + br_ref[...])
    i = jax.nn.sigmoid(_dot(xb, wi_ref[...]) + bi_ref[...])
    neg_lam = -lam_ref[...]
    softplus = jnp.maximum(neg_lam, 0.0) + jnp.log1p(jnp.exp(-jnp.abs(neg_lam)))
    log_a = -RG_C * r * softplus
    a = jnp.exp(log_a)
    u = jnp.sqrt(-jnp.tanh(log_a) * (a * a + 1.0)) * (i * xc)
    k = 1
    while k < s:
        if k < SUBLANES:
            u = a * _shift_rows(u, k, 0.0) + u
            a = a * _shift_rows(a, k, 1.0)
        else:
            u = jnp.concatenate([u[:k], a[k:] * u[:s - k] + u[k:]], axis=0)
            a = jnp.concatenate([a[:k], a[k:] * a[:s - k]], axis=0)
        k *= 2
    o_ref[...] = (jax.nn.gelu(gate_ref[...]) * u).astype(o_ref.dtype)


def _mixer_a(proj, conv_w, conv_b, w_r, b_r, w_i, b_i, lam, d_rg, cast_jobs):
    bsz, s, _ = proj.shape
    nb = d_rg // RG_BLOCK
    grid = (bsz, nb)
    vec = lambda v: v.reshape(1, d_rg)
    col = lambda off: pl.BlockSpec((None, s, RG_BLOCK), lambda b, g: (b, 0, off + g))
    row = lambda r: pl.BlockSpec((r, RG_BLOCK), lambda b, g: (0, g))
    blk = pl.BlockSpec((None, RG_BLOCK, RG_BLOCK), lambda b, g: (g, 0, 0))
    return _call_with_jobs(
        _mixer_a_kernel, cast_jobs(grid),
        out_shape=jax.ShapeDtypeStruct((bsz, s, d_rg), BF16),
        grid=grid,
        in_specs=[col(nb), col(0), row(CONV_W), row(1), blk, row(1), blk, row(1), row(1)],
        out_specs=pl.BlockSpec((None, s, RG_BLOCK), lambda b, g: (b, 0, g)),
        semantics=("arbitrary", "arbitrary"),
        name="mixer_a",
        operands=(proj, proj, conv_w, vec(conv_b), w_r.astype(BF16), vec(b_r), w_i.astype(BF16),
                  vec(b_i), vec(lam)))


def _t5_bucket_np(dist):
    max_exact = N_BUCKETS // 2
    d_f = np.maximum(dist, 1).astype(np.float32)
    large = max_exact + (np.log(d_f / np.float32(max_exact)) / np.float32(math.log(MAX_DIST / max_exact))
                         * np.float32(N_BUCKETS - max_exact)).astype(np.int32)
    large = np.minimum(large, N_BUCKETS - 1)
    return np.where(dist < max_exact, dist, large).astype(np.int32)


def _dsw_bias_vector(rel_bias, seq, qb):
    dist = np.arange(seq)
    mult = np.zeros(seq, np.float32)
    for window, dilation in DSW_PATTERNS:
        mult += ((dist % dilation == 0) & (dist <= window)).astype(np.float32)
    logm = np.where(mult > 0, np.log(np.maximum(mult, 1.0)), NEG).astype(np.float32)
    f = (rel_bias.astype(F32)[_t5_bucket_np(dist)] + logm[:, None]) * LOG2E
    n_heads = f.shape[1]
    hv = jnp.concatenate([f[::-1].T, jnp.full((n_heads, qb), NEG, F32)], axis=1)
    return hv[:, None, :]


def _mixer_b_kernel(q_ref, k_ref, v_ref, g_ref, hv_ref, o_ref, kn_ref, vb_ref, tbl_ref):
    s, hd = q_ref.shape
    qb = tbl_ref.shape[0]

    @pl.when(pl.program_id(1) == 0)
    def _():
        skew = pltpu.roll(jnp.broadcast_to(hv_ref[...], (qb, s + qb)), 1, axis=1, stride=1, stride_axis=0)
        tbl_ref[...] = skew[:, qb:]

    kn_ref[...] = _rms_rows(k_ref[...], g_ref[1:2]).astype(BF16)
    vb_ref[...] = v_ref[...].astype(BF16)
    def logits(gi):
        n_keys = (gi + 1) * qb
        qn = (_rms_rows(q_ref[gi * qb:n_keys, :], g_ref[0:1]) * (LOG2E / math.sqrt(hd))).astype(BF16)
        return _dot_nt(qn, kn_ref[0:n_keys, :]) + tbl_ref[:, s - n_keys:]

    def weights(sc):
        m = jnp.max(sc, axis=-1, keepdims=True)
        p = jnp.exp2(sc - m)
        return p.astype(BF16), jnp.sum(p, axis=-1, keepdims=True)

    def output(gi, p, l):
        n_keys = (gi + 1) * qb
        o = _dot(p, vb_ref[0:n_keys, :])
        o_ref[gi * qb:n_keys, :] = (o / l).astype(o_ref.dtype)

    n_groups = s // qb
    held = None
    for gi in reversed(range(n_groups)):
        cur = (gi,) + weights(logits(gi))
        if held is not None:
            output(*held)
        held = cur
    output(*held)


def _mixer_b(proj, qk_g, rel_bias, col0, n_heads, cast_jobs, qb=DSW_QUERY_ROWS):
    bsz, s, _ = proj.shape
    qb = _pick_tile(s, qb)
    hv = _dsw_bias_vector(rel_bias, s, qb)
    c0 = col0 // HEAD_B
    grid = (n_heads, bsz)
    seq_spec = lambda off: pl.BlockSpec((None, s, HEAD_B), lambda h, b: (b, 0, off + h))
    return _call_with_jobs(
        _mixer_b_kernel, cast_jobs(grid),
        out_shape=jax.ShapeDtypeStruct((bsz, s, n_heads * HEAD_B), BF16),
        grid=grid,
        in_specs=[
            seq_spec(c0), seq_spec(c0 + n_heads), seq_spec(c0 + 2 * n_heads),
            pl.BlockSpec((2, HEAD_B), lambda h, b: (0, 0)),
            pl.BlockSpec((None, 1, s + qb), lambda h, b: (h, 0, 0)),
        ],
        out_specs=pl.BlockSpec((None, s, HEAD_B), lambda h, b: (b, 0, h)),
        scratch_shapes=[pltpu.VMEM((s, HEAD_B), BF16), pltpu.VMEM((s, HEAD_B), BF16),
                        pltpu.VMEM((qb, s), F32)],
        semantics=("arbitrary", "arbitrary"),
        name="mixer_b",
        operands=(proj, proj, proj, qk_g, hv))


def _mixer_c_kernel(gb_ref, q_ref, k_ref, v_ref, og_ref, cwq_ref, cwk_ref, cbq_ref, cbk_ref,
                    ig_ref, fg_ref, gain_ref, o_ref, qc_ref, kc_ref, c_ref, n_ref, m_ref):
    h = pl.program_id(1)
    L = SEQ_BLOCK
    s, d = q_ref.shape
    nc = s // L

    qc_ref[...] = jax.nn.silu(_causal_conv(q_ref[...], cwq_ref[...], cbq_ref[...]))
    kc_ref[...] = jax.nn.silu(_causal_conv(k_ref[...], cwk_ref[...], cbk_ref[...])) * (d ** -0.5)
    c_ref[...] = jnp.zeros_like(c_ref)
    n_ref[...] = jnp.zeros_like(n_ref)
    m_ref[...] = jnp.zeros_like(m_ref)

    ri = lax.broadcasted_iota(jnp.int32, (L, L), 0)
    ci = lax.broadcasted_iota(jnp.int32, (L, L), 1)
    eye = ri == ci
    causal = ci <= ri

    def to_col(row):
        return jnp.sum(jnp.where(eye, row, 0.0), axis=1, keepdims=True)

    ig_all = ig_ref[...] + gb_ref[0, h]
    f_all = fg_ref[...] + gb_ref[1, h]
    ls_all = jnp.minimum(f_all, 0.0) - jnp.log1p(jnp.exp(-jnp.abs(f_all)))

    def chunk_local(n):
        rows = slice(n * L, (n + 1) * L)
        qc = qc_ref[rows, :]
        kc = kc_ref[rows, :]
        qb = qc.astype(BF16)
        kb = kc.astype(BF16)
        vb = v_ref[rows, :].astype(BF16)

        ig_row = ig_all[n:n + 1]
        ls_row = ls_all[n:n + 1]
        ls_col = to_col(ls_row)
        ig_col = to_col(ig_row)
        b_col = jnp.sum(jnp.where(causal, ls_row, 0.0), axis=1, keepdims=True)
        b_row = jnp.sum(jnp.where(ri <= ci, ls_col, 0.0), axis=0, keepdims=True)
        b_last = jnp.sum(ls_row, axis=1, keepdims=True)

        log_d = b_col - b_row + ig_row
        g_row = b_last - b_row + ig_row
        g_col = b_last - b_col + ig_col
        g_max = jnp.max(g_row, axis=1, keepdims=True)
        d_max = jnp.max(jnp.where(causal, log_d, -jnp.inf), axis=1, keepdims=True)
        qk = _dot_nt(qb, kb)
        wk = jnp.exp(g_col - g_max) * kc
        c_loc = _dot_tn(wk.astype(BF16), vb)
        n_loc = jnp.sum(wk, axis=0, keepdims=True)
        return rows, qc, qb, vb, b_col, b_last, log_d, g_max, d_max, qk, c_loc, n_loc

    def finish(rows, hout):
        hn = _rms_rows(hout, gain_ref[...])
        o_ref[rows, :] = (jax.nn.sigmoid(og_ref[rows, :]) * hn).astype(o_ref.dtype)

    pending = None
    local = chunk_local(0)
    for n in range(nc):
        rows, qc, qb, vb, b_col, b_last, log_d, g_max, d_max, qk, c_loc, n_loc = local
        if n + 1 < nc:
            local = chunk_local(n + 1)

        c_st = c_ref[...]
        n_st = n_ref[...]
        m_st = m_ref[...]

        m_new = jnp.maximum(b_last + m_st, g_max)
        decay = jnp.exp(b_last + m_st - m_new)
        fresh = jnp.exp(g_max - m_new)
        c_ref[...] = decay * c_st + fresh * c_loc
        n_ref[...] = decay * n_st + fresh * n_loc
        m_ref[...] = m_new

        inter = b_col + m_st
        m_t = jnp.maximum(inter, d_max)
        s_qk = jnp.where(causal, qk * jnp.exp(log_d - m_t), 0.0)
        inter_w = jnp.exp(inter - m_t)
        num = _dot(s_qk.astype(BF16), vb) + inter_w * _dot(qb, c_st.astype(BF16))
        qn = jnp.sum(qc * n_st, axis=1, keepdims=True)
        den = jnp.sum(s_qk, axis=1, keepdims=True) + inter_w * qn
        hout = num / jnp.maximum(jnp.abs(den), jnp.exp(-m_t))

        if pending is not None:
            finish(*pending)
        pending = (rows, hout)
    finish(*pending)


def _mixer_c(proj, gates, conv_w, conv_b, gate_b, h_gain, n_heads, cast_jobs):
    bsz, s, _ = proj.shape
    d = HEAD_C
    L = SEQ_BLOCK
    nc = s // L
    grid = (bsz, n_heads)
    g_rows = jnp.transpose(gates, (0, 2, 1)).reshape(bsz, 2 * n_heads, nc, L)
    cb = conv_b.reshape(1, -1)
    full = lambda off: pl.BlockSpec((None, s, d), lambda b, h: (b, 0, off + h))
    row = lambda r, off: pl.BlockSpec((r, d), lambda b, h: (0, off + h))
    grow = lambda off: pl.BlockSpec((None, None, nc, L), lambda b, h: (b, off + h, 0, 0))
    return _call_with_jobs(
        _mixer_c_kernel, cast_jobs(grid),
        out_shape=jax.ShapeDtypeStruct((bsz, s, n_heads * d), BF16),
        grid=grid,
        in_specs=[
            pl.BlockSpec(memory_space=pltpu.SMEM),
            full(0), full(n_heads), full(2 * n_heads), full(3 * n_heads),
            row(CONV_W, 0), row(CONV_W, n_heads), row(1, 0), row(1, n_heads),
            grow(0), grow(n_heads), row(1, 0),
        ],
        out_specs=pl.BlockSpec((None, s, d), lambda b, h: (b, 0, h)),
        scratch_shapes=[pltpu.VMEM((s, d), F32), pltpu.VMEM((s, d), F32),
                        pltpu.VMEM((d, d), F32), pltpu.VMEM((1, d), F32), pltpu.VMEM((1, 1), F32)],
        semantics=("arbitrary", "arbitrary"),
        name="mixer_c",
        operands=(gate_b, proj, proj, proj, proj, conv_w, conv_w, cb, cb, g_rows, g_rows,
                  h_gain.reshape(1, -1)))


def _mixer_d_kernel(q_ref, k_ref, v_ref, o_ref, kb_ref, vb_ref, acc_ref, suf_ref):
    s, hd = q_ref.shape
    B = SEQ_BLOCK
    G = min(SB_QUERY_ROWS, s)
    kb_ref[...] = k_ref[...].astype(BF16)
    vb_ref[...] = v_ref[...].astype(BF16)
    ri = lax.broadcasted_iota(jnp.int32, (B, B), 0)
    ci = lax.broadcasted_iota(jnp.int32, (B, B), 1)
    after = (ri > ci).astype(BF16)
    after2 = jnp.concatenate([after, after], axis=0)
    row_g = lax.broadcasted_iota(jnp.int32, (G, B), 0)
    col_g = lax.broadcasted_iota(jnp.int32, (G, B), 1)

    def span(q0, first, last, suffix):
        n = last - first + 1
        keys = slice(first * B, (last + 1) * B)
        qb = (q_ref[q0:q0 + G, :] * (hd ** -0.5)).astype(BF16)
        z = _dot_nt(qb, kb_ref[keys, :])
        sp = jnp.log(1.0 + jnp.exp2(jnp.abs(z) * (-LOG2E)))
        nm = jnp.minimum(z, 0.0)
        log_beta = nm - sp
        log_stay = (nm - z) - sp
        before = [col_g + ((first + j) * B - q0) < row_g if (first + j + 1) * B > q0 else None
                  for j in range(n)]
        later, tot = [], []
        for j in range(n):
            x = log_stay[:, j * B:(j + 1) * B]
            if before[j] is not None:
                x = jnp.where(before[j], x, 0.0)
            hi = lax.bitcast_convert_type(lax.bitcast_convert_type(x, jnp.uint32) & HI_MASK, F32)
            hl = jnp.concatenate([hi.astype(BF16), (x - hi).astype(BF16)], axis=1)
            later.append(_dot(hl, after2))
            tot.append(jnp.sum(x, axis=-1, keepdims=True))
        att = [None] * n
        for j in range(n - 1, -1, -1):
            a = jnp.exp(log_beta[:, j * B:(j + 1) * B] + later[j] + suffix)
            if before[j] is not None:
                a = jnp.where(before[j], a, 0.0)
            att[j] = a.astype(BF16)
            suffix = suffix + tot[j]
        att = att[0] if n == 1 else jnp.concatenate(att, axis=1)
        return _dot(att, vb_ref[keys, :]), suffix

    def first_near(q0):
        return max(q0 // B - SB_NEAR_BLOCKS + 1, 0)

    worst = None
    for q0 in range(0, s, G):
        rows = slice(q0, q0 + G)
        acc, suffix = span(q0, first_near(q0), (q0 + G) // B - 1, jnp.zeros((G, 1), F32))
        o_ref[rows, :] = acc.astype(o_ref.dtype)
        if first_near(q0) > 0:
            acc_ref[rows, :] = acc
            suf_ref[rows, :] = suffix
            worst = suffix if worst is None else jnp.maximum(worst, suffix)

    if worst is not None:
        @pl.when(jnp.max(worst) > SB_UNDERFLOW)
        def _():
            for q0 in range(0, s, G):
                if first_near(q0) == 0:
                    continue
                rows = slice(q0, q0 + G)
                suffix = suf_ref[rows, :]

                @pl.when(jnp.max(suffix) > SB_UNDERFLOW)
                def _():
                    far, _ = span(q0, 0, first_near(q0) - 1, suffix)
                    o_ref[rows, :] = (acc_ref[rows, :] + far).astype(o_ref.dtype)


def _mixer_d(proj, col0, n_heads, cast_jobs):
    bsz, s, _ = proj.shape
    c0 = col0 // HEAD_D
    grid = (bsz, n_heads)
    seq_spec = lambda off: pl.BlockSpec((None, s, HEAD_D), lambda b, h: (b, 0, off + h))
    return _call_with_jobs(
        _mixer_d_kernel, cast_jobs(grid),
        out_shape=jax.ShapeDtypeStruct((bsz, s, n_heads * HEAD_D), BF16),
        grid=grid,
        in_specs=[seq_spec(c0), seq_spec(c0 + n_heads), seq_spec(c0 + 2 * n_heads)],
        out_specs=pl.BlockSpec((None, s, HEAD_D), lambda b, h: (b, 0, h)),
        scratch_shapes=[pltpu.VMEM((s, HEAD_D), BF16), pltpu.VMEM((s, HEAD_D), BF16),
                        pltpu.VMEM((s, HEAD_D), F32), pltpu.VMEM((s, 1), F32)],
        semantics=("arbitrary", "arbitrary"),
        name="mixer_d",
        operands=(proj, proj, proj))


def _layer_ab(x, bsz, s, g, w_in, conv_w, conv_b, w_r, b_r, w_i, b_i, lam, qk_g, rel_bias, w_out,
              jobs):
    d_rg = conv_w.shape[-1]
    n_heads = rel_bias.shape[-1]
    proj = _norm_matmul(x, g, w_in.astype(BF16), tn=PROJ_COL_TILE_AB).reshape(bsz, s, -1)
    ya, out_a = _mixer_a(proj, conv_w, conv_b, w_r, b_r, w_i, b_i, lam, d_rg, jobs[0])
    yb, out_b = _mixer_b(proj, qk_g, rel_bias, 2 * d_rg, n_heads, jobs[1])
    n = bsz * s
    return _out_proj(x, ya.reshape(n, -1), yb.reshape(n, -1), w_out.astype(BF16)), out_a + out_b


def _cd_layout(gate_b, w_out):
    n_heads_c = gate_b.shape[-1]
    dc = n_heads_c * HEAD_C
    dd = w_out.shape[0] - dc
    n_gate = 2 * n_heads_c
    main = 4 * dc + 3 * dd
    width = -(-(main + n_gate) // CD_WIDTH_MULTIPLE) * CD_WIDTH_MULTIPLE
    return n_heads_c, dc, dd, n_gate, main, width


def _cd_weight_job(w_in, j, gate_col, n_gate, width, tw=CD_WEIGHT_COL_TILE):
    _, d, cols = w_in.shape
    main = cols - n_gate
    assert gate_col % tw == 0 and (main - gate_col) % tw == 0 and width % tw == 0 and n_gate <= tw
    n_aligned = gate_col // tw
    n_shifted = (main - gate_col) // tw
    n_blk = width // tw
    last = (cols - 1) // tw

    def job(grid):
        assert n_blk <= math.prod(grid)
        blk = lambda *idx: jnp.minimum(_linear_step(grid, idx), n_blk - 1)

        def reorder(t, a_ref, b_ref, o_ref):
            lane = lax.broadcasted_iota(jnp.int32, o_ref.shape, 1)

            @pl.when(t < n_aligned)
            def _():
                o_ref[...] = a_ref[...].astype(o_ref.dtype)

            @pl.when((t >= n_aligned) & (t < n_aligned + n_shifted))
            def _():
                a = pltpu.roll(a_ref[...], tw - n_gate, axis=1)
                b = pltpu.roll(b_ref[...], tw - n_gate, axis=1)
                o_ref[...] = jnp.where(lane < tw - n_gate, a, b).astype(o_ref.dtype)

            @pl.when(t == n_aligned + n_shifted)
            def _():
                o_ref[...] = jnp.where(lane < n_gate, a_ref[...], 0.0).astype(o_ref.dtype)

            @pl.when((t > n_aligned + n_shifted) & (t < n_blk))
            def _():
                o_ref[...] = jnp.zeros_like(o_ref)

        a_blk = lambda *idx: jnp.where(blk(*idx) < n_aligned + n_shifted, blk(*idx), n_aligned)
        return [([w_in, w_in],
                 [pl.BlockSpec((None, d, tw), lambda *idx: (j, 0, a_blk(*idx))),
                  pl.BlockSpec((None, d, tw), lambda *idx: (j, 0, jnp.minimum(blk(*idx) + 1, last)))],
                 pl.BlockSpec((d, tw), lambda *idx: (0, blk(*idx))),
                 jax.ShapeDtypeStruct((d, width), BF16),
                 reorder)]
    return job


def _layer_cd(x, bsz, s, g, w, conv_w, conv_b, gate_b, h_gain, w_out, jobs):
    n_heads_c, dc, dd, n_gate, main, width = _cd_layout(gate_b, w_out)
    n_heads_d = dd // HEAD_D
    proj = _norm_matmul(x, g, w, tn=PROJ_COL_TILE_CD).reshape(bsz, s, width)
    gates = proj[:, :, main:main + n_gate]
    yd, out_d = _mixer_d(proj, 4 * dc, n_heads_d, jobs[0])
    yc, out_c = _mixer_c(proj, gates, conv_w, conv_b, gate_b, h_gain, n_heads_c, jobs[1])
    n = bsz * s
    return _out_proj(x, yc.reshape(n, -1), yd.reshape(n, -1), w_out.astype(BF16)), out_d + out_c


def kernel(x, norm_g, ffn_w_in, ffn_w_out, ab_w_in, ab_conv_w, ab_conv_b, rg_w_r, rg_b_r, rg_w_i,
           rg_b_i, rg_lambda, qk_gain, rel_bias, ab_w_out, cd_w_in, cd_conv_w, cd_conv_b,
           mlstm_gate_bias, mlstm_h_gain, cd_w_out):
    bsz, s, d = x.shape
    depth = norm_g.shape[0]
    h = x.reshape(bsz * s, d)
    ffn_w = {(0, 0): (ffn_w_in[0, 0].astype(BF16), ffn_w_out[0, 0].astype(BF16))}
    cd_w = None
    for layer in range(depth):
        j = layer // 2
        h = _ffn(h, norm_g[layer, 0], *ffn_w[(layer, 0)])
        targets = [(layer, 1)] + ([(layer + 1, 0)] if layer + 1 < depth else [])
        jobs = [functools.partial(_ffn_cast_jobs, ffn_w_in, ffn_w_out, *t) for t in targets]
        jobs += [lambda grid: []] * (2 - len(jobs))
        if layer % 2 == 0:
            if layer + 1 < depth:
                _, dc, _, n_gate, _, width = _cd_layout(mlstm_gate_bias[j], cd_w_out[j])
                ffn_jobs, cd_job = jobs[0], _cd_weight_job(cd_w_in, j, 4 * dc, n_gate, width)
                jobs[0] = lambda grid: ffn_jobs(grid) + cd_job(grid)
            h, out = _layer_ab(h, bsz, s, norm_g[layer, 1], ab_w_in[j], ab_conv_w[j], ab_conv_b[j],
                               rg_w_r[j], rg_b_r[j], rg_w_i[j], rg_b_i[j], rg_lambda[j], qk_gain[j],
                               rel_bias, ab_w_out[j], jobs)
            if layer + 1 < depth:
                cd_w = out.pop(2)
        else:
            h, out = _layer_cd(h, bsz, s, norm_g[layer, 1], cd_w, cd_conv_w[j], cd_conv_b[j],
                               mlstm_gate_bias[j], mlstm_h_gain[j], cd_w_out[j], jobs)
        for i, t in enumerate(targets):
            ffn_w[t] = (out[2 * i], out[2 * i + 1])
        h = _ffn(h, norm_g[layer, 2], *ffn_w[(layer, 1)])
    return h.reshape(bsz, s, d)
```

```python
import functools
import math

import numpy as np
import jax
import jax.numpy as jnp
from jax import lax
from jax.experimental import pallas as pl
from jax.experimental.pallas import tpu as pltpu

F32 = jnp.float32
BF16 = jnp.bfloat16

RMS_EPS = 1e-6
RG_C = 8.0
CONV_W = 4
RG_BLOCK = 128
HEAD_B = 128
HEAD_C = 256
HEAD_D = 128
SEQ_BLOCK = 128
SUBLANES = 8
BF16_SUBLANES = 16
DSW_PATTERNS = ((128, 1), (512, 4), (2048, 16))
N_BUCKETS = 32
MAX_DIST = 2048
NEG = -1e30
LOG2E = 1.4426950408889634
HI_MASK = np.uint32(0xFFFF0000)
SB_NEAR_BLOCKS = 3
SB_QUERY_ROWS = 256
SB_UNDERFLOW = -104.0
DSW_QUERY_ROWS = 256

FFN_TOKEN_TILE = 1024
FFN_HIDDEN_TILE = 512
PROJ_TOKEN_TILE = 1024
PROJ_COL_TILE_AB = 1280
PROJ_COL_TILE_CD = 1536
OUT_PROJ_TOKEN_TILE = 512
CD_WIDTH_MULTIPLE = 512
CD_WEIGHT_COL_TILE = 256
V7X_VMEM_BYTES = 64 * 1024 * 1024
VMEM_LIMIT = V7X_VMEM_BYTES * 7 // 8


def _cparams(sem):
    return pltpu.CompilerParams(dimension_semantics=sem, vmem_limit_bytes=VMEM_LIMIT)


def _rms_rows(x, g):
    return x * lax.rsqrt(jnp.mean(x * x, axis=-1, keepdims=True) + RMS_EPS) * g


def _dot(a, b):
    return jnp.dot(a, b, preferred_element_type=F32)


def _dot_nt(a, b):
    return lax.dot_general(a, b, (((1,), (1,)), ((), ())), preferred_element_type=F32)


def _dot_tn(a, b):
    return lax.dot_general(a, b, (((0,), (0,)), ((), ())), preferred_element_type=F32)


def _pick_tile(n, pref):
    t = min(pref, n)
    while n % t:
        t //= 2
    return t


def _linear_step(grid, idx):
    t = idx[0]
    for i, g in zip(idx[1:], grid[1:]):
        t = t * g + i
    return t


def _ffn_cast_jobs(w_in, w_out, layer, half, grid):
    n_steps = math.prod(grid)
    jobs = []
    for w in (w_in, w_out):
        rows, cols = w.shape[2:]
        n = n_steps
        while rows % n or (rows // n) % BF16_SUBLANES:
            n -= 1
        slab = rows // n
        blk = lambda *idx, n=n: jnp.minimum(_linear_step(grid, idx), n - 1)

        def cast(t, src_ref, dst_ref, n=n):
            @pl.when(t < n)
            def _():
                dst_ref[...] = src_ref[...].astype(dst_ref.dtype)

        jobs.append((
            [w],
            [pl.BlockSpec((None, None, slab, cols), lambda *idx, blk=blk: (layer, half, blk(*idx), 0))],
            pl.BlockSpec((slab, cols), lambda *idx, blk=blk: (blk(*idx), 0)),
            jax.ShapeDtypeStruct((rows, cols), BF16),
            cast,
        ))
    return jobs


def _call_with_jobs(body, jobs, *, out_shape, grid, in_specs, out_specs, scratch_shapes=(),
                    semantics, name, operands):
    n_in = len(in_specs)
    n_job_in = [len(j[0]) for j in jobs]
    n_extra = sum(n_job_in)

    def kernel(*refs):
        t = _linear_step(grid, [pl.program_id(ax) for ax in range(len(grid))])
        at = n_in
        for i, j in enumerate(jobs):
            j[4](t, *refs[at:at + n_job_in[i]], refs[n_in + n_extra + 1 + i])
            at += n_job_in[i]
        body(*refs[:n_in], refs[n_in + n_extra], *refs[n_in + n_extra + 1 + len(jobs):])

    outs = pl.pallas_call(
        kernel,
        out_shape=[out_shape] + [j[3] for j in jobs],
        grid=grid,
        in_specs=list(in_specs) + [sp for j in jobs for sp in j[1]],
        out_specs=[out_specs] + [j[2] for j in jobs],
        scratch_shapes=list(scratch_shapes),
        compiler_params=_cparams(semantics),
        name=name,
    )(*operands, *[op for j in jobs for op in j[0]])
    return outs[0], list(outs[1:])


def _ffn_kernel(x_ref, g_ref, wg_ref, wu_ref, wo_ref, o_ref, xn_ref):
    j = pl.program_id(1)

    @pl.when(j == 0)
    def _():
        x = x_ref[...]
        xn_ref[...] = _rms_rows(x, g_ref[...]).astype(BF16)
        o_ref[...] = x

    xn = xn_ref[...]
    hg = _dot(xn, wg_ref[...])
    hu = _dot(xn, wu_ref[...])
    h = (0.5 * hg * jax.nn.sigmoid(hg) * hu).astype(BF16)
    o_ref[...] += _dot(h, wo_ref[...])


def _ffn(x, g, w_in, w_out, tm=FFN_TOKEN_TILE, tf=FFN_HIDDEN_TILE):
    n, d = x.shape
    d_ff = w_out.shape[0]
    tm = _pick_tile(n, tm)
    tf = _pick_tile(d_ff, tf)
    nj = d_ff // tf
    return pl.pallas_call(
        _ffn_kernel,
        out_shape=jax.ShapeDtypeStruct((n, d), F32),
        grid=(n // tm, nj),
        in_specs=[
            pl.BlockSpec((tm, d), lambda i, j: (i, 0)),
            pl.BlockSpec((1, d), lambda i, j: (0, 0)),
            pl.BlockSpec((d, tf), lambda i, j: (0, j)),
            pl.BlockSpec((d, tf), lambda i, j: (0, j + nj)),
            pl.BlockSpec((tf, d), lambda i, j: (j, 0)),
        ],
        out_specs=pl.BlockSpec((tm, d), lambda i, j: (i, 0)),
        scratch_shapes=[pltpu.VMEM((tm, d), BF16)],
        compiler_params=_cparams(("parallel", "arbitrary")),
        name="ffn",
    )(x, g.reshape(1, d), w_in, w_in, w_out)


def _norm_matmul_kernel(x_ref, g_ref, w_ref, o_ref, xn_ref):
    @pl.when(pl.program_id(1) == 0)
    def _():
        xn_ref[...] = _rms_rows(x_ref[...], g_ref[...]).astype(BF16)

    o_ref[...] = _dot(xn_ref[...], w_ref[...])


def _norm_matmul(x, g, w, tn, tm=PROJ_TOKEN_TILE):
    n, d = x.shape
    wn = w.shape[1]
    tm = _pick_tile(n, tm)
    while wn % tn:
        tn -= 128
    return pl.pallas_call(
        _norm_matmul_kernel,
        out_shape=jax.ShapeDtypeStruct((n, wn), F32),
        grid=(n // tm, wn // tn),
        in_specs=[
            pl.BlockSpec((tm, d), lambda i, j: (i, 0)),
            pl.BlockSpec((1, d), lambda i, j: (0, 0)),
            pl.BlockSpec((d, tn), lambda i, j: (0, j)),
        ],
        out_specs=pl.BlockSpec((tm, tn), lambda i, j: (i, j)),
        scratch_shapes=[pltpu.VMEM((tm, d), BF16)],
        compiler_params=_cparams(("parallel", "arbitrary")),
        name="norm_matmul",
    )(x, g.reshape(1, d), w)


def _out_proj_kernel(x_ref, ya_ref, yb_ref, wa_ref, wb_ref, o_ref):
    o_ref[...] = x_ref[...] + _dot(ya_ref[...], wa_ref[...]) + _dot(yb_ref[...], wb_ref[...])


def _out_proj(x, ya, yb, w, tm=OUT_PROJ_TOKEN_TILE):
    n, d = x.shape
    da = ya.shape[1]
    db = yb.shape[1]
    assert da == db and w.shape[0] == da + db
    tm = _pick_tile(n, tm)
    return pl.pallas_call(
        _out_proj_kernel,
        out_shape=jax.ShapeDtypeStruct((n, d), F32),
        grid=(n // tm,),
        in_specs=[
            pl.BlockSpec((tm, d), lambda i: (i, 0)),
            pl.BlockSpec((tm, da), lambda i: (i, 0)),
            pl.BlockSpec((tm, db), lambda i: (i, 0)),
            pl.BlockSpec((da, d), lambda i: (0, 0)),
            pl.BlockSpec((db, d), lambda i: (1, 0)),
        ],
        out_specs=pl.BlockSpec((tm, d), lambda i: (i, 0)),
        compiler_params=_cparams(("parallel",)),
        name="out_proj",
    )(x, ya, yb, w, w)


def _causal_conv(x, w, b):
    y = x * w[CONV_W - 1:CONV_W] + b
    for k in range(1, CONV_W):
        y = y + _shift_rows(x, k, 0.0) * w[CONV_W - 1 - k:CONV_W - k]
    return y


def _shift_rows(x, k, fill):
    assert 0 < k < SUBLANES
    rolled = pltpu.roll(x, k, axis=0)
    t = lax.broadcasted_iota(jnp.int32, (SUBLANES, x.shape[1]), 0)
    head = jnp.where(t >= k, rolled[:SUBLANES], fill)
    return jnp.concatenate([head, rolled[SUBLANES:]], axis=0)


def _mixer_a_kernel(xr_ref, gate_ref, cw_ref, cb_ref, wr_ref, br_ref, wi_ref, bi_ref,
                    lam_ref, o_ref, a_ref, u_ref):
    s = xr_ref.shape[0]
    xc = _causal_conv(xr_ref[...], cw_ref[...], cb_ref[...])
    xb = xc.astype(BF16)
    r = jax.nn.sigmoid(_dot(xb, wr_ref[...]) + br_ref[...])
    i = jax.nn.sigmoid(_dot(xb, wi_ref[...]) + bi_ref[...])
    neg_lam = -lam_ref[...]
    softplus = jnp.maximum(neg_lam, 0.0) + jnp.log1p(jnp.exp(-jnp.abs(neg_lam)))
    log_a = -RG_C * r * softplus
    a = jnp.exp(log_a)
    a_ref[...] = a
    u_ref[...] = jnp.sqrt(-jnp.tanh(log_a) * (a * a + 1.0)) * (i * xc)
    B = SEQ_BLOCK
    carry = None
    for c in range(s // B):
        rows = slice(c * B, (c + 1) * B)
        a = a_ref[rows, :]
        u = u_ref[rows, :]
        k = 1
        while k < B:
            if k < SUBLANES:
                u = a * _shift_rows(u, k, 0.0) + u
                a = a * _shift_rows(a, k, 1.0)
            else:
                u = jnp.concatenate([u[:k], a[k:] * u[:B - k] + u[k:]], axis=0)
                a = jnp.concatenate([a[:k], a[k:] * a[:B - k]], axis=0)
            k *= 2
        if carry is not None:
            u = a * carry + u
        carry = u[B - 1:B, :]
        o_ref[rows, :] = (jax.nn.gelu(gate_ref[rows, :]) * u).astype(o_ref.dtype)


def _mixer_a(proj, conv_w, conv_b, w_r, b_r, w_i, b_i, lam, d_rg, cast_jobs):
    bsz, s, _ = proj.shape
    nb = d_rg // RG_BLOCK
    grid = (bsz, nb)
    vec = lambda v: v.reshape(1, d_rg)
    col = lambda off: pl.BlockSpec((None, s, RG_BLOCK), lambda b, g: (b, 0, off + g))
    row = lambda r: pl.BlockSpec((r, RG_BLOCK), lambda b, g: (0, g))
    blk = pl.BlockSpec((None, RG_BLOCK, RG_BLOCK), lambda b, g: (g, 0, 0))
    return _call_with_jobs(
        _mixer_a_kernel, cast_jobs(grid),
        out_shape=jax.ShapeDtypeStruct((bsz, s, d_rg), BF16),
        grid=grid,
        in_specs=[col(nb), col(0), row(CONV_W), row(1), blk, row(1), blk, row(1), row(1)],
        out_specs=pl.BlockSpec((None, s, RG_BLOCK), lambda b, g: (b, 0, g)),
        scratch_shapes=[pltpu.VMEM((s, RG_BLOCK), F32), pltpu.VMEM((s, RG_BLOCK), F32)],
        semantics=("arbitrary", "arbitrary"),
        name="mixer_a",
        operands=(proj, proj, conv_w, vec(conv_b), w_r.astype(BF16), vec(b_r), w_i.astype(BF16),
                  vec(b_i), vec(lam)))


def _t5_bucket_np(dist):
    max_exact = N_BUCKETS // 2
    d_f = np.maximum(dist, 1).astype(np.float32)
    large = max_exact + (np.log(d_f / np.float32(max_exact)) / np.float32(math.log(MAX_DIST / max_exact))
                         * np.float32(N_BUCKETS - max_exact)).astype(np.int32)
    large = np.minimum(large, N_BUCKETS - 1)
    return np.where(dist < max_exact, dist, large).astype(np.int32)


def _dsw_bias_vector(rel_bias, seq, qb):
    dist = np.arange(seq)
    mult = np.zeros(seq, np.float32)
    for window, dilation in DSW_PATTERNS:
        mult += ((dist % dilation == 0) & (dist <= window)).astype(np.float32)
    logm = np.where(mult > 0, np.log(np.maximum(mult, 1.0)), NEG).astype(np.float32)
    f = (rel_bias.astype(F32)[_t5_bucket_np(dist)] + logm[:, None]) * LOG2E
    n_heads = f.shape[1]
    hv = jnp.concatenate([f[::-1].T, jnp.full((n_heads, qb), NEG, F32)], axis=1)
    return hv[:, None, :]


def _mixer_b_kernel(q_ref, k_ref, v_ref, g_ref, hv_ref, o_ref, kn_ref, vb_ref, tbl_ref):
    s, hd = q_ref.shape
    qb = tbl_ref.shape[0]

    @pl.when(pl.program_id(1) == 0)
    def _():
        skew = pltpu.roll(jnp.broadcast_to(hv_ref[...], (qb, s + qb)), 1, axis=1, stride=1, stride_axis=0)
        tbl_ref[...] = skew[:, qb:]

    kn_ref[...] = _rms_rows(k_ref[...], g_ref[1:2]).astype(BF16)
    vb_ref[...] = v_ref[...].astype(BF16)
    def logits(gi):
        n_keys = (gi + 1) * qb
        qn = (_rms_rows(q_ref[gi * qb:n_keys, :], g_ref[0:1]) * (LOG2E / math.sqrt(hd))).astype(BF16)
        return _dot_nt(qn, kn_ref[0:n_keys, :]) + tbl_ref[:, s - n_keys:]

    def weights(sc):
        m = jnp.max(sc, axis=-1, keepdims=True)
        p = jnp.exp2(sc - m)
        return p.astype(BF16), jnp.sum(p, axis=-1, keepdims=True)

    def output(gi, p, l):
        n_keys = (gi + 1) * qb
        o = _dot(p, vb_ref[0:n_keys, :])
        o_ref[gi * qb:n_keys, :] = (o / l).astype(o_ref.dtype)

    n_groups = s // qb
    held = None
    for gi in reversed(range(n_groups)):
        cur = (gi,) + weights(logits(gi))
        if held is not None:
            output(*held)
        held = cur
    output(*held)


def _mixer_b(proj, qk_g, rel_bias, col0, n_heads, cast_jobs, qb=DSW_QUERY_ROWS):
    bsz, s, _ = proj.shape
    qb = _pick_tile(s, qb)
    hv = _dsw_bias_vector(rel_bias, s, qb)
    c0 = col0 // HEAD_B
    grid = (n_heads, bsz)
    seq_spec = lambda off: pl.BlockSpec((None, s, HEAD_B), lambda h, b: (b, 0, off + h))
    return _call_with_jobs(
        _mixer_b_kernel, cast_jobs(grid),
        out_shape=jax.ShapeDtypeStruct((bsz, s, n_heads * HEAD_B), BF16),
        grid=grid,
        in_specs=[
            seq_spec(c0), seq_spec(c0 + n_heads), seq_spec(c0 + 2 * n_heads),
            pl.BlockSpec((2, HEAD_B), lambda h, b: (0, 0)),
            pl.BlockSpec((None, 1, s + qb), lambda h, b: (h, 0, 0)),
        ],
        out_specs=pl.BlockSpec((None, s, HEAD_B), lambda h, b: (b, 0, h)),
        scratch_shapes=[pltpu.VMEM((s, HEAD_B), BF16), pltpu.VMEM((s, HEAD_B), BF16),
                        pltpu.VMEM((qb, s), F32)],
        semantics=("arbitrary", "arbitrary"),
        name="mixer_b",
        operands=(proj, proj, proj, qk_g, hv))


def _mixer_c_kernel(gb_ref, q_ref, k_ref, v_ref, og_ref, cwq_ref, cwk_ref, cbq_ref, cbk_ref,
                    ig_ref, fg_ref, gain_ref, o_ref, qc_ref, kc_ref, c_ref, n_ref, m_ref):
    h = pl.program_id(1)
    L = SEQ_BLOCK
    s, d = q_ref.shape
    nc = s // L

    qc_ref[...] = jax.nn.silu(_causal_conv(q_ref[...], cwq_ref[...], cbq_ref[...]))
    kc_ref[...] = jax.nn.silu(_causal_conv(k_ref[...], cwk_ref[...], cbk_ref[...])) * (d ** -0.5)
    c_ref[...] = jnp.zeros_like(c_ref)
    n_ref[...] = jnp.zeros_like(n_ref)
    m_ref[...] = jnp.zeros_like(m_ref)

    ri = lax.broadcasted_iota(jnp.int32, (L, L), 0)
    ci = lax.broadcasted_iota(jnp.int32, (L, L), 1)
    eye = ri == ci
    causal = ci <= ri

    def to_col(row):
        return jnp.sum(jnp.where(eye, row, 0.0), axis=1, keepdims=True)

    ig_all = ig_ref[...] + gb_ref[0, h]
    f_all = fg_ref[...] + gb_ref[1, h]
    ls_all = jnp.minimum(f_all, 0.0) - jnp.log1p(jnp.exp(-jnp.abs(f_all)))

    def chunk_local(n):
        rows = slice(n * L, (n + 1) * L)
        qc = qc_ref[rows, :]
        kc = kc_ref[rows, :]
        qb = qc.astype(BF16)
        kb = kc.astype(BF16)
        vb = v_ref[rows, :].astype(BF16)

        ig_row = ig_all[n:n + 1]
        ls_row = ls_all[n:n + 1]
        ls_col = to_col(ls_row)
        ig_col = to_col(ig_row)
        b_col = jnp.sum(jnp.where(causal, ls_row, 0.0), axis=1, keepdims=True)
        b_row = jnp.sum(jnp.where(ri <= ci, ls_col, 0.0), axis=0, keepdims=True)
        b_last = jnp.sum(ls_row, axis=1, keepdims=True)

        log_d = b_col - b_row + ig_row
        g_row = b_last - b_row + ig_row
        g_col = b_last - b_col + ig_col
        g_max = jnp.max(g_row, axis=1, keepdims=True)
        d_max = jnp.max(jnp.where(causal, log_d, -jnp.inf), axis=1, keepdims=True)
        qk = _dot_nt(qb, kb)
        wk = jnp.exp(g_col - g_max) * kc
        c_loc = _dot_tn(wk.astype(BF16), vb)
        n_loc = jnp.sum(wk, axis=0, keepdims=True)
        return rows, qc, qb, vb, b_col, b_last, log_d, g_max, d_max, qk, c_loc, n_loc

    def finish(rows, hout):
        hn = _rms_rows(hout, gain_ref[...])
        o_ref[rows, :] = (jax.nn.sigmoid(og_ref[rows, :]) * hn).astype(o_ref.dtype)

    pending = None
    local = chunk_local(0)
    for n in range(nc):
        rows, qc, qb, vb, b_col, b_last, log_d, g_max, d_max, qk, c_loc, n_loc = local
        if n + 1 < nc:
            local = chunk_local(n + 1)

        c_st = c_ref[...]
        n_st = n_ref[...]
        m_st = m_ref[...]

        m_new = jnp.maximum(b_last + m_st, g_max)
        decay = jnp.exp(b_last + m_st - m_new)
        fresh = jnp.exp(g_max - m_new)
        c_ref[...] = decay * c_st + fresh * c_loc
        n_ref[...] = decay * n_st + fresh * n_loc
        m_ref[...] = m_new

        inter = b_col + m_st
        m_t = jnp.maximum(inter, d_max)
        s_qk = jnp.where(causal, qk * jnp.exp(log_d - m_t), 0.0)
        inter_w = jnp.exp(inter - m_t)
        num = _dot(s_qk.astype(BF16), vb) + inter_w * _dot(qb, c_st.astype(BF16))
        qn = jnp.sum(qc * n_st, axis=1, keepdims=True)
        den = jnp.sum(s_qk, axis=1, keepdims=True) + inter_w * qn
        hout = num / jnp.maximum(jnp.abs(den), jnp.exp(-m_t))

        if pending is not None:
            finish(*pending)
        pending = (rows, hout)
    finish(*pending)


def _mixer_c(proj, gates, conv_w, conv_b, gate_b, h_gain, n_heads, cast_jobs):
    bsz, s, _ = proj.shape
    d = HEAD_C
    L = SEQ_BLOCK
    nc = s // L
    grid = (bsz, n_heads)
    g_rows = jnp.transpose(gates, (0, 2, 1)).reshape(bsz, 2 * n_heads, nc, L)
    cb = conv_b.reshape(1, -1)
    full = lambda off: pl.BlockSpec((None, s, d), lambda b, h: (b, 0, off + h))
    row = lambda r, off: pl.BlockSpec((r, d), lambda b, h: (0, off + h))
    grow = lambda off: pl.BlockSpec((None, None, nc, L), lambda b, h: (b, off + h, 0, 0))
    return _call_with_jobs(
        _mixer_c_kernel, cast_jobs(grid),
        out_shape=jax.ShapeDtypeStruct((bsz, s, n_heads * d), BF16),
        grid=grid,
        in_specs=[
            pl.BlockSpec(memory_space=pltpu.SMEM),
            full(0), full(n_heads), full(2 * n_heads), full(3 * n_heads),
            row(CONV_W, 0), row(CONV_W, n_heads), row(1, 0), row(1, n_heads),
            grow(0), grow(n_heads), row(1, 0),
        ],
        out_specs=pl.BlockSpec((None, s, d), lambda b, h: (b, 0, h)),
        scratch_shapes=[pltpu.VMEM((s, d), F32), pltpu.VMEM((s, d), F32),
                        pltpu.VMEM((d, d), F32), pltpu.VMEM((1, d), F32), pltpu.VMEM((1, 1), F32)],
        semantics=("arbitrary", "arbitrary"),
        name="mixer_c",
        operands=(gate_b, proj, proj, proj, proj, conv_w, conv_w, cb, cb, g_rows, g_rows,
                  h_gain.reshape(1, -1)))


def _mixer_d_kernel(q_ref, k_ref, v_ref, o_ref, kb_ref, vb_ref, acc_ref, suf_ref):
    s, hd = q_ref.shape
    B = SEQ_BLOCK
    G = min(SB_QUERY_ROWS, s)
    kb_ref[...] = k_ref[...].astype(BF16)
    vb_ref[...] = v_ref[...].astype(BF16)
    ri = lax.broadcasted_iota(jnp.int32, (B, B), 0)
    ci = lax.broadcasted_iota(jnp.int32, (B, B), 1)
    after = (ri > ci).astype(BF16)
    after2 = jnp.concatenate([after, after], axis=0)
    row_g = lax.broadcasted_iota(jnp.int32, (G, B), 0)
    col_g = lax.broadcasted_iota(jnp.int32, (G, B), 1)

    def span(q0, first, last, suffix):
        n = last - first + 1
        keys = slice(first * B, (last + 1) * B)
        qb = (q_ref[q0:q0 + G, :] * (hd ** -0.5)).astype(BF16)
        z = _dot_nt(qb, kb_ref[keys, :])
        sp = jnp.log(1.0 + jnp.exp2(jnp.abs(z) * (-LOG2E)))
        nm = jnp.minimum(z, 0.0)
        log_beta = nm - sp
        log_stay = (nm - z) - sp
        before = [col_g + ((first + j) * B - q0) < row_g if (first + j + 1) * B > q0 else None
                  for j in range(n)]
        later, tot = [], []
        for j in range(n):
            x = log_stay[:, j * B:(j + 1) * B]
            if before[j] is not None:
                x = jnp.where(before[j], x, 0.0)
            hi = lax.bitcast_convert_type(lax.bitcast_convert_type(x, jnp.uint32) & HI_MASK, F32)
            hl = jnp.concatenate([hi.astype(BF16), (x - hi).astype(BF16)], axis=1)
            later.append(_dot(hl, after2))
            tot.append(jnp.sum(x, axis=-1, keepdims=True))
        att = [None] * n
        for j in range(n - 1, -1, -1):
            a = jnp.exp(log_beta[:, j * B:(j + 1) * B] + later[j] + suffix)
            if before[j] is not None:
                a = jnp.where(before[j], a, 0.0)
            att[j] = a.astype(BF16)
            suffix = suffix + tot[j]
        att = att[0] if n == 1 else jnp.concatenate(att, axis=1)
        return _dot(att, vb_ref[keys, :]), suffix

    def first_near(q0):
        return max(q0 // B - SB_NEAR_BLOCKS + 1, 0)

    worst = None
    for q0 in range(0, s, G):
        rows = slice(q0, q0 + G)
        acc, suffix = span(q0, first_near(q0), (q0 + G) // B - 1, jnp.zeros((G, 1), F32))
        o_ref[rows, :] = acc.astype(o_ref.dtype)
        if first_near(q0) > 0:
            acc_ref[rows, :] = acc
            suf_ref[rows, :] = suffix
            worst = suffix if worst is None else jnp.maximum(worst, suffix)

    if worst is not None:
        @pl.when(jnp.max(worst) > SB_UNDERFLOW)
        def _():
            for q0 in range(0, s, G):
                if first_near(q0) == 0:
                    continue
                rows = slice(q0, q0 + G)
                suffix = suf_ref[rows, :]

                @pl.when(jnp.max(suffix) > SB_UNDERFLOW)
                def _():
                    far, _ = span(q0, 0, first_near(q0) - 1, suffix)
                    o_ref[rows, :] = (acc_ref[rows, :] + far).astype(o_ref.dtype)


def _mixer_d(proj, col0, n_heads, cast_jobs):
    bsz, s, _ = proj.shape
    c0 = col0 // HEAD_D
    grid = (bsz, n_heads)
    seq_spec = lambda off: pl.BlockSpec((None, s, HEAD_D), lambda b, h: (b, 0, off + h))
    return _call_with_jobs(
        _mixer_d_kernel, cast_jobs(grid),
        out_shape=jax.ShapeDtypeStruct((bsz, s, n_heads * HEAD_D), BF16),
        grid=grid,
        in_specs=[seq_spec(c0), seq_spec(c0 + n_heads), seq_spec(c0 + 2 * n_heads)],
        out_specs=pl.BlockSpec((None, s, HEAD_D), lambda b, h: (b, 0, h)),
        scratch_shapes=[pltpu.VMEM((s, HEAD_D), BF16), pltpu.VMEM((s, HEAD_D), BF16),
                        pltpu.VMEM((s, HEAD_D), F32), pltpu.VMEM((s, 1), F32)],
        semantics=("arbitrary", "arbitrary"),
        name="mixer_d",
        operands=(proj, proj, proj))


def _layer_ab(x, bsz, s, g, w_in, conv_w, conv_b, w_r, b_r, w_i, b_i, lam, qk_g, rel_bias, w_out,
              jobs):
    d_rg = conv_w.shape[-1]
    n_heads = rel_bias.shape[-1]
    proj = _norm_matmul(x, g, w_in.astype(BF16), tn=PROJ_COL_TILE_AB).reshape(bsz, s, -1)
    ya, out_a = _mixer_a(proj, conv_w, conv_b, w_r, b_r, w_i, b_i, lam, d_rg, jobs[0])
    yb, out_b = _mixer_b(proj, qk_g, rel_bias, 2 * d_rg, n_heads, jobs[1])
    n = bsz * s
    return _out_proj(x, ya.reshape(n, -1), yb.reshape(n, -1), w_out.astype(BF16)), out_a + out_b


def _cd_layout(gate_b, w_out):
    n_heads_c = gate_b.shape[-1]
    dc = n_heads_c * HEAD_C
    dd = w_out.shape[0] - dc
    n_gate = 2 * n_heads_c
    main = 4 * dc + 3 * dd
    width = -(-(main + n_gate) // CD_WIDTH_MULTIPLE) * CD_WIDTH_MULTIPLE
    return n_heads_c, dc, dd, n_gate, main, width


def _cd_weight_job(w_in, j, gate_col, n_gate, width, tw=CD_WEIGHT_COL_TILE):
    _, d, cols = w_in.shape
    main = cols - n_gate
    assert gate_col % tw == 0 and (main - gate_col) % tw == 0 and width % tw == 0 and n_gate <= tw
    n_aligned = gate_col // tw
    n_shifted = (main - gate_col) // tw
    n_blk = width // tw
    last = (cols - 1) // tw

    def job(grid):
        assert n_blk <= math.prod(grid)
        blk = lambda *idx: jnp.minimum(_linear_step(grid, idx), n_blk - 1)

        def reorder(t, a_ref, b_ref, o_ref):
            lane = lax.broadcasted_iota(jnp.int32, o_ref.shape, 1)

            @pl.when(t < n_aligned)
            def _():
                o_ref[...] = a_ref[...].astype(o_ref.dtype)

            @pl.when((t >= n_aligned) & (t < n_aligned + n_shifted))
            def _():
                a = pltpu.roll(a_ref[...], tw - n_gate, axis=1)
                b = pltpu.roll(b_ref[...], tw - n_gate, axis=1)
                o_ref[...] = jnp.where(lane < tw - n_gate, a, b).astype(o_ref.dtype)

            @pl.when(t == n_aligned + n_shifted)
            def _():
                o_ref[...] = jnp.where(lane < n_gate, a_ref[...], 0.0).astype(o_ref.dtype)

            @pl.when((t > n_aligned + n_shifted) & (t < n_blk))
            def _():
                o_ref[...] = jnp.zeros_like(o_ref)

        a_blk = lambda *idx: jnp.where(blk(*idx) < n_aligned + n_shifted, blk(*idx), n_aligned)
        return [([w_in, w_in],
                 [pl.BlockSpec((None, d, tw), lambda *idx: (j, 0, a_blk(*idx))),
                  pl.BlockSpec((None, d, tw), lambda *idx: (j, 0, jnp.minimum(blk(*idx) + 1, last)))],
                 pl.BlockSpec((d, tw), lambda *idx: (0, blk(*idx))),
                 jax.ShapeDtypeStruct((d, width), BF16),
                 reorder)]
    return job


def _layer_cd(x, bsz, s, g, w, conv_w, conv_b, gate_b, h_gain, w_out, jobs):
    n_heads_c, dc, dd, n_gate, main, width = _cd_layout(gate_b, w_out)
    n_heads_d = dd // HEAD_D
    proj = _norm_matmul(x, g, w, tn=PROJ_COL_TILE_CD).reshape(bsz, s, width)
    gates = proj[:, :, main:main + n_gate]
    yd, out_d = _mixer_d(proj, 4 * dc, n_heads_d, jobs[0])
    yc, out_c = _mixer_c(proj, gates, conv_w, conv_b, gate_b, h_gain, n_heads_c, jobs[1])
    n = bsz * s
    return _out_proj(x, yc.reshape(n, -1), yd.reshape(n, -1), w_out.astype(BF16)), out_d + out_c


def kernel(x, norm_g, ffn_w_in, ffn_w_out, ab_w_in, ab_conv_w, ab_conv_b, rg_w_r, rg_b_r, rg_w_i,
           rg_b_i, rg_lambda, qk_gain, rel_bias, ab_w_out, cd_w_in, cd_conv_w, cd_conv_b,
           mlstm_gate_bias, mlstm_h_gain, cd_w_out):
    bsz, s, d = x.shape
    depth = norm_g.shape[0]
    h = x.reshape(bsz * s, d)
    ffn_w = {(0, 0): (ffn_w_in[0, 0].astype(BF16), ffn_w_out[0, 0].astype(BF16))}
    cd_w = None
    for layer in range(depth):
        j = layer // 2
        h = _ffn(h, norm_g[layer, 0], *ffn_w[(layer, 0)])
        targets = [(layer, 1)] + ([(layer + 1, 0)] if layer + 1 < depth else [])
        jobs = [functools.partial(_ffn_cast_jobs, ffn_w_in, ffn_w_out, *t) for t in targets]
        jobs += [lambda grid: []] * (2 - len(jobs))
        if layer % 2 == 0:
            if layer + 1 < depth:
                _, dc, _, n_gate, _, width = _cd_layout(mlstm_gate_bias[j], cd_w_out[j])
                ffn_jobs, cd_job = jobs[0], _cd_weight_job(cd_w_in, j, 4 * dc, n_gate, width)
                jobs[0] = lambda grid: ffn_jobs(grid) + cd_job(grid)
            h, out = _layer_ab(h, bsz, s, norm_g[layer, 1], ab_w_in[j], ab_conv_w[j], ab_conv_b[j],
                               rg_w_r[j], rg_b_r[j], rg_w_i[j], rg_b_i[j], rg_lambda[j], qk_gain[j],
                               rel_bias, ab_w_out[j], jobs)
            if layer + 1 < depth:
                cd_w = out.pop(2)
        else:
            h, out = _layer_cd(h, bsz, s, norm_g[layer, 1], cd_w, cd_conv_w[j], cd_conv_b[j],
                               mlstm_gate_bias[j], mlstm_h_gain[j], cd_w_out[j], jobs)
        for i, t in enumerate(targets):
            ffn_w[t] = (out[2 * i], out[2 * i + 1])
        h = _ffn(h, norm_g[layer, 2], *ffn_w[(layer, 1)])
    return h.reshape(bsz, s, d)
```

```python
import functools
import math

import numpy as np
import jax
import jax.numpy as jnp
from jax import lax
from jax.experimental import pallas as pl
from jax.experimental.pallas import tpu as pltpu

F32 = jnp.float32
BF16 = jnp.bfloat16

RMS_EPS = 1e-6
RG_C = 8.0
CONV_W = 4
RG_BLOCK = 128
HEAD_B = 128
HEAD_C = 256
HEAD_D = 128
SEQ_BLOCK = 128
SUBLANES = 8
BF16_SUBLANES = 16
DSW_PATTERNS = ((128, 1), (512, 4), (2048, 16))
N_BUCKETS = 32
MAX_DIST = 2048
NEG = -1e30
LOG2E = 1.4426950408889634
HI_MASK = np.uint32(0xFFFF0000)
SB_NEAR_BLOCKS = 3
SB_QUERY_ROWS = 256
SB_UNDERFLOW = -104.0
DSW_QUERY_ROWS = 256

FFN_TOKEN_TILE = 1024
FFN_HIDDEN_TILE = 512
PROJ_TOKEN_TILE = 1024
PROJ_COL_TILE_AB = 1280
PROJ_COL_TILE_CD = 1536
OUT_PROJ_TOKEN_TILE = 512
CD_WIDTH_MULTIPLE = 512
CD_WEIGHT_COL_TILE = 256
V7X_VMEM_BYTES = 64 * 1024 * 1024
VMEM_LIMIT = V7X_VMEM_BYTES * 7 // 8


def _cparams(sem):
    return pltpu.CompilerParams(dimension_semantics=sem, vmem_limit_bytes=VMEM_LIMIT)


def _rms_rows(x, g):
    return x * lax.rsqrt(jnp.mean(x * x, axis=-1, keepdims=True) + RMS_EPS) * g


def _dot(a, b):
    return jnp.dot(a, b, preferred_element_type=F32)


def _dot_nt(a, b):
    return lax.dot_general(a, b, (((1,), (1,)), ((), ())), preferred_element_type=F32)


def _dot_tn(a, b):
    return lax.dot_general(a, b, (((0,), (0,)), ((), ())), preferred_element_type=F32)


def _pick_tile(n, pref):
    t = min(pref, n)
    while n % t:
        t //= 2
    return t


def _linear_step(grid, idx):
    t = idx[0]
    for i, g in zip(idx[1:], grid[1:]):
        t = t * g + i
    return t


def _ffn_cast_jobs(w_in, w_out, layer, half, grid):
    n_steps = math.prod(grid)
    jobs = []
    for w in (w_in, w_out):
        rows, cols = w.shape[2:]
        n = n_steps
        while rows % n or (rows // n) % BF16_SUBLANES:
            n -= 1
        slab = rows // n
        blk = lambda *idx, n=n: jnp.minimum(_linear_step(grid, idx), n - 1)

        def cast(t, src_ref, dst_ref, n=n):
            @pl.when(t < n)
            def _():
                dst_ref[...] = src_ref[...].astype(dst_ref.dtype)

        jobs.append((
            [w],
            [pl.BlockSpec((None, None, slab, cols), lambda *idx, blk=blk: (layer, half, blk(*idx), 0))],
            pl.BlockSpec((slab, cols), lambda *idx, blk=blk: (blk(*idx), 0)),
            jax.ShapeDtypeStruct((rows, cols), BF16),
            cast,
        ))
    return jobs


def _call_with_jobs(body, jobs, *, out_shape, grid, in_specs, out_specs, scratch_shapes=(),
                    semantics, name, operands):
    n_in = len(in_specs)
    n_job_in = [len(j[0]) for j in jobs]
    n_extra = sum(n_job_in)

    def kernel(*refs):
        t = _linear_step(grid, [pl.program_id(ax) for ax in range(len(grid))])
        at = n_in
        for i, j in enumerate(jobs):
            j[4](t, *refs[at:at + n_job_in[i]], refs[n_in + n_extra + 1 + i])
            at += n_job_in[i]
        body(*refs[:n_in], refs[n_in + n_extra], *refs[n_in + n_extra + 1 + len(jobs):])

    outs = pl.pallas_call(
        kernel,
        out_shape=[out_shape] + [j[3] for j in jobs],
        grid=grid,
        in_specs=list(in_specs) + [sp for j in jobs for sp in j[1]],
        out_specs=[out_specs] + [j[2] for j in jobs],
        scratch_shapes=list(scratch_shapes),
        compiler_params=_cparams(semantics),
        name=name,
    )(*operands, *[op for j in jobs for op in j[0]])
    return outs[0], list(outs[1:])


def _ffn_kernel(x_ref, g_ref, wg_ref, wu_ref, wo_ref, o_ref, xn_ref):
    j = pl.program_id(1)

    @pl.when(j == 0)
    def _():
        x = x_ref[...]
        xn_ref[...] = _rms_rows(x, g_ref[...]).astype(BF16)
        o_ref[...] = x

    xn = xn_ref[...]
    hg = _dot(xn, wg_ref[...])
    hu = _dot(xn, wu_ref[...])
    h = (0.5 * hg * jax.nn.sigmoid(hg) * hu).astype(BF16)
    o_ref[...] += _dot(h, wo_ref[...])


def _ffn(x, g, w_in, w_out, tm=FFN_TOKEN_TILE, tf=FFN_HIDDEN_TILE):
    n, d = x.shape
    d_ff = w_out.shape[0]
    tm = _pick_tile(n, tm)
    tf = _pick_tile(d_ff, tf)
    nj = d_ff // tf
    return pl.pallas_call(
        _ffn_kernel,
        out_shape=jax.ShapeDtypeStruct((n, d), F32),
        grid=(n // tm, nj),
        in_specs=[
            pl.BlockSpec((tm, d), lambda i, j: (i, 0)),
            pl.BlockSpec((1, d), lambda i, j: (0, 0)),
            pl.BlockSpec((d, tf), lambda i, j: (0, j)),
            pl.BlockSpec((d, tf), lambda i, j: (0, j + nj)),
            pl.BlockSpec((tf, d), lambda i, j: (j, 0)),
        ],
        out_specs=pl.BlockSpec((tm, d), lambda i, j: (i, 0)),
        scratch_shapes=[pltpu.VMEM((tm, d), BF16)],
        compiler_params=_cparams(("parallel", "arbitrary")),
        name="ffn",
    )(x, g.reshape(1, d), w_in, w_in, w_out)


def _norm_matmul_kernel(x_ref, g_ref, w_ref, o_ref, xn_ref):
    @pl.when(pl.program_id(1) == 0)
    def _():
        xn_ref[...] = _rms_rows(x_ref[...], g_ref[...]).astype(BF16)

    o_ref[...] = _dot(xn_ref[...], w_ref[...])


def _norm_matmul(x, g, w, tn, tm=PROJ_TOKEN_TILE):
    n, d = x.shape
    wn = w.shape[1]
    tm = _pick_tile(n, tm)
    while wn % tn:
        tn -= 128
    return pl.pallas_call(
        _norm_matmul_kernel,
        out_shape=jax.ShapeDtypeStruct((n, wn), F32),
        grid=(n // tm, wn // tn),
        in_specs=[
            pl.BlockSpec((tm, d), lambda i, j: (i, 0)),
            pl.BlockSpec((1, d), lambda i, j: (0, 0)),
            pl.BlockSpec((d, tn), lambda i, j: (0, j)),
        ],
        out_specs=pl.BlockSpec((tm, tn), lambda i, j: (i, j)),
        scratch_shapes=[pltpu.VMEM((tm, d), BF16)],
        compiler_params=_cparams(("parallel", "arbitrary")),
        name="norm_matmul",
    )(x, g.reshape(1, d), w)


def _out_proj_kernel(x_ref, ya_ref, yb_ref, wa_ref, wb_ref, o_ref):
    o_ref[...] = x_ref[...] + _dot(ya_ref[...], wa_ref[...]) + _dot(yb_ref[...], wb_ref[...])


def _out_proj(x, ya, yb, w, tm=OUT_PROJ_TOKEN_TILE):
    n, d = x.shape
    da = ya.shape[1]
    db = yb.shape[1]
    assert da == db and w.shape[0] == da + db
    tm = _pick_tile(n, tm)
    return pl.pallas_call(
        _out_proj_kernel,
        out_shape=jax.ShapeDtypeStruct((n, d), F32),
        grid=(n // tm,),
        in_specs=[
            pl.BlockSpec((tm, d), lambda i: (i, 0)),
            pl.BlockSpec((tm, da), lambda i: (i, 0)),
            pl.BlockSpec((tm, db), lambda i: (i, 0)),
            pl.BlockSpec((da, d), lambda i: (0, 0)),
            pl.BlockSpec((db, d), lambda i: (1, 0)),
        ],
        out_specs=pl.BlockSpec((tm, d), lambda i: (i, 0)),
        compiler_params=_cparams(("parallel",)),
        name="out_proj",
    )(x, ya, yb, w, w)


def _causal_conv(x, w, b):
    y = x * w[CONV_W - 1:CONV_W] + b
    for k in range(1, CONV_W):
        y = y + _shift_rows(x, k, 0.0) * w[CONV_W - 1 - k:CONV_W - k]
    return y


def _shift_rows(x, k, fill):
    assert 0 < k < SUBLANES
    rolled = pltpu.roll(x, k, axis=0)
    t = lax.broadcasted_iota(jnp.int32, (SUBLANES, x.shape[1]), 0)
    head = jnp.where(t >= k, rolled[:SUBLANES], fill)
    return jnp.concatenate([head, rolled[SUBLANES:]], axis=0)


def _mixer_a_kernel(xr_ref, gate_ref, cw_ref, cb_ref, wr_ref, br_ref, wi_ref, bi_ref,
                    lam_ref, o_ref, a_ref, u_ref):
    s = xr_ref.shape[0]
    xc = _causal_conv(xr_ref[...], cw_ref[...], cb_ref[...])
    xb = xc.astype(BF16)
    r = jax.nn.sigmoid(_dot(xb, wr_ref[...]) + br_ref[...])
    i = jax.nn.sigmoid(_dot(xb, wi_ref[...]) + bi_ref[...])
    neg_lam = -lam_ref[...]
    softplus = jnp.maximum(neg_lam, 0.0) + jnp.log1p(jnp.exp(-jnp.abs(neg_lam)))
    log_a = -RG_C * r * softplus
    a = jnp.exp(log_a)
    a_ref[...] = a
    u_ref[...] = jnp.sqrt(-jnp.tanh(log_a) * (a * a + 1.0)) * (i * xc)
    B = SEQ_BLOCK
    carry = None
    for c in range(s // B):
        rows = slice(c * B, (c + 1) * B)
        a = a_ref[rows, :]
        u = u_ref[rows, :]
        k = 1
        while k < B:
            if k < SUBLANES:
                u = a * _shift_rows(u, k, 0.0) + u
                a = a * _shift_rows(a, k, 1.0)
            else:
                u = jnp.concatenate([u[:k], a[k:] * u[:B - k] + u[k:]], axis=0)
                a = jnp.concatenate([a[:k], a[k:] * a[:B - k]], axis=0)
            k *= 2
        if carry is not None:
            u = a * carry + u
        carry = u[B - 1:B, :]
        o_ref[rows, :] = (jax.nn.gelu(gate_ref[rows, :]) * u).astype(o_ref.dtype)


def _mixer_a(proj, conv_w, conv_b, w_r, b_r, w_i, b_i, lam, d_rg, cast_jobs):
    bsz, s, _ = proj.shape
    nb = d_rg // RG_BLOCK
    grid = (bsz, nb)
    vec = lambda v: v.reshape(1, d_rg)
    col = lambda off: pl.BlockSpec((None, s, RG_BLOCK), lambda b, g: (b, 0, off + g))
    row = lambda r: pl.BlockSpec((r, RG_BLOCK), lambda b, g: (0, g))
    blk = pl.BlockSpec((None, RG_BLOCK, RG_BLOCK), lambda b, g: (g, 0, 0))
    return _call_with_jobs(
        _mixer_a_kernel, cast_jobs(grid),
        out_shape=jax.ShapeDtypeStruct((bsz, s, d_rg), BF16),
        grid=grid,
        in_specs=[col(nb), col(0), row(CONV_W), row(1), blk, row(1), blk, row(1), row(1)],
        out_specs=pl.BlockSpec((None, s, RG_BLOCK), lambda b, g: (b, 0, g)),
        scratch_shapes=[pltpu.VMEM((s, RG_BLOCK), F32), pltpu.VMEM((s, RG_BLOCK), F32)],
        semantics=("arbitrary", "arbitrary"),
        name="mixer_a",
        operands=(proj, proj, conv_w, vec(conv_b), w_r.astype(BF16), vec(b_r), w_i.astype(BF16),
                  vec(b_i), vec(lam)))


def _t5_bucket_np(dist):
    max_exact = N_BUCKETS // 2
    d_f = np.maximum(dist, 1).astype(np.float32)
    large = max_exact + (np.log(d_f / np.float32(max_exact)) / np.float32(math.log(MAX_DIST / max_exact))
                         * np.float32(N_BUCKETS - max_exact)).astype(np.int32)
    large = np.minimum(large, N_BUCKETS - 1)
    return np.where(dist < max_exact, dist, large).astype(np.int32)


def _dsw_bias_vector(rel_bias, seq, qb):
    dist = np.arange(seq)
    mult = np.zeros(seq, np.float32)
    for window, dilation in DSW_PATTERNS:
        mult += ((dist % dilation == 0) & (dist <= window)).astype(np.float32)
    logm = np.where(mult > 0, np.log(np.maximum(mult, 1.0)), NEG).astype(np.float32)
    f = (rel_bias.astype(F32)[_t5_bucket_np(dist)] + logm[:, None]) * LOG2E
    n_heads = f.shape[1]
    hv = jnp.concatenate([f[::-1].T, jnp.full((n_heads, qb), NEG, F32)], axis=1)
    return hv[:, None, :]


def _mixer_b_kernel(q_ref, k_ref, v_ref, g_ref, hv_ref, o_ref, kn_ref, vb_ref, tbl_ref):
    s, hd = q_ref.shape
    qb = tbl_ref.shape[0]

    @pl.when(pl.program_id(1) == 0)
    def _():
        skew = pltpu.roll(jnp.broadcast_to(hv_ref[...], (qb, s + qb)), 1, axis=1, stride=1, stride_axis=0)
        tbl_ref[...] = skew[:, qb:]

    kn_ref[...] = _rms_rows(k_ref[...], g_ref[1:2]).astype(BF16)
    vb_ref[...] = v_ref[...].astype(BF16)
    def logits(gi):
        n_keys = (gi + 1) * qb
        qn = (_rms_rows(q_ref[gi * qb:n_keys, :], g_ref[0:1]) * (LOG2E / math.sqrt(hd))).astype(BF16)
        return _dot_nt(qn, kn_ref[0:n_keys, :]) + tbl_ref[:, s - n_keys:]

    def weights(sc):
        m = jnp.max(sc, axis=-1, keepdims=True)
        p = jnp.exp2(sc - m)
        return p.astype(BF16), jnp.sum(p, axis=-1, keepdims=True)

    def output(gi, p, l):
        n_keys = (gi + 1) * qb
        o = _dot(p, vb_ref[0:n_keys, :])
        o_ref[gi * qb:n_keys, :] = (o / l).astype(o_ref.dtype)

    n_groups = s // qb
    held = None
    for gi in reversed(range(n_groups)):
        cur = (gi,) + weights(logits(gi))
        if held is not None:
            output(*held)
        held = cur
    output(*held)


def _mixer_b(proj, qk_g, rel_bias, col0, n_heads, cast_jobs, qb=DSW_QUERY_ROWS):
    bsz, s, _ = proj.shape
    qb = _pick_tile(s, qb)
    hv = _dsw_bias_vector(rel_bias, s, qb)
    c0 = col0 // HEAD_B
    grid = (n_heads, bsz)
    seq_spec = lambda off: pl.BlockSpec((None, s, HEAD_B), lambda h, b: (b, 0, off + h))
    return _call_with_jobs(
        _mixer_b_kernel, cast_jobs(grid),
        out_shape=jax.ShapeDtypeStruct((bsz, s, n_heads * HEAD_B), BF16),
        grid=grid,
        in_specs=[
            seq_spec(c0), seq_spec(c0 + n_heads), seq_spec(c0 + 2 * n_heads),
            pl.BlockSpec((2, HEAD_B), lambda h, b: (0, 0)),
            pl.BlockSpec((None, 1, s + qb), lambda h, b: (h, 0, 0)),
        ],
        out_specs=pl.BlockSpec((None, s, HEAD_B), lambda h, b: (b, 0, h)),
        scratch_shapes=[pltpu.VMEM((s, HEAD_B), BF16), pltpu.VMEM((s, HEAD_B), BF16),
                        pltpu.VMEM((qb, s), F32)],
        semantics=("arbitrary", "arbitrary"),
        name="mixer_b",
        operands=(proj, proj, proj, qk_g, hv))


def _mixer_c_kernel(gb_ref, q_ref, k_ref, v_ref, og_ref, cwq_ref, cwk_ref, cbq_ref, cbk_ref,
                    ig_ref, fg_ref, gain_ref, o_ref, qc_ref, kc_ref, c_ref, n_ref, m_ref):
    h = pl.program_id(1)
    L = SEQ_BLOCK
    s, d = q_ref.shape
    nc = s // L

    qc_ref[...] = jax.nn.silu(_causal_conv(q_ref[...], cwq_ref[...], cbq_ref[...]))
    kc_ref[...] = jax.nn.silu(_causal_conv(k_ref[...], cwk_ref[...], cbk_ref[...])) * (d ** -0.5)
    c_ref[...] = jnp.zeros_like(c_ref)
    n_ref[...] = jnp.zeros_like(n_ref)
    m_ref[...] = jnp.zeros_like(m_ref)

    ri = lax.broadcasted_iota(jnp.int32, (L, L), 0)
    ci = lax.broadcasted_iota(jnp.int32, (L, L), 1)
    eye = ri == ci
    causal = ci <= ri

    def to_col(row):
        return jnp.sum(jnp.where(eye, row, 0.0), axis=1, keepdims=True)

    ig_all = ig_ref[...] + gb_ref[0, h]
    f_all = fg_ref[...] + gb_ref[1, h]
    ls_all = jnp.minimum(f_all, 0.0) - jnp.log1p(jnp.exp(-jnp.abs(f_all)))

    def chunk_local(n):
        rows = slice(n * L, (n + 1) * L)
        qc = qc_ref[rows, :]
        kc = kc_ref[rows, :]
        qb = qc.astype(BF16)
        kb = kc.astype(BF16)
        vb = v_ref[rows, :].astype(BF16)

        ig_row = ig_all[n:n + 1]
        ls_row = ls_all[n:n + 1]
        ls_col = to_col(ls_row)
        ig_col = to_col(ig_row)
        b_col = jnp.sum(jnp.where(causal, ls_row, 0.0), axis=1, keepdims=True)
        b_row = jnp.sum(jnp.where(ri <= ci, ls_col, 0.0), axis=0, keepdims=True)
        b_last = jnp.sum(ls_row, axis=1, keepdims=True)

        log_d = b_col - b_row + ig_row
        g_row = b_last - b_row + ig_row
        g_col = b_last - b_col + ig_col
        g_max = jnp.max(g_row, axis=1, keepdims=True)
        d_max = jnp.max(jnp.where(causal, log_d, -jnp.inf), axis=1, keepdims=True)
        qk = _dot_nt(qb, kb)
        wk = jnp.exp(g_col - g_max) * kc
        c_loc = _dot_tn(wk.astype(BF16), vb)
        n_loc = jnp.sum(wk, axis=0, keepdims=True)
        return rows, qc, qb, vb, b_col, b_last, log_d, g_max, d_max, qk, c_loc, n_loc

    def finish(rows, hout):
        hn = _rms_rows(hout, gain_ref[...])
        o_ref[rows, :] = (jax.nn.sigmoid(og_ref[rows, :]) * hn).astype(o_ref.dtype)

    pending = None
    local = chunk_local(0)
    for n in range(nc):
        rows, qc, qb, vb, b_col, b_last, log_d, g_max, d_max, qk, c_loc, n_loc = local
        if n + 1 < nc:
            local = chunk_local(n + 1)

        c_st = c_ref[...]
        n_st = n_ref[...]
        m_st = m_ref[...]

        m_new = jnp.maximum(b_last + m_st, g_max)
        decay = jnp.exp(b_last + m_st - m_new)
        fresh = jnp.exp(g_max - m_new)
        c_ref[...] = decay * c_st + fresh * c_loc
        n_ref[...] = decay * n_st + fresh * n_loc
        m_ref[...] = m_new

        inter = b_col + m_st
        m_t = jnp.maximum(inter, d_max)
        s_qk = jnp.where(causal, qk * jnp.exp(log_d - m_t), 0.0)
        inter_w = jnp.exp(inter - m_t)
        num = _dot(s_qk.astype(BF16), vb) + inter_w * _dot(qb, c_st.astype(BF16))
        qn = jnp.sum(qc * n_st, axis=1, keepdims=True)
        den = jnp.sum(s_qk, axis=1, keepdims=True) + inter_w * qn
        hout = num / jnp.maximum(jnp.abs(den), jnp.exp(-m_t))

        if pending is not None:
            finish(*pending)
        pending = (rows, hout)
    finish(*pending)


def _mixer_c(proj, gates, conv_w, conv_b, gate_b, h_gain, n_heads, cast_jobs):
    bsz, s, _ = proj.shape
    d = HEAD_C
    L = SEQ_BLOCK
    nc = s // L
    grid = (bsz, n_heads)
    g_rows = jnp.transpose(gates, (0, 2, 1)).reshape(bsz, 2 * n_heads, nc, L)
    cb = conv_b.reshape(1, -1)
    full = lambda off: pl.BlockSpec((None, s, d), lambda b, h: (b, 0, off + h))
    row = lambda r, off: pl.BlockSpec((r, d), lambda b, h: (0, off + h))
    grow = lambda off: pl.BlockSpec((None, None, nc, L), lambda b, h: (b, off + h, 0, 0))
    return _call_with_jobs(
        _mixer_c_kernel, cast_jobs(grid),
        out_shape=jax.ShapeDtypeStruct((bsz, s, n_heads * d), BF16),
        grid=grid,
        in_specs=[
            pl.BlockSpec(memory_space=pltpu.SMEM),
            full(0), full(n_heads), full(2 * n_heads), full(3 * n_heads),
            row(CONV_W, 0), row(CONV_W, n_heads), row(1, 0), row(1, n_heads),
            grow(0), grow(n_heads), row(1, 0),
        ],
        out_specs=pl.BlockSpec((None, s, d), lambda b, h: (b, 0, h)),
        scratch_shapes=[pltpu.VMEM((s, d), F32), pltpu.VMEM((s, d), F32),
                        pltpu.VMEM((d, d), F32), pltpu.VMEM((1, d), F32), pltpu.VMEM((1, 1), F32)],
        semantics=("arbitrary", "arbitrary"),
        name="mixer_c",
        operands=(gate_b, proj, proj, proj, proj, conv_w, conv_w, cb, cb, g_rows, g_rows,
                  h_gain.reshape(1, -1)))


def _mixer_d_kernel(q_ref, k_ref, v_ref, o_ref, kb_ref, vb_ref, acc_ref, suf_ref):
    s, hd = q_ref.shape
    B = SEQ_BLOCK
    G = min(SB_QUERY_ROWS, s)
    kb_ref[...] = k_ref[...].astype(BF16)
    vb_ref[...] = v_ref[...].astype(BF16)
    ri = lax.broadcasted_iota(jnp.int32, (B, B), 0)
    ci = lax.broadcasted_iota(jnp.int32, (B, B), 1)
    after = (ri > ci).astype(BF16)
    after2 = jnp.concatenate([after, after], axis=0)
    row_g = lax.broadcasted_iota(jnp.int32, (G, B), 0)
    col_g = lax.broadcasted_iota(jnp.int32, (G, B), 1)

    def span_stages(q0, first, last, suffix):
        n = last - first + 1
        keys = slice(first * B, (last + 1) * B)
        qb = (q_ref[q0:q0 + G, :] * (hd ** -0.5)).astype(BF16)
        z = _dot_nt(qb, kb_ref[keys, :])
        sp = jnp.log(1.0 + jnp.exp2(jnp.abs(z) * (-LOG2E)))
        nm = jnp.minimum(z, 0.0)
        log_beta = nm - sp
        log_stay = (nm - z) - sp
        before = [col_g + ((first + j) * B - q0) < row_g if (first + j + 1) * B > q0 else None
                  for j in range(n)]
        hl, tot = [], []
        for j in range(n):
            x = log_stay[:, j * B:(j + 1) * B]
            if before[j] is not None:
                x = jnp.where(before[j], x, 0.0)
            hi = lax.bitcast_convert_type(lax.bitcast_convert_type(x, jnp.uint32) & HI_MASK, F32)
            hl.append(jnp.concatenate([hi.astype(BF16), (x - hi).astype(BF16)], axis=1))
            tot.append(jnp.sum(x, axis=-1, keepdims=True))
        yield
        later = [_dot(h, after2) for h in hl]
        yield
        att = [None] * n
        for j in range(n - 1, -1, -1):
            a = jnp.exp(log_beta[:, j * B:(j + 1) * B] + later[j] + suffix)
            if before[j] is not None:
                a = jnp.where(before[j], a, 0.0)
            att[j] = a.astype(BF16)
            suffix = suffix + tot[j]
        att = att[0] if n == 1 else jnp.concatenate(att, axis=1)
        return _dot(att, vb_ref[keys, :]), suffix

    def step(gen):
        try:
            next(gen)
        except StopIteration as done:
            return done.value
        return None

    def span(q0, first, last, suffix):
        gen = span_stages(q0, first, last, suffix)
        done = None
        while done is None:
            done = step(gen)
        return done

    def first_near(q0):
        return max(q0 // B - SB_NEAR_BLOCKS + 1, 0)

    worst = None
    starts = list(range(0, s, G))
    spans = [span_stages(q0, first_near(q0), (q0 + G) // B - 1, jnp.zeros((G, 1), F32)) for q0 in starts]
    for t in range(len(spans) + 2):
        for i in (t - 2, t - 1, t):
            if not 0 <= i < len(spans):
                continue
            done = step(spans[i])
            if done is None:
                continue
            q0 = starts[i]
            rows = slice(q0, q0 + G)
            acc, suffix = done
            o_ref[rows, :] = acc.astype(o_ref.dtype)
            if first_near(q0) > 0:
                acc_ref[rows, :] = acc
                suf_ref[rows, :] = suffix
                worst = suffix if worst is None else jnp.maximum(worst, suffix)

    if worst is not None:
        @pl.when(jnp.max(worst) > SB_UNDERFLOW)
        def _():
            for q0 in range(0, s, G):
                if first_near(q0) == 0:
                    continue
                rows = slice(q0, q0 + G)
                suffix = suf_ref[rows, :]

                @pl.when(jnp.max(suffix) > SB_UNDERFLOW)
                def _():
                    far, _ = span(q0, 0, first_near(q0) - 1, suffix)
                    o_ref[rows, :] = (acc_ref[rows, :] + far).astype(o_ref.dtype)


def _mixer_d(proj, col0, n_heads, cast_jobs):
    bsz, s, _ = proj.shape
    c0 = col0 // HEAD_D
    grid = (bsz, n_heads)
    seq_spec = lambda off: pl.BlockSpec((None, s, HEAD_D), lambda b, h: (b, 0, off + h))
    return _call_with_jobs(
        _mixer_d_kernel, cast_jobs(grid),
        out_shape=jax.ShapeDtypeStruct((bsz, s, n_heads * HEAD_D), BF16),
        grid=grid,
        in_specs=[seq_spec(c0), seq_spec(c0 + n_heads), seq_spec(c0 + 2 * n_heads)],
        out_specs=pl.BlockSpec((None, s, HEAD_D), lambda b, h: (b, 0, h)),
        scratch_shapes=[pltpu.VMEM((s, HEAD_D), BF16), pltpu.VMEM((s, HEAD_D), BF16),
                        pltpu.VMEM((s, HEAD_D), F32), pltpu.VMEM((s, 1), F32)],
        semantics=("arbitrary", "arbitrary"),
        name="mixer_d",
        operands=(proj, proj, proj))


def _layer_ab(x, bsz, s, g, w_in, conv_w, conv_b, w_r, b_r, w_i, b_i, lam, qk_g, rel_bias, w_out,
              jobs):
    d_rg = conv_w.shape[-1]
    n_heads = rel_bias.shape[-1]
    proj = _norm_matmul(x, g, w_in.astype(BF16), tn=PROJ_COL_TILE_AB).reshape(bsz, s, -1)
    ya, out_a = _mixer_a(proj, conv_w, conv_b, w_r, b_r, w_i, b_i, lam, d_rg, jobs[0])
    yb, out_b = _mixer_b(proj, qk_g, rel_bias, 2 * d_rg, n_heads, jobs[1])
    n = bsz * s
    return _out_proj(x, ya.reshape(n, -1), yb.reshape(n, -1), w_out.astype(BF16)), out_a + out_b


def _cd_layout(gate_b, w_out):
    n_heads_c = gate_b.shape[-1]
    dc = n_heads_c * HEAD_C
    dd = w_out.shape[0] - dc
    n_gate = 2 * n_heads_c
    main = 4 * dc + 3 * dd
    width = -(-(main + n_gate) // CD_WIDTH_MULTIPLE) * CD_WIDTH_MULTIPLE
    return n_heads_c, dc, dd, n_gate, main, width


def _cd_weight_job(w_in, j, gate_col, n_gate, width, tw=CD_WEIGHT_COL_TILE):
    _, d, cols = w_in.shape
    main = cols - n_gate
    assert gate_col % tw == 0 and (main - gate_col) % tw == 0 and width % tw == 0 and n_gate <= tw
    n_aligned = gate_col // tw
    n_shifted = (main - gate_col) // tw
    n_blk = width // tw
    last = (cols - 1) // tw

    def job(grid):
        assert n_blk <= math.prod(grid)
        blk = lambda *idx: jnp.minimum(_linear_step(grid, idx), n_blk - 1)

        def reorder(t, a_ref, b_ref, o_ref):
            lane = lax.broadcasted_iota(jnp.int32, o_ref.shape, 1)

            @pl.when(t < n_aligned)
            def _():
                o_ref[...] = a_ref[...].astype(o_ref.dtype)

            @pl.when((t >= n_aligned) & (t < n_aligned + n_shifted))
            def _():
                a = pltpu.roll(a_ref[...], tw - n_gate, axis=1)
                b = pltpu.roll(b_ref[...], tw - n_gate, axis=1)
                o_ref[...] = jnp.where(lane < tw - n_gate, a, b).astype(o_ref.dtype)

            @pl.when(t == n_aligned + n_shifted)
            def _():
                o_ref[...] = jnp.where(lane < n_gate, a_ref[...], 0.0).astype(o_ref.dtype)

            @pl.when((t > n_aligned + n_shifted) & (t < n_blk))
            def _():
                o_ref[...] = jnp.zeros_like(o_ref)

        a_blk = lambda *idx: jnp.where(blk(*idx) < n_aligned + n_shifted, blk(*idx), n_aligned)
        return [([w_in, w_in],
                 [pl.BlockSpec((None, d, tw), lambda *idx: (j, 0, a_blk(*idx))),
                  pl.BlockSpec((None, d, tw), lambda *idx: (j, 0, jnp.minimum(blk(*idx) + 1, last)))],
                 pl.BlockSpec((d, tw), lambda *idx: (0, blk(*idx))),
                 jax.ShapeDtypeStruct((d, width), BF16),
                 reorder)]
    return job


def _layer_cd(x, bsz, s, g, w, conv_w, conv_b, gate_b, h_gain, w_out, jobs):
    n_heads_c, dc, dd, n_gate, main, width = _cd_layout(gate_b, w_out)
    n_heads_d = dd // HEAD_D
    proj = _norm_matmul(x, g, w, tn=PROJ_COL_TILE_CD).reshape(bsz, s, width)
    gates = proj[:, :, main:main + n_gate]
    yd, out_d = _mixer_d(proj, 4 * dc, n_heads_d, jobs[0])
    yc, out_c = _mixer_c(proj, gates, conv_w, conv_b, gate_b, h_gain, n_heads_c, jobs[1])
    n = bsz * s
    return _out_proj(x, yc.reshape(n, -1), yd.reshape(n, -1), w_out.astype(BF16)), out_d + out_c


def kernel(x, norm_g, ffn_w_in, ffn_w_out, ab_w_in, ab_conv_w, ab_conv_b, rg_w_r, rg_b_r, rg_w_i,
           rg_b_i, rg_lambda, qk_gain, rel_bias, ab_w_out, cd_w_in, cd_conv_w, cd_conv_b,
           mlstm_gate_bias, mlstm_h_gain, cd_w_out):
    bsz, s, d = x.shape
    depth = norm_g.shape[0]
    h = x.reshape(bsz * s, d)
    ffn_w = {(0, 0): (ffn_w_in[0, 0].astype(BF16), ffn_w_out[0, 0].astype(BF16))}
    cd_w = None
    for layer in range(depth):
        j = layer // 2
        h = _ffn(h, norm_g[layer, 0], *ffn_w[(layer, 0)])
        targets = [(layer, 1)] + ([(layer + 1, 0)] if layer + 1 < depth else [])
        jobs = [functools.partial(_ffn_cast_jobs, ffn_w_in, ffn_w_out, *t) for t in targets]
        jobs += [lambda grid: []] * (2 - len(jobs))
        if layer % 2 == 0:
            if layer + 1 < depth:
                _, dc, _, n_gate, _, width = _cd_layout(mlstm_gate_bias[j], cd_w_out[j])
                ffn_jobs, cd_job = jobs[0], _cd_weight_job(cd_w_in, j, 4 * dc, n_gate, width)
                jobs[0] = lambda grid: ffn_jobs(grid) + cd_job(grid)
            h, out = _layer_ab(h, bsz, s, norm_g[layer, 1], ab_w_in[j], ab_conv_w[j], ab_conv_b[j],
                               rg_w_r[j], rg_b_r[j], rg_w_i[j], rg_b_i[j], rg_lambda[j], qk_gain[j],
                               rel_bias, ab_w_out[j], jobs)
            if layer + 1 < depth:
                cd_w = out.pop(2)
        else:
            h, out = _layer_cd(h, bsz, s, norm_g[layer, 1], cd_w, cd_conv_w[j], cd_conv_b[j],
                               mlstm_gate_bias[j], mlstm_h_gain[j], cd_w_out[j], jobs)
        for i, t in enumerate(targets):
            ffn_w[t] = (out[2 * i], out[2 * i + 1])
        h = _ffn(h, norm_g[layer, 2], *ffn_w[(layer, 1)])
    return h.reshape(bsz, s, d)
```

```python
import functools
import math

import numpy as np
import jax
import jax.numpy as jnp
from jax import lax
from jax.experimental import pallas as pl
from jax.experimental.pallas import tpu as pltpu

F32 = jnp.float32
BF16 = jnp.bfloat16

RMS_EPS = 1e-6
RG_C = 8.0
CONV_W = 4
RG_BLOCK = 128
HEAD_B = 128
HEAD_C = 256
HEAD_D = 128
SEQ_BLOCK = 128
SUBLANES = 8
BF16_SUBLANES = 16
DSW_PATTERNS = ((128, 1), (512, 4), (2048, 16))
N_BUCKETS = 32
MAX_DIST = 2048
NEG = -1e30
LOG2E = 1.4426950408889634
HI_MASK = np.uint32(0xFFFF0000)
SB_NEAR_BLOCKS = 3
SB_QUERY_ROWS = 256
SB_UNDERFLOW = -104.0
DSW_QUERY_ROWS = 256

FFN_TOKEN_TILE = 1024
FFN_HIDDEN_TILE = 512
PROJ_TOKEN_TILE = 1024
PROJ_COL_TILE_AB = 1280
PROJ_COL_TILE_CD = 1536
OUT_PROJ_TOKEN_TILE = 512
CD_WIDTH_MULTIPLE = 512
CD_WEIGHT_COL_TILE = 256
V7X_VMEM_BYTES = 64 * 1024 * 1024
VMEM_LIMIT = V7X_VMEM_BYTES * 7 // 8


def _cparams(sem):
    return pltpu.CompilerParams(dimension_semantics=sem, vmem_limit_bytes=VMEM_LIMIT)


def _rms_rows(x, g):
    return x * lax.rsqrt(jnp.mean(x * x, axis=-1, keepdims=True) + RMS_EPS) * g


def _dot(a, b):
    return jnp.dot(a, b, preferred_element_type=F32)


def _dot_nt(a, b):
    return lax.dot_general(a, b, (((1,), (1,)), ((), ())), preferred_element_type=F32)


def _dot_tn(a, b):
    return lax.dot_general(a, b, (((0,), (0,)), ((), ())), preferred_element_type=F32)


def _pick_tile(n, pref):
    t = min(pref, n)
    while n % t:
        t //= 2
    return t


def _linear_step(grid, idx):
    t = idx[0]
    for i, g in zip(idx[1:], grid[1:]):
        t = t * g + i
    return t


def _ffn_cast_jobs(w_in, w_out, layer, half, grid):
    n_steps = math.prod(grid)
    jobs = []
    for w in (w_in, w_out):
        rows, cols = w.shape[2:]
        n = n_steps
        while rows % n or (rows // n) % BF16_SUBLANES:
            n -= 1
        slab = rows // n
        blk = lambda *idx, n=n: jnp.minimum(_linear_step(grid, idx), n - 1)

        def cast(t, src_ref, dst_ref, n=n):
            @pl.when(t < n)
            def _():
                dst_ref[...] = src_ref[...].astype(dst_ref.dtype)

        jobs.append((
            [w],
            [pl.BlockSpec((None, None, slab, cols), lambda *idx, blk=blk: (layer, half, blk(*idx), 0))],
            pl.BlockSpec((slab, cols), lambda *idx, blk=blk: (blk(*idx), 0)),
            jax.ShapeDtypeStruct((rows, cols), BF16),
            cast,
        ))
    return jobs


def _call_with_jobs(body, jobs, *, out_shape, grid, in_specs, out_specs, scratch_shapes=(),
                    semantics, name, operands):
    n_in = len(in_specs)
    n_job_in = [len(j[0]) for j in jobs]
    n_extra = sum(n_job_in)

    def kernel(*refs):
        t = _linear_step(grid, [pl.program_id(ax) for ax in range(len(grid))])
        at = n_in
        for i, j in enumerate(jobs):
            j[4](t, *refs[at:at + n_job_in[i]], refs[n_in + n_extra + 1 + i])
            at += n_job_in[i]
        body(*refs[:n_in], refs[n_in + n_extra], *refs[n_in + n_extra + 1 + len(jobs):])

    outs = pl.pallas_call(
        kernel,
        out_shape=[out_shape] + [j[3] for j in jobs],
        grid=grid,
        in_specs=list(in_specs) + [sp for j in jobs for sp in j[1]],
        out_specs=[out_specs] + [j[2] for j in jobs],
        scratch_shapes=list(scratch_shapes),
        compiler_params=_cparams(semantics),
        name=name,
    )(*operands, *[op for j in jobs for op in j[0]])
    return outs[0], list(outs[1:])


def _ffn_kernel(x_ref, g_ref, wg_ref, wu_ref, wo_ref, o_ref, xn_ref):
    j = pl.program_id(1)

    @pl.when(j == 0)
    def _():
        x = x_ref[...]
        xn_ref[...] = _rms_rows(x, g_ref[...]).astype(BF16)
        o_ref[...] = x

    xn = xn_ref[...]
    hg = _dot(xn, wg_ref[...])
    hu = _dot(xn, wu_ref[...])
    h = (0.5 * hg * jax.nn.sigmoid(hg) * hu).astype(BF16)
    o_ref[...] += _dot(h, wo_ref[...])


def _ffn(x, g, w_in, w_out, tm=FFN_TOKEN_TILE, tf=FFN_HIDDEN_TILE):
    n, d = x.shape
    d_ff = w_out.shape[0]
    tm = _pick_tile(n, tm)
    tf = _pick_tile(d_ff, tf)
    nj = d_ff // tf
    return pl.pallas_call(
        _ffn_kernel,
        out_shape=jax.ShapeDtypeStruct((n, d), F32),
        grid=(n // tm, nj),
        in_specs=[
            pl.BlockSpec((tm, d), lambda i, j: (i, 0)),
            pl.BlockSpec((1, d), lambda i, j: (0, 0)),
            pl.BlockSpec((d, tf), lambda i, j: (0, j)),
            pl.BlockSpec((d, tf), lambda i, j: (0, j + nj)),
            pl.BlockSpec((tf, d), lambda i, j: (j, 0)),
        ],
        out_specs=pl.BlockSpec((tm, d), lambda i, j: (i, 0)),
        scratch_shapes=[pltpu.VMEM((tm, d), BF16)],
        compiler_params=_cparams(("parallel", "arbitrary")),
        name="ffn",
    )(x, g.reshape(1, d), w_in, w_in, w_out)


def _norm_matmul_kernel(x_ref, g_ref, w_ref, o_ref, xn_ref):
    @pl.when(pl.program_id(1) == 0)
    def _():
        xn_ref[...] = _rms_rows(x_ref[...], g_ref[...]).astype(BF16)

    o_ref[...] = _dot(xn_ref[...], w_ref[...])


def _norm_matmul(x, g, w, tn, tm=PROJ_TOKEN_TILE):
    n, d = x.shape
    wn = w.shape[1]
    tm = _pick_tile(n, tm)
    while wn % tn:
        tn -= 128
    return pl.pallas_call(
        _norm_matmul_kernel,
        out_shape=jax.ShapeDtypeStruct((n, wn), F32),
        grid=(n // tm, wn // tn),
        in_specs=[
            pl.BlockSpec((tm, d), lambda i, j: (i, 0)),
            pl.BlockSpec((1, d), lambda i, j: (0, 0)),
            pl.BlockSpec((d, tn), lambda i, j: (0, j)),
        ],
        out_specs=pl.BlockSpec((tm, tn), lambda i, j: (i, j)),
        scratch_shapes=[pltpu.VMEM((tm, d), BF16)],
        compiler_params=_cparams(("parallel", "arbitrary")),
        name="norm_matmul",
    )(x, g.reshape(1, d), w)


def _out_proj_kernel(x_ref, ya_ref, yb_ref, wa_ref, wb_ref, o_ref):
    o_ref[...] = x_ref[...] + _dot(ya_ref[...], wa_ref[...]) + _dot(yb_ref[...], wb_ref[...])


def _out_proj(x, ya, yb, w, tm=OUT_PROJ_TOKEN_TILE):
    n, d = x.shape
    da = ya.shape[1]
    db = yb.shape[1]
    assert da == db and w.shape[0] == da + db
    tm = _pick_tile(n, tm)
    return pl.pallas_call(
        _out_proj_kernel,
        out_shape=jax.ShapeDtypeStruct((n, d), F32),
        grid=(n // tm,),
        in_specs=[
            pl.BlockSpec((tm, d), lambda i: (i, 0)),
            pl.BlockSpec((tm, da), lambda i: (i, 0)),
            pl.BlockSpec((tm, db), lambda i: (i, 0)),
            pl.BlockSpec((da, d), lambda i: (0, 0)),
            pl.BlockSpec((db, d), lambda i: (1, 0)),
        ],
        out_specs=pl.BlockSpec((tm, d), lambda i: (i, 0)),
        compiler_params=_cparams(("parallel",)),
        name="out_proj",
    )(x, ya, yb, w, w)


def _causal_conv(x, w, b):
    y = x * w[CONV_W - 1:CONV_W] + b
    for k in range(1, CONV_W):
        y = y + _shift_rows(x, k, 0.0) * w[CONV_W - 1 - k:CONV_W - k]
    return y


def _shift_rows(x, k, fill):
    assert 0 < k < SUBLANES
    rolled = pltpu.roll(x, k, axis=0)
    t = lax.broadcasted_iota(jnp.int32, (SUBLANES, x.shape[1]), 0)
    head = jnp.where(t >= k, rolled[:SUBLANES], fill)
    return jnp.concatenate([head, rolled[SUBLANES:]], axis=0)


def _mixer_a_kernel(xr_ref, gate_ref, cw_ref, cb_ref, wr_ref, br_ref, wi_ref, bi_ref,
                    lam_ref, o_ref, a_ref, u_ref):
    s = xr_ref.shape[0]
    xc = _causal_conv(xr_ref[...], cw_ref[...], cb_ref[...])
    xb = xc.astype(BF16)
    r = jax.nn.sigmoid(_dot(xb, wr_ref[...]) + br_ref[...])
    i = jax.nn.sigmoid(_dot(xb, wi_ref[...]) + bi_ref[...])
    neg_lam = -lam_ref[...]
    softplus = jnp.maximum(neg_lam, 0.0) + jnp.log1p(jnp.exp(-jnp.abs(neg_lam)))
    log_a = -RG_C * r * softplus
    a = jnp.exp(log_a)
    a_ref[...] = a
    u_ref[...] = jnp.sqrt(-jnp.tanh(log_a) * (a * a + 1.0)) * (i * xc)
    B = SEQ_BLOCK
    carry = None
    for c in range(s // B):
        rows = slice(c * B, (c + 1) * B)
        a = a_ref[rows, :]
        u = u_ref[rows, :]
        k = 1
        while k < B:
            if k < SUBLANES:
                u = a * _shift_rows(u, k, 0.0) + u
                a = a * _shift_rows(a, k, 1.0)
            else:
                u = jnp.concatenate([u[:k], a[k:] * u[:B - k] + u[k:]], axis=0)
                a = jnp.concatenate([a[:k], a[k:] * a[:B - k]], axis=0)
            k *= 2
        if carry is not None:
            u = a * carry + u
        carry = u[B - 1:B, :]
        o_ref[rows, :] = (jax.nn.gelu(gate_ref[rows, :]) * u).astype(o_ref.dtype)


def _mixer_a(proj, conv_w, conv_b, w_r, b_r, w_i, b_i, lam, d_rg, cast_jobs):
    bsz, s, _ = proj.shape
    nb = d_rg // RG_BLOCK
    grid = (bsz, nb)
    vec = lambda v: v.reshape(1, d_rg)
    col = lambda off: pl.BlockSpec((None, s, RG_BLOCK), lambda b, g: (b, 0, off + g))
    row = lambda r: pl.BlockSpec((r, RG_BLOCK), lambda b, g: (0, g))
    blk = pl.BlockSpec((None, RG_BLOCK, RG_BLOCK), lambda b, g: (g, 0, 0))
    return _call_with_jobs(
        _mixer_a_kernel, cast_jobs(grid),
        out_shape=jax.ShapeDtypeStruct((bsz, s, d_rg), BF16),
        grid=grid,
        in_specs=[col(nb), col(0), row(CONV_W), row(1), blk, row(1), blk, row(1), row(1)],
        out_specs=pl.BlockSpec((None, s, RG_BLOCK), lambda b, g: (b, 0, g)),
        scratch_shapes=[pltpu.VMEM((s, RG_BLOCK), F32), pltpu.VMEM((s, RG_BLOCK), F32)],
        semantics=("arbitrary", "arbitrary"),
        name="mixer_a",
        operands=(proj, proj, conv_w, vec(conv_b), w_r.astype(BF16), vec(b_r), w_i.astype(BF16),
                  vec(b_i), vec(lam)))


def _t5_bucket_np(dist):
    max_exact = N_BUCKETS // 2
    d_f = np.maximum(dist, 1).astype(np.float32)
    large = max_exact + (np.log(d_f / np.float32(max_exact)) / np.float32(math.log(MAX_DIST / max_exact))
                         * np.float32(N_BUCKETS - max_exact)).astype(np.int32)
    large = np.minimum(large, N_BUCKETS - 1)
    return np.where(dist < max_exact, dist, large).astype(np.int32)


def _dsw_bias_vector(rel_bias, seq, qb):
    dist = np.arange(seq)
    mult = np.zeros(seq, np.float32)
    for window, dilation in DSW_PATTERNS:
        mult += ((dist % dilation == 0) & (dist <= window)).astype(np.float32)
    logm = np.where(mult > 0, np.log(np.maximum(mult, 1.0)), NEG).astype(np.float32)
    f = (rel_bias.astype(F32)[_t5_bucket_np(dist)] + logm[:, None]) * LOG2E
    n_heads = f.shape[1]
    hv = jnp.concatenate([f[::-1].T, jnp.full((n_heads, qb), NEG, F32)], axis=1)
    return hv[:, None, :]


def _mixer_b_kernel(q_ref, k_ref, v_ref, g_ref, hv_ref, o_ref, kn_ref, vb_ref, tbl_ref):
    s, hd = q_ref.shape
    qb = tbl_ref.shape[0]

    @pl.when(pl.program_id(1) == 0)
    def _():
        skew = pltpu.roll(jnp.broadcast_to(hv_ref[...], (qb, s + qb)), 1, axis=1, stride=1, stride_axis=0)
        tbl_ref[...] = skew[:, qb:]

    kn_ref[...] = _rms_rows(k_ref[...], g_ref[1:2]).astype(BF16)
    vb_ref[...] = v_ref[...].astype(BF16)
    def logits(gi):
        n_keys = (gi + 1) * qb
        qn = (_rms_rows(q_ref[gi * qb:n_keys, :], g_ref[0:1]) * (LOG2E / math.sqrt(hd))).astype(BF16)
        return _dot_nt(qn, kn_ref[0:n_keys, :]) + tbl_ref[:, s - n_keys:]

    def weights(sc):
        m = jnp.max(sc, axis=-1, keepdims=True)
        p = jnp.exp2(sc - m)
        return p.astype(BF16), jnp.sum(p, axis=-1, keepdims=True)

    def output(gi, p, l):
        n_keys = (gi + 1) * qb
        o = _dot(p, vb_ref[0:n_keys, :])
        o_ref[gi * qb:n_keys, :] = (o / l).astype(o_ref.dtype)

    n_groups = s // qb
    held = None
    for gi in reversed(range(n_groups)):
        cur = (gi,) + weights(logits(gi))
        if held is not None:
            output(*held)
        held = cur
    output(*held)


def _mixer_b(proj, qk_g, rel_bias, col0, n_heads, cast_jobs, qb=DSW_QUERY_ROWS):
    bsz, s, _ = proj.shape
    qb = _pick_tile(s, qb)
    hv = _dsw_bias_vector(rel_bias, s, qb)
    c0 = col0 // HEAD_B
    grid = (n_heads, bsz)
    seq_spec = lambda off: pl.BlockSpec((None, s, HEAD_B), lambda h, b: (b, 0, off + h))
    return _call_with_jobs(
        _mixer_b_kernel, cast_jobs(grid),
        out_shape=jax.ShapeDtypeStruct((bsz, s, n_heads * HEAD_B), BF16),
        grid=grid,
        in_specs=[
            seq_spec(c0), seq_spec(c0 + n_heads), seq_spec(c0 + 2 * n_heads),
            pl.BlockSpec((2, HEAD_B), lambda h, b: (0, 0)),
            pl.BlockSpec((None, 1, s + qb), lambda h, b: (h, 0, 0)),
        ],
        out_specs=pl.BlockSpec((None, s, HEAD_B), lambda h, b: (b, 0, h)),
        scratch_shapes=[pltpu.VMEM((s, HEAD_B), BF16), pltpu.VMEM((s, HEAD_B), BF16),
                        pltpu.VMEM((qb, s), F32)],
        semantics=("arbitrary", "arbitrary"),
        name="mixer_b",
        operands=(proj, proj, proj, qk_g, hv))


def _mixer_c_kernel(gb_ref, q_ref, k_ref, v_ref, og_ref, cwq_ref, cwk_ref, cbq_ref, cbk_ref,
                    ig_ref, fg_ref, gain_ref, o_ref, qc_ref, kc_ref, c_ref, n_ref, m_ref):
    h = pl.program_id(1)
    L = SEQ_BLOCK
    s, d = q_ref.shape
    nc = s // L

    qc_ref[...] = jax.nn.silu(_causal_conv(q_ref[...], cwq_ref[...], cbq_ref[...]))
    kc_ref[...] = jax.nn.silu(_causal_conv(k_ref[...], cwk_ref[...], cbk_ref[...])) * (d ** -0.5)
    c_ref[...] = jnp.zeros_like(c_ref)
    n_ref[...] = jnp.zeros_like(n_ref)
    m_ref[...] = jnp.zeros_like(m_ref)

    ri = lax.broadcasted_iota(jnp.int32, (L, L), 0)
    ci = lax.broadcasted_iota(jnp.int32, (L, L), 1)
    eye = ri == ci
    causal = ci <= ri

    def to_col(row):
        return jnp.sum(jnp.where(eye, row, 0.0), axis=1, keepdims=True)

    ig_all = ig_ref[...] + gb_ref[0, h]
    f_all = fg_ref[...] + gb_ref[1, h]
    ls_all = jnp.minimum(f_all, 0.0) - jnp.log1p(jnp.exp(-jnp.abs(f_all)))

    def chunk_local(n):
        rows = slice(n * L, (n + 1) * L)
        qc = qc_ref[rows, :]
        kc = kc_ref[rows, :]
        qb = qc.astype(BF16)
        kb = kc.astype(BF16)
        vb = v_ref[rows, :].astype(BF16)

        ig_row = ig_all[n:n + 1]
        ls_row = ls_all[n:n + 1]
        ls_col = to_col(ls_row)
        ig_col = to_col(ig_row)
        b_col = jnp.sum(jnp.where(causal, ls_row, 0.0), axis=1, keepdims=True)
        b_row = jnp.sum(jnp.where(ri <= ci, ls_col, 0.0), axis=0, keepdims=True)
        b_last = jnp.sum(ls_row, axis=1, keepdims=True)

        log_d = b_col - b_row + ig_row
        g_row = b_last - b_row + ig_row
        g_col = b_last - b_col + ig_col
        g_max = jnp.max(g_row, axis=1, keepdims=True)
        d_max = jnp.max(jnp.where(causal, log_d, -jnp.inf), axis=1, keepdims=True)
        qk = _dot_nt(qb, kb)
        wk = jnp.exp(g_col - g_max) * kc
        c_loc = _dot_tn(wk.astype(BF16), vb)
        n_loc = jnp.sum(wk, axis=0, keepdims=True)
        return rows, qc, qb, vb, b_col, b_last, log_d, g_max, d_max, qk, c_loc, n_loc

    def finish(rows, hout):
        hn = _rms_rows(hout, gain_ref[...])
        o_ref[rows, :] = (jax.nn.sigmoid(og_ref[rows, :]) * hn).astype(o_ref.dtype)

    pending = None
    local = chunk_local(0)
    for n in range(nc):
        rows, qc, qb, vb, b_col, b_last, log_d, g_max, d_max, qk, c_loc, n_loc = local
        if n + 1 < nc:
            local = chunk_local(n + 1)

        c_st = c_ref[...]
        n_st = n_ref[...]
        m_st = m_ref[...]

        m_new = jnp.maximum(b_last + m_st, g_max)
        decay = jnp.exp(b_last + m_st - m_new)
        fresh = jnp.exp(g_max - m_new)
        c_ref[...] = decay * c_st + fresh * c_loc
        n_ref[...] = decay * n_st + fresh * n_loc
        m_ref[...] = m_new

        inter = b_col + m_st
        m_t = jnp.maximum(inter, d_max)
        s_qk = jnp.where(causal, qk * jnp.exp(log_d - m_t), 0.0)
        inter_w = jnp.exp(inter - m_t)
        num = _dot(s_qk.astype(BF16), vb) + inter_w * _dot(qb, c_st.astype(BF16))
        qn = jnp.sum(qc * n_st, axis=1, keepdims=True)
        den = jnp.sum(s_qk, axis=1, keepdims=True) + inter_w * qn
        hout = num / jnp.maximum(jnp.abs(den), jnp.exp(-m_t))

        if pending is not None:
            finish(*pending)
        pending = (rows, hout)
    finish(*pending)


def _mixer_c(proj, gates, conv_w, conv_b, gate_b, h_gain, n_heads, cast_jobs):
    bsz, s, _ = proj.shape
    d = HEAD_C
    L = SEQ_BLOCK
    nc = s // L
    grid = (bsz, n_heads)
    g_rows = jnp.transpose(gates, (0, 2, 1)).reshape(bsz, 2 * n_heads, nc, L)
    cb = conv_b.reshape(1, -1)
    full = lambda off: pl.BlockSpec((None, s, d), lambda b, h: (b, 0, off + h))
    row = lambda r, off: pl.BlockSpec((r, d), lambda b, h: (0, off + h))
    grow = lambda off: pl.BlockSpec((None, None, nc, L), lambda b, h: (b, off + h, 0, 0))
    return _call_with_jobs(
        _mixer_c_kernel, cast_jobs(grid),
        out_shape=jax.ShapeDtypeStruct((bsz, s, n_heads * d), BF16),
        grid=grid,
        in_specs=[
            pl.BlockSpec(memory_space=pltpu.SMEM),
            full(0), full(n_heads), full(2 * n_heads), full(3 * n_heads),
            row(CONV_W, 0), row(CONV_W, n_heads), row(1, 0), row(1, n_heads),
            grow(0), grow(n_heads), row(1, 0),
        ],
        out_specs=pl.BlockSpec((None, s, d), lambda b, h: (b, 0, h)),
        scratch_shapes=[pltpu.VMEM((s, d), F32), pltpu.VMEM((s, d), F32),
                        pltpu.VMEM((d, d), F32), pltpu.VMEM((1, d), F32), pltpu.VMEM((1, 1), F32)],
        semantics=("arbitrary", "arbitrary"),
        name="mixer_c",
        operands=(gate_b, proj, proj, proj, proj, conv_w, conv_w, cb, cb, g_rows, g_rows,
                  h_gain.reshape(1, -1)))


def _mixer_d_kernel(q_ref, k_ref, v_ref, o_ref, kb_ref, vb_ref, acc_ref, suf_ref):
    s, hd = q_ref.shape
    B = SEQ_BLOCK
    G = min(SB_QUERY_ROWS, s)
    kb_ref[...] = k_ref[...].astype(BF16)
    vb_ref[...] = v_ref[...].astype(BF16)
    ri = lax.broadcasted_iota(jnp.int32, (B, B), 0)
    ci = lax.broadcasted_iota(jnp.int32, (B, B), 1)
    after = (ri > ci).astype(BF16)
    after2 = jnp.concatenate([after, after], axis=0)
    row_g = lax.broadcasted_iota(jnp.int32, (G, B), 0)
    col_g = lax.broadcasted_iota(jnp.int32, (G, B), 1)

    def span_stages(q0, first, last, suffix):
        n = last - first + 1
        keys = slice(first * B, (last + 1) * B)
        qb = (q_ref[q0:q0 + G, :] * (hd ** -0.5)).astype(BF16)
        z = _dot_nt(qb, kb_ref[keys, :])
        sp = jnp.log(1.0 + jnp.exp2(jnp.abs(z) * (-LOG2E)))
        nm = jnp.minimum(z, 0.0)
        log_beta = nm - sp
        log_stay = (nm - z) - sp
        before = [col_g + ((first + j) * B - q0) < row_g if (first + j + 1) * B > q0 else None
                  for j in range(n)]
        hl, tot = [], []
        for j in range(n):
            x = log_stay[:, j * B:(j + 1) * B]
            if before[j] is not None:
                x = jnp.where(before[j], x, 0.0)
            hi = lax.bitcast_convert_type(lax.bitcast_convert_type(x, jnp.uint32) & HI_MASK, F32)
            hl.append(jnp.concatenate([hi.astype(BF16), (x - hi).astype(BF16)], axis=1))
            tot.append(jnp.sum(x, axis=-1, keepdims=True))
        yield
        later = [_dot(h, after2) for h in hl]
        yield
        att = [None] * n
        for j in range(n - 1, -1, -1):
            a = jnp.exp(log_beta[:, j * B:(j + 1) * B] + later[j] + suffix)
            if before[j] is not None:
                a = jnp.where(before[j], a, 0.0)
            att[j] = a.astype(BF16)
            suffix = suffix + tot[j]
        att = att[0] if n == 1 else jnp.concatenate(att, axis=1)
        return _dot(att, vb_ref[keys, :]), suffix

    def step(gen):
        try:
            next(gen)
        except StopIteration as done:
            return done.value
        return None

    def span(q0, first, last, suffix):
        gen = span_stages(q0, first, last, suffix)
        done = None
        while done is None:
            done = step(gen)
        return done

    def first_near(q0):
        return max(q0 // B - SB_NEAR_BLOCKS + 1, 0)

    worst = None
    starts = list(range(0, s, G))
    spans = [span_stages(q0, first_near(q0), (q0 + G) // B - 1, jnp.zeros((G, 1), F32)) for q0 in starts]
    for t in range(len(spans) + 2):
        for i in (t - 2, t - 1, t):
            if not 0 <= i < len(spans):
                continue
            done = step(spans[i])
            if done is None:
                continue
            q0 = starts[i]
            rows = slice(q0, q0 + G)
            acc, suffix = done
            o_ref[rows, :] = acc.astype(o_ref.dtype)
            if first_near(q0) > 0:
                acc_ref[rows, :] = acc
                suf_ref[rows, :] = suffix
                worst = suffix if worst is None else jnp.maximum(worst, suffix)

    if worst is not None:
        @pl.when(jnp.max(worst) > SB_UNDERFLOW)
        def _():
            for q0 in range(0, s, G):
                if first_near(q0) == 0:
                    continue
                rows = slice(q0, q0 + G)
                suffix = suf_ref[rows, :]

                @pl.when(jnp.max(suffix) > SB_UNDERFLOW)
                def _():
                    far, _ = span(q0, 0, first_near(q0) - 1, suffix)
                    o_ref[rows, :] = (acc_ref[rows, :] + far).astype(o_ref.dtype)


def _mixer_d(proj, col0, n_heads, cast_jobs):
    bsz, s, _ = proj.shape
    c0 = col0 // HEAD_D
    grid = (bsz, n_heads)
    seq_spec = lambda off: pl.BlockSpec((None, s, HEAD_D), lambda b, h: (b, 0, off + h))
    return _call_with_jobs(
        _mixer_d_kernel, cast_jobs(grid),
        out_shape=jax.ShapeDtypeStruct((bsz, s, n_heads * HEAD_D), BF16),
        grid=grid,
        in_specs=[seq_spec(c0), seq_spec(c0 + n_heads), seq_spec(c0 + 2 * n_heads)],
        out_specs=pl.BlockSpec((None, s, HEAD_D), lambda b, h: (b, 0, h)),
        scratch_shapes=[pltpu.VMEM((s, HEAD_D), BF16), pltpu.VMEM((s, HEAD_D), BF16),
                        pltpu.VMEM((s, HEAD_D), F32), pltpu.VMEM((s, 1), F32)],
        semantics=("arbitrary", "arbitrary"),
        name="mixer_d",
        operands=(proj, proj, proj))


def _layer_ab(x, bsz, s, g, w_in, conv_w, conv_b, w_r, b_r, w_i, b_i, lam, qk_g, rel_bias, w_out,
              jobs):
    d_rg = conv_w.shape[-1]
    n_heads = rel_bias.shape[-1]
    proj = _norm_matmul(x, g, w_in.astype(BF16), tn=PROJ_COL_TILE_AB).reshape(bsz, s, -1)
    ya, out_a = _mixer_a(proj, conv_w, conv_b, w_r, b_r, w_i, b_i, lam, d_rg, jobs[0])
    yb, out_b = _mixer_b(proj, qk_g, rel_bias, 2 * d_rg, n_heads, jobs[1])
    n = bsz * s
    return _out_proj(x, ya.reshape(n, -1), yb.reshape(n, -1), w_out.astype(BF16)), out_a + out_b


def _cd_layout(gate_b, w_out):
    n_heads_c = gate_b.shape[-1]
    dc = n_heads_c * HEAD_C
    dd = w_out.shape[0] - dc
    n_gate = 2 * n_heads_c
    main = 4 * dc + 3 * dd
    width = -(-(main + n_gate) // CD_WIDTH_MULTIPLE) * CD_WIDTH_MULTIPLE
    return n_heads_c, dc, dd, n_gate, main, width


def _cd_weight_job(w_in, j, gate_col, n_gate, width, tw=CD_WEIGHT_COL_TILE):
    _, d, cols = w_in.shape
    w_t = jnp.swapaxes(w_in, 1, 2)
    main = cols - n_gate
    assert gate_col % tw == 0 and (main - gate_col) % tw == 0 and width % tw == 0
    assert n_gate % SUBLANES == 0 and n_gate <= tw
    n_aligned = gate_col // tw
    n_shifted = (main - gate_col) // tw
    n_blk = width // tw
    last = (cols - 1) // tw

    def job(grid):
        assert n_blk <= math.prod(grid)
        blk = lambda *idx: jnp.minimum(_linear_step(grid, idx), n_blk - 1)

        def reorder(t, a_ref, b_ref, o_ref):
            def emit(rows):
                eye = (lax.broadcasted_iota(jnp.int32, (tw, tw), 0)
                       == lax.broadcasted_iota(jnp.int32, (tw, tw), 1)).astype(BF16)
                o_ref[...] = _dot_tn(rows.astype(BF16), eye).astype(o_ref.dtype)

            @pl.when(t < n_aligned)
            def _():
                emit(a_ref[...])

            @pl.when((t >= n_aligned) & (t < n_aligned + n_shifted))
            def _():
                emit(jnp.concatenate([a_ref[n_gate:, :], b_ref[:n_gate, :]], axis=0))

            @pl.when(t == n_aligned + n_shifted)
            def _():
                row = lax.broadcasted_iota(jnp.int32, a_ref.shape, 0)
                emit(jnp.where(row < n_gate, a_ref[...], 0.0))

            @pl.when((t > n_aligned + n_shifted) & (t < n_blk))
            def _():
                o_ref[...] = jnp.zeros_like(o_ref)

        a_blk = lambda *idx: jnp.where(blk(*idx) < n_aligned + n_shifted, blk(*idx), n_aligned)
        return [([w_t, w_t],
                 [pl.BlockSpec((None, tw, d), lambda *idx: (j, a_blk(*idx), 0)),
                  pl.BlockSpec((None, tw, d), lambda *idx: (j, jnp.minimum(blk(*idx) + 1, last), 0))],
                 pl.BlockSpec((d, tw), lambda *idx: (0, blk(*idx))),
                 jax.ShapeDtypeStruct((d, width), BF16),
                 reorder)]
    return job


def _layer_cd(x, bsz, s, g, w, conv_w, conv_b, gate_b, h_gain, w_out, jobs):
    n_heads_c, dc, dd, n_gate, main, width = _cd_layout(gate_b, w_out)
    n_heads_d = dd // HEAD_D
    proj = _norm_matmul(x, g, w, tn=PROJ_COL_TILE_CD).reshape(bsz, s, width)
    gates = proj[:, :, main:main + n_gate]
    yd, out_d = _mixer_d(proj, 4 * dc, n_heads_d, jobs[0])
    yc, out_c = _mixer_c(proj, gates, conv_w, conv_b, gate_b, h_gain, n_heads_c, jobs[1])
    n = bsz * s
    return _out_proj(x, yc.reshape(n, -1), yd.reshape(n, -1), w_out.astype(BF16)), out_d + out_c


def kernel(x, norm_g, ffn_w_in, ffn_w_out, ab_w_in, ab_conv_w, ab_conv_b, rg_w_r, rg_b_r, rg_w_i,
           rg_b_i, rg_lambda, qk_gain, rel_bias, ab_w_out, cd_w_in, cd_conv_w, cd_conv_b,
           mlstm_gate_bias, mlstm_h_gain, cd_w_out):
    bsz, s, d = x.shape
    depth = norm_g.shape[0]
    h = x.reshape(bsz * s, d)
    ffn_w = {(0, 0): (ffn_w_in[0, 0].astype(BF16), ffn_w_out[0, 0].astype(BF16))}
    cd_w = None
    for layer in range(depth):
        j = layer // 2
        h = _ffn(h, norm_g[layer, 0], *ffn_w[(layer, 0)])
        targets = [(layer, 1)] + ([(layer + 1, 0)] if layer + 1 < depth else [])
        jobs = [functools.partial(_ffn_cast_jobs, ffn_w_in, ffn_w_out, *t) for t in targets]
        jobs += [lambda grid: []] * (2 - len(jobs))
        if layer % 2 == 0:
            if layer + 1 < depth:
                _, dc, _, n_gate, _, width = _cd_layout(mlstm_gate_bias[j], cd_w_out[j])
                ffn_jobs, cd_job = jobs[0], _cd_weight_job(cd_w_in, j, 4 * dc, n_gate, width)
                jobs[0] = lambda grid: ffn_jobs(grid) + cd_job(grid)
            h, out = _layer_ab(h, bsz, s, norm_g[layer, 1], ab_w_in[j], ab_conv_w[j], ab_conv_b[j],
                               rg_w_r[j], rg_b_r[j], rg_w_i[j], rg_b_i[j], rg_lambda[j], qk_gain[j],
                               rel_bias, ab_w_out[j], jobs)
            if layer + 1 < depth:
                cd_w = out.pop(2)
        else:
            h, out = _layer_cd(h, bsz, s, norm_g[layer, 1], cd_w, cd_conv_w[j], cd_conv_b[j],
                               mlstm_gate_bias[j], mlstm_h_gain[j], cd_w_out[j], jobs)
        for i, t in enumerate(targets):
            ffn_w[t] = (out[2 * i], out[2 * i + 1])
        h = _ffn(h, norm_g[layer, 2], *ffn_w[(layer, 1)])
    return h.reshape(bsz, s, d)
```

```python
import functools
import math

import numpy as np
import jax
import jax.numpy as jnp
from jax import lax
from jax.experimental import pallas as pl
from jax.experimental.pallas import tpu as pltpu

F32 = jnp.float32
BF16 = jnp.bfloat16

RMS_EPS = 1e-6
RG_C = 8.0
CONV_W = 4
RG_BLOCK = 128
HEAD_B = 128
HEAD_C = 256
HEAD_D = 128
SEQ_BLOCK = 128
SUBLANES = 8
BF16_SUBLANES = 16
DSW_PATTERNS = ((128, 1), (512, 4), (2048, 16))
N_BUCKETS = 32
MAX_DIST = 2048
NEG = -1e30
LOG2E = 1.4426950408889634
HI_MASK = np.uint32(0xFFFF0000)
SB_NEAR_BLOCKS = 3
SB_QUERY_ROWS = 256
SB_UNDERFLOW = -104.0
DSW_QUERY_ROWS = 256

FFN_TOKEN_TILE = 1024
FFN_HIDDEN_TILE = 512
PROJ_TOKEN_TILE = 1024
PROJ_COL_TILE_AB = 1280
PROJ_COL_TILE_CD = 1536
OUT_PROJ_TOKEN_TILE = 512
CD_WIDTH_MULTIPLE = 512
CD_WEIGHT_COL_TILE = 256
V7X_VMEM_BYTES = 64 * 1024 * 1024
VMEM_LIMIT = V7X_VMEM_BYTES * 7 // 8


def _cparams(sem):
    return pltpu.CompilerParams(dimension_semantics=sem, vmem_limit_bytes=VMEM_LIMIT)


def _rms_rows(x, g):
    return x * lax.rsqrt(jnp.mean(x * x, axis=-1, keepdims=True) + RMS_EPS) * g


def _dot(a, b):
    return jnp.dot(a, b, preferred_element_type=F32)


def _dot_nt(a, b):
    return lax.dot_general(a, b, (((1,), (1,)), ((), ())), preferred_element_type=F32)


def _dot_tn(a, b):
    return lax.dot_general(a, b, (((0,), (0,)), ((), ())), preferred_element_type=F32)


def _pick_tile(n, pref):
    t = min(pref, n)
    while n % t:
        t //= 2
    return t


def _linear_step(grid, idx):
    t = idx[0]
    for i, g in zip(idx[1:], grid[1:]):
        t = t * g + i
    return t


def _cast_job(w, lead, grid):
    rows, cols = w.shape[-2:]
    n = math.prod(grid)
    while rows % n or (rows // n) % BF16_SUBLANES:
        n -= 1
    slab = rows // n
    blk = lambda *idx: jnp.minimum(_linear_step(grid, idx), n - 1)

    def cast(t, src_ref, dst_ref):
        @pl.when(t < n)
        def _():
            dst_ref[...] = src_ref[...].astype(dst_ref.dtype)

    return ([w],
            [pl.BlockSpec((None,) * len(lead) + (slab, cols), lambda *idx: (*lead, blk(*idx), 0))],
            pl.BlockSpec((slab, cols), lambda *idx: (blk(*idx), 0)),
            jax.ShapeDtypeStruct((rows, cols), BF16),
            cast)


def _ffn_cast_jobs(w_in, w_out, layer, half, grid):
    return [_cast_job(w_in, (layer, half), grid), _cast_job(w_out, (layer, half), grid)]


def _call_with_jobs(body, jobs, *, out_shape, grid, in_specs, out_specs, scratch_shapes=(),
                    semantics, name, operands):
    n_in = len(in_specs)
    n_job_in = [len(j[0]) for j in jobs]
    n_extra = sum(n_job_in)

    def kernel(*refs):
        t = _linear_step(grid, [pl.program_id(ax) for ax in range(len(grid))])
        at = n_in
        for i, j in enumerate(jobs):
            j[4](t, *refs[at:at + n_job_in[i]], refs[n_in + n_extra + 1 + i])
            at += n_job_in[i]
        body(*refs[:n_in], refs[n_in + n_extra], *refs[n_in + n_extra + 1 + len(jobs):])

    outs = pl.pallas_call(
        kernel,
        out_shape=[out_shape] + [j[3] for j in jobs],
        grid=grid,
        in_specs=list(in_specs) + [sp for j in jobs for sp in j[1]],
        out_specs=[out_specs] + [j[2] for j in jobs],
        scratch_shapes=list(scratch_shapes),
        compiler_params=_cparams(semantics),
        name=name,
    )(*operands, *[op for j in jobs for op in j[0]])
    return outs[0], list(outs[1:])


def _ffn_kernel(x_ref, g_ref, wg_ref, wu_ref, wo_ref, o_ref, xn_ref):
    j = pl.program_id(1)

    @pl.when(j == 0)
    def _():
        x = x_ref[...]
        xn_ref[...] = _rms_rows(x, g_ref[...]).astype(BF16)
        o_ref[...] = x

    xn = xn_ref[...]
    hg = _dot(xn, wg_ref[...])
    hu = _dot(xn, wu_ref[...])
    h = (0.5 * hg * jax.nn.sigmoid(hg) * hu).astype(BF16)
    o_ref[...] += _dot(h, wo_ref[...])


def _ffn(x, g, w_in, w_out, tm=FFN_TOKEN_TILE, tf=FFN_HIDDEN_TILE):
    n, d = x.shape
    d_ff = w_out.shape[0]
    tm = _pick_tile(n, tm)
    tf = _pick_tile(d_ff, tf)
    nj = d_ff // tf
    return pl.pallas_call(
        _ffn_kernel,
        out_shape=jax.ShapeDtypeStruct((n, d), F32),
        grid=(n // tm, nj),
        in_specs=[
            pl.BlockSpec((tm, d), lambda i, j: (i, 0)),
            pl.BlockSpec((1, d), lambda i, j: (0, 0)),
            pl.BlockSpec((d, tf), lambda i, j: (0, j)),
            pl.BlockSpec((d, tf), lambda i, j: (0, j + nj)),
            pl.BlockSpec((tf, d), lambda i, j: (j, 0)),
        ],
        out_specs=pl.BlockSpec((tm, d), lambda i, j: (i, 0)),
        scratch_shapes=[pltpu.VMEM((tm, d), BF16)],
        compiler_params=_cparams(("parallel", "arbitrary")),
        name="ffn",
    )(x, g.reshape(1, d), w_in, w_in, w_out)


def _norm_matmul_kernel(x_ref, g_ref, w_ref, o_ref, xn_ref):
    @pl.when(pl.program_id(1) == 0)
    def _():
        xn_ref[...] = _rms_rows(x_ref[...], g_ref[...]).astype(BF16)

    o_ref[...] = _dot(xn_ref[...], w_ref[...])


def _norm_matmul(x, g, w, tn, tm=PROJ_TOKEN_TILE):
    n, d = x.shape
    wn = w.shape[1]
    tm = _pick_tile(n, tm)
    while wn % tn:
        tn -= 128
    return pl.pallas_call(
        _norm_matmul_kernel,
        out_shape=jax.ShapeDtypeStruct((n, wn), F32),
        grid=(n // tm, wn // tn),
        in_specs=[
            pl.BlockSpec((tm, d), lambda i, j: (i, 0)),
            pl.BlockSpec((1, d), lambda i, j: (0, 0)),
            pl.BlockSpec((d, tn), lambda i, j: (0, j)),
        ],
        out_specs=pl.BlockSpec((tm, tn), lambda i, j: (i, j)),
        scratch_shapes=[pltpu.VMEM((tm, d), BF16)],
        compiler_params=_cparams(("parallel", "arbitrary")),
        name="norm_matmul",
    )(x, g.reshape(1, d), w)


def _out_proj_kernel(x_ref, ya_ref, yb_ref, wa_ref, wb_ref, o_ref):
    o_ref[...] = x_ref[...] + _dot(ya_ref[...], wa_ref[...]) + _dot(yb_ref[...], wb_ref[...])


def _out_proj(x, ya, yb, w, tm=OUT_PROJ_TOKEN_TILE):
    n, d = x.shape
    da = ya.shape[1]
    db = yb.shape[1]
    assert da == db and w.shape[0] == da + db
    tm = _pick_tile(n, tm)
    return pl.pallas_call(
        _out_proj_kernel,
        out_shape=jax.ShapeDtypeStruct((n, d), F32),
        grid=(n // tm,),
        in_specs=[
            pl.BlockSpec((tm, d), lambda i: (i, 0)),
            pl.BlockSpec((tm, da), lambda i: (i, 0)),
            pl.BlockSpec((tm, db), lambda i: (i, 0)),
            pl.BlockSpec((da, d), lambda i: (0, 0)),
            pl.BlockSpec((db, d), lambda i: (1, 0)),
        ],
        out_specs=pl.BlockSpec((tm, d), lambda i: (i, 0)),
        compiler_params=_cparams(("parallel",)),
        name="out_proj",
    )(x, ya, yb, w, w)


def _causal_conv(x, w, b):
    y = x * w[CONV_W - 1:CONV_W] + b
    for k in range(1, CONV_W):
        y = y + _shift_rows(x, k, 0.0) * w[CONV_W - 1 - k:CONV_W - k]
    return y


def _shift_rows(x, k, fill):
    assert 0 < k < SUBLANES
    rolled = pltpu.roll(x, k, axis=0)
    t = lax.broadcasted_iota(jnp.int32, (SUBLANES, x.shape[1]), 0)
    head = jnp.where(t >= k, rolled[:SUBLANES], fill)
    return jnp.concatenate([head, rolled[SUBLANES:]], axis=0)


def _mixer_a_kernel(xr_ref, gate_ref, cw_ref, cb_ref, wr_ref, br_ref, wi_ref, bi_ref,
                    lam_ref, o_ref, a_ref, u_ref):
    s = xr_ref.shape[0]
    xc = _causal_conv(xr_ref[...], cw_ref[...], cb_ref[...])
    xb = xc.astype(BF16)
    r = jax.nn.sigmoid(_dot(xb, wr_ref[...]) + br_ref[...])
    i = jax.nn.sigmoid(_dot(xb, wi_ref[...]) + bi_ref[...])
    neg_lam = -lam_ref[...]
    softplus = jnp.maximum(neg_lam, 0.0) + jnp.log1p(jnp.exp(-jnp.abs(neg_lam)))
    log_a = -RG_C * r * softplus
    a = jnp.exp(log_a)
    a_ref[...] = a
    u_ref[...] = jnp.sqrt(-jnp.tanh(log_a) * (a * a + 1.0)) * (i * xc)
    B = SEQ_BLOCK
    carry = None
    for c in range(s // B):
        rows = slice(c * B, (c + 1) * B)
        a = a_ref[rows, :]
        u = u_ref[rows, :]
        k = 1
        while k < B:
            if k < SUBLANES:
                u = a * _shift_rows(u, k, 0.0) + u
                a = a * _shift_rows(a, k, 1.0)
            else:
                u = jnp.concatenate([u[:k], a[k:] * u[:B - k] + u[k:]], axis=0)
                a = jnp.concatenate([a[:k], a[k:] * a[:B - k]], axis=0)
            k *= 2
        if carry is not None:
            u = a * carry + u
        carry = u[B - 1:B, :]
        o_ref[rows, :] = (jax.nn.gelu(gate_ref[rows, :]) * u).astype(o_ref.dtype)


def _mixer_a(proj, conv_w, conv_b, w_r, b_r, w_i, b_i, lam, d_rg, cast_jobs):
    bsz, s, _ = proj.shape
    nb = d_rg // RG_BLOCK
    grid = (bsz, nb)
    vec = lambda v: v.reshape(1, d_rg)
    col = lambda off: pl.BlockSpec((None, s, RG_BLOCK), lambda b, g: (b, 0, off + g))
    row = lambda r: pl.BlockSpec((r, RG_BLOCK), lambda b, g: (0, g))
    blk = pl.BlockSpec((None, RG_BLOCK, RG_BLOCK), lambda b, g: (g, 0, 0))
    return _call_with_jobs(
        _mixer_a_kernel, cast_jobs(grid),
        out_shape=jax.ShapeDtypeStruct((bsz, s, d_rg), BF16),
        grid=grid,
        in_specs=[col(nb), col(0), row(CONV_W), row(1), blk, row(1), blk, row(1), row(1)],
        out_specs=pl.BlockSpec((None, s, RG_BLOCK), lambda b, g: (b, 0, g)),
        scratch_shapes=[pltpu.VMEM((s, RG_BLOCK), F32), pltpu.VMEM((s, RG_BLOCK), F32)],
        semantics=("arbitrary", "arbitrary"),
        name="mixer_a",
        operands=(proj, proj, conv_w, vec(conv_b), w_r.astype(BF16), vec(b_r), w_i.astype(BF16),
                  vec(b_i), vec(lam)))


def _t5_bucket_np(dist):
    max_exact = N_BUCKETS // 2
    d_f = np.maximum(dist, 1).astype(np.float32)
    large = max_exact + (np.log(d_f / np.float32(max_exact)) / np.float32(math.log(MAX_DIST / max_exact))
                         * np.float32(N_BUCKETS - max_exact)).astype(np.int32)
    large = np.minimum(large, N_BUCKETS - 1)
    return np.where(dist < max_exact, dist, large).astype(np.int32)


def _dsw_bias_vector(rel_bias, seq, qb):
    dist = np.arange(seq)
    mult = np.zeros(seq, np.float32)
    for window, dilation in DSW_PATTERNS:
        mult += ((dist % dilation == 0) & (dist <= window)).astype(np.float32)
    logm = np.where(mult > 0, np.log(np.maximum(mult, 1.0)), NEG).astype(np.float32)
    f = (rel_bias.astype(F32)[_t5_bucket_np(dist)] + logm[:, None]) * LOG2E
    n_heads = f.shape[1]
    hv = jnp.concatenate([f[::-1].T, jnp.full((n_heads, qb), NEG, F32)], axis=1)
    return hv[:, None, :]


def _mixer_b_kernel(q_ref, k_ref, v_ref, g_ref, hv_ref, o_ref, kn_ref, vb_ref, tbl_ref):
    s, hd = q_ref.shape
    qb = tbl_ref.shape[0]

    @pl.when(pl.program_id(1) == 0)
    def _():
        skew = pltpu.roll(jnp.broadcast_to(hv_ref[...], (qb, s + qb)), 1, axis=1, stride=1, stride_axis=0)
        tbl_ref[...] = skew[:, qb:]

    kn_ref[...] = _rms_rows(k_ref[...], g_ref[1:2]).astype(BF16)
    vb_ref[...] = v_ref[...].astype(BF16)
    def logits(gi):
        n_keys = (gi + 1) * qb
        qn = (_rms_rows(q_ref[gi * qb:n_keys, :], g_ref[0:1]) * (LOG2E / math.sqrt(hd))).astype(BF16)
        return _dot_nt(qn, kn_ref[0:n_keys, :]) + tbl_ref[:, s - n_keys:]

    def weights(sc):
        m = jnp.max(sc, axis=-1, keepdims=True)
        p = jnp.exp2(sc - m)
        return p.astype(BF16), jnp.sum(p, axis=-1, keepdims=True)

    def output(gi, p, l):
        n_keys = (gi + 1) * qb
        o = _dot(p, vb_ref[0:n_keys, :])
        o_ref[gi * qb:n_keys, :] = (o / l).astype(o_ref.dtype)

    n_groups = s // qb
    held = None
    for gi in reversed(range(n_groups)):
        cur = (gi,) + weights(logits(gi))
        if held is not None:
            output(*held)
        held = cur
    output(*held)


def _mixer_b(proj, qk_g, rel_bias, col0, n_heads, cast_jobs, qb=DSW_QUERY_ROWS):
    bsz, s, _ = proj.shape
    qb = _pick_tile(s, qb)
    hv = _dsw_bias_vector(rel_bias, s, qb)
    c0 = col0 // HEAD_B
    grid = (n_heads, bsz)
    seq_spec = lambda off: pl.BlockSpec((None, s, HEAD_B), lambda h, b: (b, 0, off + h))
    return _call_with_jobs(
        _mixer_b_kernel, cast_jobs(grid),
        out_shape=jax.ShapeDtypeStruct((bsz, s, n_heads * HEAD_B), BF16),
        grid=grid,
        in_specs=[
            seq_spec(c0), seq_spec(c0 + n_heads), seq_spec(c0 + 2 * n_heads),
            pl.BlockSpec((2, HEAD_B), lambda h, b: (0, 0)),
            pl.BlockSpec((None, 1, s + qb), lambda h, b: (h, 0, 0)),
        ],
        out_specs=pl.BlockSpec((None, s, HEAD_B), lambda h, b: (b, 0, h)),
        scratch_shapes=[pltpu.VMEM((s, HEAD_B), BF16), pltpu.VMEM((s, HEAD_B), BF16),
                        pltpu.VMEM((qb, s), F32)],
        semantics=("arbitrary", "arbitrary"),
        name="mixer_b",
        operands=(proj, proj, proj, qk_g, hv))


def _mixer_c_kernel(gb_ref, q_ref, k_ref, v_ref, og_ref, cwq_ref, cwk_ref, cbq_ref, cbk_ref,
                    ig_ref, fg_ref, gain_ref, o_ref, qc_ref, kc_ref, c_ref, n_ref, m_ref):
    h = pl.program_id(1)
    L = SEQ_BLOCK
    s, d = q_ref.shape
    nc = s // L

    qc_ref[...] = jax.nn.silu(_causal_conv(q_ref[...], cwq_ref[...], cbq_ref[...]))
    kc_ref[...] = jax.nn.silu(_causal_conv(k_ref[...], cwk_ref[...], cbk_ref[...])) * (d ** -0.5)
    c_ref[...] = jnp.zeros_like(c_ref)
    n_ref[...] = jnp.zeros_like(n_ref)
    m_ref[...] = jnp.zeros_like(m_ref)

    ri = lax.broadcasted_iota(jnp.int32, (L, L), 0)
    ci = lax.broadcasted_iota(jnp.int32, (L, L), 1)
    eye = ri == ci
    causal = ci <= ri

    def to_col(row):
        return jnp.sum(jnp.where(eye, row, 0.0), axis=1, keepdims=True)

    ig_all = ig_ref[...] + gb_ref[0, h]
    f_all = fg_ref[...] + gb_ref[1, h]
    ls_all = jnp.minimum(f_all, 0.0) - jnp.log1p(jnp.exp(-jnp.abs(f_all)))

    def chunk_local(n):
        rows = slice(n * L, (n + 1) * L)
        qc = qc_ref[rows, :]
        kc = kc_ref[rows, :]
        qb = qc.astype(BF16)
        kb = kc.astype(BF16)
        vb = v_ref[rows, :].astype(BF16)

        ig_row = ig_all[n:n + 1]
        ls_row = ls_all[n:n + 1]
        ls_col = to_col(ls_row)
        ig_col = to_col(ig_row)
        b_col = jnp.sum(jnp.where(causal, ls_row, 0.0), axis=1, keepdims=True)
        b_row = jnp.sum(jnp.where(ri <= ci, ls_col, 0.0), axis=0, keepdims=True)
        b_last = jnp.sum(ls_row, axis=1, keepdims=True)

        log_d = b_col - b_row + ig_row
        g_row = b_last - b_row + ig_row
        g_col = b_last - b_col + ig_col
        g_max = jnp.max(g_row, axis=1, keepdims=True)
        d_max = jnp.max(jnp.where(causal, log_d, -jnp.inf), axis=1, keepdims=True)
        qk = _dot_nt(qb, kb)
        wk = jnp.exp(g_col - g_max) * kc
        c_loc = _dot_tn(wk.astype(BF16), vb)
        n_loc = jnp.sum(wk, axis=0, keepdims=True)
        return rows, qc, qb, vb, b_col, b_last, log_d, g_max, d_max, qk, c_loc, n_loc

    def finish(rows, hout):
        hn = _rms_rows(hout, gain_ref[...])
        o_ref[rows, :] = (jax.nn.sigmoid(og_ref[rows, :]) * hn).astype(o_ref.dtype)

    pending = None
    local = chunk_local(0)
    for n in range(nc):
        rows, qc, qb, vb, b_col, b_last, log_d, g_max, d_max, qk, c_loc, n_loc = local
        if n + 1 < nc:
            local = chunk_local(n + 1)

        c_st = c_ref[...]
        n_st = n_ref[...]
        m_st = m_ref[...]

        m_new = jnp.maximum(b_last + m_st, g_max)
        decay = jnp.exp(b_last + m_st - m_new)
        fresh = jnp.exp(g_max - m_new)
        c_ref[...] = decay * c_st + fresh * c_loc
        n_ref[...] = decay * n_st + fresh * n_loc
        m_ref[...] = m_new

        inter = b_col + m_st
        m_t = jnp.maximum(inter, d_max)
        s_qk = jnp.where(causal, qk * jnp.exp(log_d - m_t), 0.0)
        inter_w = jnp.exp(inter - m_t)
        num = _dot(s_qk.astype(BF16), vb) + inter_w * _dot(qb, c_st.astype(BF16))
        qn = jnp.sum(qc * n_st, axis=1, keepdims=True)
        den = jnp.sum(s_qk, axis=1, keepdims=True) + inter_w * qn
        hout = num / jnp.maximum(jnp.abs(den), jnp.exp(-m_t))

        if pending is not None:
            finish(*pending)
        pending = (rows, hout)
    finish(*pending)


def _mixer_c(proj, gates, conv_w, conv_b, gate_b, h_gain, n_heads, cast_jobs):
    bsz, s, _ = proj.shape
    d = HEAD_C
    L = SEQ_BLOCK
    nc = s // L
    grid = (bsz, n_heads)
    g_rows = jnp.transpose(gates, (0, 2, 1)).reshape(bsz, 2 * n_heads, nc, L)
    cb = conv_b.reshape(1, -1)
    full = lambda off: pl.BlockSpec((None, s, d), lambda b, h: (b, 0, off + h))
    row = lambda r, off: pl.BlockSpec((r, d), lambda b, h: (0, off + h))
    grow = lambda off: pl.BlockSpec((None, None, nc, L), lambda b, h: (b, off + h, 0, 0))
    return _call_with_jobs(
        _mixer_c_kernel, cast_jobs(grid),
        out_shape=jax.ShapeDtypeStruct((bsz, s, n_heads * d), BF16),
        grid=grid,
        in_specs=[
            pl.BlockSpec(memory_space=pltpu.SMEM),
            full(0), full(n_heads), full(2 * n_heads), full(3 * n_heads),
            row(CONV_W, 0), row(CONV_W, n_heads), row(1, 0), row(1, n_heads),
            grow(0), grow(n_heads), row(1, 0),
        ],
        out_specs=pl.BlockSpec((None, s, d), lambda b, h: (b, 0, h)),
        scratch_shapes=[pltpu.VMEM((s, d), F32), pltpu.VMEM((s, d), F32),
                        pltpu.VMEM((d, d), F32), pltpu.VMEM((1, d), F32), pltpu.VMEM((1, 1), F32)],
        semantics=("arbitrary", "arbitrary"),
        name="mixer_c",
        operands=(gate_b, proj, proj, proj, proj, conv_w, conv_w, cb, cb, g_rows, g_rows,
                  h_gain.reshape(1, -1)))


def _mixer_d_kernel(q_ref, k_ref, v_ref, o_ref, kb_ref, vb_ref, acc_ref, suf_ref):
    s, hd = q_ref.shape
    B = SEQ_BLOCK
    G = min(SB_QUERY_ROWS, s)
    kb_ref[...] = k_ref[...].astype(BF16)
    vb_ref[...] = v_ref[...].astype(BF16)
    ri = lax.broadcasted_iota(jnp.int32, (B, B), 0)
    ci = lax.broadcasted_iota(jnp.int32, (B, B), 1)
    after = (ri > ci).astype(BF16)
    after2 = jnp.concatenate([after, after], axis=0)
    row_g = lax.broadcasted_iota(jnp.int32, (G, B), 0)
    col_g = lax.broadcasted_iota(jnp.int32, (G, B), 1)

    def span_stages(q0, first, last, suffix):
        n = last - first + 1
        keys = slice(first * B, (last + 1) * B)
        qb = (q_ref[q0:q0 + G, :] * (hd ** -0.5)).astype(BF16)
        z = _dot_nt(qb, kb_ref[keys, :])
        sp = jnp.log(1.0 + jnp.exp2(jnp.abs(z) * (-LOG2E)))
        nm = jnp.minimum(z, 0.0)
        log_beta = nm - sp
        log_stay = (nm - z) - sp
        before = [col_g + ((first + j) * B - q0) < row_g if (first + j + 1) * B > q0 else None
                  for j in range(n)]
        hl, tot = [], []
        for j in range(n):
            x = log_stay[:, j * B:(j + 1) * B]
            if before[j] is not None:
                x = jnp.where(before[j], x, 0.0)
            hi = lax.bitcast_convert_type(lax.bitcast_convert_type(x, jnp.uint32) & HI_MASK, F32)
            hl.append(jnp.concatenate([hi.astype(BF16), (x - hi).astype(BF16)], axis=1))
            tot.append(jnp.sum(x, axis=-1, keepdims=True))
        yield
        later = [_dot(h, after2) for h in hl]
        yield
        att = [None] * n
        for j in range(n - 1, -1, -1):
            a = jnp.exp(log_beta[:, j * B:(j + 1) * B] + later[j] + suffix)
            if before[j] is not None:
                a = jnp.where(before[j], a, 0.0)
            att[j] = a.astype(BF16)
            suffix = suffix + tot[j]
        att = att[0] if n == 1 else jnp.concatenate(att, axis=1)
        return _dot(att, vb_ref[keys, :]), suffix

    def step(gen):
        try:
            next(gen)
        except StopIteration as done:
            return done.value
        return None

    def span(q0, first, last, suffix):
        gen = span_stages(q0, first, last, suffix)
        done = None
        while done is None:
            done = step(gen)
        return done

    def first_near(q0):
        return max(q0 // B - SB_NEAR_BLOCKS + 1, 0)

    worst = None
    starts = list(range(0, s, G))
    spans = [span_stages(q0, first_near(q0), (q0 + G) // B - 1, jnp.zeros((G, 1), F32)) for q0 in starts]
    for t in range(len(spans) + 2):
        for i in (t - 2, t - 1, t):
            if not 0 <= i < len(spans):
                continue
            done = step(spans[i])
            if done is None:
                continue
            q0 = starts[i]
            rows = slice(q0, q0 + G)
            acc, suffix = done
            o_ref[rows, :] = acc.astype(o_ref.dtype)
            if first_near(q0) > 0:
                acc_ref[rows, :] = acc
                suf_ref[rows, :] = suffix
                worst = suffix if worst is None else jnp.maximum(worst, suffix)

    if worst is not None:
        @pl.when(jnp.max(worst) > SB_UNDERFLOW)
        def _():
            for q0 in range(0, s, G):
                if first_near(q0) == 0:
                    continue
                rows = slice(q0, q0 + G)
                suffix = suf_ref[rows, :]

                @pl.when(jnp.max(suffix) > SB_UNDERFLOW)
                def _():
                    far, _ = span(q0, 0, first_near(q0) - 1, suffix)
                    o_ref[rows, :] = (acc_ref[rows, :] + far).astype(o_ref.dtype)


def _mixer_d(proj, col0, n_heads, cast_jobs):
    bsz, s, _ = proj.shape
    c0 = col0 // HEAD_D
    grid = (bsz, n_heads)
    seq_spec = lambda off: pl.BlockSpec((None, s, HEAD_D), lambda b, h: (b, 0, off + h))
    return _call_with_jobs(
        _mixer_d_kernel, cast_jobs(grid),
        out_shape=jax.ShapeDtypeStruct((bsz, s, n_heads * HEAD_D), BF16),
        grid=grid,
        in_specs=[seq_spec(c0), seq_spec(c0 + n_heads), seq_spec(c0 + 2 * n_heads)],
        out_specs=pl.BlockSpec((None, s, HEAD_D), lambda b, h: (b, 0, h)),
        scratch_shapes=[pltpu.VMEM((s, HEAD_D), BF16), pltpu.VMEM((s, HEAD_D), BF16),
                        pltpu.VMEM((s, HEAD_D), F32), pltpu.VMEM((s, 1), F32)],
        semantics=("arbitrary", "arbitrary"),
        name="mixer_d",
        operands=(proj, proj, proj))


def _layer_ab(x, bsz, s, g, w_in, conv_w, conv_b, w_r, b_r, w_i, b_i, lam, qk_g, rel_bias, w_out_all, j,
              jobs):
    d_rg = conv_w.shape[-1]
    n_heads = rel_bias.shape[-1]
    proj = _norm_matmul(x, g, w_in.astype(BF16), tn=PROJ_COL_TILE_AB).reshape(bsz, s, -1)
    ya, out_a = _mixer_a(proj, conv_w, conv_b, w_r, b_r, w_i, b_i, lam, d_rg, jobs[0])
    yb, out_b = _mixer_b(proj, qk_g, rel_bias, 2 * d_rg, n_heads,
                         lambda grid: jobs[1](grid) + [_cast_job(w_out_all, (j,), grid)])
    w_out = out_b.pop()
    n = bsz * s
    return _out_proj(x, ya.reshape(n, -1), yb.reshape(n, -1), w_out), out_a + out_b


def _cd_layout(gate_b, w_out):
    n_heads_c = gate_b.shape[-1]
    dc = n_heads_c * HEAD_C
    dd = w_out.shape[0] - dc
    n_gate = 2 * n_heads_c
    main = 4 * dc + 3 * dd
    width = -(-(main + n_gate) // CD_WIDTH_MULTIPLE) * CD_WIDTH_MULTIPLE
    return n_heads_c, dc, dd, n_gate, main, width


def _cd_weight_job(w_in, j, gate_col, n_gate, width, tw=CD_WEIGHT_COL_TILE):
    _, d, cols = w_in.shape
    w_t = jnp.swapaxes(w_in, 1, 2)
    main = cols - n_gate
    assert gate_col % tw == 0 and (main - gate_col) % tw == 0 and width % tw == 0
    assert n_gate % SUBLANES == 0 and n_gate <= tw
    n_aligned = gate_col // tw
    n_shifted = (main - gate_col) // tw
    n_blk = width // tw
    last = (cols - 1) // tw

    def job(grid):
        assert n_blk <= math.prod(grid)
        blk = lambda *idx: jnp.minimum(_linear_step(grid, idx), n_blk - 1)

        def reorder(t, a_ref, b_ref, o_ref):
            def emit(rows):
                eye = (lax.broadcasted_iota(jnp.int32, (tw, tw), 0)
                       == lax.broadcasted_iota(jnp.int32, (tw, tw), 1)).astype(BF16)
                o_ref[...] = _dot_tn(rows.astype(BF16), eye).astype(o_ref.dtype)

            @pl.when(t < n_aligned)
            def _():
                emit(a_ref[...])

            @pl.when((t >= n_aligned) & (t < n_aligned + n_shifted))
            def _():
                emit(jnp.concatenate([a_ref[n_gate:, :], b_ref[:n_gate, :]], axis=0))

            @pl.when(t == n_aligned + n_shifted)
            def _():
                row = lax.broadcasted_iota(jnp.int32, a_ref.shape, 0)
                emit(jnp.where(row < n_gate, a_ref[...], 0.0))

            @pl.when((t > n_aligned + n_shifted) & (t < n_blk))
            def _():
                o_ref[...] = jnp.zeros_like(o_ref)

        a_blk = lambda *idx: jnp.where(blk(*idx) < n_aligned + n_shifted, blk(*idx), n_aligned)
        return [([w_t, w_t],
                 [pl.BlockSpec((None, tw, d), lambda *idx: (j, a_blk(*idx), 0)),
                  pl.BlockSpec((None, tw, d), lambda *idx: (j, jnp.minimum(blk(*idx) + 1, last), 0))],
                 pl.BlockSpec((d, tw), lambda *idx: (0, blk(*idx))),
                 jax.ShapeDtypeStruct((d, width), BF16),
                 reorder)]
    return job


def _layer_cd(x, bsz, s, g, w, conv_w, conv_b, gate_b, h_gain, w_out_all, j, jobs):
    n_heads_c, dc, dd, n_gate, main, width = _cd_layout(gate_b, w_out_all[j])
    n_heads_d = dd // HEAD_D
    proj = _norm_matmul(x, g, w, tn=PROJ_COL_TILE_CD).reshape(bsz, s, width)
    gates = proj[:, :, main:main + n_gate]
    yd, out_d = _mixer_d(proj, 4 * dc, n_heads_d, jobs[0])
    yc, out_c = _mixer_c(proj, gates, conv_w, conv_b, gate_b, h_gain, n_heads_c,
                         lambda grid: jobs[1](grid) + [_cast_job(w_out_all, (j,), grid)])
    w_out = out_c.pop()
    n = bsz * s
    return _out_proj(x, yc.reshape(n, -1), yd.reshape(n, -1), w_out), out_d + out_c


def kernel(x, norm_g, ffn_w_in, ffn_w_out, ab_w_in, ab_conv_w, ab_conv_b, rg_w_r, rg_b_r, rg_w_i,
           rg_b_i, rg_lambda, qk_gain, rel_bias, ab_w_out, cd_w_in, cd_conv_w, cd_conv_b,
           mlstm_gate_bias, mlstm_h_gain, cd_w_out):
    bsz, s, d = x.shape
    depth = norm_g.shape[0]
    h = x.reshape(bsz * s, d)
    ffn_w = {(0, 0): (ffn_w_in[0, 0].astype(BF16), ffn_w_out[0, 0].astype(BF16))}
    cd_w = None
    for layer in range(depth):
        j = layer // 2
        h = _ffn(h, norm_g[layer, 0], *ffn_w[(layer, 0)])
        targets = [(layer, 1)] + ([(layer + 1, 0)] if layer + 1 < depth else [])
        jobs = [functools.partial(_ffn_cast_jobs, ffn_w_in, ffn_w_out, *t) for t in targets]
        jobs += [lambda grid: []] * (2 - len(jobs))
        if layer % 2 == 0:
            if layer + 1 < depth:
                _, dc, _, n_gate, _, width = _cd_layout(mlstm_gate_bias[j], cd_w_out[j])
                ffn_jobs, cd_job = jobs[0], _cd_weight_job(cd_w_in, j, 4 * dc, n_gate, width)
                jobs[0] = lambda grid: ffn_jobs(grid) + cd_job(grid)
            h, out = _layer_ab(h, bsz, s, norm_g[layer, 1], ab_w_in[j], ab_conv_w[j], ab_conv_b[j],
                               rg_w_r[j], rg_b_r[j], rg_w_i[j], rg_b_i[j], rg_lambda[j], qk_gain[j],
                               rel_bias, ab_w_out, j, jobs)
            if layer + 1 < depth:
                cd_w = out.pop(2)
        else:
            h, out = _layer_cd(h, bsz, s, norm_g[layer, 1], cd_w, cd_conv_w[j], cd_conv_b[j],
                               mlstm_gate_bias[j], mlstm_h_gain[j], cd_w_out, j, jobs)
        for i, t in enumerate(targets):
            ffn_w[t] = (out[2 * i], out[2 * i + 1])
        h = _ffn(h, norm_g[layer, 2], *ffn_w[(layer, 1)])
    return h.reshape(bsz, s, d)
```

```python
import functools
import math

import numpy as np
import jax
import jax.numpy as jnp
from jax import lax
from jax.experimental import pallas as pl
from jax.experimental.pallas import tpu as pltpu

F32 = jnp.float32
BF16 = jnp.bfloat16

RMS_EPS = 1e-6
RG_C = 8.0
CONV_W = 4
RG_BLOCK = 128
HEAD_B = 128
HEAD_C = 256
HEAD_D = 128
SEQ_BLOCK = 128
SUBLANES = 8
BF16_SUBLANES = 16
DSW_PATTERNS = ((128, 1), (512, 4), (2048, 16))
N_BUCKETS = 32
MAX_DIST = 2048
NEG = -1e30
LOG2E = 1.4426950408889634
HI_MASK = np.uint32(0xFFFF0000)
SB_NEAR_BLOCKS = 3
SB_QUERY_ROWS = 256
SB_UNDERFLOW = -104.0
DSW_QUERY_ROWS = 256

FFN_TOKEN_TILE = 1024
FFN_HIDDEN_TILE = 512
PROJ_TOKEN_TILE = 1024
PROJ_COL_TILE_AB = 1280
PROJ_COL_TILE_CD = 1536
OUT_PROJ_TOKEN_TILE = 512
CD_WIDTH_MULTIPLE = 512
CD_WEIGHT_COL_TILE = 256
V7X_VMEM_BYTES = 64 * 1024 * 1024
VMEM_LIMIT = V7X_VMEM_BYTES * 7 // 8


def _cparams(sem):
    return pltpu.CompilerParams(dimension_semantics=sem, vmem_limit_bytes=VMEM_LIMIT)


def _rms_rows(x, g):
    return x * lax.rsqrt(jnp.mean(x * x, axis=-1, keepdims=True) + RMS_EPS) * g


def _dot(a, b):
    return jnp.dot(a, b, preferred_element_type=F32)


def _dot_nt(a, b):
    return lax.dot_general(a, b, (((1,), (1,)), ((), ())), preferred_element_type=F32)


def _dot_tn(a, b):
    return lax.dot_general(a, b, (((0,), (0,)), ((), ())), preferred_element_type=F32)


def _pick_tile(n, pref):
    t = min(pref, n)
    while n % t:
        t //= 2
    return t


def _linear_step(grid, idx):
    t = idx[0]
    for i, g in zip(idx[1:], grid[1:]):
        t = t * g + i
    return t


def _ffn_cast_jobs(w_in, w_out, layer, half, grid):
    n_steps = math.prod(grid)
    jobs = []
    for w in (w_in, w_out):
        rows, cols = w.shape[2:]
        n = n_steps
        while rows % n or (rows // n) % BF16_SUBLANES:
            n -= 1
        slab = rows // n
        blk = lambda *idx, n=n: jnp.minimum(_linear_step(grid, idx), n - 1)

        def cast(t, src_ref, dst_ref, n=n):
            @pl.when(t < n)
            def _():
                dst_ref[...] = src_ref[...].astype(dst_ref.dtype)

        jobs.append((
            [w],
            [pl.BlockSpec((None, None, slab, cols), lambda *idx, blk=blk: (layer, half, blk(*idx), 0))],
            pl.BlockSpec((slab, cols), lambda *idx, blk=blk: (blk(*idx), 0)),
            jax.ShapeDtypeStruct((rows, cols), BF16),
            cast,
        ))
    return jobs


def _call_with_jobs(body, jobs, *, out_shape, grid, in_specs, out_specs, scratch_shapes=(),
                    semantics, name, operands):
    n_in = len(in_specs)
    n_job_in = [len(j[0]) for j in jobs]
    n_extra = sum(n_job_in)

    def kernel(*refs):
        t = _linear_step(grid, [pl.program_id(ax) for ax in range(len(grid))])
        at = n_in
        for i, j in enumerate(jobs):
            j[4](t, *refs[at:at + n_job_in[i]], refs[n_in + n_extra + 1 + i])
            at += n_job_in[i]
        body(*refs[:n_in], refs[n_in + n_extra], *refs[n_in + n_extra + 1 + len(jobs):])

    outs = pl.pallas_call(
        kernel,
        out_shape=[out_shape] + [j[3] for j in jobs],
        grid=grid,
        in_specs=list(in_specs) + [sp for j in jobs for sp in j[1]],
        out_specs=[out_specs] + [j[2] for j in jobs],
        scratch_shapes=list(scratch_shapes),
        compiler_params=_cparams(semantics),
        name=name,
    )(*operands, *[op for j in jobs for op in j[0]])
    return outs[0], list(outs[1:])


def _ffn_kernel(x_ref, g_ref, wg_ref, wu_ref, wo_ref, o_ref, xn_ref):
    j = pl.program_id(1)

    @pl.when(j == 0)
    def _():
        x = x_ref[...]
        xn_ref[...] = _rms_rows(x, g_ref[...]).astype(BF16)
        o_ref[...] = x

    xn = xn_ref[...]
    hg = _dot(xn, wg_ref[...])
    hu = _dot(xn, wu_ref[...])
    h = (0.5 * hg * jax.nn.sigmoid(hg) * hu).astype(BF16)
    o_ref[...] += _dot(h, wo_ref[...])


def _ffn(x, g, w_in, w_out, tm=FFN_TOKEN_TILE, tf=FFN_HIDDEN_TILE):
    n, d = x.shape
    d_ff = w_out.shape[0]
    tm = _pick_tile(n, tm)
    tf = _pick_tile(d_ff, tf)
    nj = d_ff // tf
    return pl.pallas_call(
        _ffn_kernel,
        out_shape=jax.ShapeDtypeStruct((n, d), F32),
        grid=(n // tm, nj),
        in_specs=[
            pl.BlockSpec((tm, d), lambda i, j: (i, 0)),
            pl.BlockSpec((1, d), lambda i, j: (0, 0)),
            pl.BlockSpec((d, tf), lambda i, j: (0, j)),
            pl.BlockSpec((d, tf), lambda i, j: (0, j + nj)),
            pl.BlockSpec((tf, d), lambda i, j: (j, 0)),
        ],
        out_specs=pl.BlockSpec((tm, d), lambda i, j: (i, 0)),
        scratch_shapes=[pltpu.VMEM((tm, d), BF16)],
        compiler_params=_cparams(("parallel", "arbitrary")),
        name="ffn",
    )(x, g.reshape(1, d), w_in, w_in, w_out)


def _norm_matmul_kernel(x_ref, g_ref, w_ref, o_ref, xn_ref):
    @pl.when(pl.program_id(1) == 0)
    def _():
        xn_ref[...] = _rms_rows(x_ref[...], g_ref[...]).astype(BF16)

    o_ref[...] = _dot(xn_ref[...], w_ref[...])


def _norm_matmul(x, g, w, tn, tm=PROJ_TOKEN_TILE):
    n, d = x.shape
    wn = w.shape[1]
    tm = _pick_tile(n, tm)
    while wn % tn:
        tn -= 128
    return pl.pallas_call(
        _norm_matmul_kernel,
        out_shape=jax.ShapeDtypeStruct((n, wn), F32),
        grid=(n // tm, wn // tn),
        in_specs=[
            pl.BlockSpec((tm, d), lambda i, j: (i, 0)),
            pl.BlockSpec((1, d), lambda i, j: (0, 0)),
            pl.BlockSpec((d, tn), lambda i, j: (0, j)),
        ],
        out_specs=pl.BlockSpec((tm, tn), lambda i, j: (i, j)),
        scratch_shapes=[pltpu.VMEM((tm, d), BF16)],
        compiler_params=_cparams(("parallel", "arbitrary")),
        name="norm_matmul",
    )(x, g.reshape(1, d), w)


def _out_proj_kernel(x_ref, ya_ref, yb_ref, w_ref, o_ref):
    y = jnp.concatenate([ya_ref[...], yb_ref[...]], axis=1)
    o_ref[...] = x_ref[...] + _dot(y, w_ref[...])


def _out_proj(x, ya, yb, w, tm=OUT_PROJ_TOKEN_TILE):
    n, d = x.shape
    da = ya.shape[1]
    db = yb.shape[1]
    assert w.shape[0] == da + db
    tm = _pick_tile(n, tm)
    return pl.pallas_call(
        _out_proj_kernel,
        out_shape=jax.ShapeDtypeStruct((n, d), F32),
        grid=(n // tm,),
        in_specs=[
            pl.BlockSpec((tm, d), lambda i: (i, 0)),
            pl.BlockSpec((tm, da), lambda i: (i, 0)),
            pl.BlockSpec((tm, db), lambda i: (i, 0)),
            pl.BlockSpec((da + db, d), lambda i: (0, 0)),
        ],
        out_specs=pl.BlockSpec((tm, d), lambda i: (i, 0)),
        compiler_params=_cparams(("parallel",)),
        name="out_proj",
    )(x, ya, yb, w)


def _causal_conv(x, w, b):
    y = x * w[CONV_W - 1:CONV_W] + b
    for k in range(1, CONV_W):
        y = y + _shift_rows(x, k, 0.0) * w[CONV_W - 1 - k:CONV_W - k]
    return y


def _shift_rows(x, k, fill):
    assert 0 < k < SUBLANES
    rolled = pltpu.roll(x, k, axis=0)
    t = lax.broadcasted_iota(jnp.int32, (SUBLANES, x.shape[1]), 0)
    head = jnp.where(t >= k, rolled[:SUBLANES], fill)
    return jnp.concatenate([head, rolled[SUBLANES:]], axis=0)


def _mixer_a_kernel(xr_ref, gate_ref, cw_ref, cb_ref, wr_ref, br_ref, wi_ref, bi_ref,
                    lam_ref, o_ref, a_ref, u_ref):
    s = xr_ref.shape[0]
    xc = _causal_conv(xr_ref[...], cw_ref[...], cb_ref[...])
    xb = xc.astype(BF16)
    r = jax.nn.sigmoid(_dot(xb, wr_ref[...]) + br_ref[...])
    i = jax.nn.sigmoid(_dot(xb, wi_ref[...]) + bi_ref[...])
    neg_lam = -lam_ref[...]
    softplus = jnp.maximum(neg_lam, 0.0) + jnp.log1p(jnp.exp(-jnp.abs(neg_lam)))
    log_a = -RG_C * r * softplus
    a = jnp.exp(log_a)
    a_ref[...] = a
    u_ref[...] = jnp.sqrt(-jnp.tanh(log_a) * (a * a + 1.0)) * (i * xc)
    B = SEQ_BLOCK
    carry = None
    for c in range(s // B):
        rows = slice(c * B, (c + 1) * B)
        a = a_ref[rows, :]
        u = u_ref[rows, :]
        k = 1
        while k < B:
            if k < SUBLANES:
                u = a * _shift_rows(u, k, 0.0) + u
                a = a * _shift_rows(a, k, 1.0)
            else:
                u = jnp.concatenate([u[:k], a[k:] * u[:B - k] + u[k:]], axis=0)
                a = jnp.concatenate([a[:k], a[k:] * a[:B - k]], axis=0)
            k *= 2
        if carry is not None:
            u = a * carry + u
        carry = u[B - 1:B, :]
        o_ref[rows, :] = (jax.nn.gelu(gate_ref[rows, :]) * u).astype(o_ref.dtype)


def _mixer_a(proj, conv_w, conv_b, w_r, b_r, w_i, b_i, lam, d_rg, cast_jobs):
    bsz, s, _ = proj.shape
    nb = d_rg // RG_BLOCK
    grid = (bsz, nb)
    vec = lambda v: v.reshape(1, d_rg)
    col = lambda off: pl.BlockSpec((None, s, RG_BLOCK), lambda b, g: (b, 0, off + g))
    row = lambda r: pl.BlockSpec((r, RG_BLOCK), lambda b, g: (0, g))
    blk = pl.BlockSpec((None, RG_BLOCK, RG_BLOCK), lambda b, g: (g, 0, 0))
    return _call_with_jobs(
        _mixer_a_kernel, cast_jobs(grid),
        out_shape=jax.ShapeDtypeStruct((bsz, s, d_rg), BF16),
        grid=grid,
        in_specs=[col(nb), col(0), row(CONV_W), row(1), blk, row(1), blk, row(1), row(1)],
        out_specs=pl.BlockSpec((None, s, RG_BLOCK), lambda b, g: (b, 0, g)),
        scratch_shapes=[pltpu.VMEM((s, RG_BLOCK), F32), pltpu.VMEM((s, RG_BLOCK), F32)],
        semantics=("arbitrary", "arbitrary"),
        name="mixer_a",
        operands=(proj, proj, conv_w, vec(conv_b), w_r.astype(BF16), vec(b_r), w_i.astype(BF16),
                  vec(b_i), vec(lam)))


def _t5_bucket_np(dist):
    max_exact = N_BUCKETS // 2
    d_f = np.maximum(dist, 1).astype(np.float32)
    large = max_exact + (np.log(d_f / np.float32(max_exact)) / np.float32(math.log(MAX_DIST / max_exact))
                         * np.float32(N_BUCKETS - max_exact)).astype(np.int32)
    large = np.minimum(large, N_BUCKETS - 1)
    return np.where(dist < max_exact, dist, large).astype(np.int32)


def _dsw_bias_vector(rel_bias, seq, qb):
    dist = np.arange(seq)
    mult = np.zeros(seq, np.float32)
    for window, dilation in DSW_PATTERNS:
        mult += ((dist % dilation == 0) & (dist <= window)).astype(np.float32)
    logm = np.where(mult > 0, np.log(np.maximum(mult, 1.0)), NEG).astype(np.float32)
    f = (rel_bias.astype(F32)[_t5_bucket_np(dist)] + logm[:, None]) * LOG2E
    n_heads = f.shape[1]
    hv = jnp.concatenate([f[::-1].T, jnp.full((n_heads, qb), NEG, F32)], axis=1)
    return hv[:, None, :]


def _mixer_b_kernel(q_ref, k_ref, v_ref, g_ref, hv_ref, o_ref, kn_ref, vb_ref, tbl_ref):
    s, hd = q_ref.shape
    qb = tbl_ref.shape[0]

    @pl.when(pl.program_id(1) == 0)
    def _():
        skew = pltpu.roll(jnp.broadcast_to(hv_ref[...], (qb, s + qb)), 1, axis=1, stride=1, stride_axis=0)
        tbl_ref[...] = skew[:, qb:]

    kn_ref[...] = _rms_rows(k_ref[...], g_ref[1:2]).astype(BF16)
    vb_ref[...] = v_ref[...].astype(BF16)
    def logits(gi):
        n_keys = (gi + 1) * qb
        qn = (_rms_rows(q_ref[gi * qb:n_keys, :], g_ref[0:1]) * (LOG2E / math.sqrt(hd))).astype(BF16)
        return _dot_nt(qn, kn_ref[0:n_keys, :]) + tbl_ref[:, s - n_keys:]

    def weights(sc):
        m = jnp.max(sc, axis=-1, keepdims=True)
        p = jnp.exp2(sc - m)
        return p.astype(BF16), jnp.sum(p, axis=-1, keepdims=True)

    def output(gi, p, l):
        n_keys = (gi + 1) * qb
        o = _dot(p, vb_ref[0:n_keys, :])
        o_ref[gi * qb:n_keys, :] = (o / l).astype(o_ref.dtype)

    order = list(reversed(range(s // qb)))
    sc = logits(order[0])
    held = None
    for t, gi in enumerate(order):
        nxt = logits(order[t + 1]) if t + 1 < len(order) else None
        cur = (gi,) + weights(sc)
        if held is not None:
            output(*held)
        held = cur
        sc = nxt
    output(*held)


def _mixer_b(proj, qk_g, rel_bias, col0, n_heads, cast_jobs, qb=DSW_QUERY_ROWS):
    bsz, s, _ = proj.shape
    qb = _pick_tile(s, qb)
    hv = _dsw_bias_vector(rel_bias, s, qb)
    c0 = col0 // HEAD_B
    grid = (n_heads, bsz)
    seq_spec = lambda off: pl.BlockSpec((None, s, HEAD_B), lambda h, b: (b, 0, off + h))
    return _call_with_jobs(
        _mixer_b_kernel, cast_jobs(grid),
        out_shape=jax.ShapeDtypeStruct((bsz, s, n_heads * HEAD_B), BF16),
        grid=grid,
        in_specs=[
            seq_spec(c0), seq_spec(c0 + n_heads), seq_spec(c0 + 2 * n_heads),
            pl.BlockSpec((2, HEAD_B), lambda h, b: (0, 0)),
            pl.BlockSpec((None, 1, s + qb), lambda h, b: (h, 0, 0)),
        ],
        out_specs=pl.BlockSpec((None, s, HEAD_B), lambda h, b: (b, 0, h)),
        scratch_shapes=[pltpu.VMEM((s, HEAD_B), BF16), pltpu.VMEM((s, HEAD_B), BF16),
                        pltpu.VMEM((qb, s), F32)],
        semantics=("arbitrary", "arbitrary"),
        name="mixer_b",
        operands=(proj, proj, proj, qk_g, hv))


def _mixer_c_kernel(gb_ref, q_ref, k_ref, v_ref, og_ref, cwq_ref, cwk_ref, cbq_ref, cbk_ref,
                    ig_ref, fg_ref, gain_ref, o_ref, qc_ref, kc_ref, c_ref, n_ref, m_ref):
    h = pl.program_id(1)
    L = SEQ_BLOCK
    s, d = q_ref.shape
    nc = s // L

    qc_ref[...] = jax.nn.silu(_causal_conv(q_ref[...], cwq_ref[...], cbq_ref[...]))
    kc_ref[...] = jax.nn.silu(_causal_conv(k_ref[...], cwk_ref[...], cbk_ref[...])) * (d ** -0.5)
    c_ref[...] = jnp.zeros_like(c_ref)
    n_ref[...] = jnp.zeros_like(n_ref)
    m_ref[...] = jnp.zeros_like(m_ref)

    ri = lax.broadcasted_iota(jnp.int32, (L, L), 0)
    ci = lax.broadcasted_iota(jnp.int32, (L, L), 1)
    eye = ri == ci
    causal = ci <= ri

    def to_col(row):
        return jnp.sum(jnp.where(eye, row, 0.0), axis=1, keepdims=True)

    ig_all = ig_ref[...] + gb_ref[0, h]
    f_all = fg_ref[...] + gb_ref[1, h]
    ls_all = jnp.minimum(f_all, 0.0) - jnp.log1p(jnp.exp(-jnp.abs(f_all)))

    def chunk_local(n):
        rows = slice(n * L, (n + 1) * L)
        qc = qc_ref[rows, :]
        kc = kc_ref[rows, :]
        qb = qc.astype(BF16)
        kb = kc.astype(BF16)
        vb = v_ref[rows, :].astype(BF16)

        ig_row = ig_all[n:n + 1]
        ls_row = ls_all[n:n + 1]
        ls_col = to_col(ls_row)
        ig_col = to_col(ig_row)
        b_col = jnp.sum(jnp.where(causal, ls_row, 0.0), axis=1, keepdims=True)
        b_row = jnp.sum(jnp.where(ri <= ci, ls_col, 0.0), axis=0, keepdims=True)
        b_last = jnp.sum(ls_row, axis=1, keepdims=True)

        log_d = b_col - b_row + ig_row
        g_row = b_last - b_row + ig_row
        g_col = b_last - b_col + ig_col
        g_max = jnp.max(g_row, axis=1, keepdims=True)
        d_max = jnp.max(jnp.where(causal, log_d, -jnp.inf), axis=1, keepdims=True)
        qk = _dot_nt(qb, kb)
        wk = jnp.exp(g_col - g_max) * kc
        c_loc = _dot_tn(wk.astype(BF16), vb)
        n_loc = jnp.sum(wk, axis=0, keepdims=True)
        return rows, qc, qb, vb, b_col, b_last, log_d, g_max, d_max, qk, c_loc, n_loc

    def finish(rows, hout):
        hn = _rms_rows(hout, gain_ref[...])
        o_ref[rows, :] = (jax.nn.sigmoid(og_ref[rows, :]) * hn).astype(o_ref.dtype)

    pending = None
    local = chunk_local(0)
    for n in range(nc):
        rows, qc, qb, vb, b_col, b_last, log_d, g_max, d_max, qk, c_loc, n_loc = local
        if n + 1 < nc:
            local = chunk_local(n + 1)

        c_st = c_ref[...]
        n_st = n_ref[...]
        m_st = m_ref[...]

        m_new = jnp.maximum(b_last + m_st, g_max)
        decay = jnp.exp(b_last + m_st - m_new)
        fresh = jnp.exp(g_max - m_new)
        c_ref[...] = decay * c_st + fresh * c_loc
        n_ref[...] = decay * n_st + fresh * n_loc
        m_ref[...] = m_new

        inter = b_col + m_st
        m_t = jnp.maximum(inter, d_max)
        s_qk = jnp.where(causal, qk * jnp.exp(log_d - m_t), 0.0)
        inter_w = jnp.exp(inter - m_t)
        num = _dot(s_qk.astype(BF16), vb) + inter_w * _dot(qb, c_st.astype(BF16))
        qn = jnp.sum(qc * n_st, axis=1, keepdims=True)
        den = jnp.sum(s_qk, axis=1, keepdims=True) + inter_w * qn
        hout = num / jnp.maximum(jnp.abs(den), jnp.exp(-m_t))

        if pending is not None:
            finish(*pending)
        pending = (rows, hout)
    finish(*pending)


def _mixer_c(proj, gates, conv_w, conv_b, gate_b, h_gain, n_heads, cast_jobs):
    bsz, s, _ = proj.shape
    d = HEAD_C
    L = SEQ_BLOCK
    nc = s // L
    grid = (bsz, n_heads)
    g_rows = jnp.transpose(gates, (0, 2, 1)).reshape(bsz, 2 * n_heads, nc, L)
    cb = conv_b.reshape(1, -1)
    full = lambda off: pl.BlockSpec((None, s, d), lambda b, h: (b, 0, off + h))
    row = lambda r, off: pl.BlockSpec((r, d), lambda b, h: (0, off + h))
    grow = lambda off: pl.BlockSpec((None, None, nc, L), lambda b, h: (b, off + h, 0, 0))
    return _call_with_jobs(
        _mixer_c_kernel, cast_jobs(grid),
        out_shape=jax.ShapeDtypeStruct((bsz, s, n_heads * d), BF16),
        grid=grid,
        in_specs=[
            pl.BlockSpec(memory_space=pltpu.SMEM),
            full(0), full(n_heads), full(2 * n_heads), full(3 * n_heads),
            row(CONV_W, 0), row(CONV_W, n_heads), row(1, 0), row(1, n_heads),
            grow(0), grow(n_heads), row(1, 0),
        ],
        out_specs=pl.BlockSpec((None, s, d), lambda b, h: (b, 0, h)),
        scratch_shapes=[pltpu.VMEM((s, d), F32), pltpu.VMEM((s, d), F32),
                        pltpu.VMEM((d, d), F32), pltpu.VMEM((1, d), F32), pltpu.VMEM((1, 1), F32)],
        semantics=("arbitrary", "arbitrary"),
        name="mixer_c",
        operands=(gate_b, proj, proj, proj, proj, conv_w, conv_w, cb, cb, g_rows, g_rows,
                  h_gain.reshape(1, -1)))


def _mixer_d_kernel(q_ref, k_ref, v_ref, o_ref, kb_ref, vb_ref, acc_ref, suf_ref):
    s, hd = q_ref.shape
    B = SEQ_BLOCK
    G = min(SB_QUERY_ROWS, s)
    kb_ref[...] = k_ref[...].astype(BF16)
    vb_ref[...] = v_ref[...].astype(BF16)
    ri = lax.broadcasted_iota(jnp.int32, (B, B), 0)
    ci = lax.broadcasted_iota(jnp.int32, (B, B), 1)
    after = (ri > ci).astype(BF16)
    after2 = jnp.concatenate([after, after], axis=0)
    row_g = lax.broadcasted_iota(jnp.int32, (G, B), 0)
    col_g = lax.broadcasted_iota(jnp.int32, (G, B), 1)

    def span_stages(q0, first, last, suffix):
        n = last - first + 1
        keys = slice(first * B, (last + 1) * B)
        qb = (q_ref[q0:q0 + G, :] * (hd ** -0.5)).astype(BF16)
        z = _dot_nt(qb, kb_ref[keys, :])
        sp = jnp.log(1.0 + jnp.exp2(jnp.abs(z) * (-LOG2E)))
        nm = jnp.minimum(z, 0.0)
        log_beta = nm - sp
        log_stay = (nm - z) - sp
        before = [col_g + ((first + j) * B - q0) < row_g if (first + j + 1) * B > q0 else None
                  for j in range(n)]
        hl, tot = [], []
        for j in range(n):
            x = log_stay[:, j * B:(j + 1) * B]
            if before[j] is not None:
                x = jnp.where(before[j], x, 0.0)
            hi = lax.bitcast_convert_type(lax.bitcast_convert_type(x, jnp.uint32) & HI_MASK, F32)
            hl.append(jnp.concatenate([hi.astype(BF16), (x - hi).astype(BF16)], axis=1))
            tot.append(jnp.sum(x, axis=-1, keepdims=True))
        yield
        later = [_dot(h, after2) for h in hl]
        yield
        att = [None] * n
        for j in range(n - 1, -1, -1):
            a = jnp.exp(log_beta[:, j * B:(j + 1) * B] + later[j] + suffix)
            if before[j] is not None:
                a = jnp.where(before[j], a, 0.0)
            att[j] = a.astype(BF16)
            suffix = suffix + tot[j]
        att = att[0] if n == 1 else jnp.concatenate(att, axis=1)
        return _dot(att, vb_ref[keys, :]), suffix

    def step(gen):
        try:
            next(gen)
        except StopIteration as done:
            return done.value
        return None

    def span(q0, first, last, suffix):
        gen = span_stages(q0, first, last, suffix)
        done = None
        while done is None:
            done = step(gen)
        return done

    def first_near(q0):
        return max(q0 // B - SB_NEAR_BLOCKS + 1, 0)

    worst = None
    starts = list(range(0, s, G))
    spans = [span_stages(q0, first_near(q0), (q0 + G) // B - 1, jnp.zeros((G, 1), F32)) for q0 in starts]
    for t in range(len(spans) + 2):
        for i in (t - 2, t - 1, t):
            if not 0 <= i < len(spans):
                continue
            done = step(spans[i])
            if done is None:
                continue
            q0 = starts[i]
            rows = slice(q0, q0 + G)
            acc, suffix = done
            o_ref[rows, :] = acc.astype(o_ref.dtype)
            if first_near(q0) > 0:
                acc_ref[rows, :] = acc
                suf_ref[rows, :] = suffix
                worst = suffix if worst is None else jnp.maximum(worst, suffix)

    if worst is not None:
        @pl.when(jnp.max(worst) > SB_UNDERFLOW)
        def _():
            for q0 in range(0, s, G):
                if first_near(q0) == 0:
                    continue
                rows = slice(q0, q0 + G)
                suffix = suf_ref[rows, :]

                @pl.when(jnp.max(suffix) > SB_UNDERFLOW)
                def _():
                    far, _ = span(q0, 0, first_near(q0) - 1, suffix)
                    o_ref[rows, :] = (acc_ref[rows, :] + far).astype(o_ref.dtype)


def _mixer_d(proj, col0, n_heads, cast_jobs):
    bsz, s, _ = proj.shape
    c0 = col0 // HEAD_D
    grid = (bsz, n_heads)
    seq_spec = lambda off: pl.BlockSpec((None, s, HEAD_D), lambda b, h: (b, 0, off + h))
    return _call_with_jobs(
        _mixer_d_kernel, cast_jobs(grid),
        out_shape=jax.ShapeDtypeStruct((bsz, s, n_heads * HEAD_D), BF16),
        grid=grid,
        in_specs=[seq_spec(c0), seq_spec(c0 + n_heads), seq_spec(c0 + 2 * n_heads)],
        out_specs=pl.BlockSpec((None, s, HEAD_D), lambda b, h: (b, 0, h)),
        scratch_shapes=[pltpu.VMEM((s, HEAD_D), BF16), pltpu.VMEM((s, HEAD_D), BF16),
                        pltpu.VMEM((s, HEAD_D), F32), pltpu.VMEM((s, 1), F32)],
        semantics=("arbitrary", "arbitrary"),
        name="mixer_d",
        operands=(proj, proj, proj))


def _layer_ab(x, bsz, s, g, w_in, conv_w, conv_b, w_r, b_r, w_i, b_i, lam, qk_g, rel_bias, w_out,
              jobs):
    d_rg = conv_w.shape[-1]
    n_heads = rel_bias.shape[-1]
    proj = _norm_matmul(x, g, w_in.astype(BF16), tn=PROJ_COL_TILE_AB).reshape(bsz, s, -1)
    ya, out_a = _mixer_a(proj, conv_w, conv_b, w_r, b_r, w_i, b_i, lam, d_rg, jobs[0])
    yb, out_b = _mixer_b(proj, qk_g, rel_bias, 2 * d_rg, n_heads, jobs[1])
    n = bsz * s
    return _out_proj(x, ya.reshape(n, -1), yb.reshape(n, -1), w_out.astype(BF16)), out_a + out_b


def _cd_layout(gate_b, w_out):
    n_heads_c = gate_b.shape[-1]
    dc = n_heads_c * HEAD_C
    dd = w_out.shape[0] - dc
    n_gate = 2 * n_heads_c
    main = 4 * dc + 3 * dd
    width = -(-(main + n_gate) // CD_WIDTH_MULTIPLE) * CD_WIDTH_MULTIPLE
    return n_heads_c, dc, dd, n_gate, main, width


def _cd_weight_job(w_in, j, gate_col, n_gate, width, tw=CD_WEIGHT_COL_TILE):
    _, d, cols = w_in.shape
    w_t = jnp.swapaxes(w_in, 1, 2)
    main = cols - n_gate
    assert gate_col % tw == 0 and (main - gate_col) % tw == 0 and width % tw == 0
    assert n_gate % SUBLANES == 0 and n_gate <= tw
    n_aligned = gate_col // tw
    n_shifted = (main - gate_col) // tw
    n_blk = width // tw
    last = (cols - 1) // tw

    def job(grid):
        assert n_blk <= math.prod(grid)
        blk = lambda *idx: jnp.minimum(_linear_step(grid, idx), n_blk - 1)

        def reorder(t, a_ref, b_ref, o_ref):
            def emit(rows):
                eye = (lax.broadcasted_iota(jnp.int32, (tw, tw), 0)
                       == lax.broadcasted_iota(jnp.int32, (tw, tw), 1)).astype(BF16)
                o_ref[...] = _dot_tn(rows.astype(BF16), eye).astype(o_ref.dtype)

            @pl.when(t < n_aligned)
            def _():
                emit(a_ref[...])

            @pl.when((t >= n_aligned) & (t < n_aligned + n_shifted))
            def _():
                emit(jnp.concatenate([a_ref[n_gate:, :], b_ref[:n_gate, :]], axis=0))

            @pl.when(t == n_aligned + n_shifted)
            def _():
                row = lax.broadcasted_iota(jnp.int32, a_ref.shape, 0)
                emit(jnp.where(row < n_gate, a_ref[...], 0.0))

            @pl.when((t > n_aligned + n_shifted) & (t < n_blk))
            def _():
                o_ref[...] = jnp.zeros_like(o_ref)

        a_blk = lambda *idx: jnp.where(blk(*idx) < n_aligned + n_shifted, blk(*idx), n_aligned)
        return [([w_t, w_t],
                 [pl.BlockSpec((None, tw, d), lambda *idx: (j, a_blk(*idx), 0)),
                  pl.BlockSpec((None, tw, d), lambda *idx: (j, jnp.minimum(blk(*idx) + 1, last), 0))],
                 pl.BlockSpec((d, tw), lambda *idx: (0, blk(*idx))),
                 jax.ShapeDtypeStruct((d, width), BF16),
                 reorder)]
    return job


def _layer_cd(x, bsz, s, g, w, conv_w, conv_b, gate_b, h_gain, w_out, jobs):
    n_heads_c, dc, dd, n_gate, main, width = _cd_layout(gate_b, w_out)
    n_heads_d = dd // HEAD_D
    proj = _norm_matmul(x, g, w, tn=PROJ_COL_TILE_CD).reshape(bsz, s, width)
    gates = proj[:, :, main:main + n_gate]
    yd, out_d = _mixer_d(proj, 4 * dc, n_heads_d, jobs[0])
    yc, out_c = _mixer_c(proj, gates, conv_w, conv_b, gate_b, h_gain, n_heads_c, jobs[1])
    n = bsz * s
    return _out_proj(x, yc.reshape(n, -1), yd.reshape(n, -1), w_out.astype(BF16)), out_d + out_c


def kernel(x, norm_g, ffn_w_in, ffn_w_out, ab_w_in, ab_conv_w, ab_conv_b, rg_w_r, rg_b_r, rg_w_i,
           rg_b_i, rg_lambda, qk_gain, rel_bias, ab_w_out, cd_w_in, cd_conv_w, cd_conv_b,
           mlstm_gate_bias, mlstm_h_gain, cd_w_out):
    bsz, s, d = x.shape
    depth = norm_g.shape[0]
    h = x.reshape(bsz * s, d)
    ffn_w = {(0, 0): (ffn_w_in[0, 0].astype(BF16), ffn_w_out[0, 0].astype(BF16))}
    cd_w = None
    for layer in range(depth):
        j = layer // 2
        h = _ffn(h, norm_g[layer, 0], *ffn_w[(layer, 0)])
        targets = [(layer, 1)] + ([(layer + 1, 0)] if layer + 1 < depth else [])
        jobs = [functools.partial(_ffn_cast_jobs, ffn_w_in, ffn_w_out, *t) for t in targets]
        jobs += [lambda grid: []] * (2 - len(jobs))
        if layer % 2 == 0:
            if layer + 1 < depth:
                _, dc, _, n_gate, _, width = _cd_layout(mlstm_gate_bias[j], cd_w_out[j])
                ffn_jobs, cd_job = jobs[0], _cd_weight_job(cd_w_in, j, 4 * dc, n_gate, width)
                jobs[0] = lambda grid: ffn_jobs(grid) + cd_job(grid)
            h, out = _layer_ab(h, bsz, s, norm_g[layer, 1], ab_w_in[j], ab_conv_w[j], ab_conv_b[j],
                               rg_w_r[j], rg_b_r[j], rg_w_i[j], rg_b_i[j], rg_lambda[j], qk_gain[j],
                               rel_bias, ab_w_out[j], jobs)
            if layer + 1 < depth:
                cd_w = out.pop(2)
        else:
            h, out = _layer_cd(h, bsz, s, norm_g[layer, 1], cd_w, cd_conv_w[j], cd_conv_b[j],
                               mlstm_gate_bias[j], mlstm_h_gain[j], cd_w_out[j], jobs)
        for i, t in enumerate(targets):
            ffn_w[t] = (out[2 * i], out[2 * i + 1])
        h = _ffn(h, norm_g[layer, 2], *ffn_w[(layer, 1)])
    return h.reshape(bsz, s, d)
```
